```python
import math
import jax, jax.numpy as jnp
from jax import lax
import numpy as np

D_MODEL = 2048
BATCH = 2
SEQ = 4096
DEPTH = 1

ATTN_HEADS = 16
HEAD_DIM = 64
ATTN_WIDTH = ATTN_HEADS * HEAD_DIM
SSM_GROUP_CH = 16
SSM_GROUPS = 64
SSM_WIDTH = SSM_GROUPS * SSM_GROUP_CH
SSM_STATE = 64
MIX_WIDTH = ATTN_WIDTH + SSM_WIDTH
IN_COLS = 3 * ATTN_WIDTH + SSM_WIDTH
MOBA_BLOCK = 256
MOBA_TOP_K = 3
Q_CHUNK = 128
NUM_BUCKETS = 32
MAX_DISTANCE = 128
D_FF = 5632
CONV_WIDTH = 3
RMS_EPS = 1e-6
DT_MIN = 0.001
DT_MAX = 0.1

kernel_name = "hymba_moba_s5_convffn"


def rms_norm(x, g):
    xf = x.astype(jnp.float32)
    y = xf * lax.rsqrt(jnp.mean(xf * xf, axis=-1, keepdims=True) + RMS_EPS)
    return (y * g.astype(jnp.float32)).astype(x.dtype)


def t5_bucket(dist):
    dist = jnp.maximum(dist, 0)
    max_exact = NUM_BUCKETS // 2
    log_ratio = jnp.log(jnp.maximum(dist, max_exact).astype(jnp.float32) / max_exact)
    large = max_exact + (log_ratio / math.log(MAX_DISTANCE / max_exact)
                         * (NUM_BUCKETS - max_exact)).astype(jnp.int32)
    large = jnp.minimum(large, NUM_BUCKETS - 1)
    return jnp.where(dist < max_exact, dist, large)


def moba_attention(q, k, v, bias_table):
    b_sz, n_h, seq, dh = q.shape
    l_pad = -(-seq // MOBA_BLOCK) * MOBA_BLOCK
    pad = ((0, 0), (0, 0), (0, l_pad - seq), (0, 0))
    q, k, v = jnp.pad(q, pad), jnp.pad(k, pad), jnp.pad(v, pad)
    n_blocks = l_pad // MOBA_BLOCK
    k_sel = min(MOBA_TOP_K, n_blocks)
    kb = k.reshape(b_sz, n_h, n_blocks, MOBA_BLOCK, dh)
    vb = v.reshape(b_sz, n_h, n_blocks, MOBA_BLOCK, dh)
    k_mean = jnp.mean(kb.astype(jnp.float32), axis=3)
    n_chunks = l_pad // Q_CHUNK
    q_chunks = q.reshape(b_sz, n_h, n_chunks, Q_CHUNK, dh).transpose(2, 0, 1, 3, 4)
    bias_t = bias_table.astype(jnp.float32).T
    b_ix = jnp.arange(b_sz)[:, None, None]
    h_ix = jnp.arange(n_h)[None, :, None]
    key_off = jnp.arange(MOBA_BLOCK)
    scale = dh ** -0.5

    def one_chunk(args):
        c, q_c = args
        q_pos = c * Q_CHUNK + jnp.arange(Q_CHUNK)
        own = (c * Q_CHUNK) // MOBA_BLOCK
        gate = jnp.einsum('bhqd,bhnd->bhqn', q_c.astype(jnp.float32), k_mean)
        gate = jnp.where(jnp.arange(n_blocks) < own, gate, -jnp.inf)
        _, sel = lax.top_k(gate, k_sel)
        logits = []
        for s in range(k_sel):
            blk = sel[..., s]
            k_s = kb[b_ix, h_ix, blk]
            lg = jnp.einsum('bhqd,bhqkd->bhqk', q_c, k_s).astype(jnp.float32) * scale
            k_pos = blk[..., None] * MOBA_BLOCK + key_off
            lg = lg + bias_t[h_ix[..., None], t5_bucket(q_pos[:, None] - k_pos)]
            logits.append(jnp.where(s < own, lg, -jnp.inf))
        k_own = lax.dynamic_index_in_dim(kb, own, axis=2, keepdims=False)
        v_own = lax.dynamic_index_in_dim(vb, own, axis=2, keepdims=False)
        rel = q_pos[:, None] - (own * MOBA_BLOCK + key_off)[None, :]
        lg_own = (jnp.einsum('bhqd,bhkd->bhqk', q_c, k_own).astype(jnp.float32) * scale
                  + bias_t[:, t5_bucket(rel)])
        logits.append(jnp.where(rel >= 0, lg_own, -jnp.inf))
        probs = jax.nn.softmax(jnp.concatenate(logits, axis=-1), axis=-1).astype(v.dtype)
        out = jnp.einsum('bhqk,bhkd->bhqd', probs[..., k_sel * MOBA_BLOCK:], v_own)
        for s in range(k_sel):
            v_s = vb[b_ix, h_ix, sel[..., s]]
            out = out + jnp.einsum('bhqk,bhqkd->bhqd',
                                   probs[..., s * MOBA_BLOCK:(s + 1) * MOBA_BLOCK], v_s)
        return out

    out = lax.map(one_chunk, (jnp.arange(n_chunks), q_chunks))
    out = out.transpose(1, 2, 0, 3, 4).reshape(b_sz, n_h, l_pad, dh)
    return out[:, :, :seq]


def s5_branch(u, lam_re, lam_im, log_dt, b_re, b_im, c_re, c_im, d_skip, w_glu, b_glu):
    b_sz, seq, _ = u.shape
    f32 = jnp.float32
    uf = u.astype(f32).reshape(b_sz, seq, SSM_GROUPS, SSM_GROUP_CH)
    lam = lax.complex(lam_re.astype(f32), lam_im.astype(f32))
    dt = jnp.exp(log_dt.astype(f32))[:, None]
    lam_bar = jnp.exp(lam * dt)
    b_bar = ((lam_bar - 1.0) / lam)[..., None] * lax.complex(b_re.astype(f32), b_im.astype(f32))
    bu = lax.complex(jnp.einsum('blgh,gph->blgp', uf, b_bar.real),
                     jnp.einsum('blgh,gph->blgp', uf, b_bar.imag))
    a = jnp.broadcast_to(lam_bar, bu.shape)

    def combine(e1, e2):
        a1, s1 = e1
        a2, s2 = e2
        return a2 * a1, a2 * s1 + s2

    _, states = lax.associative_scan(combine, (a, bu), axis=1)
    y = (jnp.einsum('blgp,ghp->blgh', states.real, c_re.astype(f32))
         - jnp.einsum('blgp,ghp->blgh', states.imag, c_im.astype(f32))
         + d_skip.astype(f32) * uf)
    y = jax.nn.gelu(y.reshape(b_sz, seq, SSM_WIDTH))
    y = y * jax.nn.sigmoid(y @ w_glu.astype(f32) + b_glu.astype(f32))
    return y.astype(u.dtype)


def causal_dwconv(h, w, b):
    seq = h.shape[1]
    hp = jnp.pad(h, ((0, 0), (CONV_WIDTH - 1, 0), (0, 0)))
    out = hp[:, CONV_WIDTH - 1:] * w[CONV_WIDTH - 1] + b
    for j in range(CONV_WIDTH - 1):
        out = out + hp[:, j:j + seq] * w[j]
    return out


def setup_inputs(seed: int = 0) -> dict:
    key = jax.random.key(seed)
    ks = jax.random.split(key, 24)
    nrm = jax.random.normal
    f32 = jnp.float32
    lam_im0 = jnp.pi * jnp.arange(SSM_STATE, dtype=f32)
    return {
        "x": nrm(ks[0], (BATCH, SEQ, D_MODEL), f32),
        "norm_mix": 1.0 + 0.02 * nrm(ks[1], (DEPTH, D_MODEL), f32),
        "w_in": nrm(ks[2], (DEPTH, D_MODEL, IN_COLS), f32) * D_MODEL ** -0.5,
        "rel_bias_table": 0.2 * nrm(ks[3], (NUM_BUCKETS, ATTN_HEADS), f32),
        "ssm_lam_re": -0.5 + 0.01 * nrm(ks[4], (DEPTH, SSM_GROUPS, SSM_STATE), f32),
        "ssm_lam_im": lam_im0 + 0.01 * nrm(ks[5], (DEPTH, SSM_GROUPS, SSM_STATE), f32),
        "ssm_log_dt": jax.random.uniform(ks[6], (DEPTH, SSM_GROUPS), f32,
                                         math.log(DT_MIN), math.log(DT_MAX)),
        "ssm_b_re": nrm(ks[7], (DEPTH, SSM_GROUPS, SSM_STATE, SSM_GROUP_CH), f32) * (2 * SSM_GROUP_CH) ** -0.5,
        "ssm_b_im": nrm(ks[8], (DEPTH, SSM_GROUPS, SSM_STATE, SSM_GROUP_CH), f32) * (2 * SSM_GROUP_CH) ** -0.5,
        "ssm_c_re": nrm(ks[9], (DEPTH, SSM_GROUPS, SSM_GROUP_CH, SSM_STATE), f32) * (2 * SSM_STATE) ** -0.5,
        "ssm_c_im": nrm(ks[10], (DEPTH, SSM_GROUPS, SSM_GROUP_CH, SSM_STATE), f32) * (2 * SSM_STATE) ** -0.5,
        "ssm_d": nrm(ks[11], (DEPTH, SSM_GROUPS, SSM_GROUP_CH), f32),
        "ssm_w_glu": nrm(ks[12], (DEPTH, SSM_WIDTH, SSM_WIDTH), f32) * SSM_WIDTH ** -0.5,
        "ssm_b_glu": 0.01 * nrm(ks[13], (DEPTH, SSM_WIDTH), f32),
        "norm_attn_out": 1.0 + 0.02 * nrm(ks[14], (DEPTH, ATTN_WIDTH), f32),
        "norm_ssm_out": 1.0 + 0.02 * nrm(ks[15], (DEPTH, SSM_WIDTH), f32),
        "w_out": nrm(ks[16], (DEPTH, MIX_WIDTH, D_MODEL), f32) * MIX_WIDTH ** -0.5,
        "norm_ffn": 1.0 + 0.02 * nrm(ks[17], (DEPTH, D_MODEL), f32),
        "w_ffn_up": nrm(ks[18], (DEPTH, D_MODEL, 2 * D_FF), f32) * D_MODEL ** -0.5,
        "ffn_conv_w": nrm(ks[19], (DEPTH, CONV_WIDTH, 2 * D_FF), f32) * CONV_WIDTH ** -0.5,
        "ffn_conv_b": 0.01 * nrm(ks[20], (DEPTH, 2 * D_FF), f32),
        "w_ffn_down": nrm(ks[21], (DEPTH, D_FF, D_MODEL), f32) * D_FF ** -0.5,
        "norm_final": 1.0 + 0.02 * nrm(ks[22], (D_MODEL,), f32),
    }


def reference(x, norm_mix, w_in, rel_bias_table, ssm_lam_re, ssm_lam_im, ssm_log_dt,
              ssm_b_re, ssm_b_im, ssm_c_re, ssm_c_im, ssm_d, ssm_w_glu, ssm_b_glu,
              norm_attn_out, norm_ssm_out, w_out, norm_ffn, w_ffn_up, ffn_conv_w,
              ffn_conv_b, w_ffn_down, norm_final):
    b_sz, seq, _ = x.shape
    for l in range(DEPTH):
        h = rms_norm(x, norm_mix[l])
        proj = h @ w_in[l]
        q, k, v, u = jnp.split(proj, [ATTN_WIDTH, 2 * ATTN_WIDTH, 3 * ATTN_WIDTH], axis=-1)
        to_heads = lambda t: t.reshape(b_sz, seq, ATTN_HEADS, HEAD_DIM).transpose(0, 2, 1, 3)
        attn = moba_attention(to_heads(q), to_heads(k), to_heads(v), rel_bias_table)
        attn = attn.transpose(0, 2, 1, 3).reshape(b_sz, seq, ATTN_WIDTH)
        ssm = s5_branch(u, ssm_lam_re[l], ssm_lam_im[l], ssm_log_dt[l], ssm_b_re[l], ssm_b_im[l],
                        ssm_c_re[l], ssm_c_im[l], ssm_d[l], ssm_w_glu[l], ssm_b_glu[l])
        mixed = jnp.concatenate([rms_norm(attn, norm_attn_out[l]),
                                 rms_norm(ssm, norm_ssm_out[l])], axis=-1)
        x = x + mixed @ w_out[l]
        h = rms_norm(x, norm_ffn[l])
        up = causal_dwconv(h @ w_ffn_up[l], ffn_conv_w[l], ffn_conv_b[l])
        gate, val = jnp.split(up, 2, axis=-1)
        x = x + (jax.nn.silu(gate) * val) @ w_ffn_down[l]
    return rms_norm(x, norm_final)
```

```python
import functools
import math

import jax
import jax.numpy as jnp
from jax import lax
from jax.experimental import pallas as pl
from jax.experimental.pallas import tpu as pltpu

F32 = jnp.float32
BF16 = jnp.bfloat16

ATTN_HEADS = 16
HEAD_DIM = 64
SSM_GROUP_CH = 16
SSM_GROUPS = 64
SSM_STATE = 64
MOBA_BLOCK = 256
MOBA_TOP_K = 3
NUM_BUCKETS = 32
MAX_DISTANCE = 128
CONV_WIDTH = 3
RMS_EPS = 1e-6

SSM_CHUNK = 16
CONV_HALO = 16
VMEM_LIMIT = 52 * 1024 * 1024

NEG_INF = float("-inf")


def _rms_rows(x, g):
    ms = jnp.mean(x * x, axis=-1, keepdims=True)
    return x * lax.rsqrt(ms + RMS_EPS) * g


def _in_proj_nn_kernel(x_ref, g_ref, w_ref, o_ref, h_ref):
    @pl.when(pl.program_id(1) == 0)
    def _():
        h_ref[...] = _rms_rows(x_ref[...], g_ref[...]).astype(BF16)

    o_ref[...] = jnp.dot(h_ref[...], w_ref[...], preferred_element_type=F32).astype(o_ref.dtype)


def _in_proj_nt_kernel(x_ref, g_ref, wt_ref, o_ref, h_ref):
    @pl.when(pl.program_id(1) == 0)
    def _():
        h_ref[...] = _rms_rows(x_ref[...], g_ref[...]).astype(BF16)

    res = lax.dot_general(wt_ref[...], h_ref[...], (((1,), (1,)), ((), ())),
                          preferred_element_type=F32).astype(o_ref.dtype)
    for c in range(o_ref.shape[0]):
        o_ref[c] = res[:, c * MOBA_BLOCK:(c + 1) * MOBA_BLOCK]


def _in_proj(x2, g, w_nn, wt_nt, tm=1024, tn=512):
    t, d = x2.shape
    n_nn = w_nn.shape[1]
    n_nt = wt_nt.shape[0]
    g2 = g.reshape(1, d)
    params = pltpu.CompilerParams(dimension_semantics=("parallel", "arbitrary"),
                                  vmem_limit_bytes=VMEM_LIMIT)
    nat = pl.pallas_call(
        _in_proj_nn_kernel,
        grid=(t // tm, n_nn // tn),
        in_specs=[pl.BlockSpec((tm, d), lambda i, j: (i, 0)),
                  pl.BlockSpec((1, d), lambda i, j: (0, 0)),
                  pl.BlockSpec((d, tn), lambda i, j: (0, j))],
        out_specs=pl.BlockSpec((tm, tn), lambda i, j: (i, j)),
        out_shape=jax.ShapeDtypeStruct((t, n_nn), BF16),
        scratch_shapes=[pltpu.VMEM((tm, d), BF16)],
        compiler_params=params,
    )(x2, g2, w_nn)
    slabs = tm // MOBA_BLOCK
    tr = pl.pallas_call(
        _in_proj_nt_kernel,
        grid=(t // tm, n_nt // tn),
        in_specs=[pl.BlockSpec((tm, d), lambda i, j: (i, 0)),
                  pl.BlockSpec((1, d), lambda i, j: (0, 0)),
                  pl.BlockSpec((tn, d), lambda i, j: (j, 0))],
        out_specs=pl.BlockSpec((slabs, tn, MOBA_BLOCK), lambda i, j: (i, j, 0)),
        out_shape=jax.ShapeDtypeStruct((t // MOBA_BLOCK, n_nt, MOBA_BLOCK), BF16),
        scratch_shapes=[pltpu.VMEM((tm, d), BF16)],
        compiler_params=params,
    )(x2, g2, wt_nt)
    return nat, tr


def _t5_bucket(dist):
    dist = jnp.maximum(dist, 0)
    max_exact = NUM_BUCKETS // 2
    log_ratio = jnp.log(jnp.maximum(dist, max_exact).astype(F32) / max_exact)
    large = max_exact + (log_ratio / math.log(MAX_DISTANCE / max_exact)
                         * (NUM_BUCKETS - max_exact)).astype(jnp.int32)
    large = jnp.minimum(large, NUM_BUCKETS - 1)
    return jnp.where(dist < max_exact, dist, large)


def _bias_tiles_kernel(tab_ref, o_ref):
    h = pl.program_id(0)
    kk = lax.broadcasted_iota(jnp.int32, (MOBA_BLOCK, MOBA_BLOCK), 0)
    qq = lax.broadcasted_iota(jnp.int32, (MOBA_BLOCK, MOBA_BLOCK), 1)
    for which, off in ((0, MOBA_BLOCK), (1, 0)):
        rel = qq - kk + off
        bucket = _t5_bucket(rel)
        acc = jnp.zeros((MOBA_BLOCK, MOBA_BLOCK), F32)
        for b in range(NUM_BUCKETS):
            acc = jnp.where(bucket == b, tab_ref[b, h], acc)
        if which == 1:
            acc = jnp.where(rel >= 0, acc, NEG_INF)
        o_ref[0, which] = acc


def _bias_tiles(table):
    return pl.pallas_call(
        _bias_tiles_kernel,
        grid=(ATTN_HEADS,),
        in_specs=[pl.BlockSpec(memory_space=pltpu.SMEM)],
        out_specs=pl.BlockSpec((1, 2, MOBA_BLOCK, MOBA_BLOCK), lambda h: (h, 0, 0, 0)),
        out_shape=jax.ShapeDtypeStruct((ATTN_HEADS, 2, MOBA_BLOCK, MOBA_BLOCK), F32),
    )(table)


def _moba_kernel(tab_ref, q_ref, k_ref, v_ref, bias_ref, o_ref,
                 kmf_ref, m_ref, l_ref, acc_ref, rb_ref, ot_ref, *, n_blocks):
    hp = pl.program_id(1)
    qi = pl.program_id(2)
    blk = MOBA_BLOCK

    @pl.when(qi == 0)
    def _():
        for j in range(n_blocks):
            kb = k_ref[j].astype(F32)
            kmf_ref[pl.ds(j, 1), :] = jnp.sum(kb, axis=0, keepdims=True) * (1.0 / blk)

    kmf = kmf_ref[...]
    km_hi = kmf.astype(BF16)
    km_lo = (kmf - km_hi.astype(F32)).astype(BF16)

    q2 = q_ref[0]
    q_zero = jnp.zeros((HEAD_DIM, blk), q2.dtype)
    blk_idx = lax.broadcasted_iota(jnp.int32, (n_blocks, blk), 0)

    for hh in range(2):
        vrows = slice(hh * HEAD_DIM, (hh + 1) * HEAD_DIM)
        qm = jnp.concatenate([q2[vrows], q_zero] if hh == 0 else [q_zero, q2[vrows]], axis=0)

        gate = (jnp.dot(km_hi, qm, preferred_element_type=F32)
                + jnp.dot(km_lo, qm, preferred_element_type=F32))
        valid = blk_idx < qi
        g = jnp.where(valid, gate, NEG_INF)
        rank = jnp.zeros(g.shape, F32)
        for j in range(n_blocks):
            gj = g[j:j + 1, :]
            beats = jnp.where(gj > g, 1.0, jnp.where((gj == g) & (blk_idx > j), 1.0, 0.0))
            rank = rank + beats
        sel = valid & (rank < MOBA_TOP_K)
        c_far = tab_ref[NUM_BUCKETS - 1, 2 * hp + hh]
        rb_ref[...] = jnp.where(sel, jnp.where(blk_idx == qi - 1, 0.0, c_far), NEG_INF)

        def update(s, vt):
            m_old = m_ref[...]
            m_new = jnp.maximum(m_old, jnp.max(s, axis=0, keepdims=True))
            alpha = jnp.exp(m_old - m_new)
            p = jnp.exp(s - m_new)
            l_ref[...] = alpha * l_ref[...] + jnp.sum(p, axis=0, keepdims=True)
            acc_ref[...] = alpha * acc_ref[...] + jnp.dot(vt, p.astype(BF16),
                                                          preferred_element_type=F32)
            m_ref[...] = m_new

        s = jnp.dot(k_ref[qi], qm, preferred_element_type=F32) + bias_ref[hh, 1]
        m0 = jnp.max(s, axis=0, keepdims=True)
        p = jnp.exp(s - m0)
        m_ref[...] = m0
        l_ref[...] = jnp.sum(p, axis=0, keepdims=True)
        acc_ref[...] = jnp.dot(v_ref[qi, vrows, :], p.astype(BF16), preferred_element_type=F32)

        @pl.when(qi >= 1)
        def _():
            j = qi - 1
            s = (jnp.dot(k_ref[j], qm, preferred_element_type=F32)
                 + rb_ref[pl.ds(j, 1), :] + bias_ref[hh, 0])
            update(s, v_ref[j, vrows, :])

        def far_body(j, carry):
            s = jnp.dot(k_ref[j], qm, preferred_element_type=F32) + rb_ref[pl.ds(j, 1), :]
            update(s, v_ref[j, vrows, :])
            return carry

        lax.fori_loop(0, jnp.maximum(qi - 1, 0), far_body, 0)

        ot_ref[vrows, :] = acc_ref[...] * (1.0 / l_ref[...])

    o_ref[...] = ot_ref[...].T.astype(o_ref.dtype)


def _moba_attention(table, qv_t, ku3, bias_tiles, batch, seq):
    n_blocks = seq // MOBA_BLOCK
    n_pairs = ATTN_HEADS // 2
    pair_w = 2 * HEAD_DIM
    kernel = functools.partial(_moba_kernel, n_blocks=n_blocks)
    return pl.pallas_call(
        kernel,
        grid=(batch, n_pairs, n_blocks),
        in_specs=[
            pl.BlockSpec(memory_space=pltpu.SMEM),
            pl.BlockSpec((1, pair_w, MOBA_BLOCK), lambda b, hp, qi: (b * n_blocks + qi, hp, 0)),
            pl.BlockSpec((n_blocks, MOBA_BLOCK, pair_w), lambda b, hp, qi: (b, 0, hp)),
            pl.BlockSpec((n_blocks, pair_w, MOBA_BLOCK), lambda b, hp, qi: (b, n_pairs + hp, 0)),
            pl.BlockSpec((2, 2, MOBA_BLOCK, MOBA_BLOCK), lambda b, hp, qi: (hp, 0, 0, 0)),
        ],
        out_specs=pl.BlockSpec((MOBA_BLOCK, pair_w), lambda b, hp, qi: (b * n_blocks + qi, hp)),
        out_shape=jax.ShapeDtypeStruct((batch * seq, ATTN_HEADS * HEAD_DIM), BF16),
        scratch_shapes=[
            pltpu.VMEM((n_blocks, pair_w), F32),
            pltpu.VMEM((1, MOBA_BLOCK), F32),
            pltpu.VMEM((1, MOBA_BLOCK), F32),
            pltpu.VMEM((HEAD_DIM, MOBA_BLOCK), F32),
            pltpu.VMEM((n_blocks, MOBA_BLOCK), F32),
            pltpu.VMEM((pair_w, MOBA_BLOCK), F32),
        ],
        compiler_params=pltpu.CompilerParams(
            dimension_semantics=("parallel", "parallel", "arbitrary")),
    )(table, qv_t, ku3, qv_t, bias_tiles)


def _hdot(a, b):
    return jnp.dot(a, b, preferred_element_type=F32, precision=lax.Precision.HIGHEST)


def _hdot_nt(a, b):
    return lax.dot_general(a, b, (((1,), (1,)), ((), ())), preferred_element_type=F32,
                           precision=lax.Precision.HIGHEST)


def _s5_tables_kernel(logdt_ref, lre_ref, lim_ref, bre_ref, bim_ref, cre_ref, cim_ref, d_ref,
                      tg_ref, bc_ref, are_ref, aim_ref):
    lc, ch, p = SSM_CHUNK, SSM_GROUP_CH, SSM_STATE
    w = lc * ch
    lam_re = lre_ref[0]
    lam_im = lim_ref[0]
    dt = jnp.exp(logdt_ref[0])
    lr = lam_re * dt
    li = lam_im * dt

    tau = lax.broadcasted_iota(jnp.int32, (2 * lc, p), 0).astype(F32)
    mag = jnp.exp(lr * tau)
    cs = jnp.cos(li * tau)
    sn = jnp.sin(li * tau)
    pos_re, pos_im = mag * cs, mag * sn
    inv = jnp.exp(-lr * tau[:lc])
    neg_re, neg_im = inv * cs[:lc], -inv * sn[:lc]

    lam1_re, lam1_im = pos_re[1:2], pos_im[1:2]
    lamk_re, lamk_im = pos_re[lc - 1:lc], pos_im[lc - 1:lc]
    are_ref[0] = pos_re[lc:lc + 1]
    aim_ref[0] = pos_im[lc:lc + 1]

    num_re, num_im = lam1_re - 1.0, lam1_im
    den = lam_re * lam_re + lam_im * lam_im
    coef_re = (num_re * lam_re + num_im * lam_im) / den
    coef_im = (num_im * lam_re - num_re * lam_im) / den
    bt_re, bt_im = bre_ref[0], bim_ref[0]
    bb_re = coef_re * bt_re - coef_im * bt_im
    bb_im = coef_re * bt_im + coef_im * bt_re

    r_i = lax.broadcasted_iota(jnp.int32, (w, lc), 0)
    c_i = lax.broadcasted_iota(jnp.int32, (w, lc), 1)
    rep = jnp.where(r_i // ch == c_i, 1.0, 0.0).astype(F32)
    til = jnp.where(r_i % ch == c_i, 1.0, 0.0).astype(F32)

    bbt_re, bbt_im = _hdot(til, bb_re), _hdot(til, bb_im)
    ngx_re, ngx_im = _hdot(rep, neg_re), _hdot(rep, neg_im)
    bneg_re = bbt_re * ngx_re - bbt_im * ngx_im
    bneg_im = bbt_re * ngx_im + bbt_im * ngx_re

    ct_re, ct_im = _hdot(til, cre_ref[0]), _hdot(til, cim_ref[0])
    psx_re, psx_im = _hdot(rep, pos_re[:lc]), _hdot(rep, pos_im[:lc])
    cpos_re = ct_re * psx_re - ct_im * psx_im
    cpos_im = ct_re * psx_im + ct_im * psx_re

    raw = _hdot_nt(bneg_re, cpos_re) - _hdot_nt(bneg_im, cpos_im)
    rr = lax.broadcasted_iota(jnp.int32, (w, w), 0)
    cc = lax.broadcasted_iota(jnp.int32, (w, w), 1)
    e_r = lax.broadcasted_iota(jnp.int32, (ch, w), 0)
    e_c = lax.broadcasted_iota(jnp.int32, (ch, w), 1)
    d_lanes = _hdot(jnp.broadcast_to(d_ref[0], (8, ch)),
                    jnp.where(e_c % ch == e_r, 1.0, 0.0).astype(F32))[0:1]
    tg = jnp.where(rr // ch <= cc // ch, raw, 0.0) + jnp.where(rr == cc, d_lanes, 0.0)
    tg_ref[0] = tg.astype(tg_ref.dtype)

    bc_ref[0, 0] = (bneg_re * lamk_re - bneg_im * lamk_im).astype(bc_ref.dtype)
    bc_ref[0, 1] = (bneg_re * lamk_im + bneg_im * lamk_re).astype(bc_ref.dtype)
    cp_re = cpos_re * lam1_re - cpos_im * lam1_im
    cp_im = cpos_re * lam1_im + cpos_im * lam1_re
    bc_ref[0, 2] = cp_re.astype(bc_ref.dtype)
    bc_ref[0, 3] = (-cp_im).astype(bc_ref.dtype)


def _s5_tables(log_dt, lam_re, lam_im, b_re, b_im, c_re, c_im, d_skip):
    g, p, ch = SSM_GROUPS, SSM_STATE, SSM_GROUP_CH
    w = SSM_CHUNK * ch
    row = lambda a, n: a.reshape(g, 1, n)
    spec3 = lambda s1, s2: pl.BlockSpec((1, s1, s2), lambda i: (i, 0, 0))
    return pl.pallas_call(
        _s5_tables_kernel,
        grid=(g,),
        in_specs=[spec3(1, 1), spec3(1, p), spec3(1, p), spec3(ch, p), spec3(ch, p),
                  spec3(ch, p), spec3(ch, p), spec3(1, ch)],
        out_specs=[spec3(w, w),
                   pl.BlockSpec((1, 4, w, p), lambda i: (i, 0, 0, 0)),
                   spec3(1, p), spec3(1, p)],
        out_shape=[jax.ShapeDtypeStruct((g, w, w), BF16),
                   jax.ShapeDtypeStruct((g, 4, w, p), BF16),
                   jax.ShapeDtypeStruct((g, 1, p), F32),
                   jax.ShapeDtypeStruct((g, 1, p), F32)],
    )(row(log_dt, 1), row(lam_re, p), row(lam_im, p),
      jnp.swapaxes(b_re, 1, 2), jnp.swapaxes(b_im, 1, 2), c_re, c_im, row(d_skip, ch))


def _s5_scan_kernel(u_ref, tg_ref, bc_ref, are_ref, aim_ref, y_ref,
                    sre_ref, sim_ref, xre_ref, xim_ref, *, n_batch):
    gb, n_rows, _ = u_ref.shape
    n_chunks = n_rows // n_batch

    for k in range(gb):
        u = u_ref[k]
        sre_ref[pl.ds(k * n_rows, n_rows), :] = jnp.dot(u, bc_ref[k, 0], preferred_element_type=F32)
        sim_ref[pl.ds(k * n_rows, n_rows), :] = jnp.dot(u, bc_ref[k, 1], preferred_element_type=F32)

    a_re = are_ref[...]
    a_im = aim_ref[...]

    def step(c, carry):
        new = []
        for b in range(n_batch):
            x_re, x_im = carry[2 * b], carry[2 * b + 1]
            rows = pl.ds(b * n_chunks + c, gb, stride=n_rows)
            xre_ref[rows, :] = x_re
            xim_ref[rows, :] = x_im
            new.append(a_re * x_re - a_im * x_im + sre_ref[rows, :])
            new.append(a_re * x_im + a_im * x_re + sim_ref[rows, :])
        return tuple(new)

    zero = jnp.zeros(a_re.shape, F32)
    lax.fori_loop(0, n_chunks, step, (zero,) * (2 * n_batch))

    nt = (((1,), (1,)), ((), ()))
    for k in range(gb):
        rows = pl.ds(k * n_rows, n_rows)
        y = jnp.dot(u_ref[k], tg_ref[k], preferred_element_type=F32)
        y = y + lax.dot_general(xre_ref[rows, :].astype(BF16), bc_ref[k, 2], nt,
                                preferred_element_type=F32)
        y = y + lax.dot_general(xim_ref[rows, :].astype(BF16), bc_ref[k, 3], nt,
                                preferred_element_type=F32)
        y_ref[k] = y.astype(y_ref.dtype)


def _s5_scan(u3, tg, bc, a_re, a_im, n_batch, gb=8):
    g, n_rows, w = u3.shape
    p = SSM_STATE
    kernel = functools.partial(_s5_scan_kernel, n_batch=n_batch)
    return pl.pallas_call(
        kernel,
        grid=(g // gb,),
        in_specs=[pl.BlockSpec((gb, n_rows, w), lambda i: (i, 0, 0)),
                  pl.BlockSpec((gb, w, w), lambda i: (i, 0, 0)),
                  pl.BlockSpec((gb, 4, w, p), lambda i: (i, 0, 0, 0)),
                  pl.BlockSpec((gb, p), lambda i: (i, 0)),
                  pl.BlockSpec((gb, p), lambda i: (i, 0))],
        out_specs=pl.BlockSpec((gb, n_rows, w), lambda i: (i, 0, 0)),
        out_shape=jax.ShapeDtypeStruct((g, n_rows, w), F32),
        scratch_shapes=[pltpu.VMEM((gb * n_rows, p), F32) for _ in range(4)],
        compiler_params=pltpu.CompilerParams(dimension_semantics=("parallel",)),
    )(u3, tg, bc, a_re.reshape(g, p), a_im.reshape(g, p))


def _out_proj_kernel(x_ref, a_ref, y_ref, wglu_ref, bglu_ref, ga_ref, gs_ref, wa_ref, ws_ref, o_ref):
    z = jax.nn.gelu(y_ref[...])
    gl = jnp.dot(z.astype(BF16), wglu_ref[...], preferred_element_type=F32) + bglu_ref[...]
    s = z * jax.nn.sigmoid(gl)
    a_n = _rms_rows(a_ref[...].astype(F32), ga_ref[...]).astype(BF16)
    s_n = _rms_rows(s, gs_ref[...]).astype(BF16)
    mix = (jnp.dot(a_n, wa_ref[...], preferred_element_type=F32)
           + jnp.dot(s_n, ws_ref[...], preferred_element_type=F32))
    o_ref[...] = x_ref[...] + mix


def _out_proj(x2, attn, y, w_glu, b_glu, g_a, g_s, w_out, tm=256):
    t, d = x2.shape
    wa = attn.shape[1]
    ws = y.shape[1]
    row = lambda i: (i, 0)
    fixed = lambda i: (0, 0)
    once = pl.Buffered(1)
    return pl.pallas_call(
        _out_proj_kernel,
        grid=(t // tm,),
        in_specs=[pl.BlockSpec((tm, d), row),
                  pl.BlockSpec((tm, wa), row),
                  pl.BlockSpec((tm, ws), row),
                  pl.BlockSpec((ws, ws), fixed, pipeline_mode=once),
                  pl.BlockSpec((1, ws), fixed),
                  pl.BlockSpec((1, wa), fixed),
                  pl.BlockSpec((1, ws), fixed),
                  pl.BlockSpec((wa, d), fixed, pipeline_mode=once),
                  pl.BlockSpec((ws, d), fixed, pipeline_mode=once)],
        out_specs=pl.BlockSpec((tm, d), row),
        out_shape=jax.ShapeDtypeStruct((t, d), F32),
        compiler_params=pltpu.CompilerParams(dimension_semantics=("parallel",),
                                             vmem_limit_bytes=VMEM_LIMIT),
    )(x2, attn, y, w_glu, b_glu.reshape(1, ws), g_a.reshape(1, wa), g_s.reshape(1, ws),
      w_out[:wa], w_out[wa:])


def _ffn_up_kernel(x_ref, halo_ref, g_ref, wg_ref, wv_ref, cwg_ref, cwv_ref, cbg_ref, cbv_ref,
                   o_ref, h_ref, *, tiles_per_seq):
    tm = x_ref.shape[0]

    @pl.when(pl.program_id(1) == 0)
    def _():
        keep = jnp.where(pl.program_id(0) % tiles_per_seq == 0, 0.0, 1.0)
        h_ref[pl.ds(0, CONV_HALO), :] = (_rms_rows(halo_ref[...], g_ref[...]) * keep).astype(BF16)
        h_ref[pl.ds(CONV_HALO, tm), :] = _rms_rows(x_ref[...], g_ref[...]).astype(BF16)

    h = h_ref[...]

    def conv(w_ref, cw_ref, cb_ref):
        up = jnp.dot(h, w_ref[...], preferred_element_type=F32)
        cw = cw_ref[...]
        out = up[CONV_HALO:] * cw[CONV_WIDTH - 1:CONV_WIDTH] + cb_ref[...]
        for j in range(CONV_WIDTH - 1):
            lag = CONV_WIDTH - 1 - j
            out = out + up[CONV_HALO - lag:CONV_HALO - lag + tm] * cw[j:j + 1]
        return out

    gate = conv(wg_ref, cwg_ref, cbg_ref)
    val = conv(wv_ref, cwv_ref, cbv_ref)
    o_ref[...] = (jax.nn.silu(gate) * val).astype(o_ref.dtype)


def _ffn_up(x1, g, w_up, conv_w, conv_b, seq, tm=1024, tn=512):
    t, d = x1.shape
    f = w_up.shape[1] // 2
    nf = f // tn
    tiles_per_seq = seq // tm
    halo_blocks = tm // CONV_HALO
    kernel = functools.partial(_ffn_up_kernel, tiles_per_seq=tiles_per_seq)
    cb = conv_b.reshape(1, 2 * f)
    return pl.pallas_call(
        kernel,
        grid=(t // tm, nf),
        in_specs=[pl.BlockSpec((tm, d), lambda i, j: (i, 0)),
                  pl.BlockSpec((CONV_HALO, d), lambda i, j: (jnp.maximum(i * halo_blocks - 1, 0), 0)),
                  pl.BlockSpec((1, d), lambda i, j: (0, 0)),
                  pl.BlockSpec((d, tn), lambda i, j: (0, j)),
                  pl.BlockSpec((d, tn), lambda i, j: (0, nf + j)),
                  pl.BlockSpec((CONV_WIDTH, tn), lambda i, j: (0, j)),
                  pl.BlockSpec((CONV_WIDTH, tn), lambda i, j: (0, nf + j)),
                  pl.BlockSpec((1, tn), lambda i, j: (0, j)),
                  pl.BlockSpec((1, tn), lambda i, j: (0, nf + j))],
        out_specs=pl.BlockSpec((tm, tn), lambda i, j: (i, j)),
        out_shape=jax.ShapeDtypeStruct((t, f), BF16),
        scratch_shapes=[pltpu.VMEM((CONV_HALO + tm, d), BF16)],
        compiler_params=pltpu.CompilerParams(dimension_semantics=("parallel", "arbitrary"),
                                             vmem_limit_bytes=VMEM_LIMIT),
    )(x1, x1, g.reshape(1, d), w_up, w_up, conv_w, conv_w, cb, cb)


def _ffn_down_kernel(a_ref, w_ref, x_ref, g_ref, o_ref, acc_ref, *, final_norm):
    k = pl.program_id(1)

    @pl.when(k == 0)
    def _():
        acc_ref[...] = x_ref[...]

    acc_ref[...] += jnp.dot(a_ref[...], w_ref[...], preferred_element_type=F32)

    @pl.when(k == pl.num_programs(1) - 1)
    def _():
        o_ref[...] = _rms_rows(acc_ref[...], g_ref[...]) if final_norm else acc_ref[...]


def _ffn_down(act, w_down, x1, g, final_norm, tm=512, tk=512):
    t, f = act.shape
    d = w_down.shape[1]
    return pl.pallas_call(
        functools.partial(_ffn_down_kernel, final_norm=final_norm),
        grid=(t // tm, f // tk),
        in_specs=[pl.BlockSpec((tm, tk), lambda i, k: (i, k)),
                  pl.BlockSpec((tk, d), lambda i, k: (k, 0)),
                  pl.BlockSpec((tm, d), lambda i, k: (i, 0)),
                  pl.BlockSpec((1, d), lambda i, k: (0, 0))],
        out_specs=pl.BlockSpec((tm, d), lambda i, k: (i, 0)),
        out_shape=jax.ShapeDtypeStruct((t, d), F32),
        scratch_shapes=[pltpu.VMEM((tm, d), F32)],
        compiler_params=pltpu.CompilerParams(dimension_semantics=("parallel", "arbitrary"),
                                             vmem_limit_bytes=VMEM_LIMIT),
    )(act, w_down, x1, g.reshape(1, d))


def kernel(x, norm_mix, w_in, rel_bias_table, ssm_lam_re, ssm_lam_im, ssm_log_dt, ssm_b_re, ssm_b_im, ssm_c_re, ssm_c_im, ssm_d, ssm_w_glu, ssm_b_glu, norm_attn_out, norm_ssm_out, w_out, norm_ffn, w_ffn_up, ffn_conv_w, ffn_conv_b, w_ffn_down, norm_final):
    batch, seq, d_model = x.shape
    depth = w_in.shape[0]
    aw = ATTN_HEADS * HEAD_DIM
    sw = SSM_GROUPS * SSM_GROUP_CH
    t = batch * seq
    n_blocks = seq // MOBA_BLOCK
    assert seq % MOBA_BLOCK == 0 and seq % SSM_CHUNK == 0

    bias_tiles = _bias_tiles(rel_bias_table)
    x2 = x.reshape(t, d_model)
    for l in range(depth):
        w = w_in[l]
        w_ku = jnp.concatenate([w[:, aw:2 * aw], w[:, 3 * aw:]], axis=1).astype(BF16)
        wt_qv = jnp.concatenate([w[:, :aw] * (HEAD_DIM ** -0.5), w[:, 2 * aw:3 * aw]],
                                axis=1).T.astype(BF16)
        ku, qv_t = _in_proj(x2, norm_mix[l], w_ku, wt_qv)

        ku3 = ku.reshape(t // MOBA_BLOCK, MOBA_BLOCK, ku.shape[1])
        attn = _moba_attention(rel_bias_table, qv_t, ku3, bias_tiles, batch, seq)

        n_rows = t // SSM_CHUNK
        u3 = (ku[:, aw:].reshape(n_rows, SSM_CHUNK, SSM_GROUPS, SSM_GROUP_CH)
              .transpose(2, 0, 1, 3).reshape(SSM_GROUPS, n_rows, SSM_CHUNK * SSM_GROUP_CH))
        tg, bc, a_re, a_im = _s5_tables(ssm_log_dt[l], ssm_lam_re[l], ssm_lam_im[l], ssm_b_re[l],
                                        ssm_b_im[l], ssm_c_re[l], ssm_c_im[l], ssm_d[l])
        y3 = _s5_scan(u3, tg, bc, a_re, a_im, batch)
        y = (y3.reshape(SSM_GROUPS, n_rows, SSM_CHUNK, SSM_GROUP_CH)
             .transpose(1, 2, 0, 3).reshape(t, sw))

        x2 = _out_proj(x2, attn, y, ssm_w_glu[l].astype(BF16), ssm_b_glu[l], norm_attn_out[l],
                       norm_ssm_out[l], w_out[l].astype(BF16))

        act = _ffn_up(x2, norm_ffn[l], w_ffn_up[l].astype(BF16), ffn_conv_w[l], ffn_conv_b[l], seq)
        x2 = _ffn_down(act, w_ffn_down[l].astype(BF16), x2, norm_final, final_norm=(l == depth - 1))
    return x2.reshape(batch, seq, d_model)
```

```python
import functools
import math

import jax
import jax.numpy as jnp
from jax import lax
from jax.experimental import pallas as pl
from jax.experimental.pallas import tpu as pltpu

F32 = jnp.float32
BF16 = jnp.bfloat16

ATTN_HEADS = 16
HEAD_DIM = 64
SSM_GROUP_CH = 16
SSM_GROUPS = 64
SSM_STATE = 64
MOBA_BLOCK = 256
MOBA_TOP_K = 3
NUM_BUCKETS = 32
MAX_DISTANCE = 128
CONV_WIDTH = 3
RMS_EPS = 1e-6

SSM_CHUNK = 16
CONV_HALO = 16
VMEM_LIMIT = 52 * 1024 * 1024

NEG_INF = float("-inf")
LOG2E = math.log2(math.e)
V_AUG_ROWS = HEAD_DIM + 16


def _rms_rows(x, g):
    ms = jnp.mean(x * x, axis=-1, keepdims=True)
    return x * lax.rsqrt(ms + RMS_EPS) * g


def _in_proj_nn_kernel(x_ref, g_ref, w_ref, o_ref, h_ref):
    @pl.when(pl.program_id(1) == 0)
    def _():
        h_ref[...] = _rms_rows(x_ref[...], g_ref[...]).astype(BF16)

    o_ref[...] = jnp.dot(h_ref[...], w_ref[...], preferred_element_type=F32).astype(o_ref.dtype)


def _in_proj_nt_kernel(x_ref, g_ref, wt_ref, o_ref, h_ref):
    @pl.when(pl.program_id(1) == 0)
    def _():
        h_ref[...] = _rms_rows(x_ref[...], g_ref[...]).astype(BF16)

    res = lax.dot_general(wt_ref[...], h_ref[...], (((1,), (1,)), ((), ())),
                          preferred_element_type=F32).astype(o_ref.dtype)
    for c in range(o_ref.shape[0]):
        o_ref[c] = res[:, c * MOBA_BLOCK:(c + 1) * MOBA_BLOCK]


def _in_proj(x2, g, w_nn, wt_nt, tm=1024, tn=512):
    t, d = x2.shape
    n_nn = w_nn.shape[1]
    n_nt = wt_nt.shape[0]
    g2 = g.reshape(1, d)
    params = pltpu.CompilerParams(dimension_semantics=("parallel", "arbitrary"),
                                  vmem_limit_bytes=VMEM_LIMIT)
    nat = pl.pallas_call(
        _in_proj_nn_kernel,
        grid=(t // tm, n_nn // tn),
        in_specs=[pl.BlockSpec((tm, d), lambda i, j: (i, 0)),
                  pl.BlockSpec((1, d), lambda i, j: (0, 0)),
                  pl.BlockSpec((d, tn), lambda i, j: (0, j))],
        out_specs=pl.BlockSpec((tm, tn), lambda i, j: (i, j)),
        out_shape=jax.ShapeDtypeStruct((t, n_nn), BF16),
        scratch_shapes=[pltpu.VMEM((tm, d), BF16)],
        compiler_params=params,
    )(x2, g2, w_nn)
    slabs = tm // MOBA_BLOCK
    tr = pl.pallas_call(
        _in_proj_nt_kernel,
        grid=(t // tm, n_nt // tn),
        in_specs=[pl.BlockSpec((tm, d), lambda i, j: (i, 0)),
                  pl.BlockSpec((1, d), lambda i, j: (0, 0)),
                  pl.BlockSpec((tn, d), lambda i, j: (j, 0))],
        out_specs=pl.BlockSpec((slabs, tn, MOBA_BLOCK), lambda i, j: (i, j, 0)),
        out_shape=jax.ShapeDtypeStruct((t // MOBA_BLOCK, n_nt, MOBA_BLOCK), BF16),
        scratch_shapes=[pltpu.VMEM((tm, d), BF16)],
        compiler_params=params,
    )(x2, g2, wt_nt)
    return nat, tr


def _t5_bucket(dist):
    dist = jnp.maximum(dist, 0)
    max_exact = NUM_BUCKETS // 2
    log_ratio = jnp.log(jnp.maximum(dist, max_exact).astype(F32) / max_exact)
    large = max_exact + (log_ratio / math.log(MAX_DISTANCE / max_exact)
                         * (NUM_BUCKETS - max_exact)).astype(jnp.int32)
    large = jnp.minimum(large, NUM_BUCKETS - 1)
    return jnp.where(dist < max_exact, dist, large)


def _bias_tiles_kernel(tab_ref, o_ref):
    h = pl.program_id(0)
    c_far = tab_ref[NUM_BUCKETS - 1, h]
    kk = lax.broadcasted_iota(jnp.int32, (MOBA_BLOCK, MOBA_BLOCK), 0)
    qq = lax.broadcasted_iota(jnp.int32, (MOBA_BLOCK, MOBA_BLOCK), 1)
    for which, off in ((0, MOBA_BLOCK), (1, 0)):
        rel = qq - kk + off
        bucket = _t5_bucket(rel)
        acc = jnp.zeros((MOBA_BLOCK, MOBA_BLOCK), F32)
        for b in range(NUM_BUCKETS):
            acc = jnp.where(bucket == b, tab_ref[b, h], acc)
        acc = (acc - c_far) * LOG2E
        if which == 1:
            acc = jnp.where(rel >= 0, acc, NEG_INF)
        o_ref[0, which] = acc


def _bias_tiles(table):
    return pl.pallas_call(
        _bias_tiles_kernel,
        grid=(ATTN_HEADS,),
        in_specs=[pl.BlockSpec(memory_space=pltpu.SMEM)],
        out_specs=pl.BlockSpec((1, 2, MOBA_BLOCK, MOBA_BLOCK), lambda h: (h, 0, 0, 0)),
        out_shape=jax.ShapeDtypeStruct((ATTN_HEADS, 2, MOBA_BLOCK, MOBA_BLOCK), F32),
    )(table)


def _moba_kernel(q_ref, k_ref, v_ref, bias_ref, o_ref,
                 kmf_ref, vaug_ref, m_ref, acc_ref, rb_ref, ot_ref, *, n_blocks):
    qi = pl.program_id(2)
    blk = MOBA_BLOCK

    @pl.when(qi == 0)
    def _():
        ones = jnp.ones((V_AUG_ROWS - HEAD_DIM, blk), BF16)
        for j in range(n_blocks):
            kb = k_ref[j].astype(F32)
            kmf_ref[pl.ds(j, 1), :] = jnp.sum(kb, axis=0, keepdims=True) * (1.0 / blk)
            for hh in range(2):
                vaug_ref[j, hh, 0:HEAD_DIM, :] = v_ref[j, hh * HEAD_DIM:(hh + 1) * HEAD_DIM, :]
                vaug_ref[j, hh, HEAD_DIM:V_AUG_ROWS, :] = ones

    kmf = kmf_ref[...]
    km_hi = kmf.astype(BF16)
    km_lo = (kmf - km_hi.astype(F32)).astype(BF16)

    q2 = q_ref[0]
    q_zero = jnp.zeros((HEAD_DIM, blk), q2.dtype)
    blk_idx = lax.broadcasted_iota(jnp.int32, (n_blocks, blk), 0)

    qms = []
    for hh in range(2):
        vrows = slice(hh * HEAD_DIM, (hh + 1) * HEAD_DIM)
        qm = jnp.concatenate([q2[vrows], q_zero] if hh == 0 else [q_zero, q2[vrows]], axis=0)
        qms.append(qm)

        gate = (jnp.dot(km_hi, qm, preferred_element_type=F32)
                + jnp.dot(km_lo, qm, preferred_element_type=F32))
        valid = blk_idx < qi
        g = jnp.where(valid, gate, NEG_INF)
        rank = jnp.zeros(g.shape, F32)
        for j in range(n_blocks):
            gj = g[j:j + 1, :]
            beats = jnp.where(gj > g, 1.0, jnp.where((gj == g) & (blk_idx > j), 1.0, 0.0))
            rank = rank + beats
        sel = (valid & (rank < MOBA_TOP_K)) | (blk_idx == qi)
        rb_ref[hh] = jnp.where(sel, 0.0, NEG_INF)
        m_ref[hh] = jnp.full((1, blk), NEG_INF, F32)
        acc_ref[hh] = jnp.zeros((V_AUG_ROWS, blk), F32)

    def process(blocks):
        for hh in range(2):
            scores, masks = [], []
            for j, exists, tile in blocks:
                s = jnp.dot(k_ref[j], qms[hh], preferred_element_type=F32)
                if tile is not None:
                    s = s + bias_ref[hh, tile]
                mk = rb_ref[hh, pl.ds(j, 1), :]
                if exists is not None:
                    mk = jnp.where(exists, mk, NEG_INF)
                scores.append(s)
                masks.append(mk)
            m_old = m_ref[hh]
            m_new = m_old
            for s, mk in zip(scores, masks):
                m_new = jnp.maximum(m_new, jnp.max(s, axis=0, keepdims=True) + mk)
            acc = jnp.exp2(m_old - m_new) * acc_ref[hh]
            for (j, _, _), s, mk in zip(blocks, scores, masks):
                p = jnp.exp2(s - m_new).astype(BF16)
                pv = jnp.dot(vaug_ref[j, hh], p, preferred_element_type=F32)
                acc = acc + jnp.where(mk == 0.0, pv, 0.0)
            acc_ref[hh] = acc
            m_ref[hh] = m_new

    process([(qi, None, 1),
             (jnp.maximum(qi - 1, 0), qi >= 1, 0),
             (jnp.maximum(qi - 2, 0), qi >= 2, None)])

    def far_body(it, carry):
        ja = qi - 3 - 2 * it
        process([(ja, None, None), (jnp.maximum(ja - 1, 0), ja >= 1, None)])
        return carry

    lax.fori_loop(0, (jnp.maximum(qi - 2, 0) + 1) // 2, far_body, 0)

    for hh in range(2):
        acc = acc_ref[hh]
        ot_ref[hh * HEAD_DIM:(hh + 1) * HEAD_DIM, :] = (
            acc[0:HEAD_DIM] * (1.0 / acc[HEAD_DIM:HEAD_DIM + 1]))

    o_ref[...] = ot_ref[...].T.astype(o_ref.dtype)


def _moba_attention(qv_t, ku3, bias_tiles, batch, seq):
    n_blocks = seq // MOBA_BLOCK
    n_pairs = ATTN_HEADS // 2
    pair_w = 2 * HEAD_DIM
    kernel = functools.partial(_moba_kernel, n_blocks=n_blocks)
    return pl.pallas_call(
        kernel,
        grid=(batch, n_pairs, n_blocks),
        in_specs=[
            pl.BlockSpec((1, pair_w, MOBA_BLOCK), lambda b, hp, qi: (b * n_blocks + qi, hp, 0)),
            pl.BlockSpec((n_blocks, MOBA_BLOCK, pair_w), lambda b, hp, qi: (b, 0, hp)),
            pl.BlockSpec((n_blocks, pair_w, MOBA_BLOCK), lambda b, hp, qi: (b, n_pairs + hp, 0)),
            pl.BlockSpec((2, 2, MOBA_BLOCK, MOBA_BLOCK), lambda b, hp, qi: (hp, 0, 0, 0)),
        ],
        out_specs=pl.BlockSpec((MOBA_BLOCK, pair_w), lambda b, hp, qi: (b * n_blocks + qi, hp)),
        out_shape=jax.ShapeDtypeStruct((batch * seq, ATTN_HEADS * HEAD_DIM), BF16),
        scratch_shapes=[
            pltpu.VMEM((n_blocks, pair_w), F32),
            pltpu.VMEM((n_blocks, 2, V_AUG_ROWS, MOBA_BLOCK), BF16),
            pltpu.VMEM((2, 1, MOBA_BLOCK), F32),
            pltpu.VMEM((2, V_AUG_ROWS, MOBA_BLOCK), F32),
            pltpu.VMEM((2, n_blocks, MOBA_BLOCK), F32),
            pltpu.VMEM((pair_w, MOBA_BLOCK), F32),
        ],
        compiler_params=pltpu.CompilerParams(
            dimension_semantics=("parallel", "parallel", "arbitrary")),
    )(qv_t, ku3, qv_t, bias_tiles)


def _hdot(a, b):
    return jnp.dot(a, b, preferred_element_type=F32, precision=lax.Precision.HIGHEST)


def _hdot_nt(a, b):
    return lax.dot_general(a, b, (((1,), (1,)), ((), ())), preferred_element_type=F32,
                           precision=lax.Precision.HIGHEST)


def _s5_tables_kernel(logdt_ref, lre_ref, lim_ref, bre_ref, bim_ref, cre_ref, cim_ref, d_ref,
                      tg_ref, bc_ref, are_ref, aim_ref):
    lc, ch, p = SSM_CHUNK, SSM_GROUP_CH, SSM_STATE
    w = lc * ch
    lam_re = lre_ref[0]
    lam_im = lim_ref[0]
    dt = jnp.exp(logdt_ref[0])
    lr = lam_re * dt
    li = lam_im * dt

    tau = lax.broadcasted_iota(jnp.int32, (2 * lc, p), 0).astype(F32)
    mag = jnp.exp(lr * tau)
    cs = jnp.cos(li * tau)
    sn = jnp.sin(li * tau)
    pos_re, pos_im = mag * cs, mag * sn
    inv = jnp.exp(-lr * tau[:lc])
    neg_re, neg_im = inv * cs[:lc], -inv * sn[:lc]

    lam1_re, lam1_im = pos_re[1:2], pos_im[1:2]
    lamk_re, lamk_im = pos_re[lc - 1:lc], pos_im[lc - 1:lc]
    are_ref[0] = pos_re[lc:lc + 1]
    aim_ref[0] = pos_im[lc:lc + 1]

    num_re, num_im = lam1_re - 1.0, lam1_im
    den = lam_re * lam_re + lam_im * lam_im
    coef_re = (num_re * lam_re + num_im * lam_im) / den
    coef_im = (num_im * lam_re - num_re * lam_im) / den
    bt_re, bt_im = bre_ref[0], bim_ref[0]
    bb_re = coef_re * bt_re - coef_im * bt_im
    bb_im = coef_re * bt_im + coef_im * bt_re

    r_i = lax.broadcasted_iota(jnp.int32, (w, lc), 0)
    c_i = lax.broadcasted_iota(jnp.int32, (w, lc), 1)
    rep = jnp.where(r_i // ch == c_i, 1.0, 0.0).astype(F32)
    til = jnp.where(r_i % ch == c_i, 1.0, 0.0).astype(F32)

    bbt_re, bbt_im = _hdot(til, bb_re), _hdot(til, bb_im)
    ngx_re, ngx_im = _hdot(rep, neg_re), _hdot(rep, neg_im)
    bneg_re = bbt_re * ngx_re - bbt_im * ngx_im
    bneg_im = bbt_re * ngx_im + bbt_im * ngx_re

    ct_re, ct_im = _hdot(til, cre_ref[0]), _hdot(til, cim_ref[0])
    psx_re, psx_im = _hdot(rep, pos_re[:lc]), _hdot(rep, pos_im[:lc])
    cpos_re = ct_re * psx_re - ct_im * psx_im
    cpos_im = ct_re * psx_im + ct_im * psx_re

    raw = _hdot_nt(bneg_re, cpos_re) - _hdot_nt(bneg_im, cpos_im)
    rr = lax.broadcasted_iota(jnp.int32, (w, w), 0)
    cc = lax.broadcasted_iota(jnp.int32, (w, w), 1)
    e_r = lax.broadcasted_iota(jnp.int32, (ch, w), 0)
    e_c = lax.broadcasted_iota(jnp.int32, (ch, w), 1)
    d_lanes = _hdot(jnp.broadcast_to(d_ref[0], (8, ch)),
                    jnp.where(e_c % ch == e_r, 1.0, 0.0).astype(F32))[0:1]
    tg = jnp.where(rr // ch <= cc // ch, raw, 0.0) + jnp.where(rr == cc, d_lanes, 0.0)
    tg_ref[0] = tg.astype(tg_ref.dtype)

    bc_ref[0, 0] = (bneg_re * lamk_re - bneg_im * lamk_im).astype(bc_ref.dtype)
    bc_ref[0, 1] = (bneg_re * lamk_im + bneg_im * lamk_re).astype(bc_ref.dtype)
    cp_re = cpos_re * lam1_re - cpos_im * lam1_im
    cp_im = cpos_re * lam1_im + cpos_im * lam1_re
    bc_ref[0, 2] = cp_re.astype(bc_ref.dtype)
    bc_ref[0, 3] = (-cp_im).astype(bc_ref.dtype)


def _s5_tables(log_dt, lam_re, lam_im, b_re, b_im, c_re, c_im, d_skip):
    g, p, ch = SSM_GROUPS, SSM_STATE, SSM_GROUP_CH
    w = SSM_CHUNK * ch
    row = lambda a, n: a.reshape(g, 1, n)
    spec3 = lambda s1, s2: pl.BlockSpec((1, s1, s2), lambda i: (i, 0, 0))
    return pl.pallas_call(
        _s5_tables_kernel,
        grid=(g,),
        in_specs=[spec3(1, 1), spec3(1, p), spec3(1, p), spec3(ch, p), spec3(ch, p),
                  spec3(ch, p), spec3(ch, p), spec3(1, ch)],
        out_specs=[spec3(w, w),
                   pl.BlockSpec((1, 4, w, p), lambda i: (i, 0, 0, 0)),
                   spec3(1, p), spec3(1, p)],
        out_shape=[jax.ShapeDtypeStruct((g, w, w), BF16),
                   jax.ShapeDtypeStruct((g, 4, w, p), BF16),
                   jax.ShapeDtypeStruct((g, 1, p), F32),
                   jax.ShapeDtypeStruct((g, 1, p), F32)],
    )(row(log_dt, 1), row(lam_re, p), row(lam_im, p),
      jnp.swapaxes(b_re, 1, 2), jnp.swapaxes(b_im, 1, 2), c_re, c_im, row(d_skip, ch))


def _s5_scan_kernel(u_ref, tg_ref, bc_ref, are_ref, aim_ref, y_ref,
                    sre_ref, sim_ref, xre_ref, xim_ref, *, n_batch):
    gb, n_rows, _ = u_ref.shape
    n_chunks = n_rows // n_batch

    for k in range(gb):
        u = u_ref[k]
        sre_ref[pl.ds(k * n_rows, n_rows), :] = jnp.dot(u, bc_ref[k, 0], preferred_element_type=F32)
        sim_ref[pl.ds(k * n_rows, n_rows), :] = jnp.dot(u, bc_ref[k, 1], preferred_element_type=F32)

    a_re = are_ref[...]
    a_im = aim_ref[...]

    def step(c, carry):
        new = []
        for b in range(n_batch):
            x_re, x_im = carry[2 * b], carry[2 * b + 1]
            rows = pl.ds(b * n_chunks + c, gb, stride=n_rows)
            xre_ref[rows, :] = x_re
            xim_ref[rows, :] = x_im
            new.append(a_re * x_re - a_im * x_im + sre_ref[rows, :])
            new.append(a_re * x_im + a_im * x_re + sim_ref[rows, :])
        return tuple(new)

    zero = jnp.zeros(a_re.shape, F32)
    lax.fori_loop(0, n_chunks, step, (zero,) * (2 * n_batch))

    nt = (((1,), (1,)), ((), ()))
    for k in range(gb):
        rows = pl.ds(k * n_rows, n_rows)
        y = jnp.dot(u_ref[k], tg_ref[k], preferred_element_type=F32)
        y = y + lax.dot_general(xre_ref[rows, :].astype(BF16), bc_ref[k, 2], nt,
                                preferred_element_type=F32)
        y = y + lax.dot_general(xim_ref[rows, :].astype(BF16), bc_ref[k, 3], nt,
                                preferred_element_type=F32)
        y_ref[k] = y.astype(y_ref.dtype)


def _s5_scan(u3, tg, bc, a_re, a_im, n_batch, gb=8):
    g, n_rows, w = u3.shape
    p = SSM_STATE
    kernel = functools.partial(_s5_scan_kernel, n_batch=n_batch)
    return pl.pallas_call(
        kernel,
        grid=(g // gb,),
        in_specs=[pl.BlockSpec((gb, n_rows, w), lambda i: (i, 0, 0)),
                  pl.BlockSpec((gb, w, w), lambda i: (i, 0, 0)),
                  pl.BlockSpec((gb, 4, w, p), lambda i: (i, 0, 0, 0)),
                  pl.BlockSpec((gb, p), lambda i: (i, 0)),
                  pl.BlockSpec((gb, p), lambda i: (i, 0))],
        out_specs=pl.BlockSpec((gb, n_rows, w), lambda i: (i, 0, 0)),
        out_shape=jax.ShapeDtypeStruct((g, n_rows, w), F32),
        scratch_shapes=[pltpu.VMEM((gb * n_rows, p), F32) for _ in range(4)],
        compiler_params=pltpu.CompilerParams(dimension_semantics=("parallel",)),
    )(u3, tg, bc, a_re.reshape(g, p), a_im.reshape(g, p))


def _out_proj_kernel(x_ref, a_ref, y_ref, wglu_ref, bglu_ref, ga_ref, gs_ref, wa_ref, ws_ref, o_ref):
    z = jax.nn.gelu(y_ref[...])
    gl = jnp.dot(z.astype(BF16), wglu_ref[...], preferred_element_type=F32) + bglu_ref[...]
    s = z * jax.nn.sigmoid(gl)
    a_n = _rms_rows(a_ref[...].astype(F32), ga_ref[...]).astype(BF16)
    s_n = _rms_rows(s, gs_ref[...]).astype(BF16)
    mix = (jnp.dot(a_n, wa_ref[...], preferred_element_type=F32)
           + jnp.dot(s_n, ws_ref[...], preferred_element_type=F32))
    o_ref[...] = x_ref[...] + mix


def _out_proj(x2, attn, y, w_glu, b_glu, g_a, g_s, w_out, tm=256):
    t, d = x2.shape
    wa = attn.shape[1]
    ws = y.shape[1]
    row = lambda i: (i, 0)
    fixed = lambda i: (0, 0)
    once = pl.Buffered(1)
    return pl.pallas_call(
        _out_proj_kernel,
        grid=(t // tm,),
        in_specs=[pl.BlockSpec((tm, d), row),
                  pl.BlockSpec((tm, wa), row),
                  pl.BlockSpec((tm, ws), row),
                  pl.BlockSpec((ws, ws), fixed, pipeline_mode=once),
                  pl.BlockSpec((1, ws), fixed),
                  pl.BlockSpec((1, wa), fixed),
                  pl.BlockSpec((1, ws), fixed),
                  pl.BlockSpec((wa, d), fixed, pipeline_mode=once),
                  pl.BlockSpec((ws, d), fixed, pipeline_mode=once)],
        out_specs=pl.BlockSpec((tm, d), row),
        out_shape=jax.ShapeDtypeStruct((t, d), F32),
        compiler_params=pltpu.CompilerParams(dimension_semantics=("parallel",),
                                             vmem_limit_bytes=VMEM_LIMIT),
    )(x2, attn, y, w_glu, b_glu.reshape(1, ws), g_a.reshape(1, wa), g_s.reshape(1, ws),
      w_out[:wa], w_out[wa:])


def _ffn_up_kernel(x_ref, halo_ref, g_ref, wg_ref, wv_ref, cwg_ref, cwv_ref, cbg_ref, cbv_ref,
                   o_ref, h_ref, *, tiles_per_seq):
    tm = x_ref.shape[0]

    @pl.when(pl.program_id(1) == 0)
    def _():
        keep = jnp.where(pl.program_id(0) % tiles_per_seq == 0, 0.0, 1.0)
        h_ref[pl.ds(0, CONV_HALO), :] = (_rms_rows(halo_ref[...], g_ref[...]) * keep).astype(BF16)
        h_ref[pl.ds(CONV_HALO, tm), :] = _rms_rows(x_ref[...], g_ref[...]).astype(BF16)

    h = h_ref[...]

    def conv(w_ref, cw_ref, cb_ref):
        up = jnp.dot(h, w_ref[...], preferred_element_type=F32)
        cw = cw_ref[...]
        out = up[CONV_HALO:] * cw[CONV_WIDTH - 1:CONV_WIDTH] + cb_ref[...]
        for j in range(CONV_WIDTH - 1):
            lag = CONV_WIDTH - 1 - j
            out = out + up[CONV_HALO - lag:CONV_HALO - lag + tm] * cw[j:j + 1]
        return out

    gate = conv(wg_ref, cwg_ref, cbg_ref)
    val = conv(wv_ref, cwv_ref, cbv_ref)
    o_ref[...] = (jax.nn.silu(gate) * val).astype(o_ref.dtype)


def _ffn_up(x1, g, w_up, conv_w, conv_b, seq, tm=1024, tn=512):
    t, d = x1.shape
    f = w_up.shape[1] // 2
    nf = f // tn
    tiles_per_seq = seq // tm
    halo_blocks = tm // CONV_HALO
    kernel = functools.partial(_ffn_up_kernel, tiles_per_seq=tiles_per_seq)
    cb = conv_b.reshape(1, 2 * f)
    return pl.pallas_call(
        kernel,
        grid=(t // tm, nf),
        in_specs=[pl.BlockSpec((tm, d), lambda i, j: (i, 0)),
                  pl.BlockSpec((CONV_HALO, d), lambda i, j: (jnp.maximum(i * halo_blocks - 1, 0), 0)),
                  pl.BlockSpec((1, d), lambda i, j: (0, 0)),
                  pl.BlockSpec((d, tn), lambda i, j: (0, j)),
                  pl.BlockSpec((d, tn), lambda i, j: (0, nf + j)),
                  pl.BlockSpec((CONV_WIDTH, tn), lambda i, j: (0, j)),
                  pl.BlockSpec((CONV_WIDTH, tn), lambda i, j: (0, nf + j)),
                  pl.BlockSpec((1, tn), lambda i, j: (0, j)),
                  pl.BlockSpec((1, tn), lambda i, j: (0, nf + j))],
        out_specs=pl.BlockSpec((tm, tn), lambda i, j: (i, j)),
        out_shape=jax.ShapeDtypeStruct((t, f), BF16),
        scratch_shapes=[pltpu.VMEM((CONV_HALO + tm, d), BF16)],
        compiler_params=pltpu.CompilerParams(dimension_semantics=("parallel", "arbitrary"),
                                             vmem_limit_bytes=VMEM_LIMIT),
    )(x1, x1, g.reshape(1, d), w_up, w_up, conv_w, conv_w, cb, cb)


def _ffn_down_kernel(a_ref, w_ref, x_ref, g_ref, o_ref, acc_ref, *, final_norm):
    k = pl.program_id(1)

    @pl.when(k == 0)
    def _():
        acc_ref[...] = x_ref[...]

    acc_ref[...] += jnp.dot(a_ref[...], w_ref[...], preferred_element_type=F32)

    @pl.when(k == pl.num_programs(1) - 1)
    def _():
        o_ref[...] = _rms_rows(acc_ref[...], g_ref[...]) if final_norm else acc_ref[...]


def _ffn_down(act, w_down, x1, g, final_norm, tm=512, tk=512):
    t, f = act.shape
    d = w_down.shape[1]
    return pl.pallas_call(
        functools.partial(_ffn_down_kernel, final_norm=final_norm),
        grid=(t // tm, f // tk),
        in_specs=[pl.BlockSpec((tm, tk), lambda i, k: (i, k)),
                  pl.BlockSpec((tk, d), lambda i, k: (k, 0)),
                  pl.BlockSpec((tm, d), lambda i, k: (i, 0)),
                  pl.BlockSpec((1, d), lambda i, k: (0, 0))],
        out_specs=pl.BlockSpec((tm, d), lambda i, k: (i, 0)),
        out_shape=jax.ShapeDtypeStruct((t, d), F32),
        scratch_shapes=[pltpu.VMEM((tm, d), F32)],
        compiler_params=pltpu.CompilerParams(dimension_semantics=("parallel", "arbitrary"),
                                             vmem_limit_bytes=VMEM_LIMIT),
    )(act, w_down, x1, g.reshape(1, d))


def kernel(x, norm_mix, w_in, rel_bias_table, ssm_lam_re, ssm_lam_im, ssm_log_dt, ssm_b_re, ssm_b_im, ssm_c_re, ssm_c_im, ssm_d, ssm_w_glu, ssm_b_glu, norm_attn_out, norm_ssm_out, w_out, norm_ffn, w_ffn_up, ffn_conv_w, ffn_conv_b, w_ffn_down, norm_final):
    batch, seq, d_model = x.shape
    depth = w_in.shape[0]
    aw = ATTN_HEADS * HEAD_DIM
    sw = SSM_GROUPS * SSM_GROUP_CH
    t = batch * seq
    n_blocks = seq // MOBA_BLOCK
    assert seq % MOBA_BLOCK == 0 and seq % SSM_CHUNK == 0

    bias_tiles = _bias_tiles(rel_bias_table)
    x2 = x.reshape(t, d_model)
    for l in range(depth):
        w = w_in[l]
        w_ku = jnp.concatenate([w[:, aw:2 * aw], w[:, 3 * aw:]], axis=1).astype(BF16)
        wt_qv = jnp.concatenate([w[:, :aw] * (HEAD_DIM ** -0.5 * LOG2E), w[:, 2 * aw:3 * aw]],
                                axis=1).T.astype(BF16)
        ku, qv_t = _in_proj(x2, norm_mix[l], w_ku, wt_qv)

        ku3 = ku.reshape(t // MOBA_BLOCK, MOBA_BLOCK, ku.shape[1])
        attn = _moba_attention(qv_t, ku3, bias_tiles, batch, seq)

        n_rows = t // SSM_CHUNK
        u3 = (ku[:, aw:].reshape(n_rows, SSM_CHUNK, SSM_GROUPS, SSM_GROUP_CH)
              .transpose(2, 0, 1, 3).reshape(SSM_GROUPS, n_rows, SSM_CHUNK * SSM_GROUP_CH))
        tg, bc, a_re, a_im = _s5_tables(ssm_log_dt[l], ssm_lam_re[l], ssm_lam_im[l], ssm_b_re[l],
                                        ssm_b_im[l], ssm_c_re[l], ssm_c_im[l], ssm_d[l])
        y3 = _s5_scan(u3, tg, bc, a_re, a_im, batch)
        y = (y3.reshape(SSM_GROUPS, n_rows, SSM_CHUNK, SSM_GROUP_CH)
             .transpose(1, 2, 0, 3).reshape(t, sw))

        x2 = _out_proj(x2, attn, y, ssm_w_glu[l].astype(BF16), ssm_b_glu[l], norm_attn_out[l],
                       norm_ssm_out[l], w_out[l].astype(BF16))

        act = _ffn_up(x2, norm_ffn[l], w_ffn_up[l].astype(BF16), ffn_conv_w[l], ffn_conv_b[l], seq)
        x2 = _ffn_down(act, w_ffn_down[l].astype(BF16), x2, norm_final, final_norm=(l == depth - 1))
    return x2.reshape(batch, seq, d_model)
```

```python
import functools
import math

import jax
import jax.numpy as jnp
from jax import lax
from jax.experimental import pallas as pl
from jax.experimental.pallas import tpu as pltpu

F32 = jnp.float32
BF16 = jnp.bfloat16

ATTN_HEADS = 16
HEAD_DIM = 64
SSM_GROUP_CH = 16
SSM_GROUPS = 64
SSM_STATE = 64
MOBA_BLOCK = 256
MOBA_TOP_K = 3
NUM_BUCKETS = 32
MAX_DISTANCE = 128
CONV_WIDTH = 3
RMS_EPS = 1e-6

SSM_CHUNK = 16
CONV_HALO = 16
VMEM_LIMIT = 52 * 1024 * 1024

NEG_INF = float("-inf")
LOG2E = math.log2(math.e)
V_AUG_ROWS = HEAD_DIM + 16


def _rms_rows(x, g):
    ms = jnp.mean(x * x, axis=-1, keepdims=True)
    return x * lax.rsqrt(ms + RMS_EPS) * g


def _in_proj_nn_kernel(x_ref, g_ref, w_ref, o_ref, h_ref):
    @pl.when(pl.program_id(1) == 0)
    def _():
        h_ref[...] = _rms_rows(x_ref[...], g_ref[...]).astype(BF16)

    o_ref[...] = jnp.dot(h_ref[...], w_ref[...], preferred_element_type=F32).astype(o_ref.dtype)


def _in_proj_nt_kernel(x_ref, g_ref, wt_ref, o_ref, h_ref):
    @pl.when(pl.program_id(1) == 0)
    def _():
        h_ref[...] = _rms_rows(x_ref[...], g_ref[...]).astype(BF16)

    res = lax.dot_general(wt_ref[...], h_ref[...], (((1,), (1,)), ((), ())),
                          preferred_element_type=F32).astype(o_ref.dtype)
    for c in range(o_ref.shape[0]):
        o_ref[c] = res[:, c * MOBA_BLOCK:(c + 1) * MOBA_BLOCK]


def _in_proj(x2, g, w_nn, wt_nt, tm=1024, tn=512):
    t, d = x2.shape
    n_nn = w_nn.shape[1]
    n_nt = wt_nt.shape[0]
    g2 = g.reshape(1, d)
    params = pltpu.CompilerParams(dimension_semantics=("parallel", "arbitrary"),
                                  vmem_limit_bytes=VMEM_LIMIT)
    nat = pl.pallas_call(
        _in_proj_nn_kernel,
        grid=(t // tm, n_nn // tn),
        in_specs=[pl.BlockSpec((tm, d), lambda i, j: (i, 0)),
                  pl.BlockSpec((1, d), lambda i, j: (0, 0)),
                  pl.BlockSpec((d, tn), lambda i, j: (0, j))],
        out_specs=pl.BlockSpec((tm, tn), lambda i, j: (i, j)),
        out_shape=jax.ShapeDtypeStruct((t, n_nn), BF16),
        scratch_shapes=[pltpu.VMEM((tm, d), BF16)],
        compiler_params=params,
    )(x2, g2, w_nn)
    slabs = tm // MOBA_BLOCK
    tr = pl.pallas_call(
        _in_proj_nt_kernel,
        grid=(t // tm, n_nt // tn),
        in_specs=[pl.BlockSpec((tm, d), lambda i, j: (i, 0)),
                  pl.BlockSpec((1, d), lambda i, j: (0, 0)),
                  pl.BlockSpec((tn, d), lambda i, j: (j, 0))],
        out_specs=pl.BlockSpec((slabs, tn, MOBA_BLOCK), lambda i, j: (i, j, 0)),
        out_shape=jax.ShapeDtypeStruct((t // MOBA_BLOCK, n_nt, MOBA_BLOCK), BF16),
        scratch_shapes=[pltpu.VMEM((tm, d), BF16)],
        compiler_params=params,
    )(x2, g2, wt_nt)
    return nat, tr


def _t5_bucket(dist):
    dist = jnp.maximum(dist, 0)
    max_exact = NUM_BUCKETS // 2
    log_ratio = jnp.log(jnp.maximum(dist, max_exact).astype(F32) / max_exact)
    large = max_exact + (log_ratio / math.log(MAX_DISTANCE / max_exact)
                         * (NUM_BUCKETS - max_exact)).astype(jnp.int32)
    large = jnp.minimum(large, NUM_BUCKETS - 1)
    return jnp.where(dist < max_exact, dist, large)


def _bias_tiles_kernel(tab_ref, o_ref):
    h = pl.program_id(0)
    c_far = tab_ref[NUM_BUCKETS - 1, h]
    kk = lax.broadcasted_iota(jnp.int32, (MOBA_BLOCK, MOBA_BLOCK), 0)
    qq = lax.broadcasted_iota(jnp.int32, (MOBA_BLOCK, MOBA_BLOCK), 1)
    for which, off in ((0, MOBA_BLOCK), (1, 0)):
        rel = qq - kk + off
        bucket = _t5_bucket(rel)
        acc = jnp.zeros((MOBA_BLOCK, MOBA_BLOCK), F32)
        for b in range(NUM_BUCKETS):
            acc = jnp.where(bucket == b, tab_ref[b, h], acc)
        acc = (acc - c_far) * LOG2E
        if which == 1:
            acc = jnp.where(rel >= 0, acc, NEG_INF)
        o_ref[0, which] = acc


def _bias_tiles(table):
    return pl.pallas_call(
        _bias_tiles_kernel,
        grid=(ATTN_HEADS,),
        in_specs=[pl.BlockSpec(memory_space=pltpu.SMEM)],
        out_specs=pl.BlockSpec((1, 2, MOBA_BLOCK, MOBA_BLOCK), lambda h: (h, 0, 0, 0)),
        out_shape=jax.ShapeDtypeStruct((ATTN_HEADS, 2, MOBA_BLOCK, MOBA_BLOCK), F32),
    )(table)


def _moba_kernel(q_ref, k_ref, v_ref, bias_ref, o_ref,
                 kmf_ref, vaug_ref, qm_ref, sa_ref, sb_ref, cma_ref, cmb_ref,
                 m_ref, acc_ref, rb_ref, ot_ref, *, n_blocks):
    qi = pl.program_id(2)
    blk = MOBA_BLOCK

    @pl.when(qi == 0)
    def _():
        ones = jnp.ones((V_AUG_ROWS - HEAD_DIM, blk), BF16)
        for j in range(n_blocks):
            kb = k_ref[j].astype(F32)
            kmf_ref[pl.ds(j, 1), :] = jnp.sum(kb, axis=0, keepdims=True) * (1.0 / blk)
            for hh in range(2):
                vaug_ref[j, hh, 0:HEAD_DIM, :] = v_ref[j, hh * HEAD_DIM:(hh + 1) * HEAD_DIM, :]
                vaug_ref[j, hh, HEAD_DIM:V_AUG_ROWS, :] = ones

    kmf = kmf_ref[...]
    km_hi = kmf.astype(BF16)
    km_lo = (kmf - km_hi.astype(F32)).astype(BF16)

    q2 = q_ref[0]
    q_zero = jnp.zeros((HEAD_DIM, blk), q2.dtype)
    blk_idx = lax.broadcasted_iota(jnp.int32, (n_blocks, blk), 0)

    for hh in range(2):
        vrows = slice(hh * HEAD_DIM, (hh + 1) * HEAD_DIM)
        qm = jnp.concatenate([q2[vrows], q_zero] if hh == 0 else [q_zero, q2[vrows]], axis=0)
        qm_ref[hh] = qm

        gate = (jnp.dot(km_hi, qm, preferred_element_type=F32)
                + jnp.dot(km_lo, qm, preferred_element_type=F32))
        valid = blk_idx < qi
        g = jnp.where(valid, gate, NEG_INF)
        rank = jnp.zeros(g.shape, F32)
        for j in range(n_blocks):
            gj = g[j:j + 1, :]
            beats = jnp.where(gj > g, 1.0, jnp.where((gj == g) & (blk_idx > j), 1.0, 0.0))
            rank = rank + beats
        sel = (valid & (rank < MOBA_TOP_K)) | (blk_idx == qi)
        rb_ref[hh] = jnp.where(sel, 0.0, NEG_INF)
        m_ref[hh] = jnp.full((1, blk), NEG_INF, F32)
        acc_ref[hh] = jnp.zeros((V_AUG_ROWS, blk), F32)

    def group_blocks(g):
        ja = qi - 2 * g
        return (ja, None), (jnp.maximum(ja - 1, 0), ja >= 1)

    def scores(g, s_buf, cm_buf, with_bias):
        for i, (j, _) in enumerate(group_blocks(g)):
            kb = k_ref[j]
            for hh in range(2):
                s = jnp.dot(kb, qm_ref[hh], preferred_element_type=F32)
                if with_bias:
                    s = s + bias_ref[hh, 1 - i]
                s_buf[hh, i] = s
                cm_buf[hh, pl.ds(i, 1), :] = jnp.max(s, axis=0, keepdims=True)

    def attend(g, s_buf, cm_buf):
        blocks = group_blocks(g)
        for hh in range(2):
            masks = []
            m_old = m_ref[hh]
            m_new = m_old
            for i, (j, exists) in enumerate(blocks):
                mk = rb_ref[hh, pl.ds(j, 1), :]
                if exists is not None:
                    mk = jnp.where(exists, mk, NEG_INF)
                masks.append(mk)
                m_new = jnp.maximum(m_new, cm_buf[hh, pl.ds(i, 1), :] + mk)
            acc = jnp.exp2(m_old - m_new) * acc_ref[hh]
            for i, (j, _) in enumerate(blocks):
                p = jnp.exp2(s_buf[hh, i] - m_new).astype(BF16)
                pv = jnp.dot(vaug_ref[j, hh], p, preferred_element_type=F32)
                acc = acc + jnp.where(masks[i] == 0.0, pv, 0.0)
            acc_ref[hh] = acc
            m_ref[hh] = m_new

    n_more = qi // 2
    scores(0, sa_ref, cma_ref, True)

    def step(it, carry):
        @pl.when(it % 2 == 0)
        def _():
            scores(it + 1, sb_ref, cmb_ref, False)
            attend(it, sa_ref, cma_ref)

        @pl.when(it % 2 == 1)
        def _():
            scores(it + 1, sa_ref, cma_ref, False)
            attend(it, sb_ref, cmb_ref)

        return carry

    lax.fori_loop(0, n_more, step, 0)

    @pl.when(n_more % 2 == 0)
    def _():
        attend(n_more, sa_ref, cma_ref)

    @pl.when(n_more % 2 == 1)
    def _():
        attend(n_more, sb_ref, cmb_ref)

    for hh in range(2):
        acc = acc_ref[hh]
        ot_ref[hh * HEAD_DIM:(hh + 1) * HEAD_DIM, :] = (
            acc[0:HEAD_DIM] * (1.0 / acc[HEAD_DIM:HEAD_DIM + 1]))

    o_ref[...] = ot_ref[...].T.astype(o_ref.dtype)


def _moba_attention(qv_t, ku3, bias_tiles, batch, seq):
    n_blocks = seq // MOBA_BLOCK
    n_pairs = ATTN_HEADS // 2
    pair_w = 2 * HEAD_DIM
    kernel = functools.partial(_moba_kernel, n_blocks=n_blocks)
    return pl.pallas_call(
        kernel,
        grid=(batch, n_pairs, n_blocks),
        in_specs=[
            pl.BlockSpec((1, pair_w, MOBA_BLOCK), lambda b, hp, qi: (b * n_blocks + qi, hp, 0)),
            pl.BlockSpec((n_blocks, MOBA_BLOCK, pair_w), lambda b, hp, qi: (b, 0, hp)),
            pl.BlockSpec((n_blocks, pair_w, MOBA_BLOCK), lambda b, hp, qi: (b, n_pairs + hp, 0)),
            pl.BlockSpec((2, 2, MOBA_BLOCK, MOBA_BLOCK), lambda b, hp, qi: (hp, 0, 0, 0)),
        ],
        out_specs=pl.BlockSpec((MOBA_BLOCK, pair_w), lambda b, hp, qi: (b * n_blocks + qi, hp)),
        out_shape=jax.ShapeDtypeStruct((batch * seq, ATTN_HEADS * HEAD_DIM), BF16),
        scratch_shapes=[
            pltpu.VMEM((n_blocks, pair_w), F32),
            pltpu.VMEM((n_blocks, 2, V_AUG_ROWS, MOBA_BLOCK), BF16),
            pltpu.VMEM((2, pair_w, MOBA_BLOCK), BF16),
            pltpu.VMEM((2, 2, MOBA_BLOCK, MOBA_BLOCK), F32),
            pltpu.VMEM((2, 2, MOBA_BLOCK, MOBA_BLOCK), F32),
            pltpu.VMEM((2, 8, MOBA_BLOCK), F32),
            pltpu.VMEM((2, 8, MOBA_BLOCK), F32),
            pltpu.VMEM((2, 1, MOBA_BLOCK), F32),
            pltpu.VMEM((2, V_AUG_ROWS, MOBA_BLOCK), F32),
            pltpu.VMEM((2, n_blocks, MOBA_BLOCK), F32),
            pltpu.VMEM((pair_w, MOBA_BLOCK), F32),
        ],
        compiler_params=pltpu.CompilerParams(
            dimension_semantics=("parallel", "parallel", "arbitrary")),
    )(qv_t, ku3, qv_t, bias_tiles)


def _hdot(a, b):
    return jnp.dot(a, b, preferred_element_type=F32, precision=lax.Precision.HIGHEST)


def _hdot_nt(a, b):
    return lax.dot_general(a, b, (((1,), (1,)), ((), ())), preferred_element_type=F32,
                           precision=lax.Precision.HIGHEST)


def _s5_tables_kernel(logdt_ref, lre_ref, lim_ref, bre_ref, bim_ref, cre_ref, cim_ref, d_ref,
                      tg_ref, bc_ref, are_ref, aim_ref):
    lc, ch, p = SSM_CHUNK, SSM_GROUP_CH, SSM_STATE
    w = lc * ch
    lam_re = lre_ref[0]
    lam_im = lim_ref[0]
    dt = jnp.exp(logdt_ref[0])
    lr = lam_re * dt
    li = lam_im * dt

    tau = lax.broadcasted_iota(jnp.int32, (2 * lc, p), 0).astype(F32)
    mag = jnp.exp(lr * tau)
    cs = jnp.cos(li * tau)
    sn = jnp.sin(li * tau)
    pos_re, pos_im = mag * cs, mag * sn
    inv = jnp.exp(-lr * tau[:lc])
    neg_re, neg_im = inv * cs[:lc], -inv * sn[:lc]

    lam1_re, lam1_im = pos_re[1:2], pos_im[1:2]
    lamk_re, lamk_im = pos_re[lc - 1:lc], pos_im[lc - 1:lc]
    are_ref[0] = pos_re[lc:lc + 1]
    aim_ref[0] = pos_im[lc:lc + 1]

    num_re, num_im = lam1_re - 1.0, lam1_im
    den = lam_re * lam_re + lam_im * lam_im
    coef_re = (num_re * lam_re + num_im * lam_im) / den
    coef_im = (num_im * lam_re - num_re * lam_im) / den
    bt_re, bt_im = bre_ref[0], bim_ref[0]
    bb_re = coef_re * bt_re - coef_im * bt_im
    bb_im = coef_re * bt_im + coef_im * bt_re

    r_i = lax.broadcasted_iota(jnp.int32, (w, lc), 0)
    c_i = lax.broadcasted_iota(jnp.int32, (w, lc), 1)
    rep = jnp.where(r_i // ch == c_i, 1.0, 0.0).astype(F32)
    til = jnp.where(r_i % ch == c_i, 1.0, 0.0).astype(F32)

    bbt_re, bbt_im = _hdot(til, bb_re), _hdot(til, bb_im)
    ngx_re, ngx_im = _hdot(rep, neg_re), _hdot(rep, neg_im)
    bneg_re = bbt_re * ngx_re - bbt_im * ngx_im
    bneg_im = bbt_re * ngx_im + bbt_im * ngx_re

    ct_re, ct_im = _hdot(til, cre_ref[0]), _hdot(til, cim_ref[0])
    psx_re, psx_im = _hdot(rep, pos_re[:lc]), _hdot(rep, pos_im[:lc])
    cpos_re = ct_re * psx_re - ct_im * psx_im
    cpos_im = ct_re * psx_im + ct_im * psx_re

    raw = _hdot_nt(bneg_re, cpos_re) - _hdot_nt(bneg_im, cpos_im)
    rr = lax.broadcasted_iota(jnp.int32, (w, w), 0)
    cc = lax.broadcasted_iota(jnp.int32, (w, w), 1)
    e_r = lax.broadcasted_iota(jnp.int32, (ch, w), 0)
    e_c = lax.broadcasted_iota(jnp.int32, (ch, w), 1)
    d_lanes = _hdot(jnp.broadcast_to(d_ref[0], (8, ch)),
                    jnp.where(e_c % ch == e_r, 1.0, 0.0).astype(F32))[0:1]
    tg = jnp.where(rr // ch <= cc // ch, raw, 0.0) + jnp.where(rr == cc, d_lanes, 0.0)
    tg_ref[0] = tg.astype(tg_ref.dtype)

    bc_ref[0, 0] = (bneg_re * lamk_re - bneg_im * lamk_im).astype(bc_ref.dtype)
    bc_ref[0, 1] = (bneg_re * lamk_im + bneg_im * lamk_re).astype(bc_ref.dtype)
    cp_re = cpos_re * lam1_re - cpos_im * lam1_im
    cp_im = cpos_re * lam1_im + cpos_im * lam1_re
    bc_ref[0, 2] = cp_re.astype(bc_ref.dtype)
    bc_ref[0, 3] = (-cp_im).astype(bc_ref.dtype)


def _s5_tables(log_dt, lam_re, lam_im, b_re, b_im, c_re, c_im, d_skip):
    g, p, ch = SSM_GROUPS, SSM_STATE, SSM_GROUP_CH
    w = SSM_CHUNK * ch
    row = lambda a, n: a.reshape(g, 1, n)
    spec3 = lambda s1, s2: pl.BlockSpec((1, s1, s2), lambda i: (i, 0, 0))
    return pl.pallas_call(
        _s5_tables_kernel,
        grid=(g,),
        in_specs=[spec3(1, 1), spec3(1, p), spec3(1, p), spec3(ch, p), spec3(ch, p),
                  spec3(ch, p), spec3(ch, p), spec3(1, ch)],
        out_specs=[spec3(w, w),
                   pl.BlockSpec((1, 4, w, p), lambda i: (i, 0, 0, 0)),
                   spec3(1, p), spec3(1, p)],
        out_shape=[jax.ShapeDtypeStruct((g, w, w), BF16),
                   jax.ShapeDtypeStruct((g, 4, w, p), BF16),
                   jax.ShapeDtypeStruct((g, 1, p), F32),
                   jax.ShapeDtypeStruct((g, 1, p), F32)],
    )(row(log_dt, 1), row(lam_re, p), row(lam_im, p),
      jnp.swapaxes(b_re, 1, 2), jnp.swapaxes(b_im, 1, 2), c_re, c_im, row(d_skip, ch))


def _s5_scan_kernel(u_ref, tg_ref, bc_ref, are_ref, aim_ref, y_ref,
                    sre_ref, sim_ref, xre_ref, xim_ref, *, n_batch):
    gb, n_rows, _ = u_ref.shape
    n_chunks = n_rows // n_batch

    for k in range(gb):
        u = u_ref[k]
        sre_ref[pl.ds(k * n_rows, n_rows), :] = jnp.dot(u, bc_ref[k, 0], preferred_element_type=F32)
        sim_ref[pl.ds(k * n_rows, n_rows), :] = jnp.dot(u, bc_ref[k, 1], preferred_element_type=F32)

    a_re = are_ref[...]
    a_im = aim_ref[...]

    def step(c, carry):
        new = []
        for b in range(n_batch):
            x_re, x_im = carry[2 * b], carry[2 * b + 1]
            rows = pl.ds(b * n_chunks + c, gb, stride=n_rows)
            xre_ref[rows, :] = x_re
            xim_ref[rows, :] = x_im
            new.append(a_re * x_re - a_im * x_im + sre_ref[rows, :])
            new.append(a_re * x_im + a_im * x_re + sim_ref[rows, :])
        return tuple(new)

    zero = jnp.zeros(a_re.shape, F32)
    lax.fori_loop(0, n_chunks, step, (zero,) * (2 * n_batch))

    nt = (((1,), (1,)), ((), ()))
    for k in range(gb):
        rows = pl.ds(k * n_rows, n_rows)
        y = jnp.dot(u_ref[k], tg_ref[k], preferred_element_type=F32)
        y = y + lax.dot_general(xre_ref[rows, :].astype(BF16), bc_ref[k, 2], nt,
                                preferred_element_type=F32)
        y = y + lax.dot_general(xim_ref[rows, :].astype(BF16), bc_ref[k, 3], nt,
                                preferred_element_type=F32)
        y_ref[k] = y.astype(y_ref.dtype)


def _s5_scan(u3, tg, bc, a_re, a_im, n_batch, gb=8):
    g, n_rows, w = u3.shape
    p = SSM_STATE
    kernel = functools.partial(_s5_scan_kernel, n_batch=n_batch)
    return pl.pallas_call(
        kernel,
        grid=(g // gb,),
        in_specs=[pl.BlockSpec((gb, n_rows, w), lambda i: (i, 0, 0)),
                  pl.BlockSpec((gb, w, w), lambda i: (i, 0, 0)),
                  pl.BlockSpec((gb, 4, w, p), lambda i: (i, 0, 0, 0)),
                  pl.BlockSpec((gb, p), lambda i: (i, 0)),
                  pl.BlockSpec((gb, p), lambda i: (i, 0))],
        out_specs=pl.BlockSpec((gb, n_rows, w), lambda i: (i, 0, 0)),
        out_shape=jax.ShapeDtypeStruct((g, n_rows, w), F32),
        scratch_shapes=[pltpu.VMEM((gb * n_rows, p), F32) for _ in range(4)],
        compiler_params=pltpu.CompilerParams(dimension_semantics=("parallel",)),
    )(u3, tg, bc, a_re.reshape(g, p), a_im.reshape(g, p))


def _out_proj_kernel(x_ref, a_ref, y_ref, wglu_ref, bglu_ref, ga_ref, gs_ref, wa_ref, ws_ref, o_ref):
    z = jax.nn.gelu(y_ref[...])
    gl = jnp.dot(z.astype(BF16), wglu_ref[...], preferred_element_type=F32) + bglu_ref[...]
    s = z * jax.nn.sigmoid(gl)
    a_n = _rms_rows(a_ref[...].astype(F32), ga_ref[...]).astype(BF16)
    s_n = _rms_rows(s, gs_ref[...]).astype(BF16)
    mix = (jnp.dot(a_n, wa_ref[...], preferred_element_type=F32)
           + jnp.dot(s_n, ws_ref[...], preferred_element_type=F32))
    o_ref[...] = x_ref[...] + mix


def _out_proj(x2, attn, y, w_glu, b_glu, g_a, g_s, w_out, tm=256):
    t, d = x2.shape
    wa = attn.shape[1]
    ws = y.shape[1]
    row = lambda i: (i, 0)
    fixed = lambda i: (0, 0)
    once = pl.Buffered(1)
    return pl.pallas_call(
        _out_proj_kernel,
        grid=(t // tm,),
        in_specs=[pl.BlockSpec((tm, d), row),
                  pl.BlockSpec((tm, wa), row),
                  pl.BlockSpec((tm, ws), row),
                  pl.BlockSpec((ws, ws), fixed, pipeline_mode=once),
                  pl.BlockSpec((1, ws), fixed),
                  pl.BlockSpec((1, wa), fixed),
                  pl.BlockSpec((1, ws), fixed),
                  pl.BlockSpec((wa, d), fixed, pipeline_mode=once),
                  pl.BlockSpec((ws, d), fixed, pipeline_mode=once)],
        out_specs=pl.BlockSpec((tm, d), row),
        out_shape=jax.ShapeDtypeStruct((t, d), F32),
        compiler_params=pltpu.CompilerParams(dimension_semantics=("parallel",),
                                             vmem_limit_bytes=VMEM_LIMIT),
    )(x2, attn, y, w_glu, b_glu.reshape(1, ws), g_a.reshape(1, wa), g_s.reshape(1, ws),
      w_out[:wa], w_out[wa:])


def _ffn_up_kernel(x_ref, halo_ref, g_ref, wg_ref, wv_ref, cwg_ref, cwv_ref, cbg_ref, cbv_ref,
                   o_ref, h_ref, *, tiles_per_seq):
    tm = x_ref.shape[0]

    @pl.when(pl.program_id(1) == 0)
    def _():
        keep = jnp.where(pl.program_id(0) % tiles_per_seq == 0, 0.0, 1.0)
        h_ref[pl.ds(0, CONV_HALO), :] = (_rms_rows(halo_ref[...], g_ref[...]) * keep).astype(BF16)
        h_ref[pl.ds(CONV_HALO, tm), :] = _rms_rows(x_ref[...], g_ref[...]).astype(BF16)

    h = h_ref[...]

    def conv(w_ref, cw_ref, cb_ref):
        up = jnp.dot(h, w_ref[...], preferred_element_type=F32)
        cw = cw_ref[...]
        out = up[CONV_HALO:] * cw[CONV_WIDTH - 1:CONV_WIDTH] + cb_ref[...]
        for j in range(CONV_WIDTH - 1):
            lag = CONV_WIDTH - 1 - j
            out = out + up[CONV_HALO - lag:CONV_HALO - lag + tm] * cw[j:j + 1]
        return out

    gate = conv(wg_ref, cwg_ref, cbg_ref)
    val = conv(wv_ref, cwv_ref, cbv_ref)
    o_ref[...] = (jax.nn.silu(gate) * val).astype(o_ref.dtype)


def _ffn_up(x1, g, w_up, conv_w, conv_b, seq, tm=1024, tn=512):
    t, d = x1.shape
    f = w_up.shape[1] // 2
    nf = f // tn
    tiles_per_seq = seq // tm
    halo_blocks = tm // CONV_HALO
    kernel = functools.partial(_ffn_up_kernel, tiles_per_seq=tiles_per_seq)
    cb = conv_b.reshape(1, 2 * f)
    return pl.pallas_call(
        kernel,
        grid=(t // tm, nf),
        in_specs=[pl.BlockSpec((tm, d), lambda i, j: (i, 0)),
                  pl.BlockSpec((CONV_HALO, d), lambda i, j: (jnp.maximum(i * halo_blocks - 1, 0), 0)),
                  pl.BlockSpec((1, d), lambda i, j: (0, 0)),
                  pl.BlockSpec((d, tn), lambda i, j: (0, j)),
                  pl.BlockSpec((d, tn), lambda i, j: (0, nf + j)),
                  pl.BlockSpec((CONV_WIDTH, tn), lambda i, j: (0, j)),
                  pl.BlockSpec((CONV_WIDTH, tn), lambda i, j: (0, nf + j)),
                  pl.BlockSpec((1, tn), lambda i, j: (0, j)),
                  pl.BlockSpec((1, tn), lambda i, j: (0, nf + j))],
        out_specs=pl.BlockSpec((tm, tn), lambda i, j: (i, j)),
        out_shape=jax.ShapeDtypeStruct((t, f), BF16),
        scratch_shapes=[pltpu.VMEM((CONV_HALO + tm, d), BF16)],
        compiler_params=pltpu.CompilerParams(dimension_semantics=("parallel", "arbitrary"),
                                             vmem_limit_bytes=VMEM_LIMIT),
    )(x1, x1, g.reshape(1, d), w_up, w_up, conv_w, conv_w, cb, cb)


def _ffn_down_kernel(a_ref, w_ref, x_ref, g_ref, o_ref, acc_ref, *, final_norm):
    k = pl.program_id(1)

    @pl.when(k == 0)
    def _():
        acc_ref[...] = x_ref[...]

    acc_ref[...] += jnp.dot(a_ref[...], w_ref[...], preferred_element_type=F32)

    @pl.when(k == pl.num_programs(1) - 1)
    def _():
        o_ref[...] = _rms_rows(acc_ref[...], g_ref[...]) if final_norm else acc_ref[...]


def _ffn_down(act, w_down, x1, g, final_norm, tm=512, tk=512):
    t, f = act.shape
    d = w_down.shape[1]
    return pl.pallas_call(
        functools.partial(_ffn_down_kernel, final_norm=final_norm),
        grid=(t // tm, f // tk),
        in_specs=[pl.BlockSpec((tm, tk), lambda i, k: (i, k)),
                  pl.BlockSpec((tk, d), lambda i, k: (k, 0)),
                  pl.BlockSpec((tm, d), lambda i, k: (i, 0)),
                  pl.BlockSpec((1, d), lambda i, k: (0, 0))],
        out_specs=pl.BlockSpec((tm, d), lambda i, k: (i, 0)),
        out_shape=jax.ShapeDtypeStruct((t, d), F32),
        scratch_shapes=[pltpu.VMEM((tm, d), F32)],
        compiler_params=pltpu.CompilerParams(dimension_semantics=("parallel", "arbitrary"),
                                             vmem_limit_bytes=VMEM_LIMIT),
    )(act, w_down, x1, g.reshape(1, d))


def kernel(x, norm_mix, w_in, rel_bias_table, ssm_lam_re, ssm_lam_im, ssm_log_dt, ssm_b_re, ssm_b_im, ssm_c_re, ssm_c_im, ssm_d, ssm_w_glu, ssm_b_glu, norm_attn_out, norm_ssm_out, w_out, norm_ffn, w_ffn_up, ffn_conv_w, ffn_conv_b, w_ffn_down, norm_final):
    batch, seq, d_model = x.shape
    depth = w_in.shape[0]
    aw = ATTN_HEADS * HEAD_DIM
    sw = SSM_GROUPS * SSM_GROUP_CH
    t = batch * seq
    n_blocks = seq // MOBA_BLOCK
    assert seq % MOBA_BLOCK == 0 and seq % SSM_CHUNK == 0

    bias_tiles = _bias_tiles(rel_bias_table)
    x2 = x.reshape(t, d_model)
    for l in range(depth):
        w = w_in[l]
        w_ku = jnp.concatenate([w[:, aw:2 * aw], w[:, 3 * aw:]], axis=1).astype(BF16)
        wt_qv = jnp.concatenate([w[:, :aw] * (HEAD_DIM ** -0.5 * LOG2E), w[:, 2 * aw:3 * aw]],
                                axis=1).T.astype(BF16)
        ku, qv_t = _in_proj(x2, norm_mix[l], w_ku, wt_qv)

        ku3 = ku.reshape(t // MOBA_BLOCK, MOBA_BLOCK, ku.shape[1])
        attn = _moba_attention(qv_t, ku3, bias_tiles, batch, seq)

        n_rows = t // SSM_CHUNK
        u3 = (ku[:, aw:].reshape(n_rows, SSM_CHUNK, SSM_GROUPS, SSM_GROUP_CH)
              .transpose(2, 0, 1, 3).reshape(SSM_GROUPS, n_rows, SSM_CHUNK * SSM_GROUP_CH))
        tg, bc, a_re, a_im = _s5_tables(ssm_log_dt[l], ssm_lam_re[l], ssm_lam_im[l], ssm_b_re[l],
                                        ssm_b_im[l], ssm_c_re[l], ssm_c_im[l], ssm_d[l])
        y3 = _s5_scan(u3, tg, bc, a_re, a_im, batch)
        y = (y3.reshape(SSM_GROUPS, n_rows, SSM_CHUNK, SSM_GROUP_CH)
             .transpose(1, 2, 0, 3).reshape(t, sw))

        x2 = _out_proj(x2, attn, y, ssm_w_glu[l].astype(BF16), ssm_b_glu[l], norm_attn_out[l],
                       norm_ssm_out[l], w_out[l].astype(BF16))

        act = _ffn_up(x2, norm_ffn[l], w_ffn_up[l].astype(BF16), ffn_conv_w[l], ffn_conv_b[l], seq)
        x2 = _ffn_down(act, w_ffn_down[l].astype(BF16), x2, norm_final, final_norm=(l == depth - 1))
    return x2.reshape(batch, seq, d_model)
```

```python
import functools
import math

import jax
import jax.numpy as jnp
from jax import lax
from jax.experimental import pallas as pl
from jax.experimental.pallas import tpu as pltpu

F32 = jnp.float32
BF16 = jnp.bfloat16

ATTN_HEADS = 16
HEAD_DIM = 64
SSM_GROUP_CH = 16
SSM_GROUPS = 64
SSM_STATE = 64
MOBA_BLOCK = 256
MOBA_TOP_K = 3
NUM_BUCKETS = 32
MAX_DISTANCE = 128
CONV_WIDTH = 3
RMS_EPS = 1e-6

SSM_CHUNK = 16
CONV_HALO = 16
VMEM_LIMIT = 52 * 1024 * 1024

NEG_INF = float("-inf")
LOG2E = math.log2(math.e)
V_AUG_ROWS = HEAD_DIM + 16


def _rms_rows(x, g):
    ms = jnp.mean(x * x, axis=-1, keepdims=True)
    return x * lax.rsqrt(ms + RMS_EPS) * g


_NT_DIMS = (((1,), (1,)), ((), ()))


def _in_proj_k_kernel(x_ref, g_ref, w_ref, k_ref, h_ref):
    @pl.when(pl.program_id(1) == 0)
    def _():
        h_ref[...] = _rms_rows(x_ref[...], g_ref[...]).astype(BF16)

    k_ref[...] = jnp.dot(h_ref[...], w_ref[...], preferred_element_type=F32).astype(k_ref.dtype)


def _in_proj_qv_kernel(h_ref, wt_ref, o_ref):
    res = lax.dot_general(wt_ref[...], h_ref[...], _NT_DIMS,
                          preferred_element_type=F32).astype(o_ref.dtype)
    for c in range(o_ref.shape[0]):
        o_ref[c] = res[:, c * MOBA_BLOCK:(c + 1) * MOBA_BLOCK]


def _in_proj_u_kernel(h_ref, wt_ref, o_ref):
    res = lax.dot_general(wt_ref[...], h_ref[...], _NT_DIMS, preferred_element_type=F32)
    o_ref[...] = res.reshape(o_ref.shape).astype(o_ref.dtype)


def _in_proj(x2, g, w_k, wt_qv, wt_u, tm=1024, tn=512):
    t, d = x2.shape
    n_k = w_k.shape[1]
    n_qv = wt_qv.shape[0]
    n_u = wt_u.shape[0]
    n_chunks = t // SSM_CHUNK
    k_nat, h = pl.pallas_call(
        _in_proj_k_kernel,
        grid=(t // tm, n_k // tn),
        in_specs=[pl.BlockSpec((tm, d), lambda i, j: (i, 0)),
                  pl.BlockSpec((1, d), lambda i, j: (0, 0)),
                  pl.BlockSpec((d, tn), lambda i, j: (0, j))],
        out_specs=[pl.BlockSpec((tm, tn), lambda i, j: (i, j)),
                   pl.BlockSpec((tm, d), lambda i, j: (i, 0))],
        out_shape=[jax.ShapeDtypeStruct((t, n_k), BF16), jax.ShapeDtypeStruct((t, d), BF16)],
        compiler_params=pltpu.CompilerParams(dimension_semantics=("parallel", "arbitrary"),
                                             vmem_limit_bytes=VMEM_LIMIT),
    )(x2, g.reshape(1, d), w_k)
    slabs = tm // MOBA_BLOCK
    qv_t = pl.pallas_call(
        _in_proj_qv_kernel,
        grid=(t // tm, n_qv // tn),
        in_specs=[pl.BlockSpec((tm, d), lambda i, j: (i, 0)),
                  pl.BlockSpec((tn, d), lambda i, j: (j, 0))],
        out_specs=pl.BlockSpec((slabs, tn, MOBA_BLOCK), lambda i, j: (i, j, 0)),
        out_shape=jax.ShapeDtypeStruct((t // MOBA_BLOCK, n_qv, MOBA_BLOCK), BF16),
        compiler_params=pltpu.CompilerParams(dimension_semantics=("parallel", "parallel"),
                                             vmem_limit_bytes=VMEM_LIMIT),
    )(h, wt_qv)
    u4 = pl.pallas_call(
        _in_proj_u_kernel,
        grid=(n_u // tn, SSM_CHUNK),
        in_specs=[pl.BlockSpec((n_chunks, d), lambda j, s: (0, s)),
                  pl.BlockSpec((tn, d), lambda j, s: (j, 0))],
        out_specs=pl.BlockSpec((tn // SSM_GROUP_CH, None, SSM_GROUP_CH, n_chunks),
                               lambda j, s: (j, s, 0, 0)),
        out_shape=jax.ShapeDtypeStruct((n_u // SSM_GROUP_CH, SSM_CHUNK, SSM_GROUP_CH, n_chunks), BF16),
        compiler_params=pltpu.CompilerParams(dimension_semantics=("parallel", "parallel"),
                                             vmem_limit_bytes=VMEM_LIMIT),
    )(h.reshape(n_chunks, SSM_CHUNK * d), wt_u)
    return k_nat, qv_t, u4


def _t5_bucket(dist):
    dist = jnp.maximum(dist, 0)
    max_exact = NUM_BUCKETS // 2
    log_ratio = jnp.log(jnp.maximum(dist, max_exact).astype(F32) / max_exact)
    large = max_exact + (log_ratio / math.log(MAX_DISTANCE / max_exact)
                         * (NUM_BUCKETS - max_exact)).astype(jnp.int32)
    large = jnp.minimum(large, NUM_BUCKETS - 1)
    return jnp.where(dist < max_exact, dist, large)


def _bias_tiles_kernel(tab_ref, o_ref):
    h = pl.program_id(0)
    c_far = tab_ref[NUM_BUCKETS - 1, h]
    kk = lax.broadcasted_iota(jnp.int32, (MOBA_BLOCK, MOBA_BLOCK), 0)
    qq = lax.broadcasted_iota(jnp.int32, (MOBA_BLOCK, MOBA_BLOCK), 1)
    for which, off in ((0, MOBA_BLOCK), (1, 0)):
        rel = qq - kk + off
        bucket = _t5_bucket(rel)
        acc = jnp.zeros((MOBA_BLOCK, MOBA_BLOCK), F32)
        for b in range(NUM_BUCKETS):
            acc = jnp.where(bucket == b, tab_ref[b, h], acc)
        acc = (acc - c_far) * LOG2E
        if which == 1:
            acc = jnp.where(rel >= 0, acc, NEG_INF)
        o_ref[0, which] = acc


def _bias_tiles(table):
    return pl.pallas_call(
        _bias_tiles_kernel,
        grid=(ATTN_HEADS,),
        in_specs=[pl.BlockSpec(memory_space=pltpu.SMEM)],
        out_specs=pl.BlockSpec((1, 2, MOBA_BLOCK, MOBA_BLOCK), lambda h: (h, 0, 0, 0)),
        out_shape=jax.ShapeDtypeStruct((ATTN_HEADS, 2, MOBA_BLOCK, MOBA_BLOCK), F32),
    )(table)


def _moba_kernel(q_ref, k_ref, v_ref, bias_ref, o_ref,
                 kmf_ref, vaug_ref, qm_ref, sa_ref, sb_ref, cma_ref, cmb_ref,
                 m_ref, acc_ref, rb_ref, ot_ref, *, n_blocks):
    qi = pl.program_id(2)
    blk = MOBA_BLOCK

    @pl.when(qi == 0)
    def _():
        ones = jnp.ones((V_AUG_ROWS - HEAD_DIM, blk), BF16)
        for j in range(n_blocks):
            kb = k_ref[j].astype(F32)
            kmf_ref[pl.ds(j, 1), :] = jnp.sum(kb, axis=0, keepdims=True) * (1.0 / blk)
            for hh in range(2):
                vaug_ref[j, hh, 0:HEAD_DIM, :] = v_ref[j, hh * HEAD_DIM:(hh + 1) * HEAD_DIM, :]
                vaug_ref[j, hh, HEAD_DIM:V_AUG_ROWS, :] = ones

    kmf = kmf_ref[...]
    km_hi = kmf.astype(BF16)
    km_lo = (kmf - km_hi.astype(F32)).astype(BF16)

    q2 = q_ref[0]
    q_zero = jnp.zeros((HEAD_DIM, blk), q2.dtype)
    blk_idx = lax.broadcasted_iota(jnp.int32, (n_blocks, blk), 0)

    for hh in range(2):
        vrows = slice(hh * HEAD_DIM, (hh + 1) * HEAD_DIM)
        qm = jnp.concatenate([q2[vrows], q_zero] if hh == 0 else [q_zero, q2[vrows]], axis=0)
        qm_ref[hh] = qm

        gate = (jnp.dot(km_hi, qm, preferred_element_type=F32)
                + jnp.dot(km_lo, qm, preferred_element_type=F32))
        valid = blk_idx < qi
        g = jnp.where(valid, gate, NEG_INF)
        rank = jnp.zeros(g.shape, F32)
        for j in range(n_blocks):
            gj = g[j:j + 1, :]
            beats = jnp.where(gj > g, 1.0, jnp.where((gj == g) & (blk_idx > j), 1.0, 0.0))
            rank = rank + beats
        sel = (valid & (rank < MOBA_TOP_K)) | (blk_idx == qi)
        rb_ref[hh] = jnp.where(sel, 0.0, NEG_INF)
        m_ref[hh] = jnp.full((1, blk), NEG_INF, F32)
        acc_ref[hh] = jnp.zeros((V_AUG_ROWS, blk), F32)

    def group_blocks(g):
        ja = qi - 2 * g
        return (ja, None), (jnp.maximum(ja - 1, 0), ja >= 1)

    def scores(g, s_buf, cm_buf, with_bias):
        for i, (j, _) in enumerate(group_blocks(g)):
            kb = k_ref[j]
            for hh in range(2):
                s = jnp.dot(kb, qm_ref[hh], preferred_element_type=F32)
                if with_bias:
                    s = s + bias_ref[hh, 1 - i]
                s_buf[hh, i] = s
                cm_buf[hh, pl.ds(i, 1), :] = jnp.max(s, axis=0, keepdims=True)

    def attend(g, s_buf, cm_buf):
        blocks = group_blocks(g)
        for hh in range(2):
            masks = []
            m_old = m_ref[hh]
            m_new = m_old
            for i, (j, exists) in enumerate(blocks):
                mk = rb_ref[hh, pl.ds(j, 1), :]
                if exists is not None:
                    mk = jnp.where(exists, mk, NEG_INF)
                masks.append(mk)
                m_new = jnp.maximum(m_new, cm_buf[hh, pl.ds(i, 1), :] + mk)
            acc = jnp.exp2(m_old - m_new) * acc_ref[hh]
            for i, (j, _) in enumerate(blocks):
                p = jnp.exp2(s_buf[hh, i] - m_new).astype(BF16)
                pv = jnp.dot(vaug_ref[j, hh], p, preferred_element_type=F32)
                acc = acc + jnp.where(masks[i] == 0.0, pv, 0.0)
            acc_ref[hh] = acc
            m_ref[hh] = m_new

    n_more = qi // 2
    scores(0, sa_ref, cma_ref, True)

    def step(it, carry):
        @pl.when(it % 2 == 0)
        def _():
            scores(it + 1, sb_ref, cmb_ref, False)
            attend(it, sa_ref, cma_ref)

        @pl.when(it % 2 == 1)
        def _():
            scores(it + 1, sa_ref, cma_ref, False)
            attend(it, sb_ref, cmb_ref)

        return carry

    lax.fori_loop(0, n_more, step, 0)

    @pl.when(n_more % 2 == 0)
    def _():
        attend(n_more, sa_ref, cma_ref)

    @pl.when(n_more % 2 == 1)
    def _():
        attend(n_more, sb_ref, cmb_ref)

    for hh in range(2):
        acc = acc_ref[hh]
        ot_ref[hh * HEAD_DIM:(hh + 1) * HEAD_DIM, :] = (
            acc[0:HEAD_DIM] * (1.0 / acc[HEAD_DIM:HEAD_DIM + 1]))

    o_ref[...] = ot_ref[...].T.astype(o_ref.dtype)


def _moba_attention(qv_t, ku3, bias_tiles, batch, seq):
    n_blocks = seq // MOBA_BLOCK
    n_pairs = ATTN_HEADS // 2
    pair_w = 2 * HEAD_DIM
    kernel = functools.partial(_moba_kernel, n_blocks=n_blocks)
    return pl.pallas_call(
        kernel,
        grid=(batch, n_pairs, n_blocks),
        in_specs=[
            pl.BlockSpec((1, pair_w, MOBA_BLOCK), lambda b, hp, qi: (b * n_blocks + qi, hp, 0)),
            pl.BlockSpec((n_blocks, MOBA_BLOCK, pair_w), lambda b, hp, qi: (b, 0, hp)),
            pl.BlockSpec((n_blocks, pair_w, MOBA_BLOCK), lambda b, hp, qi: (b, n_pairs + hp, 0)),
            pl.BlockSpec((2, 2, MOBA_BLOCK, MOBA_BLOCK), lambda b, hp, qi: (hp, 0, 0, 0)),
        ],
        out_specs=pl.BlockSpec((MOBA_BLOCK, pair_w), lambda b, hp, qi: (b * n_blocks + qi, hp)),
        out_shape=jax.ShapeDtypeStruct((batch * seq, ATTN_HEADS * HEAD_DIM), BF16),
        scratch_shapes=[
            pltpu.VMEM((n_blocks, pair_w), F32),
            pltpu.VMEM((n_blocks, 2, V_AUG_ROWS, MOBA_BLOCK), BF16),
            pltpu.VMEM((2, pair_w, MOBA_BLOCK), BF16),
            pltpu.VMEM((2, 2, MOBA_BLOCK, MOBA_BLOCK), F32),
            pltpu.VMEM((2, 2, MOBA_BLOCK, MOBA_BLOCK), F32),
            pltpu.VMEM((2, 8, MOBA_BLOCK), F32),
            pltpu.VMEM((2, 8, MOBA_BLOCK), F32),
            pltpu.VMEM((2, 1, MOBA_BLOCK), F32),
            pltpu.VMEM((2, V_AUG_ROWS, MOBA_BLOCK), F32),
            pltpu.VMEM((2, n_blocks, MOBA_BLOCK), F32),
            pltpu.VMEM((pair_w, MOBA_BLOCK), F32),
        ],
        compiler_params=pltpu.CompilerParams(
            dimension_semantics=("parallel", "parallel", "arbitrary")),
    )(qv_t, ku3, qv_t, bias_tiles)


def _hdot_nt(a, b):
    return lax.dot_general(a, b, _NT_DIMS, preferred_element_type=F32,
                           precision=lax.Precision.HIGHEST)


def _repeat_rows(x, n):
    return jnp.concatenate([jnp.broadcast_to(x[i:i + 1], (n, x.shape[1]))
                            for i in range(x.shape[0])], axis=0)


def _tile_rows(x, n):
    return jnp.concatenate([x] * n, axis=0)


def _s5_tables_kernel(logdt_ref, lre_ref, lim_ref, bre_ref, bim_ref, cre_ref, cim_ref, d_ref,
                      tg_ref, bn_ref, cn_ref, are_ref, aim_ref):
    lc, ch, p = SSM_CHUNK, SSM_GROUP_CH, SSM_STATE
    w = lc * ch
    rr = lax.broadcasted_iota(jnp.int32, (w, w), 0)
    cc = lax.broadcasted_iota(jnp.int32, (w, w), 1)
    causal = rr // ch >= cc // ch
    diag = rr == cc
    e_r = lax.broadcasted_iota(jnp.int32, (ch, w), 0)
    e_c = lax.broadcasted_iota(jnp.int32, (ch, w), 1)
    lane_tile = jnp.where(e_c % ch == e_r, 1.0, 0.0).astype(F32)
    tau = lax.broadcasted_iota(jnp.int32, (2 * lc, p), 0).astype(F32)

    for k in range(tg_ref.shape[0]):
        lam_re = lre_ref[k]
        lam_im = lim_ref[k]
        dt = jnp.exp(logdt_ref[k])
        lr = lam_re * dt
        li = lam_im * dt

        mag = jnp.exp(lr * tau)
        cs = jnp.cos(li * tau)
        sn = jnp.sin(li * tau)
        pos_re, pos_im = mag * cs, mag * sn
        inv = jnp.exp(-lr * tau[:lc])
        neg_re, neg_im = inv * cs[:lc], -inv * sn[:lc]

        lam1_re, lam1_im = pos_re[1:2], pos_im[1:2]
        lamk_re, lamk_im = pos_re[lc - 1:lc], pos_im[lc - 1:lc]
        are_ref[k] = pos_re[lc:lc + 1]
        aim_ref[k] = pos_im[lc:lc + 1]

        num_re, num_im = lam1_re - 1.0, lam1_im
        den = lam_re * lam_re + lam_im * lam_im
        coef_re = (num_re * lam_re + num_im * lam_im) / den
        coef_im = (num_im * lam_re - num_re * lam_im) / den
        bt_re, bt_im = bre_ref[k], bim_ref[k]
        bb_re = coef_re * bt_re - coef_im * bt_im
        bb_im = coef_re * bt_im + coef_im * bt_re

        bbt_re, bbt_im = _tile_rows(bb_re, lc), _tile_rows(bb_im, lc)
        ngx_re, ngx_im = _repeat_rows(neg_re, ch), _repeat_rows(neg_im, ch)
        bneg_re = bbt_re * ngx_re - bbt_im * ngx_im
        bneg_im = bbt_re * ngx_im + bbt_im * ngx_re

        ct_re, ct_im = _tile_rows(cre_ref[k], lc), _tile_rows(cim_ref[k], lc)
        psx_re, psx_im = _repeat_rows(pos_re[:lc], ch), _repeat_rows(pos_im[:lc], ch)
        cpos_re = ct_re * psx_re - ct_im * psx_im
        cpos_im = ct_re * psx_im + ct_im * psx_re

        raw = _hdot_nt(cpos_re, bneg_re) - _hdot_nt(cpos_im, bneg_im)
        d_lanes = jnp.dot(jnp.broadcast_to(d_ref[k], (8, ch)), lane_tile,
                          preferred_element_type=F32, precision=lax.Precision.HIGHEST)[0:1]
        tg = jnp.where(causal, raw, 0.0) + jnp.where(diag, d_lanes, 0.0)
        tg_ref[k] = tg.astype(tg_ref.dtype)

        bn_ref[k, 0] = (bneg_re * lamk_re - bneg_im * lamk_im).T.astype(bn_ref.dtype)
        bn_ref[k, 1] = (bneg_re * lamk_im + bneg_im * lamk_re).T.astype(bn_ref.dtype)
        cp_re = cpos_re * lam1_re - cpos_im * lam1_im
        cp_im = cpos_re * lam1_im + cpos_im * lam1_re
        cn_ref[k, 0] = cp_re.astype(cn_ref.dtype)
        cn_ref[k, 1] = (-cp_im).astype(cn_ref.dtype)


def _s5_tables(log_dt, lam_re, lam_im, b_re, b_im, c_re, c_im, d_skip, gb=4):
    g, p, ch = SSM_GROUPS, SSM_STATE, SSM_GROUP_CH
    w = SSM_CHUNK * ch
    row = lambda a, n: a.reshape(g, 1, n)
    spec3 = lambda s1, s2: pl.BlockSpec((gb, s1, s2), lambda i: (i, 0, 0))
    spec4 = lambda s1, s2: pl.BlockSpec((gb, 2, s1, s2), lambda i: (i, 0, 0, 0))
    return pl.pallas_call(
        _s5_tables_kernel,
        grid=(g // gb,),
        in_specs=[spec3(1, 1), spec3(1, p), spec3(1, p), spec3(ch, p), spec3(ch, p),
                  spec3(ch, p), spec3(ch, p), spec3(1, ch)],
        out_specs=[spec3(w, w), spec4(p, w), spec4(w, p), spec3(1, p), spec3(1, p)],
        out_shape=[jax.ShapeDtypeStruct((g, w, w), BF16),
                   jax.ShapeDtypeStruct((g, 2, p, w), BF16),
                   jax.ShapeDtypeStruct((g, 2, w, p), BF16),
                   jax.ShapeDtypeStruct((g, 1, p), F32),
                   jax.ShapeDtypeStruct((g, 1, p), F32)],
        compiler_params=pltpu.CompilerParams(dimension_semantics=("parallel",)),
    )(row(log_dt, 1), row(lam_re, p), row(lam_im, p),
      jnp.swapaxes(b_re, 1, 2), jnp.swapaxes(b_im, 1, 2), c_re, c_im, row(d_skip, ch))


def _s5_scan_kernel(u_ref, tg_ref, bn_ref, cn_ref, are_ref, aim_ref, y_ref,
                    sre_ref, sim_ref, xre_ref, xim_ref, *, n_batch):
    gb = u_ref.shape[0]
    n_rows = u_ref.shape[-1]
    n_chunks = n_rows // n_batch
    w = SSM_CHUNK * SSM_GROUP_CH

    for k in range(gb):
        u = u_ref[k].reshape(w, n_rows)
        rows = pl.ds(k * n_rows, n_rows)
        sre_ref[rows, :] = jnp.dot(bn_ref[k, 0], u, preferred_element_type=F32).T
        sim_ref[rows, :] = jnp.dot(bn_ref[k, 1], u, preferred_element_type=F32).T

    a_re = are_ref[...]
    a_im = aim_ref[...]

    def step(c, carry):
        new = []
        for b in range(n_batch):
            x_re, x_im = carry[2 * b], carry[2 * b + 1]
            rows = pl.ds(b * n_chunks + c, gb, stride=n_rows)
            xre_ref[rows, :] = x_re
            xim_ref[rows, :] = x_im
            new.append(a_re * x_re - a_im * x_im + sre_ref[rows, :])
            new.append(a_re * x_im + a_im * x_re + sim_ref[rows, :])
        return tuple(new)

    zero = jnp.zeros(a_re.shape, F32)
    lax.fori_loop(0, n_chunks, step, (zero,) * (2 * n_batch))

    for k in range(gb):
        rows = pl.ds(k * n_rows, n_rows)
        y = jnp.dot(tg_ref[k], u_ref[k].reshape(w, n_rows), preferred_element_type=F32)
        y = y + lax.dot_general(cn_ref[k, 0], xre_ref[rows, :].astype(BF16), _NT_DIMS,
                                preferred_element_type=F32)
        y = y + lax.dot_general(cn_ref[k, 1], xim_ref[rows, :].astype(BF16), _NT_DIMS,
                                preferred_element_type=F32)
        y_ref[k] = y.reshape(y_ref.shape[1:]).astype(y_ref.dtype)


def _s5_scan(u4, tg, bn, cn, a_re, a_im, n_batch, gb=8):
    g, lc, ch, n_rows = u4.shape
    p = SSM_STATE
    w = lc * ch
    kernel = functools.partial(_s5_scan_kernel, n_batch=n_batch)
    return pl.pallas_call(
        kernel,
        grid=(g // gb,),
        in_specs=[pl.BlockSpec((gb, lc, ch, n_rows), lambda i: (i, 0, 0, 0)),
                  pl.BlockSpec((gb, w, w), lambda i: (i, 0, 0)),
                  pl.BlockSpec((gb, 2, p, w), lambda i: (i, 0, 0, 0)),
                  pl.BlockSpec((gb, 2, w, p), lambda i: (i, 0, 0, 0)),
                  pl.BlockSpec((gb, p), lambda i: (i, 0)),
                  pl.BlockSpec((gb, p), lambda i: (i, 0))],
        out_specs=pl.BlockSpec((gb, lc, ch, n_rows), lambda i: (i, 0, 0, 0)),
        out_shape=jax.ShapeDtypeStruct((g, lc, ch, n_rows), F32),
        scratch_shapes=[pltpu.VMEM((gb * n_rows, p), F32) for _ in range(4)],
        compiler_params=pltpu.CompilerParams(dimension_semantics=("parallel",)),
    )(u4, tg, bn, cn, a_re.reshape(g, p), a_im.reshape(g, p))


def _out_proj_kernel(x_ref, a_ref, y_ref, wglut_ref, bglu_ref, ga_ref, gs_ref, wa_ref, ws_ref, o_ref):
    sw, nc = wglut_ref.shape[0], y_ref.shape[-1]
    z = jax.nn.gelu(y_ref[...].reshape(sw, nc))
    gl = jnp.dot(wglut_ref[...], z.astype(BF16), preferred_element_type=F32) + bglu_ref[...]
    s = z * jax.nn.sigmoid(gl)
    ms = jnp.mean(s * s, axis=0, keepdims=True)
    s_n = (s * lax.rsqrt(ms + RMS_EPS) * gs_ref[...]).T.astype(BF16)
    a_n = _rms_rows(a_ref[...].astype(F32), ga_ref[...]).astype(BF16)
    mix = (jnp.dot(a_n, wa_ref[...], preferred_element_type=F32)
           + jnp.dot(s_n, ws_ref[...], preferred_element_type=F32))
    o_ref[...] = x_ref[...] + mix


def _out_proj(x2, attn, y4, w_glu_t, b_glu, g_a, g_s, w_out, nc=256):
    t, d = x2.shape
    wa = attn.shape[1]
    g, lc, ch, n_chunks = y4.shape
    sw = g * ch
    tok = lambda width: pl.BlockSpec((nc, width), lambda s, c: (c, s))
    fixed = lambda s, c: (0, 0)
    once = pl.Buffered(1)
    out = pl.pallas_call(
        _out_proj_kernel,
        grid=(lc, n_chunks // nc),
        in_specs=[tok(d),
                  tok(wa),
                  pl.BlockSpec((g, None, ch, nc), lambda s, c: (0, s, 0, c)),
                  pl.BlockSpec((sw, sw), fixed, pipeline_mode=once),
                  pl.BlockSpec((sw, 1), fixed),
                  pl.BlockSpec((1, wa), fixed),
                  pl.BlockSpec((sw, 1), fixed),
                  pl.BlockSpec((wa, d), fixed, pipeline_mode=once),
                  pl.BlockSpec((sw, d), fixed, pipeline_mode=once)],
        out_specs=tok(d),
        out_shape=jax.ShapeDtypeStruct((n_chunks, lc * d), F32),
        compiler_params=pltpu.CompilerParams(dimension_semantics=("parallel", "parallel"),
                                             vmem_limit_bytes=VMEM_LIMIT),
    )(x2.reshape(n_chunks, lc * d), attn.reshape(n_chunks, lc * wa), y4, w_glu_t,
      b_glu.reshape(sw, 1), g_a.reshape(1, wa), g_s.reshape(sw, 1), w_out[:wa], w_out[wa:])
    return out.reshape(t, d)


def _ffn_up_kernel(x_ref, halo_ref, g_ref, wg_ref, wv_ref, cwg_ref, cwv_ref, cbg_ref, cbv_ref,
                   o_ref, h_ref, *, tiles_per_seq):
    tm = x_ref.shape[0]

    @pl.when(pl.program_id(1) == 0)
    def _():
        keep = jnp.where(pl.program_id(0) % tiles_per_seq == 0, 0.0, 1.0)
        h_ref[pl.ds(0, CONV_HALO), :] = (_rms_rows(halo_ref[...], g_ref[...]) * keep).astype(BF16)
        h_ref[pl.ds(CONV_HALO, tm), :] = _rms_rows(x_ref[...], g_ref[...]).astype(BF16)

    h = h_ref[...]

    def conv(w_ref, cw_ref, cb_ref):
        up = jnp.dot(h, w_ref[...], preferred_element_type=F32)
        cw = cw_ref[...]
        out = up[CONV_HALO:] * cw[CONV_WIDTH - 1:CONV_WIDTH] + cb_ref[...]
        for j in range(CONV_WIDTH - 1):
            lag = CONV_WIDTH - 1 - j
            out = out + up[CONV_HALO - lag:CONV_HALO - lag + tm] * cw[j:j + 1]
        return out

    gate = conv(wg_ref, cwg_ref, cbg_ref)
    val = conv(wv_ref, cwv_ref, cbv_ref)
    o_ref[...] = (jax.nn.silu(gate) * val).astype(o_ref.dtype)


def _ffn_up(x1, g, w_up, conv_w, conv_b, seq, tm=1024, tn=512):
    t, d = x1.shape
    f = w_up.shape[1] // 2
    nf = f // tn
    tiles_per_seq = seq // tm
    halo_blocks = tm // CONV_HALO
    kernel = functools.partial(_ffn_up_kernel, tiles_per_seq=tiles_per_seq)
    cb = conv_b.reshape(1, 2 * f)
    return pl.pallas_call(
        kernel,
        grid=(t // tm, nf),
        in_specs=[pl.BlockSpec((tm, d), lambda i, j: (i, 0)),
                  pl.BlockSpec((CONV_HALO, d), lambda i, j: (jnp.maximum(i * halo_blocks - 1, 0), 0)),
                  pl.BlockSpec((1, d), lambda i, j: (0, 0)),
                  pl.BlockSpec((d, tn), lambda i, j: (0, j)),
                  pl.BlockSpec((d, tn), lambda i, j: (0, nf + j)),
                  pl.BlockSpec((CONV_WIDTH, tn), lambda i, j: (0, j)),
                  pl.BlockSpec((CONV_WIDTH, tn), lambda i, j: (0, nf + j)),
                  pl.BlockSpec((1, tn), lambda i, j: (0, j)),
                  pl.BlockSpec((1, tn), lambda i, j: (0, nf + j))],
        out_specs=pl.BlockSpec((tm, tn), lambda i, j: (i, j)),
        out_shape=jax.ShapeDtypeStruct((t, f), BF16),
        scratch_shapes=[pltpu.VMEM((CONV_HALO + tm, d), BF16)],
        compiler_params=pltpu.CompilerParams(dimension_semantics=("parallel", "arbitrary"),
                                             vmem_limit_bytes=VMEM_LIMIT),
    )(x1, x1, g.reshape(1, d), w_up, w_up, conv_w, conv_w, cb, cb)


def _ffn_down_kernel(a_ref, w_ref, x_ref, g_ref, o_ref, acc_ref, *, final_norm):
    k = pl.program_id(1)

    @pl.when(k == 0)
    def _():
        acc_ref[...] = x_ref[...]

    acc_ref[...] += jnp.dot(a_ref[...], w_ref[...], preferred_element_type=F32)

    @pl.when(k == pl.num_programs(1) - 1)
    def _():
        o_ref[...] = _rms_rows(acc_ref[...], g_ref[...]) if final_norm else acc_ref[...]


def _ffn_down(act, w_down, x1, g, final_norm, tm=512, tk=512):
    t, f = act.shape
    d = w_down.shape[1]
    return pl.pallas_call(
        functools.partial(_ffn_down_kernel, final_norm=final_norm),
        grid=(t // tm, f // tk),
        in_specs=[pl.BlockSpec((tm, tk), lambda i, k: (i, k)),
                  pl.BlockSpec((tk, d), lambda i, k: (k, 0)),
                  pl.BlockSpec((tm, d), lambda i, k: (i, 0)),
                  pl.BlockSpec((1, d), lambda i, k: (0, 0))],
        out_specs=pl.BlockSpec((tm, d), lambda i, k: (i, 0)),
        out_shape=jax.ShapeDtypeStruct((t, d), F32),
        scratch_shapes=[pltpu.VMEM((tm, d), F32)],
        compiler_params=pltpu.CompilerParams(dimension_semantics=("parallel", "arbitrary"),
                                             vmem_limit_bytes=VMEM_LIMIT),
    )(act, w_down, x1, g.reshape(1, d))


def kernel(x, norm_mix, w_in, rel_bias_table, ssm_lam_re, ssm_lam_im, ssm_log_dt, ssm_b_re, ssm_b_im, ssm_c_re, ssm_c_im, ssm_d, ssm_w_glu, ssm_b_glu, norm_attn_out, norm_ssm_out, w_out, norm_ffn, w_ffn_up, ffn_conv_w, ffn_conv_b, w_ffn_down, norm_final):
    batch, seq, d_model = x.shape
    depth = w_in.shape[0]
    aw = ATTN_HEADS * HEAD_DIM
    sw = SSM_GROUPS * SSM_GROUP_CH
    t = batch * seq
    n_blocks = seq // MOBA_BLOCK
    assert seq % MOBA_BLOCK == 0 and seq % SSM_CHUNK == 0

    bias_tiles = _bias_tiles(rel_bias_table)
    x2 = x.reshape(t, d_model)
    for l in range(depth):
        w = w_in[l]
        w_k = w[:, aw:2 * aw].astype(BF16)
        wt_qv = jnp.concatenate([w[:, :aw] * (HEAD_DIM ** -0.5 * LOG2E), w[:, 2 * aw:3 * aw]],
                                axis=1).T.astype(BF16)
        wt_u = w[:, 3 * aw:].T.astype(BF16)
        k_nat, qv_t, u4 = _in_proj(x2, norm_mix[l], w_k, wt_qv, wt_u)

        k3 = k_nat.reshape(t // MOBA_BLOCK, MOBA_BLOCK, aw)
        attn = _moba_attention(qv_t, k3, bias_tiles, batch, seq)

        tg, bn, cn, a_re, a_im = _s5_tables(ssm_log_dt[l], ssm_lam_re[l], ssm_lam_im[l],
                                            ssm_b_re[l], ssm_b_im[l], ssm_c_re[l], ssm_c_im[l],
                                            ssm_d[l])
        y4 = _s5_scan(u4, tg, bn, cn, a_re, a_im, batch)

        x2 = _out_proj(x2, attn, y4, ssm_w_glu[l].T.astype(BF16), ssm_b_glu[l], norm_attn_out[l],
                       norm_ssm_out[l], w_out[l].astype(BF16))

        act = _ffn_up(x2, norm_ffn[l], w_ffn_up[l].astype(BF16), ffn_conv_w[l], ffn_conv_b[l], seq)
        x2 = _ffn_down(act, w_ffn_down[l].astype(BF16), x2, norm_final, final_norm=(l == depth - 1))
    return x2.reshape(batch, seq, d_model)
```

```python
import functools
import math

import jax
import jax.numpy as jnp
from jax import lax
from jax.experimental import pallas as pl
from jax.experimental.pallas import tpu as pltpu

F32 = jnp.float32
BF16 = jnp.bfloat16

ATTN_HEADS = 16
HEAD_DIM = 64
SSM_GROUP_CH = 16
SSM_GROUPS = 64
SSM_STATE = 64
MOBA_BLOCK = 256
MOBA_TOP_K = 3
NUM_BUCKETS = 32
MAX_DISTANCE = 128
CONV_WIDTH = 3
RMS_EPS = 1e-6

SSM_CHUNK = 16
CONV_HALO = 16
VMEM_LIMIT = 52 * 1024 * 1024

NEG_INF = float("-inf")
LOG2E = math.log2(math.e)
V_AUG_ROWS = HEAD_DIM + 16


def _rms_rows(x, g):
    ms = jnp.mean(x * x, axis=-1, keepdims=True)
    return x * lax.rsqrt(ms + RMS_EPS) * g


_NT_DIMS = (((1,), (1,)), ((), ()))


def _in_proj_k_kernel(x_ref, g_ref, w_ref, k_ref, h_ref):
    @pl.when(pl.program_id(1) == 0)
    def _():
        h_ref[...] = _rms_rows(x_ref[...], g_ref[...]).astype(BF16)

    k_ref[...] = jnp.dot(h_ref[...], w_ref[...], preferred_element_type=F32).astype(k_ref.dtype)


def _in_proj_qv_kernel(h_ref, wt_ref, o_ref):
    res = lax.dot_general(wt_ref[...], h_ref[...], _NT_DIMS,
                          preferred_element_type=F32).astype(o_ref.dtype)
    for c in range(o_ref.shape[0]):
        o_ref[c] = res[:, c * MOBA_BLOCK:(c + 1) * MOBA_BLOCK]


def _in_proj(x2, g, w_k, wt_qv, tm=1024, tn=512):
    t, d = x2.shape
    n_k = w_k.shape[1]
    n_qv = wt_qv.shape[0]
    k_nat, h = pl.pallas_call(
        _in_proj_k_kernel,
        grid=(t // tm, n_k // tn),
        in_specs=[pl.BlockSpec((tm, d), lambda i, j: (i, 0)),
                  pl.BlockSpec((1, d), lambda i, j: (0, 0)),
                  pl.BlockSpec((d, tn), lambda i, j: (0, j))],
        out_specs=[pl.BlockSpec((tm, tn), lambda i, j: (i, j)),
                   pl.BlockSpec((tm, d), lambda i, j: (i, 0))],
        out_shape=[jax.ShapeDtypeStruct((t, n_k), BF16), jax.ShapeDtypeStruct((t, d), BF16)],
        compiler_params=pltpu.CompilerParams(dimension_semantics=("parallel", "arbitrary"),
                                             vmem_limit_bytes=VMEM_LIMIT),
    )(x2, g.reshape(1, d), w_k)
    slabs = tm // MOBA_BLOCK
    qv_t = pl.pallas_call(
        _in_proj_qv_kernel,
        grid=(t // tm, n_qv // tn),
        in_specs=[pl.BlockSpec((tm, d), lambda i, j: (i, 0)),
                  pl.BlockSpec((tn, d), lambda i, j: (j, 0))],
        out_specs=pl.BlockSpec((slabs, tn, MOBA_BLOCK), lambda i, j: (i, j, 0)),
        out_shape=jax.ShapeDtypeStruct((t // MOBA_BLOCK, n_qv, MOBA_BLOCK), BF16),
        compiler_params=pltpu.CompilerParams(dimension_semantics=("parallel", "parallel"),
                                             vmem_limit_bytes=VMEM_LIMIT),
    )(h, wt_qv)
    return k_nat, qv_t


def _t5_bucket(dist):
    dist = jnp.maximum(dist, 0)
    max_exact = NUM_BUCKETS // 2
    log_ratio = jnp.log(jnp.maximum(dist, max_exact).astype(F32) / max_exact)
    large = max_exact + (log_ratio / math.log(MAX_DISTANCE / max_exact)
                         * (NUM_BUCKETS - max_exact)).astype(jnp.int32)
    large = jnp.minimum(large, NUM_BUCKETS - 1)
    return jnp.where(dist < max_exact, dist, large)


def _bias_tiles_kernel(tab_ref, o_ref):
    h = pl.program_id(0)
    c_far = tab_ref[NUM_BUCKETS - 1, h]
    kk = lax.broadcasted_iota(jnp.int32, (MOBA_BLOCK, MOBA_BLOCK), 0)
    qq = lax.broadcasted_iota(jnp.int32, (MOBA_BLOCK, MOBA_BLOCK), 1)
    for which, off in ((0, MOBA_BLOCK), (1, 0)):
        rel = qq - kk + off
        bucket = _t5_bucket(rel)
        acc = jnp.zeros((MOBA_BLOCK, MOBA_BLOCK), F32)
        for b in range(NUM_BUCKETS):
            acc = jnp.where(bucket == b, tab_ref[b, h], acc)
        acc = (acc - c_far) * LOG2E
        if which == 1:
            acc = jnp.where(rel >= 0, acc, NEG_INF)
        o_ref[0, which] = acc


def _bias_tiles(table):
    return pl.pallas_call(
        _bias_tiles_kernel,
        grid=(ATTN_HEADS,),
        in_specs=[pl.BlockSpec(memory_space=pltpu.SMEM)],
        out_specs=pl.BlockSpec((1, 2, MOBA_BLOCK, MOBA_BLOCK), lambda h: (h, 0, 0, 0)),
        out_shape=jax.ShapeDtypeStruct((ATTN_HEADS, 2, MOBA_BLOCK, MOBA_BLOCK), F32),
    )(table)


def _moba_kernel(q_ref, k_ref, v_ref, bias_ref, o_ref,
                 kmf_ref, vaug_ref, qm_ref, sa_ref, sb_ref, cma_ref, cmb_ref,
                 m_ref, acc_ref, rb_ref, ot_ref, *, n_blocks):
    qi = pl.program_id(2)
    blk = MOBA_BLOCK

    @pl.when(qi == 0)
    def _():
        ones = jnp.ones((V_AUG_ROWS - HEAD_DIM, blk), BF16)
        for j in range(n_blocks):
            kb = k_ref[j].astype(F32)
            kmf_ref[pl.ds(j, 1), :] = jnp.sum(kb, axis=0, keepdims=True) * (1.0 / blk)
            for hh in range(2):
                vaug_ref[j, hh, 0:HEAD_DIM, :] = v_ref[j, hh * HEAD_DIM:(hh + 1) * HEAD_DIM, :]
                vaug_ref[j, hh, HEAD_DIM:V_AUG_ROWS, :] = ones

    kmf = kmf_ref[...]
    km_hi = kmf.astype(BF16)
    km_lo = (kmf - km_hi.astype(F32)).astype(BF16)

    q2 = q_ref[0]
    q_zero = jnp.zeros((HEAD_DIM, blk), q2.dtype)
    blk_idx = lax.broadcasted_iota(jnp.int32, (n_blocks, blk), 0)

    for hh in range(2):
        vrows = slice(hh * HEAD_DIM, (hh + 1) * HEAD_DIM)
        qm = jnp.concatenate([q2[vrows], q_zero] if hh == 0 else [q_zero, q2[vrows]], axis=0)
        qm_ref[hh] = qm

        gate = (jnp.dot(km_hi, qm, preferred_element_type=F32)
                + jnp.dot(km_lo, qm, preferred_element_type=F32))
        valid = blk_idx < qi
        g = jnp.where(valid, gate, NEG_INF)
        rank = jnp.zeros(g.shape, F32)
        for j in range(n_blocks):
            gj = g[j:j + 1, :]
            beats = jnp.where(gj > g, 1.0, jnp.where((gj == g) & (blk_idx > j), 1.0, 0.0))
            rank = rank + beats
        sel = (valid & (rank < MOBA_TOP_K)) | (blk_idx == qi)
        rb_ref[hh] = jnp.where(sel, 0.0, NEG_INF)
        m_ref[hh] = jnp.full((1, blk), NEG_INF, F32)
        acc_ref[hh] = jnp.zeros((V_AUG_ROWS, blk), F32)

    def group_blocks(g):
        ja = qi - 2 * g
        return (ja, None), (jnp.maximum(ja - 1, 0), ja >= 1)

    def scores(g, s_buf, cm_buf, with_bias):
        for i, (j, _) in enumerate(group_blocks(g)):
            kb = k_ref[j]
            for hh in range(2):
                s = jnp.dot(kb, qm_ref[hh], preferred_element_type=F32)
                if with_bias:
                    s = s + bias_ref[hh, 1 - i]
                s_buf[hh, i] = s
                cm_buf[hh, pl.ds(i, 1), :] = jnp.max(s, axis=0, keepdims=True)

    def attend(g, s_buf, cm_buf):
        blocks = group_blocks(g)
        for hh in range(2):
            masks = []
            m_old = m_ref[hh]
            m_new = m_old
            for i, (j, exists) in enumerate(blocks):
                mk = rb_ref[hh, pl.ds(j, 1), :]
                if exists is not None:
                    mk = jnp.where(exists, mk, NEG_INF)
                masks.append(mk)
                m_new = jnp.maximum(m_new, cm_buf[hh, pl.ds(i, 1), :] + mk)
            acc = jnp.exp2(m_old - m_new) * acc_ref[hh]
            for i, (j, _) in enumerate(blocks):
                p = jnp.exp2(s_buf[hh, i] - m_new).astype(BF16)
                pv = jnp.dot(vaug_ref[j, hh], p, preferred_element_type=F32)
                acc = acc + jnp.where(masks[i] == 0.0, pv, 0.0)
            acc_ref[hh] = acc
            m_ref[hh] = m_new

    n_more = qi // 2
    scores(0, sa_ref, cma_ref, True)

    def step(it, carry):
        @pl.when(it % 2 == 0)
        def _():
            scores(it + 1, sb_ref, cmb_ref, False)
            attend(it, sa_ref, cma_ref)

        @pl.when(it % 2 == 1)
        def _():
            scores(it + 1, sa_ref, cma_ref, False)
            attend(it, sb_ref, cmb_ref)

        return carry

    lax.fori_loop(0, n_more, step, 0)

    @pl.when(n_more % 2 == 0)
    def _():
        attend(n_more, sa_ref, cma_ref)

    @pl.when(n_more % 2 == 1)
    def _():
        attend(n_more, sb_ref, cmb_ref)

    for hh in range(2):
        acc = acc_ref[hh]
        ot_ref[hh * HEAD_DIM:(hh + 1) * HEAD_DIM, :] = (
            acc[0:HEAD_DIM] * (1.0 / acc[HEAD_DIM:HEAD_DIM + 1]))

    o_ref[...] = ot_ref[...].T.astype(o_ref.dtype)


def _moba_attention(qv_t, ku3, bias_tiles, batch, seq):
    n_blocks = seq // MOBA_BLOCK
    n_pairs = ATTN_HEADS // 2
    pair_w = 2 * HEAD_DIM
    kernel = functools.partial(_moba_kernel, n_blocks=n_blocks)
    return pl.pallas_call(
        kernel,
        grid=(batch, n_pairs, n_blocks),
        in_specs=[
            pl.BlockSpec((1, pair_w, MOBA_BLOCK), lambda b, hp, qi: (b * n_blocks + qi, hp, 0)),
            pl.BlockSpec((n_blocks, MOBA_BLOCK, pair_w), lambda b, hp, qi: (b, 0, hp)),
            pl.BlockSpec((n_blocks, pair_w, MOBA_BLOCK), lambda b, hp, qi: (b, n_pairs + hp, 0)),
            pl.BlockSpec((2, 2, MOBA_BLOCK, MOBA_BLOCK), lambda b, hp, qi: (hp, 0, 0, 0)),
        ],
        out_specs=pl.BlockSpec((MOBA_BLOCK, pair_w), lambda b, hp, qi: (b * n_blocks + qi, hp)),
        out_shape=jax.ShapeDtypeStruct((batch * seq, ATTN_HEADS * HEAD_DIM), BF16),
        scratch_shapes=[
            pltpu.VMEM((n_blocks, pair_w), F32),
            pltpu.VMEM((n_blocks, 2, V_AUG_ROWS, MOBA_BLOCK), BF16),
            pltpu.VMEM((2, pair_w, MOBA_BLOCK), BF16),
            pltpu.VMEM((2, 2, MOBA_BLOCK, MOBA_BLOCK), F32),
            pltpu.VMEM((2, 2, MOBA_BLOCK, MOBA_BLOCK), F32),
            pltpu.VMEM((2, 8, MOBA_BLOCK), F32),
            pltpu.VMEM((2, 8, MOBA_BLOCK), F32),
            pltpu.VMEM((2, 1, MOBA_BLOCK), F32),
            pltpu.VMEM((2, V_AUG_ROWS, MOBA_BLOCK), F32),
            pltpu.VMEM((2, n_blocks, MOBA_BLOCK), F32),
            pltpu.VMEM((pair_w, MOBA_BLOCK), F32),
        ],
        compiler_params=pltpu.CompilerParams(
            dimension_semantics=("parallel", "parallel", "arbitrary")),
    )(qv_t, ku3, qv_t, bias_tiles)


def _hdot_nt(a, b):
    return lax.dot_general(a, b, _NT_DIMS, preferred_element_type=F32,
                           precision=lax.Precision.HIGHEST)


def _repeat_rows(x, n):
    return jnp.concatenate([jnp.broadcast_to(x[i:i + 1], (n, x.shape[1]))
                            for i in range(x.shape[0])], axis=0)


def _tile_rows(x, n):
    return jnp.concatenate([x] * n, axis=0)


def _s5_tables_kernel(logdt_ref, lre_ref, lim_ref, bre_ref, bim_ref, cre_ref, cim_ref, d_ref,
                      tg_ref, bn_ref, cn_ref, are_ref, aim_ref):
    lc, ch, p = SSM_CHUNK, SSM_GROUP_CH, SSM_STATE
    w = lc * ch
    rr = lax.broadcasted_iota(jnp.int32, (w, w), 0)
    cc = lax.broadcasted_iota(jnp.int32, (w, w), 1)
    causal = rr // ch >= cc // ch
    diag = rr == cc
    e_r = lax.broadcasted_iota(jnp.int32, (ch, w), 0)
    e_c = lax.broadcasted_iota(jnp.int32, (ch, w), 1)
    lane_tile = jnp.where(e_c % ch == e_r, 1.0, 0.0).astype(F32)
    tau = lax.broadcasted_iota(jnp.int32, (2 * lc, p), 0).astype(F32)

    for k in range(tg_ref.shape[0]):
        lam_re = lre_ref[k]
        lam_im = lim_ref[k]
        dt = jnp.exp(logdt_ref[k])
        lr = lam_re * dt
        li = lam_im * dt

        mag = jnp.exp(lr * tau)
        cs = jnp.cos(li * tau)
        sn = jnp.sin(li * tau)
        pos_re, pos_im = mag * cs, mag * sn
        inv = jnp.exp(-lr * tau[:lc])
        neg_re, neg_im = inv * cs[:lc], -inv * sn[:lc]

        lam1_re, lam1_im = pos_re[1:2], pos_im[1:2]
        lamk_re, lamk_im = pos_re[lc - 1:lc], pos_im[lc - 1:lc]
        are_ref[k] = pos_re[lc:lc + 1]
        aim_ref[k] = pos_im[lc:lc + 1]

        num_re, num_im = lam1_re - 1.0, lam1_im
        den = lam_re * lam_re + lam_im * lam_im
        coef_re = (num_re * lam_re + num_im * lam_im) / den
        coef_im = (num_im * lam_re - num_re * lam_im) / den
        bt_re, bt_im = bre_ref[k], bim_ref[k]
        bb_re = coef_re * bt_re - coef_im * bt_im
        bb_im = coef_re * bt_im + coef_im * bt_re

        bbt_re, bbt_im = _tile_rows(bb_re, lc), _tile_rows(bb_im, lc)
        ngx_re, ngx_im = _repeat_rows(neg_re, ch), _repeat_rows(neg_im, ch)
        bneg_re = bbt_re * ngx_re - bbt_im * ngx_im
        bneg_im = bbt_re * ngx_im + bbt_im * ngx_re

        ct_re, ct_im = _tile_rows(cre_ref[k], lc), _tile_rows(cim_ref[k], lc)
        psx_re, psx_im = _repeat_rows(pos_re[:lc], ch), _repeat_rows(pos_im[:lc], ch)
        cpos_re = ct_re * psx_re - ct_im * psx_im
        cpos_im = ct_re * psx_im + ct_im * psx_re

        raw = _hdot_nt(cpos_re, bneg_re) - _hdot_nt(cpos_im, bneg_im)
        d_lanes = jnp.dot(jnp.broadcast_to(d_ref[k], (8, ch)), lane_tile,
                          preferred_element_type=F32, precision=lax.Precision.HIGHEST)[0:1]
        tg = jnp.where(causal, raw, 0.0) + jnp.where(diag, d_lanes, 0.0)
        tg_ref[k] = tg.astype(tg_ref.dtype)

        bn_ref[k, 0] = (bneg_re * lamk_re - bneg_im * lamk_im).T.astype(bn_ref.dtype)
        bn_ref[k, 1] = (bneg_re * lamk_im + bneg_im * lamk_re).T.astype(bn_ref.dtype)
        cp_re = cpos_re * lam1_re - cpos_im * lam1_im
        cp_im = cpos_re * lam1_im + cpos_im * lam1_re
        cn_ref[k, 0] = cp_re.astype(cn_ref.dtype)
        cn_ref[k, 1] = (-cp_im).astype(cn_ref.dtype)


def _s5_tables(log_dt, lam_re, lam_im, b_re, b_im, c_re, c_im, d_skip, gb=4):
    g, p, ch = SSM_GROUPS, SSM_STATE, SSM_GROUP_CH
    w = SSM_CHUNK * ch
    row = lambda a, n: a.reshape(g, 1, n)
    spec3 = lambda s1, s2: pl.BlockSpec((gb, s1, s2), lambda i: (i, 0, 0))
    spec4 = lambda s1, s2: pl.BlockSpec((gb, 2, s1, s2), lambda i: (i, 0, 0, 0))
    return pl.pallas_call(
        _s5_tables_kernel,
        grid=(g // gb,),
        in_specs=[spec3(1, 1), spec3(1, p), spec3(1, p), spec3(ch, p), spec3(ch, p),
                  spec3(ch, p), spec3(ch, p), spec3(1, ch)],
        out_specs=[spec3(w, w), spec4(p, w), spec4(w, p), spec3(1, p), spec3(1, p)],
        out_shape=[jax.ShapeDtypeStruct((g, w, w), BF16),
                   jax.ShapeDtypeStruct((g, 2, p, w), BF16),
                   jax.ShapeDtypeStruct((g, 2, w, p), BF16),
                   jax.ShapeDtypeStruct((g, 1, p), F32),
                   jax.ShapeDtypeStruct((g, 1, p), F32)],
        compiler_params=pltpu.CompilerParams(dimension_semantics=("parallel",)),
    )(row(log_dt, 1), row(lam_re, p), row(lam_im, p),
      jnp.swapaxes(b_re, 1, 2), jnp.swapaxes(b_im, 1, 2), c_re, c_im, row(d_skip, ch))


def _s5_scan_kernel(u_ref, tg_ref, bn_ref, cn_ref, are_ref, aim_ref, y_ref,
                    uf_ref, v_ref, yt_ref, sre_ref, sim_ref, xre_ref, xim_ref, *, n_batch):
    lc, ch = SSM_CHUNK, SSM_GROUP_CH
    gb = tg_ref.shape[0]
    n_rows = u_ref.shape[0] // lc
    n_chunks = n_rows // n_batch
    w = lc * ch

    uf_ref[...] = u_ref[...].astype(F32)
    for s in range(lc):
        ust = uf_ref[pl.ds(s, n_rows, stride=lc), :].T
        for k in range(gb):
            v_ref[k, s * ch:(s + 1) * ch, :] = ust[k * ch:(k + 1) * ch, :].astype(v_ref.dtype)

    for k in range(gb):
        u = v_ref[k]
        rows = pl.ds(k * n_rows, n_rows)
        sre_ref[rows, :] = jnp.dot(bn_ref[k, 0], u, preferred_element_type=F32).T
        sim_ref[rows, :] = jnp.dot(bn_ref[k, 1], u, preferred_element_type=F32).T

    a_re = are_ref[...]
    a_im = aim_ref[...]

    def step(c, carry):
        new = []
        for b in range(n_batch):
            x_re, x_im = carry[2 * b], carry[2 * b + 1]
            rows = pl.ds(b * n_chunks + c, gb, stride=n_rows)
            xre_ref[rows, :] = x_re
            xim_ref[rows, :] = x_im
            new.append(a_re * x_re - a_im * x_im + sre_ref[rows, :])
            new.append(a_re * x_im + a_im * x_re + sim_ref[rows, :])
        return tuple(new)

    zero = jnp.zeros(a_re.shape, F32)
    lax.fori_loop(0, n_chunks, step, (zero,) * (2 * n_batch))

    for k in range(gb):
        rows = pl.ds(k * n_rows, n_rows)
        y = jnp.dot(tg_ref[k], v_ref[k], preferred_element_type=F32)
        y = y + lax.dot_general(cn_ref[k, 0], xre_ref[rows, :].astype(BF16), _NT_DIMS,
                                preferred_element_type=F32)
        y = y + lax.dot_general(cn_ref[k, 1], xim_ref[rows, :].astype(BF16), _NT_DIMS,
                                preferred_element_type=F32)
        for s in range(lc):
            yt_ref[s, k * ch:(k + 1) * ch, :] = y[s * ch:(s + 1) * ch, :]

    for s in range(lc):
        y_ref[pl.ds(s, n_rows, stride=lc), :] = yt_ref[s].T.astype(y_ref.dtype)


def _s5_scan(ku, tg, bn, cn, a_re, a_im, n_batch, u_col0, gb=8):
    t = ku.shape[0]
    g, w, _ = tg.shape
    p, ch = SSM_STATE, SSM_GROUP_CH
    n_rows = t // SSM_CHUNK
    lanes = gb * ch
    kernel = functools.partial(_s5_scan_kernel, n_batch=n_batch)
    return pl.pallas_call(
        kernel,
        grid=(g // gb,),
        in_specs=[pl.BlockSpec((t, lanes), lambda i: (0, u_col0 // lanes + i)),
                  pl.BlockSpec((gb, w, w), lambda i: (i, 0, 0)),
                  pl.BlockSpec((gb, 2, p, w), lambda i: (i, 0, 0, 0)),
                  pl.BlockSpec((gb, 2, w, p), lambda i: (i, 0, 0, 0)),
                  pl.BlockSpec((gb, p), lambda i: (i, 0)),
                  pl.BlockSpec((gb, p), lambda i: (i, 0))],
        out_specs=pl.BlockSpec((t, lanes), lambda i: (0, i)),
        out_shape=jax.ShapeDtypeStruct((t, g * ch), F32),
        scratch_shapes=[pltpu.VMEM((t, lanes), F32),
                        pltpu.VMEM((gb, w, n_rows), BF16),
                        pltpu.VMEM((SSM_CHUNK, lanes, n_rows), F32)]
                       + [pltpu.VMEM((gb * n_rows, p), F32) for _ in range(4)],
        compiler_params=pltpu.CompilerParams(dimension_semantics=("parallel",),
                                             vmem_limit_bytes=VMEM_LIMIT),
    )(ku, tg, bn, cn, a_re.reshape(g, p), a_im.reshape(g, p))


def _out_proj_kernel(x_ref, a_ref, y_ref, wglu_ref, bglu_ref, ga_ref, gs_ref, wa_ref, ws_ref, o_ref):
    z = jax.nn.gelu(y_ref[...])
    gl = jnp.dot(z.astype(BF16), wglu_ref[...], preferred_element_type=F32) + bglu_ref[...]
    s = z * jax.nn.sigmoid(gl)
    a_n = _rms_rows(a_ref[...].astype(F32), ga_ref[...]).astype(BF16)
    s_n = _rms_rows(s, gs_ref[...]).astype(BF16)
    mix = (jnp.dot(a_n, wa_ref[...], preferred_element_type=F32)
           + jnp.dot(s_n, ws_ref[...], preferred_element_type=F32))
    o_ref[...] = x_ref[...] + mix


def _out_proj(x2, attn, y, w_glu, b_glu, g_a, g_s, w_out, tm=256):
    t, d = x2.shape
    wa = attn.shape[1]
    ws = y.shape[1]
    row = lambda i: (i, 0)
    fixed = lambda i: (0, 0)
    once = pl.Buffered(1)
    return pl.pallas_call(
        _out_proj_kernel,
        grid=(t // tm,),
        in_specs=[pl.BlockSpec((tm, d), row),
                  pl.BlockSpec((tm, wa), row),
                  pl.BlockSpec((tm, ws), row),
                  pl.BlockSpec((ws, ws), fixed, pipeline_mode=once),
                  pl.BlockSpec((1, ws), fixed),
                  pl.BlockSpec((1, wa), fixed),
                  pl.BlockSpec((1, ws), fixed),
                  pl.BlockSpec((wa, d), fixed, pipeline_mode=once),
                  pl.BlockSpec((ws, d), fixed, pipeline_mode=once)],
        out_specs=pl.BlockSpec((tm, d), row),
        out_shape=jax.ShapeDtypeStruct((t, d), F32),
        compiler_params=pltpu.CompilerParams(dimension_semantics=("parallel",),
                                             vmem_limit_bytes=VMEM_LIMIT),
    )(x2, attn, y, w_glu, b_glu.reshape(1, ws), g_a.reshape(1, wa), g_s.reshape(1, ws),
      w_out[:wa], w_out[wa:])


def _ffn_up_kernel(x_ref, halo_ref, g_ref, wg_ref, wv_ref, cwg_ref, cwv_ref, cbg_ref, cbv_ref,
                   o_ref, h_ref, *, tiles_per_seq):
    tm = x_ref.shape[0]

    @pl.when(pl.program_id(1) == 0)
    def _():
        keep = jnp.where(pl.program_id(0) % tiles_per_seq == 0, 0.0, 1.0)
        h_ref[pl.ds(0, CONV_HALO), :] = (_rms_rows(halo_ref[...], g_ref[...]) * keep).astype(BF16)
        h_ref[pl.ds(CONV_HALO, tm), :] = _rms_rows(x_ref[...], g_ref[...]).astype(BF16)

    h = h_ref[...]

    def conv(w_ref, cw_ref, cb_ref):
        up = jnp.dot(h, w_ref[...], preferred_element_type=F32)
        cw = cw_ref[...]
        out = up[CONV_HALO:] * cw[CONV_WIDTH - 1:CONV_WIDTH] + cb_ref[...]
        for j in range(CONV_WIDTH - 1):
            lag = CONV_WIDTH - 1 - j
            out = out + up[CONV_HALO - lag:CONV_HALO - lag + tm] * cw[j:j + 1]
        return out

    gate = conv(wg_ref, cwg_ref, cbg_ref)
    val = conv(wv_ref, cwv_ref, cbv_ref)
    o_ref[...] = (jax.nn.silu(gate) * val).astype(o_ref.dtype)


def _ffn_up(x1, g, w_up, conv_w, conv_b, seq, tm=1024, tn=512):
    t, d = x1.shape
    f = w_up.shape[1] // 2
    nf = f // tn
    tiles_per_seq = seq // tm
    halo_blocks = tm // CONV_HALO
    kernel = functools.partial(_ffn_up_kernel, tiles_per_seq=tiles_per_seq)
    cb = conv_b.reshape(1, 2 * f)
    return pl.pallas_call(
        kernel,
        grid=(t // tm, nf),
        in_specs=[pl.BlockSpec((tm, d), lambda i, j: (i, 0)),
                  pl.BlockSpec((CONV_HALO, d), lambda i, j: (jnp.maximum(i * halo_blocks - 1, 0), 0)),
                  pl.BlockSpec((1, d), lambda i, j: (0, 0)),
                  pl.BlockSpec((d, tn), lambda i, j: (0, j)),
                  pl.BlockSpec((d, tn), lambda i, j: (0, nf + j)),
                  pl.BlockSpec((CONV_WIDTH, tn), lambda i, j: (0, j)),
                  pl.BlockSpec((CONV_WIDTH, tn), lambda i, j: (0, nf + j)),
                  pl.BlockSpec((1, tn), lambda i, j: (0, j)),
                  pl.BlockSpec((1, tn), lambda i, j: (0, nf + j))],
        out_specs=pl.BlockSpec((tm, tn), lambda i, j: (i, j)),
        out_shape=jax.ShapeDtypeStruct((t, f), BF16),
        scratch_shapes=[pltpu.VMEM((CONV_HALO + tm, d), BF16)],
        compiler_params=pltpu.CompilerParams(dimension_semantics=("parallel", "arbitrary"),
                                             vmem_limit_bytes=VMEM_LIMIT),
    )(x1, x1, g.reshape(1, d), w_up, w_up, conv_w, conv_w, cb, cb)


def _ffn_down_kernel(a_ref, w_ref, x_ref, g_ref, o_ref, acc_ref, *, final_norm):
    k = pl.program_id(1)

    @pl.when(k == 0)
    def _():
        acc_ref[...] = x_ref[...]

    acc_ref[...] += jnp.dot(a_ref[...], w_ref[...], preferred_element_type=F32)

    @pl.when(k == pl.num_programs(1) - 1)
    def _():
        o_ref[...] = _rms_rows(acc_ref[...], g_ref[...]) if final_norm else acc_ref[...]


def _ffn_down(act, w_down, x1, g, final_norm, tm=512, tk=512):
    t, f = act.shape
    d = w_down.shape[1]
    return pl.pallas_call(
        functools.partial(_ffn_down_kernel, final_norm=final_norm),
        grid=(t // tm, f // tk),
        in_specs=[pl.BlockSpec((tm, tk), lambda i, k: (i, k)),
                  pl.BlockSpec((tk, d), lambda i, k: (k, 0)),
                  pl.BlockSpec((tm, d), lambda i, k: (i, 0)),
                  pl.BlockSpec((1, d), lambda i, k: (0, 0))],
        out_specs=pl.BlockSpec((tm, d), lambda i, k: (i, 0)),
        out_shape=jax.ShapeDtypeStruct((t, d), F32),
        scratch_shapes=[pltpu.VMEM((tm, d), F32)],
        compiler_params=pltpu.CompilerParams(dimension_semantics=("parallel", "arbitrary"),
                                             vmem_limit_bytes=VMEM_LIMIT),
    )(act, w_down, x1, g.reshape(1, d))


def kernel(x, norm_mix, w_in, rel_bias_table, ssm_lam_re, ssm_lam_im, ssm_log_dt, ssm_b_re, ssm_b_im, ssm_c_re, ssm_c_im, ssm_d, ssm_w_glu, ssm_b_glu, norm_attn_out, norm_ssm_out, w_out, norm_ffn, w_ffn_up, ffn_conv_w, ffn_conv_b, w_ffn_down, norm_final):
    batch, seq, d_model = x.shape
    depth = w_in.shape[0]
    aw = ATTN_HEADS * HEAD_DIM
    sw = SSM_GROUPS * SSM_GROUP_CH
    t = batch * seq
    n_blocks = seq // MOBA_BLOCK
    assert seq % MOBA_BLOCK == 0 and seq % SSM_CHUNK == 0

    bias_tiles = _bias_tiles(rel_bias_table)
    x2 = x.reshape(t, d_model)
    for l in range(depth):
        w = w_in[l]
        w_ku = jnp.concatenate([w[:, aw:2 * aw], w[:, 3 * aw:]], axis=1).astype(BF16)
        wt_qv = jnp.concatenate([w[:, :aw] * (HEAD_DIM ** -0.5 * LOG2E), w[:, 2 * aw:3 * aw]],
                                axis=1).T.astype(BF16)
        ku, qv_t = _in_proj(x2, norm_mix[l], w_ku, wt_qv)

        ku3 = ku.reshape(t // MOBA_BLOCK, MOBA_BLOCK, ku.shape[1])
        attn = _moba_attention(qv_t, ku3, bias_tiles, batch, seq)

        tg, bn, cn, a_re, a_im = _s5_tables(ssm_log_dt[l], ssm_lam_re[l], ssm_lam_im[l],
                                            ssm_b_re[l], ssm_b_im[l], ssm_c_re[l], ssm_c_im[l],
                                            ssm_d[l])
        y = _s5_scan(ku, tg, bn, cn, a_re, a_im, batch, u_col0=aw)

        x2 = _out_proj(x2, attn, y, ssm_w_glu[l].astype(BF16), ssm_b_glu[l], norm_attn_out[l],
                       norm_ssm_out[l], w_out[l].astype(BF16))

        act = _ffn_up(x2, norm_ffn[l], w_ffn_up[l].astype(BF16), ffn_conv_w[l], ffn_conv_b[l], seq)
        x2 = _ffn_down(act, w_ffn_down[l].astype(BF16), x2, norm_final, final_norm=(l == depth - 1))
    return x2.reshape(batch, seq, d_model)
```

```python
import functools
import math

import jax
import jax.numpy as jnp
from jax import lax
from jax.experimental import pallas as pl
from jax.experimental.pallas import tpu as pltpu

F32 = jnp.float32
BF16 = jnp.bfloat16

ATTN_HEADS = 16
HEAD_DIM = 64
SSM_GROUP_CH = 16
SSM_GROUPS = 64
SSM_STATE = 64
MOBA_BLOCK = 256
MOBA_TOP_K = 3
NUM_BUCKETS = 32
MAX_DISTANCE = 128
CONV_WIDTH = 3
RMS_EPS = 1e-6

SSM_CHUNK = 16
CONV_HALO = 16
VMEM_LIMIT = 52 * 1024 * 1024

NEG_INF = float("-inf")
LOG2E = math.log2(math.e)
V_AUG_ROWS = HEAD_DIM + 16


def _rms_rows(x, g):
    ms = jnp.mean(x * x, axis=-1, keepdims=True)
    return x * lax.rsqrt(ms + RMS_EPS) * g


_NT_DIMS = (((1,), (1,)), ((), ()))


def _in_proj_k_kernel(x_ref, g_ref, w_ref, k_ref, h_ref):
    @pl.when(pl.program_id(1) == 0)
    def _():
        h_ref[...] = _rms_rows(x_ref[...], g_ref[...]).astype(BF16)

    k_ref[...] = jnp.dot(h_ref[...], w_ref[...], preferred_element_type=F32).astype(k_ref.dtype)


def _in_proj_qv_kernel(h_ref, wt_ref, o_ref):
    res = lax.dot_general(wt_ref[...], h_ref[...], _NT_DIMS,
                          preferred_element_type=F32).astype(o_ref.dtype)
    for c in range(o_ref.shape[0]):
        o_ref[c] = res[:, c * MOBA_BLOCK:(c + 1) * MOBA_BLOCK]


def _in_proj(x2, g, w_k, wt_qv, tm=1024, tn=512):
    t, d = x2.shape
    n_k = w_k.shape[1]
    n_qv = wt_qv.shape[0]
    k_nat, h = pl.pallas_call(
        _in_proj_k_kernel,
        grid=(t // tm, n_k // tn),
        in_specs=[pl.BlockSpec((tm, d), lambda i, j: (i, 0)),
                  pl.BlockSpec((1, d), lambda i, j: (0, 0)),
                  pl.BlockSpec((d, tn), lambda i, j: (0, j))],
        out_specs=[pl.BlockSpec((tm, tn), lambda i, j: (i, j)),
                   pl.BlockSpec((tm, d), lambda i, j: (i, 0))],
        out_shape=[jax.ShapeDtypeStruct((t, n_k), BF16), jax.ShapeDtypeStruct((t, d), BF16)],
        compiler_params=pltpu.CompilerParams(dimension_semantics=("parallel", "arbitrary"),
                                             vmem_limit_bytes=VMEM_LIMIT),
    )(x2, g.reshape(1, d), w_k)
    slabs = tm // MOBA_BLOCK
    qv_t = pl.pallas_call(
        _in_proj_qv_kernel,
        grid=(t // tm, n_qv // tn),
        in_specs=[pl.BlockSpec((tm, d), lambda i, j: (i, 0)),
                  pl.BlockSpec((tn, d), lambda i, j: (j, 0))],
        out_specs=pl.BlockSpec((slabs, tn, MOBA_BLOCK), lambda i, j: (i, j, 0)),
        out_shape=jax.ShapeDtypeStruct((t // MOBA_BLOCK, n_qv, MOBA_BLOCK), BF16),
        compiler_params=pltpu.CompilerParams(dimension_semantics=("parallel", "parallel"),
                                             vmem_limit_bytes=VMEM_LIMIT),
    )(h, wt_qv)
    return k_nat, qv_t


def _t5_bucket(dist):
    dist = jnp.maximum(dist, 0)
    max_exact = NUM_BUCKETS // 2
    log_ratio = jnp.log(jnp.maximum(dist, max_exact).astype(F32) / max_exact)
    large = max_exact + (log_ratio / math.log(MAX_DISTANCE / max_exact)
                         * (NUM_BUCKETS - max_exact)).astype(jnp.int32)
    large = jnp.minimum(large, NUM_BUCKETS - 1)
    return jnp.where(dist < max_exact, dist, large)


def _bias_tiles_kernel(tab_ref, o_ref):
    h = pl.program_id(0)
    c_far = tab_ref[NUM_BUCKETS - 1, h]
    kk = lax.broadcasted_iota(jnp.int32, (MOBA_BLOCK, MOBA_BLOCK), 0)
    qq = lax.broadcasted_iota(jnp.int32, (MOBA_BLOCK, MOBA_BLOCK), 1)
    for which, off in ((0, MOBA_BLOCK), (1, 0)):
        rel = qq - kk + off
        bucket = _t5_bucket(rel)
        acc = jnp.zeros((MOBA_BLOCK, MOBA_BLOCK), F32)
        for b in range(NUM_BUCKETS):
            acc = jnp.where(bucket == b, tab_ref[b, h], acc)
        acc = (acc - c_far) * LOG2E
        if which == 1:
            acc = jnp.where(rel >= 0, acc, NEG_INF)
        o_ref[0, which] = acc


def _bias_tiles(table):
    return pl.pallas_call(
        _bias_tiles_kernel,
        grid=(ATTN_HEADS,),
        in_specs=[pl.BlockSpec(memory_space=pltpu.SMEM)],
        out_specs=pl.BlockSpec((1, 2, MOBA_BLOCK, MOBA_BLOCK), lambda h: (h, 0, 0, 0)),
        out_shape=jax.ShapeDtypeStruct((ATTN_HEADS, 2, MOBA_BLOCK, MOBA_BLOCK), F32),
    )(table)


def _moba_kernel(q_ref, k_ref, v_ref, bias_ref, o_ref,
                 kmf_ref, vaug_ref, qm_ref, sa_ref, sb_ref, cma_ref, cmb_ref,
                 m_ref, acc_ref, rb_ref, ot_ref, *, n_blocks):
    qi = pl.program_id(2)
    blk = MOBA_BLOCK

    @pl.when(qi == 0)
    def _():
        ones = jnp.ones((V_AUG_ROWS - HEAD_DIM, blk), BF16)
        for j in range(n_blocks):
            kb = k_ref[j].astype(F32)
            kmf_ref[pl.ds(j, 1), :] = jnp.sum(kb, axis=0, keepdims=True) * (1.0 / blk)
            for hh in range(2):
                vaug_ref[j, hh, 0:HEAD_DIM, :] = v_ref[j, hh * HEAD_DIM:(hh + 1) * HEAD_DIM, :]
                vaug_ref[j, hh, HEAD_DIM:V_AUG_ROWS, :] = ones

    kmf = kmf_ref[...]
    km_hi = kmf.astype(BF16)
    km_lo = (kmf - km_hi.astype(F32)).astype(BF16)

    q2 = q_ref[0]
    q_zero = jnp.zeros((HEAD_DIM, blk), q2.dtype)
    blk_idx = lax.broadcasted_iota(jnp.int32, (n_blocks, blk), 0)

    for hh in range(2):
        vrows = slice(hh * HEAD_DIM, (hh + 1) * HEAD_DIM)
        qm = jnp.concatenate([q2[vrows], q_zero] if hh == 0 else [q_zero, q2[vrows]], axis=0)
        qm_ref[hh] = qm

        gate = (jnp.dot(km_hi, qm, preferred_element_type=F32)
                + jnp.dot(km_lo, qm, preferred_element_type=F32))
        valid = blk_idx < qi
        g = jnp.where(valid, gate, NEG_INF)
        rank = jnp.zeros(g.shape, F32)
        for j in range(n_blocks):
            gj = g[j:j + 1, :]
            beats = jnp.where(gj > g, 1.0, jnp.where((gj == g) & (blk_idx > j), 1.0, 0.0))
            rank = rank + beats
        sel = (valid & (rank < MOBA_TOP_K)) | (blk_idx == qi)
        rb_ref[hh] = jnp.where(sel, 0.0, NEG_INF)
        m_ref[hh] = jnp.full((1, blk), NEG_INF, F32)
        acc_ref[hh] = jnp.zeros((V_AUG_ROWS, blk), F32)

    def group_blocks(g):
        ja = qi - 2 * g
        return (ja, None), (jnp.maximum(ja - 1, 0), ja >= 1)

    def scores(g, s_buf, cm_buf, with_bias):
        for i, (j, _) in enumerate(group_blocks(g)):
            kb = k_ref[j]
            for hh in range(2):
                s = jnp.dot(kb, qm_ref[hh], preferred_element_type=F32)
                if with_bias:
                    s = s + bias_ref[hh, 1 - i]
                s_buf[hh, i] = s
                cm_buf[hh, pl.ds(i, 1), :] = jnp.max(s, axis=0, keepdims=True)

    def attend(g, s_buf, cm_buf):
        blocks = group_blocks(g)
        for hh in range(2):
            masks = []
            m_old = m_ref[hh]
            m_new = m_old
            for i, (j, exists) in enumerate(blocks):
                mk = rb_ref[hh, pl.ds(j, 1), :]
                if exists is not None:
                    mk = jnp.where(exists, mk, NEG_INF)
                masks.append(mk)
                m_new = jnp.maximum(m_new, cm_buf[hh, pl.ds(i, 1), :] + mk)
            acc = jnp.exp2(m_old - m_new) * acc_ref[hh]
            for i, (j, _) in enumerate(blocks):
                p = jnp.exp2(s_buf[hh, i] - m_new).astype(BF16)
                pv = jnp.dot(vaug_ref[j, hh], p, preferred_element_type=F32)
                acc = acc + jnp.where(masks[i] == 0.0, pv, 0.0)
            acc_ref[hh] = acc
            m_ref[hh] = m_new

    n_more = qi // 2
    scores(0, sa_ref, cma_ref, True)

    def step(it, carry):
        @pl.when(it % 2 == 0)
        def _():
            scores(it + 1, sb_ref, cmb_ref, False)
            attend(it, sa_ref, cma_ref)

        @pl.when(it % 2 == 1)
        def _():
            scores(it + 1, sa_ref, cma_ref, False)
            attend(it, sb_ref, cmb_ref)

        return carry

    lax.fori_loop(0, n_more, step, 0)

    @pl.when(n_more % 2 == 0)
    def _():
        attend(n_more, sa_ref, cma_ref)

    @pl.when(n_more % 2 == 1)
    def _():
        attend(n_more, sb_ref, cmb_ref)

    for hh in range(2):
        acc = acc_ref[hh]
        ot_ref[hh * HEAD_DIM:(hh + 1) * HEAD_DIM, :] = (
            acc[0:HEAD_DIM] * (1.0 / acc[HEAD_DIM:HEAD_DIM + 1]))

    o_ref[...] = ot_ref[...].T.astype(o_ref.dtype)


def _moba_attention(qv_t, ku3, bias_tiles, batch, seq):
    n_blocks = seq // MOBA_BLOCK
    n_pairs = ATTN_HEADS // 2
    pair_w = 2 * HEAD_DIM
    kernel = functools.partial(_moba_kernel, n_blocks=n_blocks)
    return pl.pallas_call(
        kernel,
        grid=(batch, n_pairs, n_blocks),
        in_specs=[
            pl.BlockSpec((1, pair_w, MOBA_BLOCK), lambda b, hp, qi: (b * n_blocks + qi, hp, 0)),
            pl.BlockSpec((n_blocks, MOBA_BLOCK, pair_w), lambda b, hp, qi: (b, 0, hp)),
            pl.BlockSpec((n_blocks, pair_w, MOBA_BLOCK), lambda b, hp, qi: (b, n_pairs + hp, 0)),
            pl.BlockSpec((2, 2, MOBA_BLOCK, MOBA_BLOCK), lambda b, hp, qi: (hp, 0, 0, 0)),
        ],
        out_specs=pl.BlockSpec((MOBA_BLOCK, pair_w), lambda b, hp, qi: (b * n_blocks + qi, hp)),
        out_shape=jax.ShapeDtypeStruct((batch * seq, ATTN_HEADS * HEAD_DIM), BF16),
        scratch_shapes=[
            pltpu.VMEM((n_blocks, pair_w), F32),
            pltpu.VMEM((n_blocks, 2, V_AUG_ROWS, MOBA_BLOCK), BF16),
            pltpu.VMEM((2, pair_w, MOBA_BLOCK), BF16),
            pltpu.VMEM((2, 2, MOBA_BLOCK, MOBA_BLOCK), F32),
            pltpu.VMEM((2, 2, MOBA_BLOCK, MOBA_BLOCK), F32),
            pltpu.VMEM((2, 8, MOBA_BLOCK), F32),
            pltpu.VMEM((2, 8, MOBA_BLOCK), F32),
            pltpu.VMEM((2, 1, MOBA_BLOCK), F32),
            pltpu.VMEM((2, V_AUG_ROWS, MOBA_BLOCK), F32),
            pltpu.VMEM((2, n_blocks, MOBA_BLOCK), F32),
            pltpu.VMEM((pair_w, MOBA_BLOCK), F32),
        ],
        compiler_params=pltpu.CompilerParams(
            dimension_semantics=("parallel", "parallel", "arbitrary")),
    )(qv_t, ku3, qv_t, bias_tiles)


def _hdot_nt(a, b):
    return lax.dot_general(a, b, _NT_DIMS, preferred_element_type=F32,
                           precision=lax.Precision.HIGHEST)


def _repeat_rows(x, n):
    return jnp.concatenate([jnp.broadcast_to(x[i:i + 1], (n, x.shape[1]))
                            for i in range(x.shape[0])], axis=0)


def _tile_rows(x, n):
    return jnp.concatenate([x] * n, axis=0)


def _s5_tables_kernel(logdt_ref, lre_ref, lim_ref, bre_ref, bim_ref, cre_ref, cim_ref, d_ref,
                      tg_ref, bn_ref, cn_ref, are_ref, aim_ref):
    lc, ch, p = SSM_CHUNK, SSM_GROUP_CH, SSM_STATE
    w = lc * ch
    rr = lax.broadcasted_iota(jnp.int32, (w, w), 0)
    cc = lax.broadcasted_iota(jnp.int32, (w, w), 1)
    causal = rr // ch >= cc // ch
    diag = rr == cc
    e_r = lax.broadcasted_iota(jnp.int32, (ch, w), 0)
    e_c = lax.broadcasted_iota(jnp.int32, (ch, w), 1)
    lane_tile = jnp.where(e_c % ch == e_r, 1.0, 0.0).astype(F32)
    tau = lax.broadcasted_iota(jnp.int32, (2 * lc, p), 0).astype(F32)

    for k in range(tg_ref.shape[0]):
        lam_re = lre_ref[k]
        lam_im = lim_ref[k]
        dt = jnp.exp(logdt_ref[k])
        lr = lam_re * dt
        li = lam_im * dt

        mag = jnp.exp(lr * tau)
        cs = jnp.cos(li * tau)
        sn = jnp.sin(li * tau)
        pos_re, pos_im = mag * cs, mag * sn
        inv = jnp.exp(-lr * tau[:lc])
        neg_re, neg_im = inv * cs[:lc], -inv * sn[:lc]

        lam1_re, lam1_im = pos_re[1:2], pos_im[1:2]
        lamk_re, lamk_im = pos_re[lc - 1:lc], pos_im[lc - 1:lc]
        are_ref[k] = pos_re[lc:lc + 1]
        aim_ref[k] = pos_im[lc:lc + 1]

        num_re, num_im = lam1_re - 1.0, lam1_im
        den = lam_re * lam_re + lam_im * lam_im
        coef_re = (num_re * lam_re + num_im * lam_im) / den
        coef_im = (num_im * lam_re - num_re * lam_im) / den
        bt_re, bt_im = bre_ref[k], bim_ref[k]
        bb_re = coef_re * bt_re - coef_im * bt_im
        bb_im = coef_re * bt_im + coef_im * bt_re

        bbt_re, bbt_im = _tile_rows(bb_re, lc), _tile_rows(bb_im, lc)
        ngx_re, ngx_im = _repeat_rows(neg_re, ch), _repeat_rows(neg_im, ch)
        bneg_re = bbt_re * ngx_re - bbt_im * ngx_im
        bneg_im = bbt_re * ngx_im + bbt_im * ngx_re

        ct_re, ct_im = _tile_rows(cre_ref[k], lc), _tile_rows(cim_ref[k], lc)
        psx_re, psx_im = _repeat_rows(pos_re[:lc], ch), _repeat_rows(pos_im[:lc], ch)
        cpos_re = ct_re * psx_re - ct_im * psx_im
        cpos_im = ct_re * psx_im + ct_im * psx_re

        raw = _hdot_nt(cpos_re, bneg_re) - _hdot_nt(cpos_im, bneg_im)
        d_lanes = jnp.dot(jnp.broadcast_to(d_ref[k], (8, ch)), lane_tile,
                          preferred_element_type=F32, precision=lax.Precision.HIGHEST)[0:1]
        tg = jnp.where(causal, raw, 0.0) + jnp.where(diag, d_lanes, 0.0)
        tg_ref[k] = tg.astype(tg_ref.dtype)

        bn_ref[k, 0] = (bneg_re * lamk_re - bneg_im * lamk_im).T.astype(bn_ref.dtype)
        bn_ref[k, 1] = (bneg_re * lamk_im + bneg_im * lamk_re).T.astype(bn_ref.dtype)
        cp_re = cpos_re * lam1_re - cpos_im * lam1_im
        cp_im = cpos_re * lam1_im + cpos_im * lam1_re
        cn_ref[k, 0] = cp_re.astype(cn_ref.dtype)
        cn_ref[k, 1] = (-cp_im).astype(cn_ref.dtype)


def _s5_tables(log_dt, lam_re, lam_im, b_re, b_im, c_re, c_im, d_skip, gb=4):
    g, p, ch = SSM_GROUPS, SSM_STATE, SSM_GROUP_CH
    w = SSM_CHUNK * ch
    row = lambda a, n: a.reshape(g, 1, n)
    spec3 = lambda s1, s2: pl.BlockSpec((gb, s1, s2), lambda i: (i, 0, 0))
    spec4 = lambda s1, s2: pl.BlockSpec((gb, 2, s1, s2), lambda i: (i, 0, 0, 0))
    return pl.pallas_call(
        _s5_tables_kernel,
        grid=(g // gb,),
        in_specs=[spec3(1, 1), spec3(1, p), spec3(1, p), spec3(ch, p), spec3(ch, p),
                  spec3(ch, p), spec3(ch, p), spec3(1, ch)],
        out_specs=[spec3(w, w), spec4(p, w), spec4(w, p), spec3(1, p), spec3(1, p)],
        out_shape=[jax.ShapeDtypeStruct((g, w, w), BF16),
                   jax.ShapeDtypeStruct((g, 2, p, w), BF16),
                   jax.ShapeDtypeStruct((g, 2, w, p), BF16),
                   jax.ShapeDtypeStruct((g, 1, p), F32),
                   jax.ShapeDtypeStruct((g, 1, p), F32)],
        compiler_params=pltpu.CompilerParams(dimension_semantics=("parallel",)),
    )(row(log_dt, 1), row(lam_re, p), row(lam_im, p),
      jnp.swapaxes(b_re, 1, 2), jnp.swapaxes(b_im, 1, 2), c_re, c_im, row(d_skip, ch))


def _s5_scan_kernel(u_ref, tg_ref, bn_ref, cn_ref, are_ref, aim_ref, y_ref,
                    uf_ref, v_ref, yt_ref, sre_ref, sim_ref, xre_ref, xim_ref, *, n_batch):
    lc, ch = SSM_CHUNK, SSM_GROUP_CH
    gb = tg_ref.shape[0]
    n_rows = u_ref.shape[0] // lc
    n_chunks = n_rows // n_batch
    w = lc * ch

    uf_ref[...] = u_ref[...].astype(F32)
    for s in range(lc):
        ust = uf_ref[pl.ds(s, n_rows, stride=lc), :].T
        for k in range(gb):
            v_ref[k, s * ch:(s + 1) * ch, :] = ust[k * ch:(k + 1) * ch, :].astype(v_ref.dtype)

    for k in range(gb):
        u = v_ref[k]
        rows = pl.ds(k * n_rows, n_rows)
        sre_ref[rows, :] = jnp.dot(bn_ref[k, 0], u, preferred_element_type=F32).T
        sim_ref[rows, :] = jnp.dot(bn_ref[k, 1], u, preferred_element_type=F32).T

    a_re = are_ref[...]
    a_im = aim_ref[...]

    def step(c, carry):
        new = []
        for b in range(n_batch):
            x_re, x_im = carry[2 * b], carry[2 * b + 1]
            rows = pl.ds(b * n_chunks + c, gb, stride=n_rows)
            xre_ref[rows, :] = x_re
            xim_ref[rows, :] = x_im
            new.append(a_re * x_re - a_im * x_im + sre_ref[rows, :])
            new.append(a_re * x_im + a_im * x_re + sim_ref[rows, :])
        return tuple(new)

    zero = jnp.zeros(a_re.shape, F32)
    lax.fori_loop(0, n_chunks, step, (zero,) * (2 * n_batch))

    for k in range(gb):
        rows = pl.ds(k * n_rows, n_rows)
        y = jnp.dot(tg_ref[k], v_ref[k], preferred_element_type=F32)
        y = y + lax.dot_general(cn_ref[k, 0], xre_ref[rows, :].astype(BF16), _NT_DIMS,
                                preferred_element_type=F32)
        y = y + lax.dot_general(cn_ref[k, 1], xim_ref[rows, :].astype(BF16), _NT_DIMS,
                                preferred_element_type=F32)
        for s in range(lc):
            yt_ref[s, k * ch:(k + 1) * ch, :] = y[s * ch:(s + 1) * ch, :]

    for s in range(lc):
        y_ref[pl.ds(s, n_rows, stride=lc), :] = yt_ref[s].T.astype(y_ref.dtype)


def _s5_scan(ku, tg, bn, cn, a_re, a_im, n_batch, u_col0, gb=8):
    t = ku.shape[0]
    g, w, _ = tg.shape
    p, ch = SSM_STATE, SSM_GROUP_CH
    n_rows = t // SSM_CHUNK
    lanes = gb * ch
    kernel = functools.partial(_s5_scan_kernel, n_batch=n_batch)
    return pl.pallas_call(
        kernel,
        grid=(g // gb,),
        in_specs=[pl.BlockSpec((t, lanes), lambda i: (0, u_col0 // lanes + i)),
                  pl.BlockSpec((gb, w, w), lambda i: (i, 0, 0)),
                  pl.BlockSpec((gb, 2, p, w), lambda i: (i, 0, 0, 0)),
                  pl.BlockSpec((gb, 2, w, p), lambda i: (i, 0, 0, 0)),
                  pl.BlockSpec((gb, p), lambda i: (i, 0)),
                  pl.BlockSpec((gb, p), lambda i: (i, 0))],
        out_specs=pl.BlockSpec((t, lanes), lambda i: (0, i)),
        out_shape=jax.ShapeDtypeStruct((t, g * ch), F32),
        scratch_shapes=[pltpu.VMEM((t, lanes), F32),
                        pltpu.VMEM((gb, w, n_rows), BF16),
                        pltpu.VMEM((SSM_CHUNK, lanes, n_rows), F32)]
                       + [pltpu.VMEM((gb * n_rows, p), F32) for _ in range(4)],
        compiler_params=pltpu.CompilerParams(dimension_semantics=("parallel",),
                                             vmem_limit_bytes=VMEM_LIMIT),
    )(ku, tg, bn, cn, a_re.reshape(g, p), a_im.reshape(g, p))


def _out_proj_kernel(x_ref, a_ref, y_ref, wglu_ref, bglu_ref, ga_ref, gs_ref, wa_ref, ws_ref, o_ref):
    z = jax.nn.gelu(y_ref[...])
    gl = jnp.dot(z.astype(BF16), wglu_ref[...], preferred_element_type=F32) + bglu_ref[...]
    s = z * jax.nn.sigmoid(gl)
    a_n = _rms_rows(a_ref[...].astype(F32), ga_ref[...]).astype(BF16)
    s_n = _rms_rows(s, gs_ref[...]).astype(BF16)
    mix = (jnp.dot(a_n, wa_ref[...], preferred_element_type=F32)
           + jnp.dot(s_n, ws_ref[...], preferred_element_type=F32))
    o_ref[...] = x_ref[...] + mix


def _out_proj(x2, attn, y, w_glu, b_glu, g_a, g_s, w_out, tm=256):
    t, d = x2.shape
    wa = attn.shape[1]
    ws = y.shape[1]
    row = lambda i: (i, 0)
    fixed = lambda i: (0, 0)
    once = pl.Buffered(1)
    return pl.pallas_call(
        _out_proj_kernel,
        grid=(t // tm,),
        in_specs=[pl.BlockSpec((tm, d), row),
                  pl.BlockSpec((tm, wa), row),
                  pl.BlockSpec((tm, ws), row),
                  pl.BlockSpec((ws, ws), fixed, pipeline_mode=once),
                  pl.BlockSpec((1, ws), fixed),
                  pl.BlockSpec((1, wa), fixed),
                  pl.BlockSpec((1, ws), fixed),
                  pl.BlockSpec((wa, d), fixed, pipeline_mode=once),
                  pl.BlockSpec((ws, d), fixed, pipeline_mode=once)],
        out_specs=pl.BlockSpec((tm, d), row),
        out_shape=jax.ShapeDtypeStruct((t, d), F32),
        compiler_params=pltpu.CompilerParams(dimension_semantics=("parallel",),
                                             vmem_limit_bytes=VMEM_LIMIT),
    )(x2, attn, y, w_glu, b_glu.reshape(1, ws), g_a.reshape(1, wa), g_s.reshape(1, ws),
      w_out[:wa], w_out[wa:])


def _ffn_up_kernel(x_ref, halo_ref, g_ref, wg_ref, wv_ref, cwg_ref, cwv_ref, cbg_ref, cbv_ref,
                   o_ref, h_ref, *, tiles_per_seq):
    tm = x_ref.shape[0]

    @pl.when(pl.program_id(1) == 0)
    def _():
        keep = jnp.where(pl.program_id(0) % tiles_per_seq == 0, 0.0, 1.0)
        h_ref[pl.ds(0, CONV_HALO), :] = (_rms_rows(halo_ref[...], g_ref[...]) * keep).astype(BF16)
        h_ref[pl.ds(CONV_HALO, tm), :] = _rms_rows(x_ref[...], g_ref[...]).astype(BF16)

    h = h_ref[...]

    def conv(w_ref, cw_ref, cb_ref):
        up = jnp.dot(h, w_ref[...], preferred_element_type=F32)
        cw = cw_ref[...]
        out = up[CONV_HALO:] * cw[CONV_WIDTH - 1:CONV_WIDTH] + cb_ref[...]
        for j in range(CONV_WIDTH - 1):
            lag = CONV_WIDTH - 1 - j
            out = out + up[CONV_HALO - lag:CONV_HALO - lag + tm] * cw[j:j + 1]
        return out

    gate = conv(wg_ref, cwg_ref, cbg_ref)
    val = conv(wv_ref, cwv_ref, cbv_ref)
    o_ref[...] = (jax.nn.silu(gate) * val).astype(o_ref.dtype)


def _ffn_up(x1, g, w_up, conv_w, conv_b, seq, tm=1024, tn=512):
    t, d = x1.shape
    f = w_up.shape[1] // 2
    nf = f // tn
    tiles_per_seq = seq // tm
    halo_blocks = tm // CONV_HALO
    kernel = functools.partial(_ffn_up_kernel, tiles_per_seq=tiles_per_seq)
    cb = conv_b.reshape(1, 2 * f)
    return pl.pallas_call(
        kernel,
        grid=(t // tm, nf),
        in_specs=[pl.BlockSpec((tm, d), lambda i, j: (i, 0)),
                  pl.BlockSpec((CONV_HALO, d), lambda i, j: (jnp.maximum(i * halo_blocks - 1, 0), 0)),
                  pl.BlockSpec((1, d), lambda i, j: (0, 0)),
                  pl.BlockSpec((d, tn), lambda i, j: (0, j)),
                  pl.BlockSpec((d, tn), lambda i, j: (0, nf + j)),
                  pl.BlockSpec((CONV_WIDTH, tn), lambda i, j: (0, j)),
                  pl.BlockSpec((CONV_WIDTH, tn), lambda i, j: (0, nf + j)),
                  pl.BlockSpec((1, tn), lambda i, j: (0, j)),
                  pl.BlockSpec((1, tn), lambda i, j: (0, nf + j))],
        out_specs=pl.BlockSpec((tm, tn), lambda i, j: (i, j)),
        out_shape=jax.ShapeDtypeStruct((t, f), BF16),
        scratch_shapes=[pltpu.VMEM((CONV_HALO + tm, d), BF16)],
        compiler_params=pltpu.CompilerParams(dimension_semantics=("parallel", "arbitrary"),
                                             vmem_limit_bytes=VMEM_LIMIT),
    )(x1, x1, g.reshape(1, d), w_up, w_up, conv_w, conv_w, cb, cb)


def _ffn_down_kernel(a_ref, w_ref, x_ref, g_ref, o_ref, *, final_norm):
    y = x_ref[...] + jnp.dot(a_ref[...], w_ref[...], preferred_element_type=F32)
    o_ref[...] = _rms_rows(y, g_ref[...]) if final_norm else y


def _ffn_down(act, w_down, x1, g, final_norm, tm=256):
    t, f = act.shape
    d = w_down.shape[1]
    return pl.pallas_call(
        functools.partial(_ffn_down_kernel, final_norm=final_norm),
        grid=(t // tm,),
        in_specs=[pl.BlockSpec((tm, f), lambda i: (i, 0)),
                  pl.BlockSpec((f, d), lambda i: (0, 0), pipeline_mode=pl.Buffered(1)),
                  pl.BlockSpec((tm, d), lambda i: (i, 0)),
                  pl.BlockSpec((1, d), lambda i: (0, 0))],
        out_specs=pl.BlockSpec((tm, d), lambda i: (i, 0)),
        out_shape=jax.ShapeDtypeStruct((t, d), F32),
        compiler_params=pltpu.CompilerParams(dimension_semantics=("parallel",),
                                             vmem_limit_bytes=VMEM_LIMIT),
    )(act, w_down, x1, g.reshape(1, d))


def kernel(x, norm_mix, w_in, rel_bias_table, ssm_lam_re, ssm_lam_im, ssm_log_dt, ssm_b_re, ssm_b_im, ssm_c_re, ssm_c_im, ssm_d, ssm_w_glu, ssm_b_glu, norm_attn_out, norm_ssm_out, w_out, norm_ffn, w_ffn_up, ffn_conv_w, ffn_conv_b, w_ffn_down, norm_final):
    batch, seq, d_model = x.shape
    depth = w_in.shape[0]
    aw = ATTN_HEADS * HEAD_DIM
    sw = SSM_GROUPS * SSM_GROUP_CH
    t = batch * seq
    n_blocks = seq // MOBA_BLOCK
    assert seq % MOBA_BLOCK == 0 and seq % SSM_CHUNK == 0

    bias_tiles = _bias_tiles(rel_bias_table)
    x2 = x.reshape(t, d_model)
    for l in range(depth):
        w = w_in[l]
        w_ku = jnp.concatenate([w[:, aw:2 * aw], w[:, 3 * aw:]], axis=1).astype(BF16)
        wt_qv = jnp.concatenate([w[:, :aw] * (HEAD_DIM ** -0.5 * LOG2E), w[:, 2 * aw:3 * aw]],
                                axis=1).T.astype(BF16)
        ku, qv_t = _in_proj(x2, norm_mix[l], w_ku, wt_qv)

        ku3 = ku.reshape(t // MOBA_BLOCK, MOBA_BLOCK, ku.shape[1])
        attn = _moba_attention(qv_t, ku3, bias_tiles, batch, seq)

        tg, bn, cn, a_re, a_im = _s5_tables(ssm_log_dt[l], ssm_lam_re[l], ssm_lam_im[l],
                                            ssm_b_re[l], ssm_b_im[l], ssm_c_re[l], ssm_c_im[l],
                                            ssm_d[l])
        y = _s5_scan(ku, tg, bn, cn, a_re, a_im, batch, u_col0=aw)

        x2 = _out_proj(x2, attn, y, ssm_w_glu[l].astype(BF16), ssm_b_glu[l], norm_attn_out[l],
                       norm_ssm_out[l], w_out[l].astype(BF16))

        act = _ffn_up(x2, norm_ffn[l], w_ffn_up[l].astype(BF16), ffn_conv_w[l], ffn_conv_b[l], seq)
        x2 = _ffn_down(act, w_ffn_down[l].astype(BF16), x2, norm_final, final_norm=(l == depth - 1))
    return x2.reshape(batch, seq, d_model)
```

```python
import functools
import math

import jax
import jax.numpy as jnp
from jax import lax
from jax.experimental import pallas as pl
from jax.experimental.pallas import tpu as pltpu

F32 = jnp.float32
BF16 = jnp.bfloat16

ATTN_HEADS = 16
HEAD_DIM = 64
SSM_GROUP_CH = 16
SSM_GROUPS = 64
SSM_STATE = 64
MOBA_BLOCK = 256
MOBA_TOP_K = 3
NUM_BUCKETS = 32
MAX_DISTANCE = 128
CONV_WIDTH = 3
RMS_EPS = 1e-6

SSM_CHUNK = 16
SCAN_UNROLL = 8
CONV_HALO = 16
VMEM_LIMIT = 52 * 1024 * 1024

NEG_INF = float("-inf")
LOG2E = math.log2(math.e)
V_AUG_ROWS = HEAD_DIM + 16


def _rms_rows(x, g):
    ms = jnp.mean(x * x, axis=-1, keepdims=True)
    return x * lax.rsqrt(ms + RMS_EPS) * g


_NT_DIMS = (((1,), (1,)), ((), ()))


def _in_proj_kernel(x_ref, g_ref, w_ref, o_ref, h_ref):
    @pl.when(pl.program_id(1) == 0)
    def _():
        h_ref[...] = _rms_rows(x_ref[...], g_ref[...]).astype(BF16)

    o_ref[...] = jnp.dot(h_ref[...], w_ref[...], preferred_element_type=F32).astype(o_ref.dtype)


def _in_proj(x2, g, w, tm=1024, tn=512):
    t, d = x2.shape
    n = w.shape[1]
    return pl.pallas_call(
        _in_proj_kernel,
        grid=(t // tm, n // tn),
        in_specs=[pl.BlockSpec((tm, d), lambda i, j: (i, 0)),
                  pl.BlockSpec((1, d), lambda i, j: (0, 0)),
                  pl.BlockSpec((d, tn), lambda i, j: (0, j))],
        out_specs=pl.BlockSpec((tm, tn), lambda i, j: (i, j)),
        out_shape=jax.ShapeDtypeStruct((t, n), BF16),
        scratch_shapes=[pltpu.VMEM((tm, d), BF16)],
        compiler_params=pltpu.CompilerParams(dimension_semantics=("parallel", "arbitrary"),
                                             vmem_limit_bytes=VMEM_LIMIT),
    )(x2, g.reshape(1, d), w)


def _t5_bucket(dist):
    dist = jnp.maximum(dist, 0)
    max_exact = NUM_BUCKETS // 2
    log_ratio = jnp.log(jnp.maximum(dist, max_exact).astype(F32) / max_exact)
    large = max_exact + (log_ratio / math.log(MAX_DISTANCE / max_exact)
                         * (NUM_BUCKETS - max_exact)).astype(jnp.int32)
    large = jnp.minimum(large, NUM_BUCKETS - 1)
    return jnp.where(dist < max_exact, dist, large)


def _bias_tiles_kernel(tab_ref, o_ref):
    h = pl.program_id(0)
    c_far = tab_ref[NUM_BUCKETS - 1, h]
    kk = lax.broadcasted_iota(jnp.int32, (MOBA_BLOCK, MOBA_BLOCK), 0)
    qq = lax.broadcasted_iota(jnp.int32, (MOBA_BLOCK, MOBA_BLOCK), 1)
    for which, off in ((0, MOBA_BLOCK), (1, 0)):
        rel = qq - kk + off
        bucket = _t5_bucket(rel)
        acc = jnp.zeros((MOBA_BLOCK, MOBA_BLOCK), F32)
        for b in range(NUM_BUCKETS):
            acc = jnp.where(bucket == b, tab_ref[b, h], acc)
        acc = (acc - c_far) * LOG2E
        if which == 1:
            acc = jnp.where(rel >= 0, acc, NEG_INF)
        o_ref[0, which] = acc


def _bias_tiles(table):
    return pl.pallas_call(
        _bias_tiles_kernel,
        grid=(ATTN_HEADS,),
        in_specs=[pl.BlockSpec(memory_space=pltpu.SMEM)],
        out_specs=pl.BlockSpec((1, 2, MOBA_BLOCK, MOBA_BLOCK), lambda h: (h, 0, 0, 0)),
        out_shape=jax.ShapeDtypeStruct((ATTN_HEADS, 2, MOBA_BLOCK, MOBA_BLOCK), F32),
    )(table)


def _moba_kernel(q_ref, k_ref, v_ref, bias_ref, o_ref,
                 kmf_ref, vaug_ref, qm_ref, sa_ref, sb_ref, cma_ref, cmb_ref,
                 m_ref, acc_ref, rb_ref, ot_ref, *, n_blocks):
    qi = pl.program_id(2)
    blk = MOBA_BLOCK

    @pl.when(qi == 0)
    def _():
        ones = jnp.ones((V_AUG_ROWS - HEAD_DIM, blk), BF16)
        for j in range(n_blocks):
            kb = k_ref[j].astype(F32)
            kmf_ref[pl.ds(j, 1), :] = jnp.sum(kb, axis=0, keepdims=True) * (1.0 / blk)
            vt = v_ref[j].T
            for hh in range(2):
                vaug_ref[j, hh, 0:HEAD_DIM, :] = vt[hh * HEAD_DIM:(hh + 1) * HEAD_DIM, :]
                vaug_ref[j, hh, HEAD_DIM:V_AUG_ROWS, :] = ones

    kmf = kmf_ref[...]
    km_hi = kmf.astype(BF16)
    km_lo = (kmf - km_hi.astype(F32)).astype(BF16)

    q2 = q_ref[0].T
    q_zero = jnp.zeros((HEAD_DIM, blk), q2.dtype)
    blk_idx = lax.broadcasted_iota(jnp.int32, (n_blocks, blk), 0)

    for hh in range(2):
        vrows = slice(hh * HEAD_DIM, (hh + 1) * HEAD_DIM)
        qm = jnp.concatenate([q2[vrows], q_zero] if hh == 0 else [q_zero, q2[vrows]], axis=0)
        qm_ref[hh] = qm

        gate = (jnp.dot(km_hi, qm, preferred_element_type=F32)
                + jnp.dot(km_lo, qm, preferred_element_type=F32))
        valid = blk_idx < qi
        g = jnp.where(valid, gate, NEG_INF)
        rank = jnp.zeros(g.shape, F32)
        for j in range(n_blocks):
            gj = g[j:j + 1, :]
            beats = jnp.where(gj > g, 1.0, jnp.where((gj == g) & (blk_idx > j), 1.0, 0.0))
            rank = rank + beats
        sel = (valid & (rank < MOBA_TOP_K)) | (blk_idx == qi)
        rb_ref[hh] = jnp.where(sel, 0.0, NEG_INF)
        m_ref[hh] = jnp.full((1, blk), NEG_INF, F32)
        acc_ref[hh] = jnp.zeros((V_AUG_ROWS, blk), F32)

    def group_blocks(g):
        ja = qi - 2 * g
        return (ja, None), (jnp.maximum(ja - 1, 0), ja >= 1)

    def scores(g, s_buf, cm_buf, with_bias):
        for i, (j, _) in enumerate(group_blocks(g)):
            kb = k_ref[j]
            for hh in range(2):
                s = jnp.dot(kb, qm_ref[hh], preferred_element_type=F32)
                if with_bias:
                    s = s + bias_ref[hh, 1 - i]
                s_buf[hh, i] = s
                cm_buf[hh, pl.ds(i, 1), :] = jnp.max(s, axis=0, keepdims=True)

    def attend(g, s_buf, cm_buf):
        blocks = group_blocks(g)
        for hh in range(2):
            masks = []
            m_old = m_ref[hh]
            m_new = m_old
            for i, (j, exists) in enumerate(blocks):
                mk = rb_ref[hh, pl.ds(j, 1), :]
                if exists is not None:
                    mk = jnp.where(exists, mk, NEG_INF)
                masks.append(mk)
                m_new = jnp.maximum(m_new, cm_buf[hh, pl.ds(i, 1), :] + mk)
            acc = jnp.exp2(m_old - m_new) * acc_ref[hh]
            for i, (j, _) in enumerate(blocks):
                p = jnp.exp2(s_buf[hh, i] - m_new).astype(BF16)
                pv = jnp.dot(vaug_ref[j, hh], p, preferred_element_type=F32)
                acc = acc + jnp.where(masks[i] == 0.0, pv, 0.0)
            acc_ref[hh] = acc
            m_ref[hh] = m_new

    n_more = qi // 2
    scores(0, sa_ref, cma_ref, True)

    def step(it, carry):
        @pl.when(it % 2 == 0)
        def _():
            scores(it + 1, sb_ref, cmb_ref, False)
            attend(it, sa_ref, cma_ref)

        @pl.when(it % 2 == 1)
        def _():
            scores(it + 1, sa_ref, cma_ref, False)
            attend(it, sb_ref, cmb_ref)

        return carry

    lax.fori_loop(0, n_more, step, 0)

    @pl.when(n_more % 2 == 0)
    def _():
        attend(n_more, sa_ref, cma_ref)

    @pl.when(n_more % 2 == 1)
    def _():
        attend(n_more, sb_ref, cmb_ref)

    for hh in range(2):
        acc = acc_ref[hh]
        ot_ref[hh * HEAD_DIM:(hh + 1) * HEAD_DIM, :] = (
            acc[0:HEAD_DIM] * (1.0 / acc[HEAD_DIM:HEAD_DIM + 1]))

    o_ref[...] = ot_ref[...].T.astype(o_ref.dtype)


def _moba_attention(proj3, bias_tiles, batch, seq):
    n_blocks = seq // MOBA_BLOCK
    n_pairs = ATTN_HEADS // 2
    pair_w = 2 * HEAD_DIM
    kernel = functools.partial(_moba_kernel, n_blocks=n_blocks)
    return pl.pallas_call(
        kernel,
        grid=(batch, n_pairs, n_blocks),
        in_specs=[
            pl.BlockSpec((1, MOBA_BLOCK, pair_w), lambda b, hp, qi: (b * n_blocks + qi, 0, hp)),
            pl.BlockSpec((n_blocks, MOBA_BLOCK, pair_w), lambda b, hp, qi: (b, 0, n_pairs + hp)),
            pl.BlockSpec((n_blocks, MOBA_BLOCK, pair_w), lambda b, hp, qi: (b, 0, 2 * n_pairs + hp)),
            pl.BlockSpec((2, 2, MOBA_BLOCK, MOBA_BLOCK), lambda b, hp, qi: (hp, 0, 0, 0)),
        ],
        out_specs=pl.BlockSpec((MOBA_BLOCK, pair_w), lambda b, hp, qi: (b * n_blocks + qi, hp)),
        out_shape=jax.ShapeDtypeStruct((batch * seq, ATTN_HEADS * HEAD_DIM), BF16),
        scratch_shapes=[
            pltpu.VMEM((n_blocks, pair_w), F32),
            pltpu.VMEM((n_blocks, 2, V_AUG_ROWS, MOBA_BLOCK), BF16),
            pltpu.VMEM((2, pair_w, MOBA_BLOCK), BF16),
            pltpu.VMEM((2, 2, MOBA_BLOCK, MOBA_BLOCK), F32),
            pltpu.VMEM((2, 2, MOBA_BLOCK, MOBA_BLOCK), F32),
            pltpu.VMEM((2, 8, MOBA_BLOCK), F32),
            pltpu.VMEM((2, 8, MOBA_BLOCK), F32),
            pltpu.VMEM((2, 1, MOBA_BLOCK), F32),
            pltpu.VMEM((2, V_AUG_ROWS, MOBA_BLOCK), F32),
            pltpu.VMEM((2, n_blocks, MOBA_BLOCK), F32),
            pltpu.VMEM((pair_w, MOBA_BLOCK), F32),
        ],
        compiler_params=pltpu.CompilerParams(
            dimension_semantics=("parallel", "parallel", "arbitrary")),
    )(proj3, proj3, proj3, bias_tiles)


def _hdot_nt(a, b):
    return lax.dot_general(a, b, _NT_DIMS, preferred_element_type=F32,
                           precision=lax.Precision.HIGHEST)


def _repeat_rows(x, n):
    return jnp.concatenate([jnp.broadcast_to(x[i:i + 1], (n, x.shape[1]))
                            for i in range(x.shape[0])], axis=0)


def _tile_rows(x, n):
    return jnp.concatenate([x] * n, axis=0)


def _s5_tables_kernel(logdt_ref, lre_ref, lim_ref, bre_ref, bim_ref, cre_ref, cim_ref, d_ref,
                      tg_ref, bn_ref, cn_ref, are_ref, aim_ref):
    lc, ch, p = SSM_CHUNK, SSM_GROUP_CH, SSM_STATE
    w = lc * ch
    rr = lax.broadcasted_iota(jnp.int32, (w, w), 0)
    cc = lax.broadcasted_iota(jnp.int32, (w, w), 1)
    causal = rr // ch >= cc // ch
    diag = rr == cc
    e_r = lax.broadcasted_iota(jnp.int32, (ch, w), 0)
    e_c = lax.broadcasted_iota(jnp.int32, (ch, w), 1)
    lane_tile = jnp.where(e_c % ch == e_r, 1.0, 0.0).astype(F32)
    tau = lax.broadcasted_iota(jnp.int32, (2 * lc, p), 0).astype(F32)

    for k in range(tg_ref.shape[0]):
        lam_re = lre_ref[k]
        lam_im = lim_ref[k]
        dt = jnp.exp(logdt_ref[k])
        lr = lam_re * dt
        li = lam_im * dt

        mag = jnp.exp(lr * tau)
        cs = jnp.cos(li * tau)
        sn = jnp.sin(li * tau)
        pos_re, pos_im = mag * cs, mag * sn
        inv = jnp.exp(-lr * tau[:lc])
        neg_re, neg_im = inv * cs[:lc], -inv * sn[:lc]

        lam1_re, lam1_im = pos_re[1:2], pos_im[1:2]
        lamk_re, lamk_im = pos_re[lc - 1:lc], pos_im[lc - 1:lc]
        are_ref[k] = pos_re[lc:lc + 1]
        aim_ref[k] = pos_im[lc:lc + 1]

        num_re, num_im = lam1_re - 1.0, lam1_im
        den = lam_re * lam_re + lam_im * lam_im
        coef_re = (num_re * lam_re + num_im * lam_im) / den
        coef_im = (num_im * lam_re - num_re * lam_im) / den
        bt_re, bt_im = bre_ref[k], bim_ref[k]
        bb_re = coef_re * bt_re - coef_im * bt_im
        bb_im = coef_re * bt_im + coef_im * bt_re

        bbt_re, bbt_im = _tile_rows(bb_re, lc), _tile_rows(bb_im, lc)
        ngx_re, ngx_im = _repeat_rows(neg_re, ch), _repeat_rows(neg_im, ch)
        bneg_re = bbt_re * ngx_re - bbt_im * ngx_im
        bneg_im = bbt_re * ngx_im + bbt_im * ngx_re

        ct_re, ct_im = _tile_rows(cre_ref[k], lc), _tile_rows(cim_ref[k], lc)
        psx_re, psx_im = _repeat_rows(pos_re[:lc], ch), _repeat_rows(pos_im[:lc], ch)
        cpos_re = ct_re * psx_re - ct_im * psx_im
        cpos_im = ct_re * psx_im + ct_im * psx_re

        raw = _hdot_nt(cpos_re, bneg_re) - _hdot_nt(cpos_im, bneg_im)
        d_lanes = jnp.dot(jnp.broadcast_to(d_ref[k], (8, ch)), lane_tile,
                          preferred_element_type=F32, precision=lax.Precision.HIGHEST)[0:1]
        tg = jnp.where(causal, raw, 0.0) + jnp.where(diag, d_lanes, 0.0)
        tg_ref[k] = tg.astype(tg_ref.dtype)

        bn_ref[k, 0] = (bneg_re * lamk_re - bneg_im * lamk_im).T.astype(bn_ref.dtype)
        bn_ref[k, 1] = (bneg_re * lamk_im + bneg_im * lamk_re).T.astype(bn_ref.dtype)
        cp_re = cpos_re * lam1_re - cpos_im * lam1_im
        cp_im = cpos_re * lam1_im + cpos_im * lam1_re
        cn_ref[k, 0] = cp_re.astype(cn_ref.dtype)
        cn_ref[k, 1] = (-cp_im).astype(cn_ref.dtype)


def _s5_tables(log_dt, lam_re, lam_im, b_re, b_im, c_re, c_im, d_skip, gb=4):
    g, p, ch = SSM_GROUPS, SSM_STATE, SSM_GROUP_CH
    w = SSM_CHUNK * ch
    row = lambda a, n: a.reshape(g, 1, n)
    spec3 = lambda s1, s2: pl.BlockSpec((gb, s1, s2), lambda i: (i, 0, 0))
    spec4 = lambda s1, s2: pl.BlockSpec((gb, 2, s1, s2), lambda i: (i, 0, 0, 0))
    return pl.pallas_call(
        _s5_tables_kernel,
        grid=(g // gb,),
        in_specs=[spec3(1, 1), spec3(1, p), spec3(1, p), spec3(ch, p), spec3(ch, p),
                  spec3(ch, p), spec3(ch, p), spec3(1, ch)],
        out_specs=[spec3(w, w), spec4(p, w), spec4(w, p), spec3(1, p), spec3(1, p)],
        out_shape=[jax.ShapeDtypeStruct((g, w, w), BF16),
                   jax.ShapeDtypeStruct((g, 2, p, w), BF16),
                   jax.ShapeDtypeStruct((g, 2, w, p), BF16),
                   jax.ShapeDtypeStruct((g, 1, p), F32),
                   jax.ShapeDtypeStruct((g, 1, p), F32)],
        compiler_params=pltpu.CompilerParams(dimension_semantics=("parallel",)),
    )(row(log_dt, 1), row(lam_re, p), row(lam_im, p),
      jnp.swapaxes(b_re, 1, 2), jnp.swapaxes(b_im, 1, 2), c_re, c_im, row(d_skip, ch))


def _s5_scan_kernel(u_ref, tg_ref, bn_ref, cn_ref, are_ref, aim_ref, y_ref,
                    uf_ref, v_ref, yt_ref, sre_ref, sim_ref, xre_ref, xim_ref, *, n_batch):
    lc, ch = SSM_CHUNK, SSM_GROUP_CH
    gb = tg_ref.shape[0]
    n_rows = u_ref.shape[0] // lc
    n_chunks = n_rows // n_batch
    w = lc * ch

    uf_ref[...] = u_ref[...].astype(F32)
    for s in range(lc):
        ust = uf_ref[pl.ds(s, n_rows, stride=lc), :].T
        for k in range(gb):
            v_ref[k, s * ch:(s + 1) * ch, :] = ust[k * ch:(k + 1) * ch, :].astype(v_ref.dtype)

    for k in range(gb):
        u = v_ref[k]
        rows = pl.ds(k, n_rows, stride=gb)
        sre_ref[rows, :] = jnp.dot(bn_ref[k, 0], u, preferred_element_type=F32).T
        sim_ref[rows, :] = jnp.dot(bn_ref[k, 1], u, preferred_element_type=F32).T

    a_re = are_ref[...]
    a_im = aim_ref[...]

    def step(c, carry):
        new = []
        for b in range(n_batch):
            x_re, x_im = carry[2 * b], carry[2 * b + 1]
            rows = pl.ds(pl.multiple_of((b * n_chunks + c) * gb, gb), gb)
            xre_ref[rows, :] = x_re
            xim_ref[rows, :] = x_im
            new.append(a_re * x_re - a_im * x_im + sre_ref[rows, :])
            new.append(a_re * x_im + a_im * x_re + sim_ref[rows, :])
        return tuple(new)

    zero = jnp.zeros(a_re.shape, F32)
    lax.fori_loop(0, n_chunks, step, (zero,) * (2 * n_batch), unroll=SCAN_UNROLL)

    for k in range(gb):
        rows = pl.ds(k, n_rows, stride=gb)
        y = jnp.dot(tg_ref[k], v_ref[k], preferred_element_type=F32)
        y = y + lax.dot_general(cn_ref[k, 0], xre_ref[rows, :].astype(BF16), _NT_DIMS,
                                preferred_element_type=F32)
        y = y + lax.dot_general(cn_ref[k, 1], xim_ref[rows, :].astype(BF16), _NT_DIMS,
                                preferred_element_type=F32)
        for s in range(lc):
            yt_ref[s, k * ch:(k + 1) * ch, :] = y[s * ch:(s + 1) * ch, :]

    for s in range(lc):
        y_ref[pl.ds(s, n_rows, stride=lc), :] = yt_ref[s].T.astype(y_ref.dtype)


def _s5_scan(ku, tg, bn, cn, a_re, a_im, n_batch, u_col0, gb=8):
    t = ku.shape[0]
    g, w, _ = tg.shape
    p, ch = SSM_STATE, SSM_GROUP_CH
    n_rows = t // SSM_CHUNK
    lanes = gb * ch
    kernel = functools.partial(_s5_scan_kernel, n_batch=n_batch)
    return pl.pallas_call(
        kernel,
        grid=(g // gb,),
        in_specs=[pl.BlockSpec((t, lanes), lambda i: (0, u_col0 // lanes + i)),
                  pl.BlockSpec((gb, w, w), lambda i: (i, 0, 0)),
                  pl.BlockSpec((gb, 2, p, w), lambda i: (i, 0, 0, 0)),
                  pl.BlockSpec((gb, 2, w, p), lambda i: (i, 0, 0, 0)),
                  pl.BlockSpec((gb, p), lambda i: (i, 0)),
                  pl.BlockSpec((gb, p), lambda i: (i, 0))],
        out_specs=pl.BlockSpec((t, lanes), lambda i: (0, i)),
        out_shape=jax.ShapeDtypeStruct((t, g * ch), F32),
        scratch_shapes=[pltpu.VMEM((t, lanes), F32),
                        pltpu.VMEM((gb, w, n_rows), BF16),
                        pltpu.VMEM((SSM_CHUNK, lanes, n_rows), F32)]
                       + [pltpu.VMEM((gb * n_rows, p), F32) for _ in range(4)],
        compiler_params=pltpu.CompilerParams(dimension_semantics=("parallel",),
                                             vmem_limit_bytes=VMEM_LIMIT),
    )(ku, tg, bn, cn, a_re.reshape(g, p), a_im.reshape(g, p))


def _out_proj_kernel(x_ref, a_ref, y_ref, wglu_ref, bglu_ref, ga_ref, gs_ref, wa_ref, ws_ref, o_ref):
    z = jax.nn.gelu(y_ref[...])
    gl = jnp.dot(z.astype(BF16), wglu_ref[...], preferred_element_type=F32) + bglu_ref[...]
    s = z * jax.nn.sigmoid(gl)
    a_n = _rms_rows(a_ref[...].astype(F32), ga_ref[...]).astype(BF16)
    s_n = _rms_rows(s, gs_ref[...]).astype(BF16)
    mix = (jnp.dot(a_n, wa_ref[...], preferred_element_type=F32)
           + jnp.dot(s_n, ws_ref[...], preferred_element_type=F32))
    o_ref[...] = x_ref[...] + mix


def _out_proj(x2, attn, y, w_glu, b_glu, g_a, g_s, w_out, tm=256):
    t, d = x2.shape
    wa = attn.shape[1]
    ws = y.shape[1]
    row = lambda i: (i, 0)
    fixed = lambda i: (0, 0)
    once = pl.Buffered(1)
    return pl.pallas_call(
        _out_proj_kernel,
        grid=(t // tm,),
        in_specs=[pl.BlockSpec((tm, d), row),
                  pl.BlockSpec((tm, wa), row),
                  pl.BlockSpec((tm, ws), row),
                  pl.BlockSpec((ws, ws), fixed, pipeline_mode=once),
                  pl.BlockSpec((1, ws), fixed),
                  pl.BlockSpec((1, wa), fixed),
                  pl.BlockSpec((1, ws), fixed),
                  pl.BlockSpec((wa, d), fixed, pipeline_mode=once),
                  pl.BlockSpec((ws, d), fixed, pipeline_mode=once)],
        out_specs=pl.BlockSpec((tm, d), row),
        out_shape=jax.ShapeDtypeStruct((t, d), F32),
        compiler_params=pltpu.CompilerParams(dimension_semantics=("parallel",),
                                             vmem_limit_bytes=VMEM_LIMIT),
    )(x2, attn, y, w_glu, b_glu.reshape(1, ws), g_a.reshape(1, wa), g_s.reshape(1, ws),
      w_out[:wa], w_out[wa:])


def _ffn_up_kernel(x_ref, halo_ref, g_ref, wg_ref, wv_ref, cwg_ref, cwv_ref, cbg_ref, cbv_ref,
                   o_ref, h_ref, *, tiles_per_seq):
    tm = x_ref.shape[0]

    @pl.when(pl.program_id(1) == 0)
    def _():
        keep = jnp.where(pl.program_id(0) % tiles_per_seq == 0, 0.0, 1.0)
        h_ref[pl.ds(0, CONV_HALO), :] = (_rms_rows(halo_ref[...], g_ref[...]) * keep).astype(BF16)
        h_ref[pl.ds(CONV_HALO, tm), :] = _rms_rows(x_ref[...], g_ref[...]).astype(BF16)

    h = h_ref[...]

    def conv(w_ref, cw_ref, cb_ref):
        up = jnp.dot(h, w_ref[...], preferred_element_type=F32)
        cw = cw_ref[...]
        out = up[CONV_HALO:] * cw[CONV_WIDTH - 1:CONV_WIDTH] + cb_ref[...]
        for j in range(CONV_WIDTH - 1):
            lag = CONV_WIDTH - 1 - j
            out = out + up[CONV_HALO - lag:CONV_HALO - lag + tm] * cw[j:j + 1]
        return out

    gate = conv(wg_ref, cwg_ref, cbg_ref)
    val = conv(wv_ref, cwv_ref, cbv_ref)
    o_ref[...] = (jax.nn.silu(gate) * val).astype(o_ref.dtype)


def _ffn_up(x1, g, w_up, conv_w, conv_b, seq, tm=1024, tn=512):
    t, d = x1.shape
    f = w_up.shape[1] // 2
    nf = f // tn
    tiles_per_seq = seq // tm
    halo_blocks = tm // CONV_HALO
    kernel = functools.partial(_ffn_up_kernel, tiles_per_seq=tiles_per_seq)
    cb = conv_b.reshape(1, 2 * f)
    return pl.pallas_call(
        kernel,
        grid=(t // tm, nf),
        in_specs=[pl.BlockSpec((tm, d), lambda i, j: (i, 0)),
                  pl.BlockSpec((CONV_HALO, d), lambda i, j: (jnp.maximum(i * halo_blocks - 1, 0), 0)),
                  pl.BlockSpec((1, d), lambda i, j: (0, 0)),
                  pl.BlockSpec((d, tn), lambda i, j: (0, j)),
                  pl.BlockSpec((d, tn), lambda i, j: (0, nf + j)),
                  pl.BlockSpec((CONV_WIDTH, tn), lambda i, j: (0, j)),
                  pl.BlockSpec((CONV_WIDTH, tn), lambda i, j: (0, nf + j)),
                  pl.BlockSpec((1, tn), lambda i, j: (0, j)),
                  pl.BlockSpec((1, tn), lambda i, j: (0, nf + j))],
        out_specs=pl.BlockSpec((tm, tn), lambda i, j: (i, j)),
        out_shape=jax.ShapeDtypeStruct((t, f), BF16),
        scratch_shapes=[pltpu.VMEM((CONV_HALO + tm, d), BF16)],
        compiler_params=pltpu.CompilerParams(dimension_semantics=("parallel", "arbitrary"),
                                             vmem_limit_bytes=VMEM_LIMIT),
    )(x1, x1, g.reshape(1, d), w_up, w_up, conv_w, conv_w, cb, cb)


def _ffn_down_kernel(a_ref, w_ref, x_ref, g_ref, o_ref, *, final_norm):
    y = x_ref[...] + jnp.dot(a_ref[...], w_ref[...], preferred_element_type=F32)
    o_ref[...] = _rms_rows(y, g_ref[...]) if final_norm else y


def _ffn_down(act, w_down, x1, g, final_norm, tm=256):
    t, f = act.shape
    d = w_down.shape[1]
    return pl.pallas_call(
        functools.partial(_ffn_down_kernel, final_norm=final_norm),
        grid=(t // tm,),
        in_specs=[pl.BlockSpec((tm, f), lambda i: (i, 0)),
                  pl.BlockSpec((f, d), lambda i: (0, 0), pipeline_mode=pl.Buffered(1)),
                  pl.BlockSpec((tm, d), lambda i: (i, 0)),
                  pl.BlockSpec((1, d), lambda i: (0, 0))],
        out_specs=pl.BlockSpec((tm, d), lambda i: (i, 0)),
        out_shape=jax.ShapeDtypeStruct((t, d), F32),
        compiler_params=pltpu.CompilerParams(dimension_semantics=("parallel",),
                                             vmem_limit_bytes=VMEM_LIMIT),
    )(act, w_down, x1, g.reshape(1, d))


def kernel(x, norm_mix, w_in, rel_bias_table, ssm_lam_re, ssm_lam_im, ssm_log_dt, ssm_b_re, ssm_b_im, ssm_c_re, ssm_c_im, ssm_d, ssm_w_glu, ssm_b_glu, norm_attn_out, norm_ssm_out, w_out, norm_ffn, w_ffn_up, ffn_conv_w, ffn_conv_b, w_ffn_down, norm_final):
    batch, seq, d_model = x.shape
    depth = w_in.shape[0]
    aw = ATTN_HEADS * HEAD_DIM
    sw = SSM_GROUPS * SSM_GROUP_CH
    t = batch * seq
    n_blocks = seq // MOBA_BLOCK
    assert seq % MOBA_BLOCK == 0 and seq % SSM_CHUNK == 0

    bias_tiles = _bias_tiles(rel_bias_table)
    x2 = x.reshape(t, d_model)
    for l in range(depth):
        w = w_in[l]
        w = jnp.concatenate([w[:, :aw] * (HEAD_DIM ** -0.5 * LOG2E), w[:, aw:]], axis=1).astype(BF16)
        proj = _in_proj(x2, norm_mix[l], w)

        proj3 = proj.reshape(t // MOBA_BLOCK, MOBA_BLOCK, proj.shape[1])
        attn = _moba_attention(proj3, bias_tiles, batch, seq)

        tg, bn, cn, a_re, a_im = _s5_tables(ssm_log_dt[l], ssm_lam_re[l], ssm_lam_im[l],
                                            ssm_b_re[l], ssm_b_im[l], ssm_c_re[l], ssm_c_im[l],
                                            ssm_d[l])
        y = _s5_scan(proj, tg, bn, cn, a_re, a_im, batch, u_col0=3 * aw)

        x2 = _out_proj(x2, attn, y, ssm_w_glu[l].astype(BF16), ssm_b_glu[l], norm_attn_out[l],
                       norm_ssm_out[l], w_out[l].astype(BF16))

        act = _ffn_up(x2, norm_ffn[l], w_ffn_up[l].astype(BF16), ffn_conv_w[l], ffn_conv_b[l], seq)
        x2 = _ffn_down(act, w_ffn_down[l].astype(BF16), x2, norm_final, final_norm=(l == depth - 1))
    return x2.reshape(batch, seq, d_model)
```

```python
import functools
import math

import jax
import jax.numpy as jnp
from jax import lax
from jax.experimental import pallas as pl
from jax.experimental.pallas import tpu as pltpu

F32 = jnp.float32
BF16 = jnp.bfloat16

ATTN_HEADS = 16
HEAD_DIM = 64
SSM_GROUP_CH = 16
SSM_GROUPS = 64
SSM_STATE = 64
MOBA_BLOCK = 256
MOBA_TOP_K = 3
NUM_BUCKETS = 32
MAX_DISTANCE = 128
CONV_WIDTH = 3
RMS_EPS = 1e-6

SSM_CHUNK = 16
MOBA_HEADS_PER_STEP = 4
SCAN_UNROLL = 8
CONV_HALO = 16
VMEM_LIMIT = 52 * 1024 * 1024

NEG_INF = float("-inf")
LOG2E = math.log2(math.e)
V_AUG_ROWS = HEAD_DIM + 16


def _rms_rows(x, g):
    ms = jnp.mean(x * x, axis=-1, keepdims=True)
    return x * lax.rsqrt(ms + RMS_EPS) * g


_NT_DIMS = (((1,), (1,)), ((), ()))


def _in_proj_kernel(x_ref, g_ref, w_ref, o_ref, h_ref):
    @pl.when(pl.program_id(1) == 0)
    def _():
        h_ref[...] = _rms_rows(x_ref[...], g_ref[...]).astype(BF16)

    o_ref[...] = jnp.dot(h_ref[...], w_ref[...], preferred_element_type=F32).astype(o_ref.dtype)


def _in_proj(x2, g, w, tm=1024, tn=512):
    t, d = x2.shape
    n = w.shape[1]
    return pl.pallas_call(
        _in_proj_kernel,
        grid=(t // tm, n // tn),
        in_specs=[pl.BlockSpec((tm, d), lambda i, j: (i, 0)),
                  pl.BlockSpec((1, d), lambda i, j: (0, 0)),
                  pl.BlockSpec((d, tn), lambda i, j: (0, j))],
        out_specs=pl.BlockSpec((tm, tn), lambda i, j: (i, j)),
        out_shape=jax.ShapeDtypeStruct((t, n), BF16),
        scratch_shapes=[pltpu.VMEM((tm, d), BF16)],
        compiler_params=pltpu.CompilerParams(dimension_semantics=("parallel", "arbitrary"),
                                             vmem_limit_bytes=VMEM_LIMIT),
    )(x2, g.reshape(1, d), w)


def _t5_bucket(dist):
    dist = jnp.maximum(dist, 0)
    max_exact = NUM_BUCKETS // 2
    log_ratio = jnp.log(jnp.maximum(dist, max_exact).astype(F32) / max_exact)
    large = max_exact + (log_ratio / math.log(MAX_DISTANCE / max_exact)
                         * (NUM_BUCKETS - max_exact)).astype(jnp.int32)
    large = jnp.minimum(large, NUM_BUCKETS - 1)
    return jnp.where(dist < max_exact, dist, large)


def _bias_tiles_kernel(tab_ref, o_ref):
    h = pl.program_id(0)
    c_far = tab_ref[NUM_BUCKETS - 1, h]
    kk = lax.broadcasted_iota(jnp.int32, (MOBA_BLOCK, MOBA_BLOCK), 0)
    qq = lax.broadcasted_iota(jnp.int32, (MOBA_BLOCK, MOBA_BLOCK), 1)
    for which, off in ((0, MOBA_BLOCK), (1, 0)):
        rel = qq - kk + off
        bucket = _t5_bucket(rel)
        acc = jnp.zeros((MOBA_BLOCK, MOBA_BLOCK), F32)
        for b in range(NUM_BUCKETS):
            acc = jnp.where(bucket == b, tab_ref[b, h], acc)
        acc = (acc - c_far) * LOG2E
        if which == 1:
            acc = jnp.where(rel >= 0, acc, NEG_INF)
        o_ref[0, which] = acc


def _bias_tiles(table):
    return pl.pallas_call(
        _bias_tiles_kernel,
        grid=(ATTN_HEADS,),
        in_specs=[pl.BlockSpec(memory_space=pltpu.SMEM)],
        out_specs=pl.BlockSpec((1, 2, MOBA_BLOCK, MOBA_BLOCK), lambda h: (h, 0, 0, 0)),
        out_shape=jax.ShapeDtypeStruct((ATTN_HEADS, 2, MOBA_BLOCK, MOBA_BLOCK), F32),
    )(table)


def _moba_kernel(q_ref, k_ref, v_ref, bias_ref, o_ref,
                 kmf_ref, vaug_ref, qm_ref, sa_ref, sb_ref, cma_ref, cmb_ref,
                 m_ref, acc_ref, rb_ref, ot_ref, *, n_blocks):
    qi = pl.program_id(2)
    blk = MOBA_BLOCK
    nh = qm_ref.shape[0]

    @pl.when(qi == 0)
    def _():
        ones = jnp.ones((V_AUG_ROWS - HEAD_DIM, blk), BF16)
        for j in range(n_blocks):
            kb = k_ref[j].astype(F32)
            kmf_ref[pl.ds(j, 1), :] = jnp.sum(kb, axis=0, keepdims=True) * (1.0 / blk)
            vt = v_ref[j].T
            for hh in range(nh):
                vaug_ref[j, hh, 0:HEAD_DIM, :] = vt[hh * HEAD_DIM:(hh + 1) * HEAD_DIM, :]
                vaug_ref[j, hh, HEAD_DIM:V_AUG_ROWS, :] = ones

    kmf = kmf_ref[...]
    km_hi = kmf.astype(BF16)
    km_lo = (kmf - km_hi.astype(F32)).astype(BF16)

    q2 = q_ref[0].T
    q_zero = jnp.zeros((HEAD_DIM, blk), q2.dtype)
    blk_idx = lax.broadcasted_iota(jnp.int32, (n_blocks, blk), 0)

    for hh in range(nh):
        vrows = slice(hh * HEAD_DIM, (hh + 1) * HEAD_DIM)
        qm = jnp.concatenate([q_zero] * hh + [q2[vrows]] + [q_zero] * (nh - 1 - hh), axis=0)
        qm_ref[hh] = qm

        gate = (jnp.dot(km_hi, qm, preferred_element_type=F32)
                + jnp.dot(km_lo, qm, preferred_element_type=F32))
        valid = blk_idx < qi
        g = jnp.where(valid, gate, NEG_INF)
        rank = jnp.zeros(g.shape, F32)
        for j in range(n_blocks):
            gj = g[j:j + 1, :]
            beats = jnp.where(gj > g, 1.0, jnp.where((gj == g) & (blk_idx > j), 1.0, 0.0))
            rank = rank + beats
        sel = (valid & (rank < MOBA_TOP_K)) | (blk_idx == qi)
        rb_ref[hh] = jnp.where(sel, 0.0, NEG_INF)
        m_ref[hh] = jnp.full((1, blk), NEG_INF, F32)
        acc_ref[hh] = jnp.zeros((V_AUG_ROWS, blk), F32)

    def group_blocks(g):
        ja = qi - 2 * g
        return (ja, None), (jnp.maximum(ja - 1, 0), ja >= 1)

    def scores(g, s_buf, cm_buf, with_bias):
        for i, (j, _) in enumerate(group_blocks(g)):
            kb = k_ref[j]
            for hh in range(nh):
                s = jnp.dot(kb, qm_ref[hh], preferred_element_type=F32)
                if with_bias:
                    s = s + bias_ref[hh, 1 - i]
                s_buf[hh, i] = s
                cm_buf[hh, pl.ds(i, 1), :] = jnp.max(s, axis=0, keepdims=True)

    def attend(g, s_buf, cm_buf):
        blocks = group_blocks(g)
        for hh in range(nh):
            masks = []
            m_old = m_ref[hh]
            m_new = m_old
            for i, (j, exists) in enumerate(blocks):
                mk = rb_ref[hh, pl.ds(j, 1), :]
                if exists is not None:
                    mk = jnp.where(exists, mk, NEG_INF)
                masks.append(mk)
                m_new = jnp.maximum(m_new, cm_buf[hh, pl.ds(i, 1), :] + mk)
            acc = jnp.exp2(m_old - m_new) * acc_ref[hh]
            for i, (j, _) in enumerate(blocks):
                p = jnp.exp2((s_buf[hh, i] - m_new).astype(BF16))
                pv = jnp.dot(vaug_ref[j, hh], p, preferred_element_type=F32)
                acc = acc + jnp.where(masks[i] == 0.0, pv, 0.0)
            acc_ref[hh] = acc
            m_ref[hh] = m_new

    n_more = qi // 2
    scores(0, sa_ref, cma_ref, True)

    def step(it, carry):
        @pl.when(it % 2 == 0)
        def _():
            scores(it + 1, sb_ref, cmb_ref, False)
            attend(it, sa_ref, cma_ref)

        @pl.when(it % 2 == 1)
        def _():
            scores(it + 1, sa_ref, cma_ref, False)
            attend(it, sb_ref, cmb_ref)

        return carry

    lax.fori_loop(0, n_more, step, 0)

    @pl.when(n_more % 2 == 0)
    def _():
        attend(n_more, sa_ref, cma_ref)

    @pl.when(n_more % 2 == 1)
    def _():
        attend(n_more, sb_ref, cmb_ref)

    for hh in range(nh):
        acc = acc_ref[hh]
        ot_ref[hh * HEAD_DIM:(hh + 1) * HEAD_DIM, :] = (
            acc[0:HEAD_DIM] * (1.0 / acc[HEAD_DIM:HEAD_DIM + 1]))

    o_ref[...] = ot_ref[...].T.astype(o_ref.dtype)


def _moba_attention(proj3, bias_tiles, batch, seq):
    n_blocks = seq // MOBA_BLOCK
    nh = MOBA_HEADS_PER_STEP
    n_hg = ATTN_HEADS // nh
    hg_w = nh * HEAD_DIM
    kernel = functools.partial(_moba_kernel, n_blocks=n_blocks)
    return pl.pallas_call(
        kernel,
        grid=(batch, n_hg, n_blocks),
        in_specs=[
            pl.BlockSpec((1, MOBA_BLOCK, hg_w), lambda b, hg, qi: (b * n_blocks + qi, 0, hg)),
            pl.BlockSpec((n_blocks, MOBA_BLOCK, hg_w), lambda b, hg, qi: (b, 0, n_hg + hg)),
            pl.BlockSpec((n_blocks, MOBA_BLOCK, hg_w), lambda b, hg, qi: (b, 0, 2 * n_hg + hg)),
            pl.BlockSpec((nh, 2, MOBA_BLOCK, MOBA_BLOCK), lambda b, hg, qi: (hg, 0, 0, 0)),
        ],
        out_specs=pl.BlockSpec((MOBA_BLOCK, hg_w), lambda b, hg, qi: (b * n_blocks + qi, hg)),
        out_shape=jax.ShapeDtypeStruct((batch * seq, ATTN_HEADS * HEAD_DIM), BF16),
        scratch_shapes=[
            pltpu.VMEM((n_blocks, hg_w), F32),
            pltpu.VMEM((n_blocks, nh, V_AUG_ROWS, MOBA_BLOCK), BF16),
            pltpu.VMEM((nh, hg_w, MOBA_BLOCK), BF16),
            pltpu.VMEM((nh, 2, MOBA_BLOCK, MOBA_BLOCK), F32),
            pltpu.VMEM((nh, 2, MOBA_BLOCK, MOBA_BLOCK), F32),
            pltpu.VMEM((nh, 8, MOBA_BLOCK), F32),
            pltpu.VMEM((nh, 8, MOBA_BLOCK), F32),
            pltpu.VMEM((nh, 1, MOBA_BLOCK), F32),
            pltpu.VMEM((nh, V_AUG_ROWS, MOBA_BLOCK), F32),
            pltpu.VMEM((nh, n_blocks, MOBA_BLOCK), F32),
            pltpu.VMEM((hg_w, MOBA_BLOCK), F32),
        ],
        compiler_params=pltpu.CompilerParams(
            dimension_semantics=("parallel", "parallel", "arbitrary"),
            vmem_limit_bytes=VMEM_LIMIT),
    )(proj3, proj3, proj3, bias_tiles)


def _hdot_nt(a, b):
    return lax.dot_general(a, b, _NT_DIMS, preferred_element_type=F32,
                           precision=lax.Precision.HIGHEST)


def _repeat_rows(x, n):
    return jnp.concatenate([jnp.broadcast_to(x[i:i + 1], (n, x.shape[1]))
                            for i in range(x.shape[0])], axis=0)


def _tile_rows(x, n):
    return jnp.concatenate([x] * n, axis=0)


def _s5_tables_kernel(logdt_ref, lre_ref, lim_ref, bre_ref, bim_ref, cre_ref, cim_ref, d_ref,
                      tg_ref, bn_ref, cn_ref, are_ref, aim_ref):
    lc, ch, p = SSM_CHUNK, SSM_GROUP_CH, SSM_STATE
    w = lc * ch
    rr = lax.broadcasted_iota(jnp.int32, (w, w), 0)
    cc = lax.broadcasted_iota(jnp.int32, (w, w), 1)
    causal = rr // ch >= cc // ch
    diag = rr == cc
    e_r = lax.broadcasted_iota(jnp.int32, (ch, w), 0)
    e_c = lax.broadcasted_iota(jnp.int32, (ch, w), 1)
    lane_tile = jnp.where(e_c % ch == e_r, 1.0, 0.0).astype(F32)
    tau = lax.broadcasted_iota(jnp.int32, (2 * lc, p), 0).astype(F32)

    for k in range(tg_ref.shape[0]):
        lam_re = lre_ref[k]
        lam_im = lim_ref[k]
        dt = jnp.exp(logdt_ref[k])
        lr = lam_re * dt
        li = lam_im * dt

        mag = jnp.exp(lr * tau)
        cs = jnp.cos(li * tau)
        sn = jnp.sin(li * tau)
        pos_re, pos_im = mag * cs, mag * sn
        inv = jnp.exp(-lr * tau[:lc])
        neg_re, neg_im = inv * cs[:lc], -inv * sn[:lc]

        lam1_re, lam1_im = pos_re[1:2], pos_im[1:2]
        lamk_re, lamk_im = pos_re[lc - 1:lc], pos_im[lc - 1:lc]
        are_ref[k] = pos_re[lc:lc + 1]
        aim_ref[k] = pos_im[lc:lc + 1]

        num_re, num_im = lam1_re - 1.0, lam1_im
        den = lam_re * lam_re + lam_im * lam_im
        coef_re = (num_re * lam_re + num_im * lam_im) / den
        coef_im = (num_im * lam_re - num_re * lam_im) / den
        bt_re, bt_im = bre_ref[k], bim_ref[k]
        bb_re = coef_re * bt_re - coef_im * bt_im
        bb_im = coef_re * bt_im + coef_im * bt_re

        bbt_re, bbt_im = _tile_rows(bb_re, lc), _tile_rows(bb_im, lc)
        ngx_re, ngx_im = _repeat_rows(neg_re, ch), _repeat_rows(neg_im, ch)
        bneg_re = bbt_re * ngx_re - bbt_im * ngx_im
        bneg_im = bbt_re * ngx_im + bbt_im * ngx_re

        ct_re, ct_im = _tile_rows(cre_ref[k], lc), _tile_rows(cim_ref[k], lc)
        psx_re, psx_im = _repeat_rows(pos_re[:lc], ch), _repeat_rows(pos_im[:lc], ch)
        cpos_re = ct_re * psx_re - ct_im * psx_im
        cpos_im = ct_re * psx_im + ct_im * psx_re

        raw = _hdot_nt(cpos_re, bneg_re) - _hdot_nt(cpos_im, bneg_im)
        d_lanes = jnp.dot(jnp.broadcast_to(d_ref[k], (8, ch)), lane_tile,
                          preferred_element_type=F32, precision=lax.Precision.HIGHEST)[0:1]
        tg = jnp.where(causal, raw, 0.0) + jnp.where(diag, d_lanes, 0.0)
        tg_ref[k] = tg.astype(tg_ref.dtype)

        bn_ref[k, 0] = (bneg_re * lamk_re - bneg_im * lamk_im).T.astype(bn_ref.dtype)
        bn_ref[k, 1] = (bneg_re * lamk_im + bneg_im * lamk_re).T.astype(bn_ref.dtype)
        cp_re = cpos_re * lam1_re - cpos_im * lam1_im
        cp_im = cpos_re * lam1_im + cpos_im * lam1_re
        cn_ref[k, 0] = cp_re.astype(cn_ref.dtype)
        cn_ref[k, 1] = (-cp_im).astype(cn_ref.dtype)


def _s5_tables(log_dt, lam_re, lam_im, b_re, b_im, c_re, c_im, d_skip, gb=4):
    g, p, ch = SSM_GROUPS, SSM_STATE, SSM_GROUP_CH
    w = SSM_CHUNK * ch
    row = lambda a, n: a.reshape(g, 1, n)
    spec3 = lambda s1, s2: pl.BlockSpec((gb, s1, s2), lambda i: (i, 0, 0))
    spec4 = lambda s1, s2: pl.BlockSpec((gb, 2, s1, s2), lambda i: (i, 0, 0, 0))
    return pl.pallas_call(
        _s5_tables_kernel,
        grid=(g // gb,),
        in_specs=[spec3(1, 1), spec3(1, p), spec3(1, p), spec3(ch, p), spec3(ch, p),
                  spec3(ch, p), spec3(ch, p), spec3(1, ch)],
        out_specs=[spec3(w, w), spec4(p, w), spec4(w, p), spec3(1, p), spec3(1, p)],
        out_shape=[jax.ShapeDtypeStruct((g, w, w), BF16),
                   jax.ShapeDtypeStruct((g, 2, p, w), BF16),
                   jax.ShapeDtypeStruct((g, 2, w, p), BF16),
                   jax.ShapeDtypeStruct((g, 1, p), F32),
                   jax.ShapeDtypeStruct((g, 1, p), F32)],
        compiler_params=pltpu.CompilerParams(dimension_semantics=("parallel",)),
    )(row(log_dt, 1), row(lam_re, p), row(lam_im, p),
      jnp.swapaxes(b_re, 1, 2), jnp.swapaxes(b_im, 1, 2), c_re, c_im, row(d_skip, ch))


def _s5_scan_kernel(u_ref, tg_ref, bn_ref, cn_ref, are_ref, aim_ref, y_ref,
                    uf_ref, v_ref, yt_ref, sre_ref, sim_ref, xre_ref, xim_ref, *, n_batch):
    lc, ch = SSM_CHUNK, SSM_GROUP_CH
    gb = tg_ref.shape[0]
    n_rows = u_ref.shape[0] // lc
    n_chunks = n_rows // n_batch
    w = lc * ch

    uf_ref[...] = u_ref[...].astype(F32)
    for s in range(lc):
        ust = uf_ref[pl.ds(s, n_rows, stride=lc), :].T
        for k in range(gb):
            v_ref[k, s * ch:(s + 1) * ch, :] = ust[k * ch:(k + 1) * ch, :].astype(v_ref.dtype)

    for k in range(gb):
        u = v_ref[k]
        rows = pl.ds(k, n_rows, stride=gb)
        sre_ref[rows, :] = jnp.dot(bn_ref[k, 0], u, preferred_element_type=F32).T
        sim_ref[rows, :] = jnp.dot(bn_ref[k, 1], u, preferred_element_type=F32).T

    a_re = are_ref[...]
    a_im = aim_ref[...]

    def step(c, carry):
        new = []
        for b in range(n_batch):
            x_re, x_im = carry[2 * b], carry[2 * b + 1]
            rows = pl.ds(pl.multiple_of((b * n_chunks + c) * gb, gb), gb)
            xre_ref[rows, :] = x_re
            xim_ref[rows, :] = x_im
            new.append(a_re * x_re - a_im * x_im + sre_ref[rows, :])
            new.append(a_re * x_im + a_im * x_re + sim_ref[rows, :])
        return tuple(new)

    zero = jnp.zeros(a_re.shape, F32)
    lax.fori_loop(0, n_chunks, step, (zero,) * (2 * n_batch), unroll=SCAN_UNROLL)

    for k in range(gb):
        rows = pl.ds(k, n_rows, stride=gb)
        y = jnp.dot(tg_ref[k], v_ref[k], preferred_element_type=F32)
        y = y + lax.dot_general(cn_ref[k, 0], xre_ref[rows, :].astype(BF16), _NT_DIMS,
                                preferred_element_type=F32)
        y = y + lax.dot_general(cn_ref[k, 1], xim_ref[rows, :].astype(BF16), _NT_DIMS,
                                preferred_element_type=F32)
        for s in range(lc):
            yt_ref[s, k * ch:(k + 1) * ch, :] = y[s * ch:(s + 1) * ch, :]

    for s in range(lc):
        y_ref[pl.ds(s, n_rows, stride=lc), :] = yt_ref[s].T.astype(y_ref.dtype)


def _s5_scan(ku, tg, bn, cn, a_re, a_im, n_batch, u_col0, gb=8):
    t = ku.shape[0]
    g, w, _ = tg.shape
    p, ch = SSM_STATE, SSM_GROUP_CH
    n_rows = t // SSM_CHUNK
    lanes = gb * ch
    kernel = functools.partial(_s5_scan_kernel, n_batch=n_batch)
    return pl.pallas_call(
        kernel,
        grid=(g // gb,),
        in_specs=[pl.BlockSpec((t, lanes), lambda i: (0, u_col0 // lanes + i)),
                  pl.BlockSpec((gb, w, w), lambda i: (i, 0, 0)),
                  pl.BlockSpec((gb, 2, p, w), lambda i: (i, 0, 0, 0)),
                  pl.BlockSpec((gb, 2, w, p), lambda i: (i, 0, 0, 0)),
                  pl.BlockSpec((gb, p), lambda i: (i, 0)),
                  pl.BlockSpec((gb, p), lambda i: (i, 0))],
        out_specs=pl.BlockSpec((t, lanes), lambda i: (0, i)),
        out_shape=jax.ShapeDtypeStruct((t, g * ch), F32),
        scratch_shapes=[pltpu.VMEM((t, lanes), F32),
                        pltpu.VMEM((gb, w, n_rows), BF16),
                        pltpu.VMEM((SSM_CHUNK, lanes, n_rows), F32)]
                       + [pltpu.VMEM((gb * n_rows, p), F32) for _ in range(4)],
        compiler_params=pltpu.CompilerParams(dimension_semantics=("parallel",),
                                             vmem_limit_bytes=VMEM_LIMIT),
    )(ku, tg, bn, cn, a_re.reshape(g, p), a_im.reshape(g, p))


def _out_proj_kernel(x_ref, a_ref, y_ref, wglu_ref, bglu_ref, ga_ref, gs_ref, wa_ref, ws_ref, o_ref):
    z = jax.nn.gelu(y_ref[...])
    gl = jnp.dot(z.astype(BF16), wglu_ref[...], preferred_element_type=F32) + bglu_ref[...]
    s = z * jax.nn.sigmoid(gl)
    a_n = _rms_rows(a_ref[...].astype(F32), ga_ref[...]).astype(BF16)
    s_n = _rms_rows(s, gs_ref[...]).astype(BF16)
    mix = (jnp.dot(a_n, wa_ref[...], preferred_element_type=F32)
           + jnp.dot(s_n, ws_ref[...], preferred_element_type=F32))
    o_ref[...] = x_ref[...] + mix


def _out_proj(x2, attn, y, w_glu, b_glu, g_a, g_s, w_out, tm=256):
    t, d = x2.shape
    wa = attn.shape[1]
    ws = y.shape[1]
    row = lambda i: (i, 0)
    fixed = lambda i: (0, 0)
    once = pl.Buffered(1)
    return pl.pallas_call(
        _out_proj_kernel,
        grid=(t // tm,),
        in_specs=[pl.BlockSpec((tm, d), row),
                  pl.BlockSpec((tm, wa), row),
                  pl.BlockSpec((tm, ws), row),
                  pl.BlockSpec((ws, ws), fixed, pipeline_mode=once),
                  pl.BlockSpec((1, ws), fixed),
                  pl.BlockSpec((1, wa), fixed),
                  pl.BlockSpec((1, ws), fixed),
                  pl.BlockSpec((wa, d), fixed, pipeline_mode=once),
                  pl.BlockSpec((ws, d), fixed, pipeline_mode=once)],
        out_specs=pl.BlockSpec((tm, d), row),
        out_shape=jax.ShapeDtypeStruct((t, d), F32),
        compiler_params=pltpu.CompilerParams(dimension_semantics=("parallel",),
                                             vmem_limit_bytes=VMEM_LIMIT),
    )(x2, attn, y, w_glu, b_glu.reshape(1, ws), g_a.reshape(1, wa), g_s.reshape(1, ws),
      w_out[:wa], w_out[wa:])


def _ffn_up_kernel(x_ref, halo_ref, g_ref, wg_ref, wv_ref, cwg_ref, cwv_ref, cbg_ref, cbv_ref,
                   o_ref, h_ref, *, tiles_per_seq):
    tm = x_ref.shape[0]

    @pl.when(pl.program_id(1) == 0)
    def _():
        keep = jnp.where(pl.program_id(0) % tiles_per_seq == 0, 0.0, 1.0)
        h_ref[pl.ds(0, CONV_HALO), :] = (_rms_rows(halo_ref[...], g_ref[...]) * keep).astype(BF16)
        h_ref[pl.ds(CONV_HALO, tm), :] = _rms_rows(x_ref[...], g_ref[...]).astype(BF16)

    h = h_ref[...]

    def conv(w_ref, cw_ref, cb_ref):
        up = jnp.dot(h, w_ref[...], preferred_element_type=F32)
        cw = cw_ref[...]
        out = up[CONV_HALO:] * cw[CONV_WIDTH - 1:CONV_WIDTH] + cb_ref[...]
        for j in range(CONV_WIDTH - 1):
            lag = CONV_WIDTH - 1 - j
            out = out + up[CONV_HALO - lag:CONV_HALO - lag + tm] * cw[j:j + 1]
        return out

    gate = conv(wg_ref, cwg_ref, cbg_ref)
    val = conv(wv_ref, cwv_ref, cbv_ref)
    o_ref[...] = (jax.nn.silu(gate) * val).astype(o_ref.dtype)


def _ffn_up(x1, g, w_up, conv_w, conv_b, seq, tm=1024, tn=512):
    t, d = x1.shape
    f = w_up.shape[1] // 2
    nf = f // tn
    tiles_per_seq = seq // tm
    halo_blocks = tm // CONV_HALO
    kernel = functools.partial(_ffn_up_kernel, tiles_per_seq=tiles_per_seq)
    cb = conv_b.reshape(1, 2 * f)
    return pl.pallas_call(
        kernel,
        grid=(t // tm, nf),
        in_specs=[pl.BlockSpec((tm, d), lambda i, j: (i, 0)),
                  pl.BlockSpec((CONV_HALO, d), lambda i, j: (jnp.maximum(i * halo_blocks - 1, 0), 0)),
                  pl.BlockSpec((1, d), lambda i, j: (0, 0)),
                  pl.BlockSpec((d, tn), lambda i, j: (0, j)),
                  pl.BlockSpec((d, tn), lambda i, j: (0, nf + j)),
                  pl.BlockSpec((CONV_WIDTH, tn), lambda i, j: (0, j)),
                  pl.BlockSpec((CONV_WIDTH, tn), lambda i, j: (0, nf + j)),
                  pl.BlockSpec((1, tn), lambda i, j: (0, j)),
                  pl.BlockSpec((1, tn), lambda i, j: (0, nf + j))],
        out_specs=pl.BlockSpec((tm, tn), lambda i, j: (i, j)),
        out_shape=jax.ShapeDtypeStruct((t, f), BF16),
        scratch_shapes=[pltpu.VMEM((CONV_HALO + tm, d), BF16)],
        compiler_params=pltpu.CompilerParams(dimension_semantics=("parallel", "arbitrary"),
                                             vmem_limit_bytes=VMEM_LIMIT),
    )(x1, x1, g.reshape(1, d), w_up, w_up, conv_w, conv_w, cb, cb)


def _ffn_down_kernel(a_ref, w_ref, x_ref, g_ref, o_ref, *, final_norm):
    y = x_ref[...] + jnp.dot(a_ref[...], w_ref[...], preferred_element_type=F32)
    o_ref[...] = _rms_rows(y, g_ref[...]) if final_norm else y


def _ffn_down(act, w_down, x1, g, final_norm, tm=256):
    t, f = act.shape
    d = w_down.shape[1]
    return pl.pallas_call(
        functools.partial(_ffn_down_kernel, final_norm=final_norm),
        grid=(t // tm,),
        in_specs=[pl.BlockSpec((tm, f), lambda i: (i, 0)),
                  pl.BlockSpec((f, d), lambda i: (0, 0), pipeline_mode=pl.Buffered(1)),
                  pl.BlockSpec((tm, d), lambda i: (i, 0)),
                  pl.BlockSpec((1, d), lambda i: (0, 0))],
        out_specs=pl.BlockSpec((tm, d), lambda i: (i, 0)),
        out_shape=jax.ShapeDtypeStruct((t, d), F32),
        compiler_params=pltpu.CompilerParams(dimension_semantics=("parallel",),
                                             vmem_limit_bytes=VMEM_LIMIT),
    )(act, w_down, x1, g.reshape(1, d))


def kernel(x, norm_mix, w_in, rel_bias_table, ssm_lam_re, ssm_lam_im, ssm_log_dt, ssm_b_re, ssm_b_im, ssm_c_re, ssm_c_im, ssm_d, ssm_w_glu, ssm_b_glu, norm_attn_out, norm_ssm_out, w_out, norm_ffn, w_ffn_up, ffn_conv_w, ffn_conv_b, w_ffn_down, norm_final):
    batch, seq, d_model = x.shape
    depth = w_in.shape[0]
    aw = ATTN_HEADS * HEAD_DIM
    sw = SSM_GROUPS * SSM_GROUP_CH
    t = batch * seq
    n_blocks = seq // MOBA_BLOCK
    assert seq % MOBA_BLOCK == 0 and seq % SSM_CHUNK == 0

    bias_tiles = _bias_tiles(rel_bias_table)
    x2 = x.reshape(t, d_model)
    for l in range(depth):
        w = w_in[l]
        w = jnp.concatenate([w[:, :aw] * (HEAD_DIM ** -0.5 * LOG2E), w[:, aw:]], axis=1).astype(BF16)
        proj = _in_proj(x2, norm_mix[l], w)

        proj3 = proj.reshape(t // MOBA_BLOCK, MOBA_BLOCK, proj.shape[1])
        attn = _moba_attention(proj3, bias_tiles, batch, seq)

        tg, bn, cn, a_re, a_im = _s5_tables(ssm_log_dt[l], ssm_lam_re[l], ssm_lam_im[l],
                                            ssm_b_re[l], ssm_b_im[l], ssm_c_re[l], ssm_c_im[l],
                                            ssm_d[l])
        y = _s5_scan(proj, tg, bn, cn, a_re, a_im, batch, u_col0=3 * aw)

        x2 = _out_proj(x2, attn, y, ssm_w_glu[l].astype(BF16), ssm_b_glu[l], norm_attn_out[l],
                       norm_ssm_out[l], w_out[l].astype(BF16))

        act = _ffn_up(x2, norm_ffn[l], w_ffn_up[l].astype(BF16), ffn_conv_w[l], ffn_conv_b[l], seq)
        x2 = _ffn_down(act, w_ffn_down[l].astype(BF16), x2, norm_final, final_norm=(l == depth - 1))
    return x2.reshape(batch, seq, d_model)
```

```python
import functools
import math

import jax
import jax.numpy as jnp
from jax import lax
from jax.experimental import pallas as pl
from jax.experimental.pallas import tpu as pltpu

F32 = jnp.float32
BF16 = jnp.bfloat16

ATTN_HEADS = 16
HEAD_DIM = 64
SSM_GROUP_CH = 16
SSM_GROUPS = 64
SSM_STATE = 64
MOBA_BLOCK = 256
MOBA_TOP_K = 3
NUM_BUCKETS = 32
MAX_DISTANCE = 128
CONV_WIDTH = 3
RMS_EPS = 1e-6

SSM_CHUNK = 16
MOBA_HEADS_PER_STEP = 8
MXU_DEPTH = 256
SCAN_UNROLL = 8
CONV_HALO = 16
VMEM_LIMIT = 52 * 1024 * 1024

NEG_INF = float("-inf")
LOG2E = math.log2(math.e)
V_AUG_ROWS = HEAD_DIM + 16


def _rms_rows(x, g):
    ms = jnp.mean(x * x, axis=-1, keepdims=True)
    return x * lax.rsqrt(ms + RMS_EPS) * g


_NT_DIMS = (((1,), (1,)), ((), ()))


def _in_proj_kernel(x_ref, g_ref, w_ref, o_ref, h_ref):
    @pl.when(pl.program_id(1) == 0)
    def _():
        h_ref[...] = _rms_rows(x_ref[...], g_ref[...]).astype(BF16)

    o_ref[...] = jnp.dot(h_ref[...], w_ref[...], preferred_element_type=F32).astype(o_ref.dtype)


def _in_proj(x2, g, w, tm=1024, tn=512):
    t, d = x2.shape
    n = w.shape[1]
    return pl.pallas_call(
        _in_proj_kernel,
        grid=(t // tm, n // tn),
        in_specs=[pl.BlockSpec((tm, d), lambda i, j: (i, 0)),
                  pl.BlockSpec((1, d), lambda i, j: (0, 0)),
                  pl.BlockSpec((d, tn), lambda i, j: (0, j))],
        out_specs=pl.BlockSpec((tm, tn), lambda i, j: (i, j)),
        out_shape=jax.ShapeDtypeStruct((t, n), BF16),
        scratch_shapes=[pltpu.VMEM((tm, d), BF16)],
        compiler_params=pltpu.CompilerParams(dimension_semantics=("parallel", "arbitrary"),
                                             vmem_limit_bytes=VMEM_LIMIT),
    )(x2, g.reshape(1, d), w)


def _t5_bucket(dist):
    dist = jnp.maximum(dist, 0)
    max_exact = NUM_BUCKETS // 2
    log_ratio = jnp.log(jnp.maximum(dist, max_exact).astype(F32) / max_exact)
    large = max_exact + (log_ratio / math.log(MAX_DISTANCE / max_exact)
                         * (NUM_BUCKETS - max_exact)).astype(jnp.int32)
    large = jnp.minimum(large, NUM_BUCKETS - 1)
    return jnp.where(dist < max_exact, dist, large)


def _bias_tiles_kernel(tab_ref, o_ref):
    h = pl.program_id(0)
    c_far = tab_ref[NUM_BUCKETS - 1, h]
    kk = lax.broadcasted_iota(jnp.int32, (MOBA_BLOCK, MOBA_BLOCK), 0)
    qq = lax.broadcasted_iota(jnp.int32, (MOBA_BLOCK, MOBA_BLOCK), 1)
    for which, off in ((0, MOBA_BLOCK), (1, 0)):
        rel = qq - kk + off
        bucket = _t5_bucket(rel)
        acc = jnp.zeros((MOBA_BLOCK, MOBA_BLOCK), F32)
        for b in range(NUM_BUCKETS):
            acc = jnp.where(bucket == b, tab_ref[b, h], acc)
        acc = (acc - c_far) * LOG2E
        if which == 1:
            acc = jnp.where(rel >= 0, acc, NEG_INF)
        o_ref[0, which] = acc


def _bias_tiles(table):
    return pl.pallas_call(
        _bias_tiles_kernel,
        grid=(ATTN_HEADS,),
        in_specs=[pl.BlockSpec(memory_space=pltpu.SMEM)],
        out_specs=pl.BlockSpec((1, 2, MOBA_BLOCK, MOBA_BLOCK), lambda h: (h, 0, 0, 0)),
        out_shape=jax.ShapeDtypeStruct((ATTN_HEADS, 2, MOBA_BLOCK, MOBA_BLOCK), F32),
    )(table)


def _moba_kernel(q_ref, k_ref, v_ref, bias_ref, o_ref,
                 kmf_ref, vaug_ref, qm_ref, sa_ref, sb_ref, cma_ref, cmb_ref,
                 m_ref, acc_ref, rb_ref, ot_ref, *, n_blocks):
    qi = pl.program_id(2)
    blk = MOBA_BLOCK
    nh = qm_ref.shape[0]

    @pl.when(qi == 0)
    def _():
        ones = jnp.ones((V_AUG_ROWS - HEAD_DIM, blk), BF16)
        for j in range(n_blocks):
            kb = k_ref[j * blk:(j + 1) * blk, :].astype(F32)
            kmf_ref[pl.ds(j, 1), :] = jnp.sum(kb, axis=0, keepdims=True) * (1.0 / blk)
            vt = v_ref[j].T
            for hh in range(nh):
                vaug_ref[j, hh, 0:HEAD_DIM, :] = vt[hh * HEAD_DIM:(hh + 1) * HEAD_DIM, :]
                vaug_ref[j, hh, HEAD_DIM:V_AUG_ROWS, :] = ones

    kmf = kmf_ref[...]
    km_hi = kmf.astype(BF16)
    km_lo = (kmf - km_hi.astype(F32)).astype(BF16)

    q2 = q_ref[0].T
    q_zero = jnp.zeros((HEAD_DIM, blk), q2.dtype)
    blk_idx = lax.broadcasted_iota(jnp.int32, (n_blocks, blk), 0)
    blk_idx_f = blk_idx.astype(F32)
    valid = blk_idx < qi
    hps = qm_ref.shape[1] // HEAD_DIM
    slab = lambda hh: slice((hh // hps) * hps * HEAD_DIM, (hh // hps + 1) * hps * HEAD_DIM)

    for hh in range(nh):
        vrows = slice(hh * HEAD_DIM, (hh + 1) * HEAD_DIM)
        qm = jnp.concatenate([q_zero] * (hh % hps) + [q2[vrows]] + [q_zero] * (hps - 1 - hh % hps),
                             axis=0)
        qm_ref[hh] = qm

        gate = (jnp.dot(km_hi[:, slab(hh)], qm, preferred_element_type=F32)
                + jnp.dot(km_lo[:, slab(hh)], qm, preferred_element_type=F32))
        g = jnp.where(valid, gate, NEG_INF)
        sel = blk_idx == qi
        for _ in range(MOBA_TOP_K):
            top = jnp.max(g, axis=0, keepdims=True)
            first = jnp.min(jnp.where(g == top, blk_idx_f, float(n_blocks)), axis=0, keepdims=True)
            hit = blk_idx_f == first
            sel = sel | (hit & valid)
            g = jnp.where(hit, NEG_INF, g)
        rb_ref[hh] = jnp.where(sel, 0.0, NEG_INF)
        m_ref[hh] = jnp.full((1, blk), NEG_INF, F32)
        acc_ref[hh] = jnp.zeros((V_AUG_ROWS, blk), F32)

    def group_rows(g):
        ja = qi - 2 * g
        return ja, jnp.maximum(ja - 1, 0)

    def scores(hh, g, s_buf, cm_buf, first_group):
        _, lo = group_rows(g)
        kslab = k_ref[pl.ds(pl.multiple_of(lo * blk, blk), 2 * blk), slab(hh)]
        s = jnp.dot(kslab, qm_ref[hh], preferred_element_type=F32)
        for i in range(2):
            si = s[i * blk:(i + 1) * blk]
            if first_group:
                tile = jnp.where(qi == 0, 1, 0) if i == 0 else 1
                si = si + bias_ref[hh, tile]
            s_buf[hh, i] = si
            cm_buf[hh, pl.ds(i, 1), :] = jnp.max(si, axis=0, keepdims=True)

    def attend(hh, g, s_buf, cm_buf):
        ja, lo = group_rows(g)
        masks = [rb_ref[hh, pl.ds(lo, 1), :],
                 jnp.where(ja >= 1, rb_ref[hh, pl.ds(lo + 1, 1), :], NEG_INF)]
        m_old = m_ref[hh]
        m_new = m_old
        for i in range(2):
            m_new = jnp.maximum(m_new, cm_buf[hh, pl.ds(i, 1), :] + masks[i])
        acc = jnp.exp2(m_old - m_new) * acc_ref[hh]
        for i in range(2):
            p = jnp.exp2((s_buf[hh, i] - m_new).astype(BF16))
            pv = jnp.dot(vaug_ref[lo + i, hh], p, preferred_element_type=F32)
            acc = acc + jnp.where(masks[i] == 0.0, pv, 0.0)
        acc_ref[hh] = acc
        m_ref[hh] = m_new

    n_more = qi // 2
    for hh in range(nh):
        scores(hh, 0, sa_ref, cma_ref, True)

    def stage(it, cur, nxt):
        lead = 1
        for hh in range(lead):
            scores(hh, it + 1, *nxt, False)
        for hh in range(nh):
            if hh + lead < nh:
                scores(hh + lead, it + 1, *nxt, False)
            attend(hh, it, *cur)

    buf_a, buf_b = (sa_ref, cma_ref), (sb_ref, cmb_ref)

    def step(it, carry):
        @pl.when(it % 2 == 0)
        def _():
            stage(it, buf_a, buf_b)

        @pl.when(it % 2 == 1)
        def _():
            stage(it, buf_b, buf_a)

        return carry

    lax.fori_loop(0, n_more, step, 0)

    @pl.when(n_more % 2 == 0)
    def _():
        for hh in range(nh):
            attend(hh, n_more, *buf_a)

    @pl.when(n_more % 2 == 1)
    def _():
        for hh in range(nh):
            attend(hh, n_more, *buf_b)

    for hh in range(nh):
        acc = acc_ref[hh]
        ot_ref[hh * HEAD_DIM:(hh + 1) * HEAD_DIM, :] = (
            acc[0:HEAD_DIM] * (1.0 / acc[HEAD_DIM:HEAD_DIM + 1]))

    o_ref[...] = ot_ref[...].T.astype(o_ref.dtype)


def _moba_attention(proj, bias_tiles, batch, seq):
    proj3 = proj.reshape(proj.shape[0] // MOBA_BLOCK, MOBA_BLOCK, proj.shape[1])
    n_blocks = seq // MOBA_BLOCK
    nh = MOBA_HEADS_PER_STEP
    n_hg = ATTN_HEADS // nh
    hg_w = nh * HEAD_DIM
    kernel = functools.partial(_moba_kernel, n_blocks=n_blocks)
    return pl.pallas_call(
        kernel,
        grid=(batch, n_hg, n_blocks),
        in_specs=[
            pl.BlockSpec((1, MOBA_BLOCK, hg_w), lambda b, hg, qi: (b * n_blocks + qi, 0, hg)),
            pl.BlockSpec((seq, hg_w), lambda b, hg, qi: (b, n_hg + hg)),
            pl.BlockSpec((n_blocks, MOBA_BLOCK, hg_w), lambda b, hg, qi: (b, 0, 2 * n_hg + hg)),
            pl.BlockSpec((nh, 2, MOBA_BLOCK, MOBA_BLOCK), lambda b, hg, qi: (hg, 0, 0, 0)),
        ],
        out_specs=pl.BlockSpec((MOBA_BLOCK, hg_w), lambda b, hg, qi: (b * n_blocks + qi, hg)),
        out_shape=jax.ShapeDtypeStruct((batch * seq, ATTN_HEADS * HEAD_DIM), BF16),
        scratch_shapes=[
            pltpu.VMEM((n_blocks, hg_w), F32),
            pltpu.VMEM((n_blocks, nh, V_AUG_ROWS, MOBA_BLOCK), BF16),
            pltpu.VMEM((nh, min(hg_w, MXU_DEPTH), MOBA_BLOCK), BF16),
            pltpu.VMEM((nh, 2, MOBA_BLOCK, MOBA_BLOCK), F32),
            pltpu.VMEM((nh, 2, MOBA_BLOCK, MOBA_BLOCK), F32),
            pltpu.VMEM((nh, 8, MOBA_BLOCK), F32),
            pltpu.VMEM((nh, 8, MOBA_BLOCK), F32),
            pltpu.VMEM((nh, 1, MOBA_BLOCK), F32),
            pltpu.VMEM((nh, V_AUG_ROWS, MOBA_BLOCK), F32),
            pltpu.VMEM((nh, n_blocks, MOBA_BLOCK), F32),
            pltpu.VMEM((hg_w, MOBA_BLOCK), F32),
        ],
        compiler_params=pltpu.CompilerParams(
            dimension_semantics=("parallel", "parallel", "arbitrary"),
            vmem_limit_bytes=VMEM_LIMIT),
    )(proj3, proj, proj3, bias_tiles)


def _hdot_nt(a, b):
    return lax.dot_general(a, b, _NT_DIMS, preferred_element_type=F32,
                           precision=lax.Precision.HIGHEST)


def _repeat_rows(x, n):
    return jnp.concatenate([jnp.broadcast_to(x[i:i + 1], (n, x.shape[1]))
                            for i in range(x.shape[0])], axis=0)


def _tile_rows(x, n):
    return jnp.concatenate([x] * n, axis=0)


def _s5_tables_kernel(logdt_ref, lre_ref, lim_ref, bre_ref, bim_ref, cre_ref, cim_ref, d_ref,
                      tg_ref, bn_ref, cn_ref, are_ref, aim_ref):
    lc, ch, p = SSM_CHUNK, SSM_GROUP_CH, SSM_STATE
    w = lc * ch
    rr = lax.broadcasted_iota(jnp.int32, (w, w), 0)
    cc = lax.broadcasted_iota(jnp.int32, (w, w), 1)
    causal = rr // ch >= cc // ch
    diag = rr == cc
    e_r = lax.broadcasted_iota(jnp.int32, (ch, w), 0)
    e_c = lax.broadcasted_iota(jnp.int32, (ch, w), 1)
    lane_tile = jnp.where(e_c % ch == e_r, 1.0, 0.0).astype(F32)
    tau = lax.broadcasted_iota(jnp.int32, (2 * lc, p), 0).astype(F32)

    for k in range(tg_ref.shape[0]):
        lam_re = lre_ref[k]
        lam_im = lim_ref[k]
        dt = jnp.exp(logdt_ref[k])
        lr = lam_re * dt
        li = lam_im * dt

        mag = jnp.exp(lr * tau)
        cs = jnp.cos(li * tau)
        sn = jnp.sin(li * tau)
        pos_re, pos_im = mag * cs, mag * sn
        inv = jnp.exp(-lr * tau[:lc])
        neg_re, neg_im = inv * cs[:lc], -inv * sn[:lc]

        lam1_re, lam1_im = pos_re[1:2], pos_im[1:2]
        lamk_re, lamk_im = pos_re[lc - 1:lc], pos_im[lc - 1:lc]
        are_ref[k] = pos_re[lc:lc + 1]
        aim_ref[k] = pos_im[lc:lc + 1]

        num_re, num_im = lam1_re - 1.0, lam1_im
        den = lam_re * lam_re + lam_im * lam_im
        coef_re = (num_re * lam_re + num_im * lam_im) / den
        coef_im = (num_im * lam_re - num_re * lam_im) / den
        bt_re, bt_im = bre_ref[k], bim_ref[k]
        bb_re = coef_re * bt_re - coef_im * bt_im
        bb_im = coef_re * bt_im + coef_im * bt_re

        bbt_re, bbt_im = _tile_rows(bb_re, lc), _tile_rows(bb_im, lc)
        ngx_re, ngx_im = _repeat_rows(neg_re, ch), _repeat_rows(neg_im, ch)
        bneg_re = bbt_re * ngx_re - bbt_im * ngx_im
        bneg_im = bbt_re * ngx_im + bbt_im * ngx_re

        ct_re, ct_im = _tile_rows(cre_ref[k], lc), _tile_rows(cim_ref[k], lc)
        psx_re, psx_im = _repeat_rows(pos_re[:lc], ch), _repeat_rows(pos_im[:lc], ch)
        cpos_re = ct_re * psx_re - ct_im * psx_im
        cpos_im = ct_re * psx_im + ct_im * psx_re

        raw = _hdot_nt(cpos_re, bneg_re) - _hdot_nt(cpos_im, bneg_im)
        d_lanes = jnp.dot(jnp.broadcast_to(d_ref[k], (8, ch)), lane_tile,
                          preferred_element_type=F32, precision=lax.Precision.HIGHEST)[0:1]
        tg = jnp.where(causal, raw, 0.0) + jnp.where(diag, d_lanes, 0.0)
        tg_ref[k] = tg.astype(tg_ref.dtype)

        bn_ref[k, 0] = (bneg_re * lamk_re - bneg_im * lamk_im).T.astype(bn_ref.dtype)
        bn_ref[k, 1] = (bneg_re * lamk_im + bneg_im * lamk_re).T.astype(bn_ref.dtype)
        cp_re = cpos_re * lam1_re - cpos_im * lam1_im
        cp_im = cpos_re * lam1_im + cpos_im * lam1_re
        cn_ref[k, 0] = cp_re.astype(cn_ref.dtype)
        cn_ref[k, 1] = (-cp_im).astype(cn_ref.dtype)


def _s5_tables(log_dt, lam_re, lam_im, b_re, b_im, c_re, c_im, d_skip, gb=4):
    g, p, ch = SSM_GROUPS, SSM_STATE, SSM_GROUP_CH
    w = SSM_CHUNK * ch
    row = lambda a, n: a.reshape(g, 1, n)
    spec3 = lambda s1, s2: pl.BlockSpec((gb, s1, s2), lambda i: (i, 0, 0))
    spec4 = lambda s1, s2: pl.BlockSpec((gb, 2, s1, s2), lambda i: (i, 0, 0, 0))
    return pl.pallas_call(
        _s5_tables_kernel,
        grid=(g // gb,),
        in_specs=[spec3(1, 1), spec3(1, p), spec3(1, p), spec3(ch, p), spec3(ch, p),
                  spec3(ch, p), spec3(ch, p), spec3(1, ch)],
        out_specs=[spec3(w, w), spec4(p, w), spec4(w, p), spec3(1, p), spec3(1, p)],
        out_shape=[jax.ShapeDtypeStruct((g, w, w), BF16),
                   jax.ShapeDtypeStruct((g, 2, p, w), BF16),
                   jax.ShapeDtypeStruct((g, 2, w, p), BF16),
                   jax.ShapeDtypeStruct((g, 1, p), F32),
                   jax.ShapeDtypeStruct((g, 1, p), F32)],
        compiler_params=pltpu.CompilerParams(dimension_semantics=("parallel",)),
    )(row(log_dt, 1), row(lam_re, p), row(lam_im, p),
      jnp.swapaxes(b_re, 1, 2), jnp.swapaxes(b_im, 1, 2), c_re, c_im, row(d_skip, ch))


def _s5_scan_kernel(u_ref, tg_ref, bn_ref, cn_ref, are_ref, aim_ref, y_ref,
                    uf_ref, v_ref, yt_ref, sre_ref, sim_ref, xre_ref, xim_ref, *, n_batch):
    lc, ch = SSM_CHUNK, SSM_GROUP_CH
    gb = tg_ref.shape[0]
    n_rows = u_ref.shape[0] // lc
    n_chunks = n_rows // n_batch
    w = lc * ch

    uf_ref[...] = u_ref[...].astype(F32)
    for s in range(lc):
        ust = uf_ref[pl.ds(s, n_rows, stride=lc), :].T
        for k in range(gb):
            v_ref[k, s * ch:(s + 1) * ch, :] = ust[k * ch:(k + 1) * ch, :].astype(v_ref.dtype)

    for k in range(gb):
        u = v_ref[k]
        rows = pl.ds(k, n_rows, stride=gb)
        sre_ref[rows, :] = jnp.dot(bn_ref[k, 0], u, preferred_element_type=F32).T
        sim_ref[rows, :] = jnp.dot(bn_ref[k, 1], u, preferred_element_type=F32).T

    a_re = are_ref[...]
    a_im = aim_ref[...]

    def step(c, carry):
        new = []
        for b in range(n_batch):
            x_re, x_im = carry[2 * b], carry[2 * b + 1]
            rows = pl.ds(pl.multiple_of((b * n_chunks + c) * gb, gb), gb)
            xre_ref[rows, :] = x_re
            xim_ref[rows, :] = x_im
            new.append(a_re * x_re - a_im * x_im + sre_ref[rows, :])
            new.append(a_re * x_im + a_im * x_re + sim_ref[rows, :])
        return tuple(new)

    zero = jnp.zeros(a_re.shape, F32)
    lax.fori_loop(0, n_chunks, step, (zero,) * (2 * n_batch), unroll=SCAN_UNROLL)

    for k in range(gb):
        rows = pl.ds(k, n_rows, stride=gb)
        y = jnp.dot(tg_ref[k], v_ref[k], preferred_element_type=F32)
        y = y + lax.dot_general(cn_ref[k, 0], xre_ref[rows, :].astype(BF16), _NT_DIMS,
                                preferred_element_type=F32)
        y = y + lax.dot_general(cn_ref[k, 1], xim_ref[rows, :].astype(BF16), _NT_DIMS,
                                preferred_element_type=F32)
        for s in range(lc):
            yt_ref[s, k * ch:(k + 1) * ch, :] = y[s * ch:(s + 1) * ch, :]

    for s in range(lc):
        y_ref[pl.ds(s, n_rows, stride=lc), :] = yt_ref[s].T.astype(y_ref.dtype)


def _s5_scan(ku, tg, bn, cn, a_re, a_im, n_batch, u_col0, gb=8):
    t = ku.shape[0]
    g, w, _ = tg.shape
    p, ch = SSM_STATE, SSM_GROUP_CH
    n_rows = t // SSM_CHUNK
    lanes = gb * ch
    kernel = functools.partial(_s5_scan_kernel, n_batch=n_batch)
    return pl.pallas_call(
        kernel,
        grid=(g // gb,),
        in_specs=[pl.BlockSpec((t, lanes), lambda i: (0, u_col0 // lanes + i)),
                  pl.BlockSpec((gb, w, w), lambda i: (i, 0, 0)),
                  pl.BlockSpec((gb, 2, p, w), lambda i: (i, 0, 0, 0)),
                  pl.BlockSpec((gb, 2, w, p), lambda i: (i, 0, 0, 0)),
                  pl.BlockSpec((gb, p), lambda i: (i, 0)),
                  pl.BlockSpec((gb, p), lambda i: (i, 0))],
        out_specs=pl.BlockSpec((t, lanes), lambda i: (0, i)),
        out_shape=jax.ShapeDtypeStruct((t, g * ch), F32),
        scratch_shapes=[pltpu.VMEM((t, lanes), F32),
                        pltpu.VMEM((gb, w, n_rows), BF16),
                        pltpu.VMEM((SSM_CHUNK, lanes, n_rows), F32)]
                       + [pltpu.VMEM((gb * n_rows, p), F32) for _ in range(4)],
        compiler_params=pltpu.CompilerParams(dimension_semantics=("parallel",),
                                             vmem_limit_bytes=VMEM_LIMIT),
    )(ku, tg, bn, cn, a_re.reshape(g, p), a_im.reshape(g, p))


def _out_proj_kernel(x_ref, a_ref, y_ref, wglu_ref, bglu_ref, ga_ref, gs_ref, wa_ref, ws_ref, o_ref):
    z = jax.nn.gelu(y_ref[...])
    gl = jnp.dot(z.astype(BF16), wglu_ref[...], preferred_element_type=F32) + bglu_ref[...]
    s = z * jax.nn.sigmoid(gl)
    a_n = _rms_rows(a_ref[...].astype(F32), ga_ref[...]).astype(BF16)
    s_n = _rms_rows(s, gs_ref[...]).astype(BF16)
    mix = (jnp.dot(a_n, wa_ref[...], preferred_element_type=F32)
           + jnp.dot(s_n, ws_ref[...], preferred_element_type=F32))
    o_ref[...] = x_ref[...] + mix


def _out_proj(x2, attn, y, w_glu, b_glu, g_a, g_s, w_out, tm=256):
    t, d = x2.shape
    wa = attn.shape[1]
    ws = y.shape[1]
    row = lambda i: (i, 0)
    fixed = lambda i: (0, 0)
    once = pl.Buffered(1)
    return pl.pallas_call(
        _out_proj_kernel,
        grid=(t // tm,),
        in_specs=[pl.BlockSpec((tm, d), row),
                  pl.BlockSpec((tm, wa), row),
                  pl.BlockSpec((tm, ws), row),
                  pl.BlockSpec((ws, ws), fixed, pipeline_mode=once),
                  pl.BlockSpec((1, ws), fixed),
                  pl.BlockSpec((1, wa), fixed),
                  pl.BlockSpec((1, ws), fixed),
                  pl.BlockSpec((wa, d), fixed, pipeline_mode=once),
                  pl.BlockSpec((ws, d), fixed, pipeline_mode=once)],
        out_specs=pl.BlockSpec((tm, d), row),
        out_shape=jax.ShapeDtypeStruct((t, d), F32),
        compiler_params=pltpu.CompilerParams(dimension_semantics=("parallel",),
                                             vmem_limit_bytes=VMEM_LIMIT),
    )(x2, attn, y, w_glu, b_glu.reshape(1, ws), g_a.reshape(1, wa), g_s.reshape(1, ws),
      w_out[:wa], w_out[wa:])


def _ffn_up_kernel(x_ref, halo_ref, g_ref, wg_ref, wv_ref, cwg_ref, cwv_ref, cbg_ref, cbv_ref,
                   o_ref, h_ref, *, tiles_per_seq):
    tm = x_ref.shape[0]

    @pl.when(pl.program_id(1) == 0)
    def _():
        keep = jnp.where(pl.program_id(0) % tiles_per_seq == 0, 0.0, 1.0)
        h_ref[pl.ds(0, CONV_HALO), :] = (_rms_rows(halo_ref[...], g_ref[...]) * keep).astype(BF16)
        h_ref[pl.ds(CONV_HALO, tm), :] = _rms_rows(x_ref[...], g_ref[...]).astype(BF16)

    h = h_ref[...]

    def conv(w_ref, cw_ref, cb_ref):
        up = jnp.dot(h, w_ref[...], preferred_element_type=F32)
        cw = cw_ref[...]
        out = up[CONV_HALO:] * cw[CONV_WIDTH - 1:CONV_WIDTH] + cb_ref[...]
        for j in range(CONV_WIDTH - 1):
            lag = CONV_WIDTH - 1 - j
            out = out + up[CONV_HALO - lag:CONV_HALO - lag + tm] * cw[j:j + 1]
        return out

    gate = conv(wg_ref, cwg_ref, cbg_ref)
    val = conv(wv_ref, cwv_ref, cbv_ref)
    o_ref[...] = (jax.nn.silu(gate) * val).astype(o_ref.dtype)


def _ffn_up(x1, g, w_up, conv_w, conv_b, seq, tm=1024, tn=512):
    t, d = x1.shape
    f = w_up.shape[1] // 2
    nf = f // tn
    tiles_per_seq = seq // tm
    halo_blocks = tm // CONV_HALO
    kernel = functools.partial(_ffn_up_kernel, tiles_per_seq=tiles_per_seq)
    cb = conv_b.reshape(1, 2 * f)
    return pl.pallas_call(
        kernel,
        grid=(t // tm, nf),
        in_specs=[pl.BlockSpec((tm, d), lambda i, j: (i, 0)),
                  pl.BlockSpec((CONV_HALO, d), lambda i, j: (jnp.maximum(i * halo_blocks - 1, 0), 0)),
                  pl.BlockSpec((1, d), lambda i, j: (0, 0)),
                  pl.BlockSpec((d, tn), lambda i, j: (0, j)),
                  pl.BlockSpec((d, tn), lambda i, j: (0, nf + j)),
                  pl.BlockSpec((CONV_WIDTH, tn), lambda i, j: (0, j)),
                  pl.BlockSpec((CONV_WIDTH, tn), lambda i, j: (0, nf + j)),
                  pl.BlockSpec((1, tn), lambda i, j: (0, j)),
                  pl.BlockSpec((1, tn), lambda i, j: (0, nf + j))],
        out_specs=pl.BlockSpec((tm, tn), lambda i, j: (i, j)),
        out_shape=jax.ShapeDtypeStruct((t, f), BF16),
        scratch_shapes=[pltpu.VMEM((CONV_HALO + tm, d), BF16)],
        compiler_params=pltpu.CompilerParams(dimension_semantics=("parallel", "arbitrary"),
                                             vmem_limit_bytes=VMEM_LIMIT),
    )(x1, x1, g.reshape(1, d), w_up, w_up, conv_w, conv_w, cb, cb)


def _ffn_down_kernel(a_ref, w_ref, x_ref, g_ref, o_ref, *, final_norm):
    y = x_ref[...] + jnp.dot(a_ref[...], w_ref[...], preferred_element_type=F32)
    o_ref[...] = _rms_rows(y, g_ref[...]) if final_norm else y


def _ffn_down(act, w_down, x1, g, final_norm, tm=256):
    t, f = act.shape
    d = w_down.shape[1]
    return pl.pallas_call(
        functools.partial(_ffn_down_kernel, final_norm=final_norm),
        grid=(t // tm,),
        in_specs=[pl.BlockSpec((tm, f), lambda i: (i, 0)),
                  pl.BlockSpec((f, d), lambda i: (0, 0), pipeline_mode=pl.Buffered(1)),
                  pl.BlockSpec((tm, d), lambda i: (i, 0)),
                  pl.BlockSpec((1, d), lambda i: (0, 0))],
        out_specs=pl.BlockSpec((tm, d), lambda i: (i, 0)),
        out_shape=jax.ShapeDtypeStruct((t, d), F32),
        compiler_params=pltpu.CompilerParams(dimension_semantics=("parallel",),
                                             vmem_limit_bytes=VMEM_LIMIT),
    )(act, w_down, x1, g.reshape(1, d))


def kernel(x, norm_mix, w_in, rel_bias_table, ssm_lam_re, ssm_lam_im, ssm_log_dt, ssm_b_re, ssm_b_im, ssm_c_re, ssm_c_im, ssm_d, ssm_w_glu, ssm_b_glu, norm_attn_out, norm_ssm_out, w_out, norm_ffn, w_ffn_up, ffn_conv_w, ffn_conv_b, w_ffn_down, norm_final):
    batch, seq, d_model = x.shape
    depth = w_in.shape[0]
    aw = ATTN_HEADS * HEAD_DIM
    sw = SSM_GROUPS * SSM_GROUP_CH
    t = batch * seq
    n_blocks = seq // MOBA_BLOCK
    assert seq % MOBA_BLOCK == 0 and seq % SSM_CHUNK == 0

    bias_tiles = _bias_tiles(rel_bias_table)
    x2 = x.reshape(t, d_model)
    for l in range(depth):
        w = w_in[l]
        w = jnp.concatenate([w[:, :aw] * (HEAD_DIM ** -0.5 * LOG2E), w[:, aw:]], axis=1).astype(BF16)
        proj = _in_proj(x2, norm_mix[l], w)

        attn = _moba_attention(proj, bias_tiles, batch, seq)

        tg, bn, cn, a_re, a_im = _s5_tables(ssm_log_dt[l], ssm_lam_re[l], ssm_lam_im[l],
                                            ssm_b_re[l], ssm_b_im[l], ssm_c_re[l], ssm_c_im[l],
                                            ssm_d[l])
        y = _s5_scan(proj, tg, bn, cn, a_re, a_im, batch, u_col0=3 * aw)

        x2 = _out_proj(x2, attn, y, ssm_w_glu[l].astype(BF16), ssm_b_glu[l], norm_attn_out[l],
                       norm_ssm_out[l], w_out[l].astype(BF16))

        act = _ffn_up(x2, norm_ffn[l], w_ffn_up[l].astype(BF16), ffn_conv_w[l], ffn_conv_b[l], seq)
        x2 = _ffn_down(act, w_ffn_down[l].astype(BF16), x2, norm_final, final_norm=(l == depth - 1))
    return x2.reshape(batch, seq, d_model)
```

```python
import functools
import math

import jax
import jax.numpy as jnp
from jax import lax
from jax.experimental import pallas as pl
from jax.experimental.pallas import tpu as pltpu

F32 = jnp.float32
BF16 = jnp.bfloat16

ATTN_HEADS = 16
HEAD_DIM = 64
SSM_GROUP_CH = 16
SSM_GROUPS = 64
SSM_STATE = 64
MOBA_BLOCK = 256
MOBA_TOP_K = 3
NUM_BUCKETS = 32
MAX_DISTANCE = 128
CONV_WIDTH = 3
RMS_EPS = 1e-6

SSM_CHUNK = 16
MOBA_HEADS_PER_STEP = 8
MXU_DEPTH = 256
SCAN_UNROLL = 8
CONV_HALO = 16
VMEM_LIMIT = 52 * 1024 * 1024

NEG_INF = float("-inf")
LOG2E = math.log2(math.e)
V_AUG_ROWS = HEAD_DIM + 16


def _rms_rows(x, g):
    ms = jnp.mean(x * x, axis=-1, keepdims=True)
    return x * lax.rsqrt(ms + RMS_EPS) * g


_NT_DIMS = (((1,), (1,)), ((), ()))


def _in_proj_kernel(x_ref, g_ref, w_ref, o_ref, h_ref):
    @pl.when(pl.program_id(1) == 0)
    def _():
        h_ref[...] = _rms_rows(x_ref[...], g_ref[...]).astype(BF16)

    o_ref[...] = jnp.dot(h_ref[...], w_ref[...], preferred_element_type=F32).astype(o_ref.dtype)


def _in_proj(x2, g, w, tm=1024, tn=1024):
    t, d = x2.shape
    n = w.shape[1]
    return pl.pallas_call(
        _in_proj_kernel,
        grid=(t // tm, n // tn),
        in_specs=[pl.BlockSpec((tm, d), lambda i, j: (i, 0)),
                  pl.BlockSpec((1, d), lambda i, j: (0, 0)),
                  pl.BlockSpec((d, tn), lambda i, j: (0, j))],
        out_specs=pl.BlockSpec((tm, tn), lambda i, j: (i, j)),
        out_shape=jax.ShapeDtypeStruct((t, n), BF16),
        scratch_shapes=[pltpu.VMEM((tm, d), BF16)],
        compiler_params=pltpu.CompilerParams(dimension_semantics=("parallel", "arbitrary"),
                                             vmem_limit_bytes=VMEM_LIMIT),
    )(x2, g.reshape(1, d), w)


def _t5_bucket(dist):
    dist = jnp.maximum(dist, 0)
    max_exact = NUM_BUCKETS // 2
    log_ratio = jnp.log(jnp.maximum(dist, max_exact).astype(F32) / max_exact)
    large = max_exact + (log_ratio / math.log(MAX_DISTANCE / max_exact)
                         * (NUM_BUCKETS - max_exact)).astype(jnp.int32)
    large = jnp.minimum(large, NUM_BUCKETS - 1)
    return jnp.where(dist < max_exact, dist, large)


def _bias_tiles_kernel(tab_ref, o_ref):
    h = pl.program_id(0)
    c_far = tab_ref[NUM_BUCKETS - 1, h]
    kk = lax.broadcasted_iota(jnp.int32, (MOBA_BLOCK, MOBA_BLOCK), 0)
    qq = lax.broadcasted_iota(jnp.int32, (MOBA_BLOCK, MOBA_BLOCK), 1)
    for which, off in ((0, MOBA_BLOCK), (1, 0)):
        rel = qq - kk + off
        bucket = _t5_bucket(rel)
        acc = jnp.zeros((MOBA_BLOCK, MOBA_BLOCK), F32)
        for b in range(NUM_BUCKETS):
            acc = jnp.where(bucket == b, tab_ref[b, h], acc)
        acc = (acc - c_far) * LOG2E
        if which == 1:
            acc = jnp.where(rel >= 0, acc, NEG_INF)
        o_ref[0, which] = acc


def _bias_tiles(table):
    return pl.pallas_call(
        _bias_tiles_kernel,
        grid=(ATTN_HEADS,),
        in_specs=[pl.BlockSpec(memory_space=pltpu.SMEM)],
        out_specs=pl.BlockSpec((1, 2, MOBA_BLOCK, MOBA_BLOCK), lambda h: (h, 0, 0, 0)),
        out_shape=jax.ShapeDtypeStruct((ATTN_HEADS, 2, MOBA_BLOCK, MOBA_BLOCK), F32),
    )(table)


def _moba_kernel(q_ref, k_ref, v_ref, bias_ref, o_ref,
                 kmf_ref, vaug_ref, qm_ref, sa_ref, sb_ref, cma_ref, cmb_ref,
                 m_ref, acc_ref, rb_ref, ot_ref, *, n_blocks):
    qi = pl.program_id(2)
    blk = MOBA_BLOCK
    nh = qm_ref.shape[0]

    @pl.when(qi == 0)
    def _():
        ones = jnp.ones((V_AUG_ROWS - HEAD_DIM, blk), BF16)
        for j in range(n_blocks):
            kb = k_ref[j * blk:(j + 1) * blk, :].astype(F32)
            kmf_ref[pl.ds(j, 1), :] = jnp.sum(kb, axis=0, keepdims=True) * (1.0 / blk)
            vt = v_ref[j].T
            for hh in range(nh):
                vaug_ref[j, hh, 0:HEAD_DIM, :] = vt[hh * HEAD_DIM:(hh + 1) * HEAD_DIM, :]
                vaug_ref[j, hh, HEAD_DIM:V_AUG_ROWS, :] = ones

    kmf = kmf_ref[...]
    km_hi = kmf.astype(BF16)
    km_lo = (kmf - km_hi.astype(F32)).astype(BF16)

    q2 = q_ref[0].T
    q_zero = jnp.zeros((HEAD_DIM, blk), q2.dtype)
    blk_idx = lax.broadcasted_iota(jnp.int32, (n_blocks, blk), 0)
    blk_idx_f = blk_idx.astype(F32)
    valid = blk_idx < qi
    hps = qm_ref.shape[1] // HEAD_DIM
    slab = lambda hh: slice((hh // hps) * hps * HEAD_DIM, (hh // hps + 1) * hps * HEAD_DIM)

    for hh in range(nh):
        vrows = slice(hh * HEAD_DIM, (hh + 1) * HEAD_DIM)
        qm = jnp.concatenate([q_zero] * (hh % hps) + [q2[vrows]] + [q_zero] * (hps - 1 - hh % hps),
                             axis=0)
        qm_ref[hh] = qm

        gate = (jnp.dot(km_hi[:, slab(hh)], qm, preferred_element_type=F32)
                + jnp.dot(km_lo[:, slab(hh)], qm, preferred_element_type=F32))
        g = jnp.where(valid, gate, NEG_INF)
        sel = blk_idx == qi
        for _ in range(MOBA_TOP_K):
            top = jnp.max(g, axis=0, keepdims=True)
            first = jnp.min(jnp.where(g == top, blk_idx_f, float(n_blocks)), axis=0, keepdims=True)
            hit = blk_idx_f == first
            sel = sel | (hit & valid)
            g = jnp.where(hit, NEG_INF, g)
        rb_ref[hh] = jnp.where(sel, 0.0, NEG_INF)
        m_ref[hh] = jnp.full((1, blk), NEG_INF, F32)
        acc_ref[hh] = jnp.zeros((V_AUG_ROWS, blk), F32)

    def group_rows(g):
        ja = qi - 2 * g
        return ja, jnp.maximum(ja - 1, 0)

    def scores(hh, g, s_buf, cm_buf, first_group):
        _, lo = group_rows(g)
        kslab = k_ref[pl.ds(pl.multiple_of(lo * blk, blk), 2 * blk), slab(hh)]
        s = jnp.dot(kslab, qm_ref[hh], preferred_element_type=F32)
        for i in range(2):
            si = s[i * blk:(i + 1) * blk]
            if first_group:
                tile = jnp.where(qi == 0, 1, 0) if i == 0 else 1
                si = si + bias_ref[hh, tile]
            s_buf[hh, i] = si
            cm_buf[hh, pl.ds(i, 1), :] = jnp.max(si, axis=0, keepdims=True)

    def attend(hh, g, s_buf, cm_buf):
        ja, lo = group_rows(g)
        masks = [rb_ref[hh, pl.ds(lo, 1), :],
                 jnp.where(ja >= 1, rb_ref[hh, pl.ds(lo + 1, 1), :], NEG_INF)]
        m_old = m_ref[hh]
        m_new = m_old
        for i in range(2):
            m_new = jnp.maximum(m_new, cm_buf[hh, pl.ds(i, 1), :] + masks[i])
        acc = jnp.exp2(m_old - m_new) * acc_ref[hh]
        for i in range(2):
            p = jnp.exp2((s_buf[hh, i] - m_new).astype(BF16))
            pv = jnp.dot(vaug_ref[lo + i, hh], p, preferred_element_type=F32)
            acc = acc + jnp.where(masks[i] == 0.0, pv, 0.0)
        acc_ref[hh] = acc
        m_ref[hh] = m_new

    n_more = qi // 2
    for hh in range(nh):
        scores(hh, 0, sa_ref, cma_ref, True)

    def stage(it, cur, nxt):
        lead = 1
        for hh in range(lead):
            scores(hh, it + 1, *nxt, False)
        for hh in range(nh):
            if hh + lead < nh:
                scores(hh + lead, it + 1, *nxt, False)
            attend(hh, it, *cur)

    buf_a, buf_b = (sa_ref, cma_ref), (sb_ref, cmb_ref)

    def step(it, carry):
        @pl.when(it % 2 == 0)
        def _():
            stage(it, buf_a, buf_b)

        @pl.when(it % 2 == 1)
        def _():
            stage(it, buf_b, buf_a)

        return carry

    lax.fori_loop(0, n_more, step, 0)

    @pl.when(n_more % 2 == 0)
    def _():
        for hh in range(nh):
            attend(hh, n_more, *buf_a)

    @pl.when(n_more % 2 == 1)
    def _():
        for hh in range(nh):
            attend(hh, n_more, *buf_b)

    for hh in range(nh):
        acc = acc_ref[hh]
        ot_ref[hh * HEAD_DIM:(hh + 1) * HEAD_DIM, :] = (
            acc[0:HEAD_DIM] * (1.0 / acc[HEAD_DIM:HEAD_DIM + 1]))

    o_ref[...] = ot_ref[...].T.astype(o_ref.dtype)


def _moba_attention(proj, bias_tiles, batch, seq):
    proj3 = proj.reshape(proj.shape[0] // MOBA_BLOCK, MOBA_BLOCK, proj.shape[1])
    n_blocks = seq // MOBA_BLOCK
    nh = MOBA_HEADS_PER_STEP
    n_hg = ATTN_HEADS // nh
    hg_w = nh * HEAD_DIM
    kernel = functools.partial(_moba_kernel, n_blocks=n_blocks)
    return pl.pallas_call(
        kernel,
        grid=(batch, n_hg, n_blocks),
        in_specs=[
            pl.BlockSpec((1, MOBA_BLOCK, hg_w), lambda b, hg, qi: (b * n_blocks + qi, 0, hg)),
            pl.BlockSpec((seq, hg_w), lambda b, hg, qi: (b, n_hg + hg)),
            pl.BlockSpec((n_blocks, MOBA_BLOCK, hg_w), lambda b, hg, qi: (b, 0, 2 * n_hg + hg)),
            pl.BlockSpec((nh, 2, MOBA_BLOCK, MOBA_BLOCK), lambda b, hg, qi: (hg, 0, 0, 0)),
        ],
        out_specs=pl.BlockSpec((MOBA_BLOCK, hg_w), lambda b, hg, qi: (b * n_blocks + qi, hg)),
        out_shape=jax.ShapeDtypeStruct((batch * seq, ATTN_HEADS * HEAD_DIM), BF16),
        scratch_shapes=[
            pltpu.VMEM((n_blocks, hg_w), F32),
            pltpu.VMEM((n_blocks, nh, V_AUG_ROWS, MOBA_BLOCK), BF16),
            pltpu.VMEM((nh, min(hg_w, MXU_DEPTH), MOBA_BLOCK), BF16),
            pltpu.VMEM((nh, 2, MOBA_BLOCK, MOBA_BLOCK), F32),
            pltpu.VMEM((nh, 2, MOBA_BLOCK, MOBA_BLOCK), F32),
            pltpu.VMEM((nh, 8, MOBA_BLOCK), F32),
            pltpu.VMEM((nh, 8, MOBA_BLOCK), F32),
            pltpu.VMEM((nh, 1, MOBA_BLOCK), F32),
            pltpu.VMEM((nh, V_AUG_ROWS, MOBA_BLOCK), F32),
            pltpu.VMEM((nh, n_blocks, MOBA_BLOCK), F32),
            pltpu.VMEM((hg_w, MOBA_BLOCK), F32),
        ],
        compiler_params=pltpu.CompilerParams(
            dimension_semantics=("parallel", "parallel", "arbitrary"),
            vmem_limit_bytes=VMEM_LIMIT),
    )(proj3, proj, proj3, bias_tiles)


def _hdot_nt(a, b):
    def split(x):
        hi = x.astype(BF16)
        return hi, (x - hi.astype(F32)).astype(BF16)

    def nt(u, v):
        return lax.dot_general(u, v, _NT_DIMS, preferred_element_type=F32)

    a_hi, a_lo = split(a)
    b_hi, b_lo = split(b)
    return nt(a_hi, b_hi) + nt(a_hi, b_lo) + nt(a_lo, b_hi)


def _repeat_rows(x, n):
    return jnp.concatenate([jnp.broadcast_to(x[i:i + 1], (n, x.shape[1]))
                            for i in range(x.shape[0])], axis=0)


def _tile_rows(x, n):
    return jnp.concatenate([x] * n, axis=0)


def _s5_tables_kernel(logdt_ref, lre_ref, lim_ref, bre_ref, bim_ref, cre_ref, cim_ref, d_ref,
                      tg_ref, bn_ref, cn_ref, are_ref, aim_ref):
    lc, ch, p = SSM_CHUNK, SSM_GROUP_CH, SSM_STATE
    w = lc * ch
    rr = lax.broadcasted_iota(jnp.int32, (w, w), 0)
    cc = lax.broadcasted_iota(jnp.int32, (w, w), 1)
    causal = rr // ch >= cc // ch
    diag = rr == cc
    e_r = lax.broadcasted_iota(jnp.int32, (ch, w), 0)
    e_c = lax.broadcasted_iota(jnp.int32, (ch, w), 1)
    lane_tile = jnp.where(e_c % ch == e_r, 1.0, 0.0).astype(F32)
    tau = lax.broadcasted_iota(jnp.int32, (2 * lc, p), 0).astype(F32)

    for k in range(tg_ref.shape[0]):
        lam_re = lre_ref[k]
        lam_im = lim_ref[k]
        dt = jnp.exp(logdt_ref[k])
        lr = lam_re * dt
        li = lam_im * dt

        mag = jnp.exp(lr * tau)
        cs = jnp.cos(li * tau)
        sn = jnp.sin(li * tau)
        pos_re, pos_im = mag * cs, mag * sn
        inv = jnp.exp(-lr * tau[:lc])
        neg_re, neg_im = inv * cs[:lc], -inv * sn[:lc]

        lam1_re, lam1_im = pos_re[1:2], pos_im[1:2]
        lamk_re, lamk_im = pos_re[lc - 1:lc], pos_im[lc - 1:lc]
        are_ref[k] = pos_re[lc:lc + 1]
        aim_ref[k] = pos_im[lc:lc + 1]

        num_re, num_im = lam1_re - 1.0, lam1_im
        den = lam_re * lam_re + lam_im * lam_im
        coef_re = (num_re * lam_re + num_im * lam_im) / den
        coef_im = (num_im * lam_re - num_re * lam_im) / den
        bt_re, bt_im = bre_ref[k], bim_ref[k]
        bb_re = coef_re * bt_re - coef_im * bt_im
        bb_im = coef_re * bt_im + coef_im * bt_re

        bbt_re, bbt_im = _tile_rows(bb_re, lc), _tile_rows(bb_im, lc)
        ngx_re, ngx_im = _repeat_rows(neg_re, ch), _repeat_rows(neg_im, ch)
        bneg_re = bbt_re * ngx_re - bbt_im * ngx_im
        bneg_im = bbt_re * ngx_im + bbt_im * ngx_re

        ct_re, ct_im = _tile_rows(cre_ref[k], lc), _tile_rows(cim_ref[k], lc)
        psx_re, psx_im = _repeat_rows(pos_re[:lc], ch), _repeat_rows(pos_im[:lc], ch)
        cpos_re = ct_re * psx_re - ct_im * psx_im
        cpos_im = ct_re * psx_im + ct_im * psx_re

        raw = _hdot_nt(cpos_re, bneg_re) - _hdot_nt(cpos_im, bneg_im)
        d_lanes = jnp.dot(jnp.broadcast_to(d_ref[k], (8, ch)), lane_tile,
                          preferred_element_type=F32, precision=lax.Precision.HIGHEST)[0:1]
        tg = jnp.where(causal, raw, 0.0) + jnp.where(diag, d_lanes, 0.0)
        tg_ref[k] = tg.astype(tg_ref.dtype)

        bn_ref[k, 0] = (bneg_re * lamk_re - bneg_im * lamk_im).T.astype(bn_ref.dtype)
        bn_ref[k, 1] = (bneg_re * lamk_im + bneg_im * lamk_re).T.astype(bn_ref.dtype)
        cp_re = cpos_re * lam1_re - cpos_im * lam1_im
        cp_im = cpos_re * lam1_im + cpos_im * lam1_re
        cn_ref[k, 0] = cp_re.astype(cn_ref.dtype)
        cn_ref[k, 1] = (-cp_im).astype(cn_ref.dtype)


def _s5_tables(log_dt, lam_re, lam_im, b_re, b_im, c_re, c_im, d_skip, gb=4):
    g, p, ch = SSM_GROUPS, SSM_STATE, SSM_GROUP_CH
    w = SSM_CHUNK * ch
    row = lambda a, n: a.reshape(g, 1, n)
    spec3 = lambda s1, s2: pl.BlockSpec((gb, s1, s2), lambda i: (i, 0, 0))
    spec4 = lambda s1, s2: pl.BlockSpec((gb, 2, s1, s2), lambda i: (i, 0, 0, 0))
    return pl.pallas_call(
        _s5_tables_kernel,
        grid=(g // gb,),
        in_specs=[spec3(1, 1), spec3(1, p), spec3(1, p), spec3(ch, p), spec3(ch, p),
                  spec3(ch, p), spec3(ch, p), spec3(1, ch)],
        out_specs=[spec3(w, w), spec4(p, w), spec4(w, p), spec3(1, p), spec3(1, p)],
        out_shape=[jax.ShapeDtypeStruct((g, w, w), BF16),
                   jax.ShapeDtypeStruct((g, 2, p, w), BF16),
                   jax.ShapeDtypeStruct((g, 2, w, p), BF16),
                   jax.ShapeDtypeStruct((g, 1, p), F32),
                   jax.ShapeDtypeStruct((g, 1, p), F32)],
        compiler_params=pltpu.CompilerParams(dimension_semantics=("parallel",)),
    )(row(log_dt, 1), row(lam_re, p), row(lam_im, p),
      jnp.swapaxes(b_re, 1, 2), jnp.swapaxes(b_im, 1, 2), c_re, c_im, row(d_skip, ch))


def _s5_scan_kernel(u_ref, tg_ref, bn_ref, cn_ref, are_ref, aim_ref, y_ref,
                    uf_ref, v_ref, yt_ref, sre_ref, sim_ref, xre_ref, xim_ref, *, n_batch):
    lc, ch = SSM_CHUNK, SSM_GROUP_CH
    gb = tg_ref.shape[0]
    n_rows = u_ref.shape[0] // lc
    n_chunks = n_rows // n_batch
    w = lc * ch

    uf_ref[...] = u_ref[...].astype(F32)
    for s in range(lc):
        ust = uf_ref[pl.ds(s, n_rows, stride=lc), :].T
        for k in range(gb):
            v_ref[k, s * ch:(s + 1) * ch, :] = ust[k * ch:(k + 1) * ch, :].astype(v_ref.dtype)

    for k in range(gb):
        u = v_ref[k]
        rows = pl.ds(k, n_rows, stride=gb)
        sre_ref[rows, :] = jnp.dot(bn_ref[k, 0], u, preferred_element_type=F32).T
        sim_ref[rows, :] = jnp.dot(bn_ref[k, 1], u, preferred_element_type=F32).T

    a_re = are_ref[...]
    a_im = aim_ref[...]

    def step(c, carry):
        new = []
        for b in range(n_batch):
            x_re, x_im = carry[2 * b], carry[2 * b + 1]
            rows = pl.ds(pl.multiple_of((b * n_chunks + c) * gb, gb), gb)
            xre_ref[rows, :] = x_re
            xim_ref[rows, :] = x_im
            new.append(a_re * x_re - a_im * x_im + sre_ref[rows, :])
            new.append(a_re * x_im + a_im * x_re + sim_ref[rows, :])
        return tuple(new)

    zero = jnp.zeros(a_re.shape, F32)
    lax.fori_loop(0, n_chunks, step, (zero,) * (2 * n_batch), unroll=SCAN_UNROLL)

    for k in range(gb):
        rows = pl.ds(k, n_rows, stride=gb)
        y = jnp.dot(tg_ref[k], v_ref[k], preferred_element_type=F32)
        y = y + lax.dot_general(cn_ref[k, 0], xre_ref[rows, :].astype(BF16), _NT_DIMS,
                                preferred_element_type=F32)
        y = y + lax.dot_general(cn_ref[k, 1], xim_ref[rows, :].astype(BF16), _NT_DIMS,
                                preferred_element_type=F32)
        for s in range(lc):
            yt_ref[s, k * ch:(k + 1) * ch, :] = y[s * ch:(s + 1) * ch, :]

    for s in range(lc):
        y_ref[pl.ds(s, n_rows, stride=lc), :] = yt_ref[s].T.astype(y_ref.dtype)


def _s5_scan(ku, tg, bn, cn, a_re, a_im, n_batch, u_col0, gb=8):
    t = ku.shape[0]
    g, w, _ = tg.shape
    p, ch = SSM_STATE, SSM_GROUP_CH
    n_rows = t // SSM_CHUNK
    lanes = gb * ch
    kernel = functools.partial(_s5_scan_kernel, n_batch=n_batch)
    return pl.pallas_call(
        kernel,
        grid=(g // gb,),
        in_specs=[pl.BlockSpec((t, lanes), lambda i: (0, u_col0 // lanes + i)),
                  pl.BlockSpec((gb, w, w), lambda i: (i, 0, 0)),
                  pl.BlockSpec((gb, 2, p, w), lambda i: (i, 0, 0, 0)),
                  pl.BlockSpec((gb, 2, w, p), lambda i: (i, 0, 0, 0)),
                  pl.BlockSpec((gb, p), lambda i: (i, 0)),
                  pl.BlockSpec((gb, p), lambda i: (i, 0))],
        out_specs=pl.BlockSpec((t, lanes), lambda i: (0, i)),
        out_shape=jax.ShapeDtypeStruct((t, g * ch), F32),
        scratch_shapes=[pltpu.VMEM((t, lanes), F32),
                        pltpu.VMEM((gb, w, n_rows), BF16),
                        pltpu.VMEM((SSM_CHUNK, lanes, n_rows), F32)]
                       + [pltpu.VMEM((gb * n_rows, p), F32) for _ in range(4)],
        compiler_params=pltpu.CompilerParams(dimension_semantics=("parallel",),
                                             vmem_limit_bytes=VMEM_LIMIT),
    )(ku, tg, bn, cn, a_re.reshape(g, p), a_im.reshape(g, p))


def _out_proj_kernel(x_ref, a_ref, y_ref, wglu_ref, bglu_ref, ga_ref, gs_ref, wa_ref, ws_ref, o_ref):
    z = jax.nn.gelu(y_ref[...])
    gl = jnp.dot(z.astype(BF16), wglu_ref[...], preferred_element_type=F32) + bglu_ref[...]
    s = z * jax.nn.sigmoid(gl)
    a_n = _rms_rows(a_ref[...].astype(F32), ga_ref[...]).astype(BF16)
    s_n = _rms_rows(s, gs_ref[...]).astype(BF16)
    mix = (jnp.dot(a_n, wa_ref[...], preferred_element_type=F32)
           + jnp.dot(s_n, ws_ref[...], preferred_element_type=F32))
    o_ref[...] = x_ref[...] + mix


def _out_proj(x2, attn, y, w_glu, b_glu, g_a, g_s, w_out, tm=512):
    t, d = x2.shape
    wa = attn.shape[1]
    ws = y.shape[1]
    row = lambda i: (i, 0)
    fixed = lambda i: (0, 0)
    once = pl.Buffered(1)
    return pl.pallas_call(
        _out_proj_kernel,
        grid=(t // tm,),
        in_specs=[pl.BlockSpec((tm, d), row),
                  pl.BlockSpec((tm, wa), row),
                  pl.BlockSpec((tm, ws), row),
                  pl.BlockSpec((ws, ws), fixed, pipeline_mode=once),
                  pl.BlockSpec((1, ws), fixed),
                  pl.BlockSpec((1, wa), fixed),
                  pl.BlockSpec((1, ws), fixed),
                  pl.BlockSpec((wa, d), fixed, pipeline_mode=once),
                  pl.BlockSpec((ws, d), lambda i: (wa // ws, 0), pipeline_mode=once)],
        out_specs=pl.BlockSpec((tm, d), row),
        out_shape=jax.ShapeDtypeStruct((t, d), F32),
        compiler_params=pltpu.CompilerParams(dimension_semantics=("parallel",),
                                             vmem_limit_bytes=VMEM_LIMIT),
    )(x2, attn, y, w_glu, b_glu.reshape(1, ws), g_a.reshape(1, wa), g_s.reshape(1, ws),
      w_out, w_out)


def _ffn_up_kernel(x_ref, halo_ref, g_ref, wg_ref, wv_ref, cwg_ref, cwv_ref, cbg_ref, cbv_ref,
                   o_ref, h_ref, *, tiles_per_seq):
    tm = x_ref.shape[0]

    @pl.when(pl.program_id(1) == 0)
    def _():
        keep = jnp.where(pl.program_id(0) % tiles_per_seq == 0, 0.0, 1.0)
        h_ref[pl.ds(0, CONV_HALO), :] = (_rms_rows(halo_ref[...], g_ref[...]) * keep).astype(BF16)
        h_ref[pl.ds(CONV_HALO, tm), :] = _rms_rows(x_ref[...], g_ref[...]).astype(BF16)

    h = h_ref[...]

    def conv(w_ref, cw_ref, cb_ref):
        up = jnp.dot(h, w_ref[...], preferred_element_type=F32)
        cw = cw_ref[...]
        out = up[CONV_HALO:] * cw[CONV_WIDTH - 1:CONV_WIDTH] + cb_ref[...]
        for j in range(CONV_WIDTH - 1):
            lag = CONV_WIDTH - 1 - j
            out = out + up[CONV_HALO - lag:CONV_HALO - lag + tm] * cw[j:j + 1]
        return out

    gate = conv(wg_ref, cwg_ref, cbg_ref)
    val = conv(wv_ref, cwv_ref, cbv_ref)
    o_ref[...] = (jax.nn.silu(gate) * val).astype(o_ref.dtype)


def _ffn_up(x1, g, w_up, conv_w, conv_b, seq, tm=1024, tn=512):
    t, d = x1.shape
    f = w_up.shape[1] // 2
    nf = f // tn
    tiles_per_seq = seq // tm
    halo_blocks = tm // CONV_HALO
    kernel = functools.partial(_ffn_up_kernel, tiles_per_seq=tiles_per_seq)
    cb = conv_b.reshape(1, 2 * f)
    return pl.pallas_call(
        kernel,
        grid=(t // tm, nf),
        in_specs=[pl.BlockSpec((tm, d), lambda i, j: (i, 0)),
                  pl.BlockSpec((CONV_HALO, d), lambda i, j: (jnp.maximum(i * halo_blocks - 1, 0), 0)),
                  pl.BlockSpec((1, d), lambda i, j: (0, 0)),
                  pl.BlockSpec((d, tn), lambda i, j: (0, j)),
                  pl.BlockSpec((d, tn), lambda i, j: (0, nf + j)),
                  pl.BlockSpec((CONV_WIDTH, tn), lambda i, j: (0, j)),
                  pl.BlockSpec((CONV_WIDTH, tn), lambda i, j: (0, nf + j)),
                  pl.BlockSpec((1, tn), lambda i, j: (0, j)),
                  pl.BlockSpec((1, tn), lambda i, j: (0, nf + j))],
        out_specs=pl.BlockSpec((tm, tn), lambda i, j: (i, j)),
        out_shape=jax.ShapeDtypeStruct((t, f), BF16),
        scratch_shapes=[pltpu.VMEM((CONV_HALO + tm, d), BF16)],
        compiler_params=pltpu.CompilerParams(dimension_semantics=("parallel", "arbitrary"),
                                             vmem_limit_bytes=VMEM_LIMIT),
    )(x1, x1, g.reshape(1, d), w_up, w_up, conv_w, conv_w, cb, cb)


def _ffn_down_kernel(a_ref, w_ref, x_ref, g_ref, o_ref, *, final_norm):
    y = x_ref[...] + jnp.dot(a_ref[...], w_ref[...], preferred_element_type=F32)
    o_ref[...] = _rms_rows(y, g_ref[...]) if final_norm else y


def _ffn_down(act, w_down, x1, g, final_norm, tm=256):
    t, f = act.shape
    d = w_down.shape[1]
    return pl.pallas_call(
        functools.partial(_ffn_down_kernel, final_norm=final_norm),
        grid=(t // tm,),
        in_specs=[pl.BlockSpec((tm, f), lambda i: (i, 0)),
                  pl.BlockSpec((f, d), lambda i: (0, 0), pipeline_mode=pl.Buffered(1)),
                  pl.BlockSpec((tm, d), lambda i: (i, 0)),
                  pl.BlockSpec((1, d), lambda i: (0, 0))],
        out_specs=pl.BlockSpec((tm, d), lambda i: (i, 0)),
        out_shape=jax.ShapeDtypeStruct((t, d), F32),
        compiler_params=pltpu.CompilerParams(dimension_semantics=("parallel",),
                                             vmem_limit_bytes=VMEM_LIMIT),
    )(act, w_down, x1, g.reshape(1, d))


def kernel(x, norm_mix, w_in, rel_bias_table, ssm_lam_re, ssm_lam_im, ssm_log_dt, ssm_b_re, ssm_b_im, ssm_c_re, ssm_c_im, ssm_d, ssm_w_glu, ssm_b_glu, norm_attn_out, norm_ssm_out, w_out, norm_ffn, w_ffn_up, ffn_conv_w, ffn_conv_b, w_ffn_down, norm_final):
    batch, seq, d_model = x.shape
    depth = w_in.shape[0]
    aw = ATTN_HEADS * HEAD_DIM
    sw = SSM_GROUPS * SSM_GROUP_CH
    t = batch * seq
    n_blocks = seq // MOBA_BLOCK
    assert seq % MOBA_BLOCK == 0 and seq % SSM_CHUNK == 0

    bias_tiles = _bias_tiles(rel_bias_table)
    x2 = x.reshape(t, d_model)
    for l in range(depth):
        w = w_in[l]
        w = jnp.concatenate([w[:, :aw] * (HEAD_DIM ** -0.5 * LOG2E), w[:, aw:]], axis=1).astype(BF16)
        proj = _in_proj(x2, norm_mix[l], w)

        attn = _moba_attention(proj, bias_tiles, batch, seq)

        tg, bn, cn, a_re, a_im = _s5_tables(ssm_log_dt[l], ssm_lam_re[l], ssm_lam_im[l],
                                            ssm_b_re[l], ssm_b_im[l], ssm_c_re[l], ssm_c_im[l],
                                            ssm_d[l])
        y = _s5_scan(proj, tg, bn, cn, a_re, a_im, batch, u_col0=3 * aw)

        x2 = _out_proj(x2, attn, y, ssm_w_glu[l].astype(BF16), ssm_b_glu[l], norm_attn_out[l],
                       norm_ssm_out[l], w_out[l].astype(BF16))

        act = _ffn_up(x2, norm_ffn[l], w_ffn_up[l].astype(BF16), ffn_conv_w[l], ffn_conv_b[l], seq)
        x2 = _ffn_down(act, w_ffn_down[l].astype(BF16), x2, norm_final, final_norm=(l == depth - 1))
    return x2.reshape(batch, seq, d_model)
```

```python
import functools
import math

import jax
import jax.numpy as jnp
from jax import lax
from jax.experimental import pallas as pl
from jax.experimental.pallas import tpu as pltpu

F32 = jnp.float32
BF16 = jnp.bfloat16

ATTN_HEADS = 16
HEAD_DIM = 64
SSM_GROUP_CH = 16
SSM_GROUPS = 64
SSM_STATE = 64
MOBA_BLOCK = 256
MOBA_TOP_K = 3
NUM_BUCKETS = 32
MAX_DISTANCE = 128
CONV_WIDTH = 3
RMS_EPS = 1e-6

SSM_CHUNK = 16
MOBA_HEADS_PER_STEP = 8
MXU_DEPTH = 256
SCAN_UNROLL = 8
CONV_HALO = 16
VMEM_LIMIT = 52 * 1024 * 1024

NEG_INF = float("-inf")
LOG2E = math.log2(math.e)
V_AUG_ROWS = HEAD_DIM + 16


def _rms_rows(x, g):
    ms = jnp.mean(x * x, axis=-1, keepdims=True)
    return x * lax.rsqrt(ms + RMS_EPS) * g


_NT_DIMS = (((1,), (1,)), ((), ()))


def _in_proj_kernel(x_ref, g_ref, w_ref, o_ref, h_ref, *, q_tiles, q_scale):
    j = pl.program_id(1)

    @pl.when(j == 0)
    def _():
        h_ref[...] = _rms_rows(x_ref[...], g_ref[...]).astype(BF16)

    res = jnp.dot(h_ref[...], w_ref[...].astype(BF16), preferred_element_type=F32)
    res = res * jnp.where(j < q_tiles, q_scale, 1.0)
    o_ref[...] = res.astype(o_ref.dtype)


def _in_proj(x2, g, w_all, layer, q_cols, q_scale, tm=1024, tn=1024):
    t, d = x2.shape
    n = w_all.shape[2]
    assert q_cols % tn == 0
    kernel = functools.partial(_in_proj_kernel, q_tiles=q_cols // tn, q_scale=q_scale)
    return pl.pallas_call(
        kernel,
        grid=(t // tm, n // tn),
        in_specs=[pl.BlockSpec((tm, d), lambda i, j: (i, 0)),
                  pl.BlockSpec((1, d), lambda i, j: (0, 0)),
                  pl.BlockSpec((None, d, tn), lambda i, j: (layer, 0, j))],
        out_specs=pl.BlockSpec((tm, tn), lambda i, j: (i, j)),
        out_shape=jax.ShapeDtypeStruct((t, n), BF16),
        scratch_shapes=[pltpu.VMEM((tm, d), BF16)],
        compiler_params=pltpu.CompilerParams(dimension_semantics=("parallel", "arbitrary"),
                                             vmem_limit_bytes=VMEM_LIMIT),
    )(x2, g.reshape(1, d), w_all)


def _t5_bucket(dist):
    dist = jnp.maximum(dist, 0)
    max_exact = NUM_BUCKETS // 2
    log_ratio = jnp.log(jnp.maximum(dist, max_exact).astype(F32) / max_exact)
    large = max_exact + (log_ratio / math.log(MAX_DISTANCE / max_exact)
                         * (NUM_BUCKETS - max_exact)).astype(jnp.int32)
    large = jnp.minimum(large, NUM_BUCKETS - 1)
    return jnp.where(dist < max_exact, dist, large)


def _bias_tiles_kernel(tab_ref, o_ref):
    h = pl.program_id(0)
    c_far = tab_ref[NUM_BUCKETS - 1, h]
    kk = lax.broadcasted_iota(jnp.int32, (MOBA_BLOCK, MOBA_BLOCK), 0)
    qq = lax.broadcasted_iota(jnp.int32, (MOBA_BLOCK, MOBA_BLOCK), 1)
    for which, off in ((0, MOBA_BLOCK), (1, 0)):
        rel = qq - kk + off
        bucket = _t5_bucket(rel)
        acc = jnp.zeros((MOBA_BLOCK, MOBA_BLOCK), F32)
        for b in range(NUM_BUCKETS):
            acc = jnp.where(bucket == b, tab_ref[b, h], acc)
        acc = (acc - c_far) * LOG2E
        if which == 1:
            acc = jnp.where(rel >= 0, acc, NEG_INF)
        o_ref[0, which] = acc


def _bias_tiles(table):
    return pl.pallas_call(
        _bias_tiles_kernel,
        grid=(ATTN_HEADS,),
        in_specs=[pl.BlockSpec(memory_space=pltpu.SMEM)],
        out_specs=pl.BlockSpec((1, 2, MOBA_BLOCK, MOBA_BLOCK), lambda h: (h, 0, 0, 0)),
        out_shape=jax.ShapeDtypeStruct((ATTN_HEADS, 2, MOBA_BLOCK, MOBA_BLOCK), F32),
    )(table)


def _moba_kernel(q_ref, k_ref, v_ref, bias_ref, o_ref,
                 kmf_ref, vaug_ref, qm_ref, sa_ref, sb_ref, cma_ref, cmb_ref,
                 m_ref, acc_ref, rb_ref, ot_ref, *, n_blocks):
    qi = pl.program_id(2)
    blk = MOBA_BLOCK
    nh = qm_ref.shape[0]

    @pl.when(qi == 0)
    def _():
        ones = jnp.ones((V_AUG_ROWS - HEAD_DIM, blk), BF16)
        for j in range(n_blocks):
            kb = k_ref[j * blk:(j + 1) * blk, :].astype(F32)
            kmf_ref[pl.ds(j, 1), :] = jnp.sum(kb, axis=0, keepdims=True) * (1.0 / blk)
            vt = v_ref[j].T
            for hh in range(nh):
                vaug_ref[j, hh, 0:HEAD_DIM, :] = vt[hh * HEAD_DIM:(hh + 1) * HEAD_DIM, :]
                vaug_ref[j, hh, HEAD_DIM:V_AUG_ROWS, :] = ones

    kmf = kmf_ref[...]
    km_hi = kmf.astype(BF16)
    km_lo = (kmf - km_hi.astype(F32)).astype(BF16)

    q2 = q_ref[0].T
    q_zero = jnp.zeros((HEAD_DIM, blk), q2.dtype)
    blk_idx = lax.broadcasted_iota(jnp.int32, (n_blocks, blk), 0)
    blk_idx_f = blk_idx.astype(F32)
    valid = blk_idx < qi
    hps = qm_ref.shape[1] // HEAD_DIM
    slab = lambda hh: slice((hh // hps) * hps * HEAD_DIM, (hh // hps + 1) * hps * HEAD_DIM)

    for hh in range(nh):
        vrows = slice(hh * HEAD_DIM, (hh + 1) * HEAD_DIM)
        qm = jnp.concatenate([q_zero] * (hh % hps) + [q2[vrows]] + [q_zero] * (hps - 1 - hh % hps),
                             axis=0)
        qm_ref[hh] = qm

        gate = (jnp.dot(km_hi[:, slab(hh)], qm, preferred_element_type=F32)
                + jnp.dot(km_lo[:, slab(hh)], qm, preferred_element_type=F32))
        g = jnp.where(valid, gate, NEG_INF)
        sel = blk_idx == qi
        for _ in range(MOBA_TOP_K):
            top = jnp.max(g, axis=0, keepdims=True)
            first = jnp.min(jnp.where(g == top, blk_idx_f, float(n_blocks)), axis=0, keepdims=True)
            hit = blk_idx_f == first
            sel = sel | (hit & valid)
            g = jnp.where(hit, NEG_INF, g)
        rb_ref[hh] = jnp.where(sel, 0.0, NEG_INF)
        m_ref[hh] = jnp.full((1, blk), NEG_INF, F32)
        acc_ref[hh] = jnp.zeros((V_AUG_ROWS, blk), F32)

    def group_rows(g):
        ja = qi - 2 * g
        return ja, jnp.maximum(ja - 1, 0)

    def scores(hh, g, s_buf, cm_buf, first_group):
        _, lo = group_rows(g)
        kslab = k_ref[pl.ds(pl.multiple_of(lo * blk, blk), 2 * blk), slab(hh)]
        s = jnp.dot(kslab, qm_ref[hh], preferred_element_type=F32)
        for i in range(2):
            si = s[i * blk:(i + 1) * blk]
            if first_group:
                tile = jnp.where(qi == 0, 1, 0) if i == 0 else 1
                si = si + bias_ref[hh, tile]
            s_buf[hh, i] = si
            cm_buf[hh, pl.ds(i, 1), :] = jnp.max(si, axis=0, keepdims=True)

    def attend(hh, g, s_buf, cm_buf):
        ja, lo = group_rows(g)
        masks = [rb_ref[hh, pl.ds(lo, 1), :],
                 jnp.where(ja >= 1, rb_ref[hh, pl.ds(lo + 1, 1), :], NEG_INF)]
        m_old = m_ref[hh]
        m_new = m_old
        for i in range(2):
            m_new = jnp.maximum(m_new, cm_buf[hh, pl.ds(i, 1), :] + masks[i])
        acc = jnp.exp2(m_old - m_new) * acc_ref[hh]
        for i in range(2):
            p = jnp.exp2((s_buf[hh, i] - m_new).astype(BF16))
            pv = jnp.dot(vaug_ref[lo + i, hh], p, preferred_element_type=F32)
            acc = acc + jnp.where(masks[i] == 0.0, pv, 0.0)
        acc_ref[hh] = acc
        m_ref[hh] = m_new

    n_more = qi // 2
    for hh in range(nh):
        scores(hh, 0, sa_ref, cma_ref, True)

    def stage(it, cur, nxt):
        lead = 1
        for hh in range(lead):
            scores(hh, it + 1, *nxt, False)
        for hh in range(nh):
            if hh + lead < nh:
                scores(hh + lead, it + 1, *nxt, False)
            attend(hh, it, *cur)

    buf_a, buf_b = (sa_ref, cma_ref), (sb_ref, cmb_ref)

    def step(it, carry):
        @pl.when(it % 2 == 0)
        def _():
            stage(it, buf_a, buf_b)

        @pl.when(it % 2 == 1)
        def _():
            stage(it, buf_b, buf_a)

        return carry

    lax.fori_loop(0, n_more, step, 0)

    @pl.when(n_more % 2 == 0)
    def _():
        for hh in range(nh):
            attend(hh, n_more, *buf_a)

    @pl.when(n_more % 2 == 1)
    def _():
        for hh in range(nh):
            attend(hh, n_more, *buf_b)

    for hh in range(nh):
        acc = acc_ref[hh]
        ot_ref[hh * HEAD_DIM:(hh + 1) * HEAD_DIM, :] = (
            acc[0:HEAD_DIM] * (1.0 / acc[HEAD_DIM:HEAD_DIM + 1]))

    o_ref[...] = ot_ref[...].T.astype(o_ref.dtype)


def _moba_attention(proj, bias_tiles, batch, seq):
    proj3 = proj.reshape(proj.shape[0] // MOBA_BLOCK, MOBA_BLOCK, proj.shape[1])
    n_blocks = seq // MOBA_BLOCK
    nh = MOBA_HEADS_PER_STEP
    n_hg = ATTN_HEADS // nh
    hg_w = nh * HEAD_DIM
    kernel = functools.partial(_moba_kernel, n_blocks=n_blocks)
    return pl.pallas_call(
        kernel,
        grid=(batch, n_hg, n_blocks),
        in_specs=[
            pl.BlockSpec((1, MOBA_BLOCK, hg_w), lambda b, hg, qi: (b * n_blocks + qi, 0, hg)),
            pl.BlockSpec((seq, hg_w), lambda b, hg, qi: (b, n_hg + hg)),
            pl.BlockSpec((n_blocks, MOBA_BLOCK, hg_w), lambda b, hg, qi: (b, 0, 2 * n_hg + hg)),
            pl.BlockSpec((nh, 2, MOBA_BLOCK, MOBA_BLOCK), lambda b, hg, qi: (hg, 0, 0, 0)),
        ],
        out_specs=pl.BlockSpec((MOBA_BLOCK, hg_w), lambda b, hg, qi: (b * n_blocks + qi, hg)),
        out_shape=jax.ShapeDtypeStruct((batch * seq, ATTN_HEADS * HEAD_DIM), BF16),
        scratch_shapes=[
            pltpu.VMEM((n_blocks, hg_w), F32),
            pltpu.VMEM((n_blocks, nh, V_AUG_ROWS, MOBA_BLOCK), BF16),
            pltpu.VMEM((nh, min(hg_w, MXU_DEPTH), MOBA_BLOCK), BF16),
            pltpu.VMEM((nh, 2, MOBA_BLOCK, MOBA_BLOCK), F32),
            pltpu.VMEM((nh, 2, MOBA_BLOCK, MOBA_BLOCK), F32),
            pltpu.VMEM((nh, 8, MOBA_BLOCK), F32),
            pltpu.VMEM((nh, 8, MOBA_BLOCK), F32),
            pltpu.VMEM((nh, 1, MOBA_BLOCK), F32),
            pltpu.VMEM((nh, V_AUG_ROWS, MOBA_BLOCK), F32),
            pltpu.VMEM((nh, n_blocks, MOBA_BLOCK), F32),
            pltpu.VMEM((hg_w, MOBA_BLOCK), F32),
        ],
        compiler_params=pltpu.CompilerParams(
            dimension_semantics=("parallel", "parallel", "arbitrary"),
            vmem_limit_bytes=VMEM_LIMIT),
    )(proj3, proj, proj3, bias_tiles)


def _hdot_nt(a, b):
    def split(x):
        hi = x.astype(BF16)
        return hi, (x - hi.astype(F32)).astype(BF16)

    def nt(u, v):
        return lax.dot_general(u, v, _NT_DIMS, preferred_element_type=F32)

    a_hi, a_lo = split(a)
    b_hi, b_lo = split(b)
    return nt(a_hi, b_hi) + nt(a_hi, b_lo) + nt(a_lo, b_hi)


def _repeat_rows(x, n):
    return jnp.concatenate([jnp.broadcast_to(x[i:i + 1], (n, x.shape[1]))
                            for i in range(x.shape[0])], axis=0)


def _tile_rows(x, n):
    return jnp.concatenate([x] * n, axis=0)


def _s5_tables_kernel(logdt_ref, lre_ref, lim_ref, bre_ref, bim_ref, cre_ref, cim_ref, d_ref,
                      tg_ref, bn_ref, cn_ref, are_ref, aim_ref):
    lc, ch, p = SSM_CHUNK, SSM_GROUP_CH, SSM_STATE
    w = lc * ch
    rr = lax.broadcasted_iota(jnp.int32, (w, w), 0)
    cc = lax.broadcasted_iota(jnp.int32, (w, w), 1)
    causal = rr // ch >= cc // ch
    diag = rr == cc
    e_r = lax.broadcasted_iota(jnp.int32, (ch, w), 0)
    e_c = lax.broadcasted_iota(jnp.int32, (ch, w), 1)
    lane_tile = jnp.where(e_c % ch == e_r, 1.0, 0.0).astype(F32)
    tau = lax.broadcasted_iota(jnp.int32, (2 * lc, p), 0).astype(F32)

    for k in range(tg_ref.shape[0]):
        lam_re = lre_ref[k]
        lam_im = lim_ref[k]
        dt = jnp.exp(logdt_ref[k])
        lr = lam_re * dt
        li = lam_im * dt

        mag = jnp.exp(lr * tau)
        cs = jnp.cos(li * tau)
        sn = jnp.sin(li * tau)
        pos_re, pos_im = mag * cs, mag * sn
        inv = jnp.exp(-lr * tau[:lc])
        neg_re, neg_im = inv * cs[:lc], -inv * sn[:lc]

        lam1_re, lam1_im = pos_re[1:2], pos_im[1:2]
        lamk_re, lamk_im = pos_re[lc - 1:lc], pos_im[lc - 1:lc]
        are_ref[k] = pos_re[lc:lc + 1]
        aim_ref[k] = pos_im[lc:lc + 1]

        num_re, num_im = lam1_re - 1.0, lam1_im
        den = lam_re * lam_re + lam_im * lam_im
        coef_re = (num_re * lam_re + num_im * lam_im) / den
        coef_im = (num_im * lam_re - num_re * lam_im) / den
        bt_re, bt_im = bre_ref[k], bim_ref[k]
        bb_re = coef_re * bt_re - coef_im * bt_im
        bb_im = coef_re * bt_im + coef_im * bt_re

        bbt_re, bbt_im = _tile_rows(bb_re, lc), _tile_rows(bb_im, lc)
        ngx_re, ngx_im = _repeat_rows(neg_re, ch), _repeat_rows(neg_im, ch)
        bneg_re = bbt_re * ngx_re - bbt_im * ngx_im
        bneg_im = bbt_re * ngx_im + bbt_im * ngx_re

        ct_re, ct_im = _tile_rows(cre_ref[k], lc), _tile_rows(cim_ref[k], lc)
        psx_re, psx_im = _repeat_rows(pos_re[:lc], ch), _repeat_rows(pos_im[:lc], ch)
        cpos_re = ct_re * psx_re - ct_im * psx_im
        cpos_im = ct_re * psx_im + ct_im * psx_re

        raw = _hdot_nt(cpos_re, bneg_re) - _hdot_nt(cpos_im, bneg_im)
        d_lanes = jnp.dot(jnp.broadcast_to(d_ref[k], (8, ch)), lane_tile,
                          preferred_element_type=F32, precision=lax.Precision.HIGHEST)[0:1]
        tg = jnp.where(causal, raw, 0.0) + jnp.where(diag, d_lanes, 0.0)
        tg_ref[k] = tg.astype(tg_ref.dtype)

        bn_ref[k, 0] = (bneg_re * lamk_re - bneg_im * lamk_im).T.astype(bn_ref.dtype)
        bn_ref[k, 1] = (bneg_re * lamk_im + bneg_im * lamk_re).T.astype(bn_ref.dtype)
        cp_re = cpos_re * lam1_re - cpos_im * lam1_im
        cp_im = cpos_re * lam1_im + cpos_im * lam1_re
        cn_ref[k, 0] = cp_re.astype(cn_ref.dtype)
        cn_ref[k, 1] = (-cp_im).astype(cn_ref.dtype)


def _s5_tables(log_dt, lam_re, lam_im, b_re, b_im, c_re, c_im, d_skip, gb=4):
    g, p, ch = SSM_GROUPS, SSM_STATE, SSM_GROUP_CH
    w = SSM_CHUNK * ch
    row = lambda a, n: a.reshape(g, 1, n)
    spec3 = lambda s1, s2: pl.BlockSpec((gb, s1, s2), lambda i: (i, 0, 0))
    spec4 = lambda s1, s2: pl.BlockSpec((gb, 2, s1, s2), lambda i: (i, 0, 0, 0))
    return pl.pallas_call(
        _s5_tables_kernel,
        grid=(g // gb,),
        in_specs=[spec3(1, 1), spec3(1, p), spec3(1, p), spec3(ch, p), spec3(ch, p),
                  spec3(ch, p), spec3(ch, p), spec3(1, ch)],
        out_specs=[spec3(w, w), spec4(p, w), spec4(w, p), spec3(1, p), spec3(1, p)],
        out_shape=[jax.ShapeDtypeStruct((g, w, w), BF16),
                   jax.ShapeDtypeStruct((g, 2, p, w), BF16),
                   jax.ShapeDtypeStruct((g, 2, w, p), BF16),
                   jax.ShapeDtypeStruct((g, 1, p), F32),
                   jax.ShapeDtypeStruct((g, 1, p), F32)],
        compiler_params=pltpu.CompilerParams(dimension_semantics=("parallel",)),
    )(row(log_dt, 1), row(lam_re, p), row(lam_im, p),
      jnp.swapaxes(b_re, 1, 2), jnp.swapaxes(b_im, 1, 2), c_re, c_im, row(d_skip, ch))


def _s5_scan_kernel(u_ref, tg_ref, bn_ref, cn_ref, are_ref, aim_ref, y_ref,
                    uf_ref, v_ref, yt_ref, sre_ref, sim_ref, xre_ref, xim_ref, *, n_batch):
    lc, ch = SSM_CHUNK, SSM_GROUP_CH
    gb = tg_ref.shape[0]
    n_rows = u_ref.shape[0] // lc
    n_chunks = n_rows // n_batch
    w = lc * ch

    uf_ref[...] = u_ref[...].astype(F32)
    for s in range(lc):
        ust = uf_ref[pl.ds(s, n_rows, stride=lc), :].T
        for k in range(gb):
            v_ref[k, s * ch:(s + 1) * ch, :] = ust[k * ch:(k + 1) * ch, :].astype(v_ref.dtype)

    for k in range(gb):
        u = v_ref[k]
        rows = pl.ds(k, n_rows, stride=gb)
        sre_ref[rows, :] = jnp.dot(bn_ref[k, 0], u, preferred_element_type=F32).T
        sim_ref[rows, :] = jnp.dot(bn_ref[k, 1], u, preferred_element_type=F32).T

    a_re = are_ref[...]
    a_im = aim_ref[...]

    def step(c, carry):
        new = []
        for b in range(n_batch):
            x_re, x_im = carry[2 * b], carry[2 * b + 1]
            rows = pl.ds(pl.multiple_of((b * n_chunks + c) * gb, gb), gb)
            xre_ref[rows, :] = x_re
            xim_ref[rows, :] = x_im
            new.append(a_re * x_re - a_im * x_im + sre_ref[rows, :])
            new.append(a_re * x_im + a_im * x_re + sim_ref[rows, :])
        return tuple(new)

    zero = jnp.zeros(a_re.shape, F32)
    lax.fori_loop(0, n_chunks, step, (zero,) * (2 * n_batch), unroll=SCAN_UNROLL)

    for k in range(gb):
        rows = pl.ds(k, n_rows, stride=gb)
        y = jnp.dot(tg_ref[k], v_ref[k], preferred_element_type=F32)
        y = y + lax.dot_general(cn_ref[k, 0], xre_ref[rows, :].astype(BF16), _NT_DIMS,
                                preferred_element_type=F32)
        y = y + lax.dot_general(cn_ref[k, 1], xim_ref[rows, :].astype(BF16), _NT_DIMS,
                                preferred_element_type=F32)
        for s in range(lc):
            yt_ref[s, k * ch:(k + 1) * ch, :] = y[s * ch:(s + 1) * ch, :]

    for s in range(lc):
        y_ref[pl.ds(s, n_rows, stride=lc), :] = yt_ref[s].T.astype(y_ref.dtype)


def _s5_scan(ku, tg, bn, cn, a_re, a_im, n_batch, u_col0, gb=8):
    t = ku.shape[0]
    g, w, _ = tg.shape
    p, ch = SSM_STATE, SSM_GROUP_CH
    n_rows = t // SSM_CHUNK
    lanes = gb * ch
    kernel = functools.partial(_s5_scan_kernel, n_batch=n_batch)
    return pl.pallas_call(
        kernel,
        grid=(g // gb,),
        in_specs=[pl.BlockSpec((t, lanes), lambda i: (0, u_col0 // lanes + i)),
                  pl.BlockSpec((gb, w, w), lambda i: (i, 0, 0)),
                  pl.BlockSpec((gb, 2, p, w), lambda i: (i, 0, 0, 0)),
                  pl.BlockSpec((gb, 2, w, p), lambda i: (i, 0, 0, 0)),
                  pl.BlockSpec((gb, p), lambda i: (i, 0)),
                  pl.BlockSpec((gb, p), lambda i: (i, 0))],
        out_specs=pl.BlockSpec((t, lanes), lambda i: (0, i)),
        out_shape=jax.ShapeDtypeStruct((t, g * ch), F32),
        scratch_shapes=[pltpu.VMEM((t, lanes), F32),
                        pltpu.VMEM((gb, w, n_rows), BF16),
                        pltpu.VMEM((SSM_CHUNK, lanes, n_rows), F32)]
                       + [pltpu.VMEM((gb * n_rows, p), F32) for _ in range(4)],
        compiler_params=pltpu.CompilerParams(dimension_semantics=("parallel",),
                                             vmem_limit_bytes=VMEM_LIMIT),
    )(ku, tg, bn, cn, a_re.reshape(g, p), a_im.reshape(g, p))


def _out_proj_kernel(x_ref, a_ref, y_ref, wglu_ref, bglu_ref, ga_ref, gs_ref, wa_ref, ws_ref, o_ref):
    z = jax.nn.gelu(y_ref[...])
    gl = jnp.dot(z.astype(BF16), wglu_ref[...], preferred_element_type=F32) + bglu_ref[...]
    s = z * jax.nn.sigmoid(gl)
    a_n = _rms_rows(a_ref[...].astype(F32), ga_ref[...]).astype(BF16)
    s_n = _rms_rows(s, gs_ref[...]).astype(BF16)
    mix = (jnp.dot(a_n, wa_ref[...], preferred_element_type=F32)
           + jnp.dot(s_n, ws_ref[...], preferred_element_type=F32))
    o_ref[...] = x_ref[...] + mix


def _out_proj(x2, attn, y, w_glu, b_glu, g_a, g_s, w_out, tm=512):
    t, d = x2.shape
    wa = attn.shape[1]
    ws = y.shape[1]
    row = lambda i: (i, 0)
    fixed = lambda i: (0, 0)
    once = pl.Buffered(1)
    return pl.pallas_call(
        _out_proj_kernel,
        grid=(t // tm,),
        in_specs=[pl.BlockSpec((tm, d), row),
                  pl.BlockSpec((tm, wa), row),
                  pl.BlockSpec((tm, ws), row),
                  pl.BlockSpec((ws, ws), fixed, pipeline_mode=once),
                  pl.BlockSpec((1, ws), fixed),
                  pl.BlockSpec((1, wa), fixed),
                  pl.BlockSpec((1, ws), fixed),
                  pl.BlockSpec((wa, d), fixed, pipeline_mode=once),
                  pl.BlockSpec((ws, d), lambda i: (wa // ws, 0), pipeline_mode=once)],
        out_specs=pl.BlockSpec((tm, d), row),
        out_shape=jax.ShapeDtypeStruct((t, d), F32),
        compiler_params=pltpu.CompilerParams(dimension_semantics=("parallel",),
                                             vmem_limit_bytes=VMEM_LIMIT),
    )(x2, attn, y, w_glu, b_glu.reshape(1, ws), g_a.reshape(1, wa), g_s.reshape(1, ws),
      w_out, w_out)


def _ffn_up_kernel(x_ref, halo_ref, g_ref, wg_ref, wv_ref, cwg_ref, cwv_ref, cbg_ref, cbv_ref,
                   o_ref, h_ref, *, tiles_per_seq):
    tm = x_ref.shape[0]

    @pl.when(pl.program_id(1) == 0)
    def _():
        keep = jnp.where(pl.program_id(0) % tiles_per_seq == 0, 0.0, 1.0)
        h_ref[pl.ds(0, CONV_HALO), :] = (_rms_rows(halo_ref[...], g_ref[...]) * keep).astype(BF16)
        h_ref[pl.ds(CONV_HALO, tm), :] = _rms_rows(x_ref[...], g_ref[...]).astype(BF16)

    h = h_ref[...]

    def conv(w_ref, cw_ref, cb_ref):
        up = jnp.dot(h, w_ref[...].astype(BF16), preferred_element_type=F32)
        cw = cw_ref[...]
        out = up[CONV_HALO:] * cw[CONV_WIDTH - 1:CONV_WIDTH] + cb_ref[...]
        for j in range(CONV_WIDTH - 1):
            lag = CONV_WIDTH - 1 - j
            out = out + up[CONV_HALO - lag:CONV_HALO - lag + tm] * cw[j:j + 1]
        return out

    gate = conv(wg_ref, cwg_ref, cbg_ref)
    val = conv(wv_ref, cwv_ref, cbv_ref)
    o_ref[...] = (jax.nn.silu(gate) * val).astype(o_ref.dtype)


def _ffn_up(x1, g, w_up, conv_w, conv_b, seq, tm=1024, tn=512):
    t, d = x1.shape
    f = w_up.shape[1] // 2
    nf = f // tn
    tiles_per_seq = seq // tm
    halo_blocks = tm // CONV_HALO
    kernel = functools.partial(_ffn_up_kernel, tiles_per_seq=tiles_per_seq)
    cb = conv_b.reshape(1, 2 * f)
    return pl.pallas_call(
        kernel,
        grid=(t // tm, nf),
        in_specs=[pl.BlockSpec((tm, d), lambda i, j: (i, 0)),
                  pl.BlockSpec((CONV_HALO, d), lambda i, j: (jnp.maximum(i * halo_blocks - 1, 0), 0)),
                  pl.BlockSpec((1, d), lambda i, j: (0, 0)),
                  pl.BlockSpec((d, tn), lambda i, j: (0, j)),
                  pl.BlockSpec((d, tn), lambda i, j: (0, nf + j)),
                  pl.BlockSpec((CONV_WIDTH, tn), lambda i, j: (0, j)),
                  pl.BlockSpec((CONV_WIDTH, tn), lambda i, j: (0, nf + j)),
                  pl.BlockSpec((1, tn), lambda i, j: (0, j)),
                  pl.BlockSpec((1, tn), lambda i, j: (0, nf + j))],
        out_specs=pl.BlockSpec((tm, tn), lambda i, j: (i, j)),
        out_shape=jax.ShapeDtypeStruct((t, f), BF16),
        scratch_shapes=[pltpu.VMEM((CONV_HALO + tm, d), BF16)],
        compiler_params=pltpu.CompilerParams(dimension_semantics=("parallel", "arbitrary"),
                                             vmem_limit_bytes=VMEM_LIMIT),
    )(x1, x1, g.reshape(1, d), w_up, w_up, conv_w, conv_w, cb, cb)


def _ffn_down_kernel(a_ref, w_ref, x_ref, g_ref, o_ref, *, final_norm):
    y = x_ref[...] + jnp.dot(a_ref[...], w_ref[...], preferred_element_type=F32)
    o_ref[...] = _rms_rows(y, g_ref[...]) if final_norm else y


def _ffn_down(act, w_down, x1, g, final_norm, tm=256):
    t, f = act.shape
    d = w_down.shape[1]
    return pl.pallas_call(
        functools.partial(_ffn_down_kernel, final_norm=final_norm),
        grid=(t // tm,),
        in_specs=[pl.BlockSpec((tm, f), lambda i: (i, 0)),
                  pl.BlockSpec((f, d), lambda i: (0, 0), pipeline_mode=pl.Buffered(1)),
                  pl.BlockSpec((tm, d), lambda i: (i, 0)),
                  pl.BlockSpec((1, d), lambda i: (0, 0))],
        out_specs=pl.BlockSpec((tm, d), lambda i: (i, 0)),
        out_shape=jax.ShapeDtypeStruct((t, d), F32),
        compiler_params=pltpu.CompilerParams(dimension_semantics=("parallel",),
                                             vmem_limit_bytes=VMEM_LIMIT),
    )(act, w_down, x1, g.reshape(1, d))


def kernel(x, norm_mix, w_in, rel_bias_table, ssm_lam_re, ssm_lam_im, ssm_log_dt, ssm_b_re, ssm_b_im, ssm_c_re, ssm_c_im, ssm_d, ssm_w_glu, ssm_b_glu, norm_attn_out, norm_ssm_out, w_out, norm_ffn, w_ffn_up, ffn_conv_w, ffn_conv_b, w_ffn_down, norm_final):
    batch, seq, d_model = x.shape
    depth = w_in.shape[0]
    aw = ATTN_HEADS * HEAD_DIM
    sw = SSM_GROUPS * SSM_GROUP_CH
    t = batch * seq
    n_blocks = seq // MOBA_BLOCK
    assert seq % MOBA_BLOCK == 0 and seq % SSM_CHUNK == 0

    bias_tiles = _bias_tiles(rel_bias_table)
    x2 = x.reshape(t, d_model)
    for l in range(depth):
        proj = _in_proj(x2, norm_mix[l], w_in, l, q_cols=aw, q_scale=HEAD_DIM ** -0.5 * LOG2E)

        attn = _moba_attention(proj, bias_tiles, batch, seq)

        tg, bn, cn, a_re, a_im = _s5_tables(ssm_log_dt[l], ssm_lam_re[l], ssm_lam_im[l],
                                            ssm_b_re[l], ssm_b_im[l], ssm_c_re[l], ssm_c_im[l],
                                            ssm_d[l])
        y = _s5_scan(proj, tg, bn, cn, a_re, a_im, batch, u_col0=3 * aw)

        x2 = _out_proj(x2, attn, y, ssm_w_glu[l].astype(BF16), ssm_b_glu[l], norm_attn_out[l],
                       norm_ssm_out[l], w_out[l].astype(BF16))

        act = _ffn_up(x2, norm_ffn[l], w_ffn_up[l], ffn_conv_w[l], ffn_conv_b[l], seq)
        x2 = _ffn_down(act, w_ffn_down[l].astype(BF16), x2, norm_final, final_norm=(l == depth - 1))
    return x2.reshape(batch, seq, d_model)
```

```python
import functools
import math

import jax
import jax.numpy as jnp
from jax import lax
from jax.experimental import pallas as pl
from jax.experimental.pallas import tpu as pltpu

F32 = jnp.float32
BF16 = jnp.bfloat16

ATTN_HEADS = 16
HEAD_DIM = 64
SSM_GROUP_CH = 16
SSM_GROUPS = 64
SSM_STATE = 64
MOBA_BLOCK = 256
MOBA_TOP_K = 3
NUM_BUCKETS = 32
MAX_DISTANCE = 128
CONV_WIDTH = 3
RMS_EPS = 1e-6

SSM_CHUNK = 16
MOBA_HEADS_PER_STEP = 8
MXU_DEPTH = 256
SCAN_UNROLL = 8
CONV_HALO = 16
VMEM_LIMIT = 52 * 1024 * 1024

NEG_INF = float("-inf")
LOG2E = math.log2(math.e)
V_AUG_ROWS = HEAD_DIM + 16


def _rms_rows(x, g):
    ms = jnp.mean(x * x, axis=-1, keepdims=True)
    return x * lax.rsqrt(ms + RMS_EPS) * g


_NT_DIMS = (((1,), (1,)), ((), ()))


def _in_proj_kernel(x_ref, g_ref, w_ref, *rest):
    n_extra = (len(rest) - 2) // 2
    extra_in, o_ref = rest[:n_extra], rest[n_extra]
    extra_out, h_ref = rest[n_extra + 1:2 * n_extra + 1], rest[-1]

    @pl.when(pl.program_id(1) == 0)
    def _():
        h_ref[...] = _rms_rows(x_ref[...], g_ref[...]).astype(BF16)

    o_ref[...] = jnp.dot(h_ref[...], w_ref[...], preferred_element_type=F32).astype(o_ref.dtype)
    for src, dst in zip(extra_in, extra_out):
        dst[...] = src[...].astype(dst.dtype)


def _in_proj(x2, g, w, cast_along, tm=1024, tn=1024):
    t, d = x2.shape
    n = w.shape[1]
    n_i, n_j = t // tm, n // tn
    steps = n_i * n_j
    slabs = [pl.BlockSpec((a.shape[0] // steps, a.shape[1]), lambda i, j: (i * n_j + j, 0))
             for a in cast_along]
    outs = pl.pallas_call(
        _in_proj_kernel,
        grid=(n_i, n_j),
        in_specs=[pl.BlockSpec((tm, d), lambda i, j: (i, 0)),
                  pl.BlockSpec((1, d), lambda i, j: (0, 0)),
                  pl.BlockSpec((d, tn), lambda i, j: (0, j))] + slabs,
        out_specs=[pl.BlockSpec((tm, tn), lambda i, j: (i, j))] + slabs,
        out_shape=[jax.ShapeDtypeStruct((t, n), BF16)]
                  + [jax.ShapeDtypeStruct(a.shape, BF16) for a in cast_along],
        scratch_shapes=[pltpu.VMEM((tm, d), BF16)],
        compiler_params=pltpu.CompilerParams(dimension_semantics=("parallel", "arbitrary"),
                                             vmem_limit_bytes=VMEM_LIMIT),
    )(x2, g.reshape(1, d), w, *cast_along)
    return outs[0], outs[1:]


def _t5_bucket(dist):
    dist = jnp.maximum(dist, 0)
    max_exact = NUM_BUCKETS // 2
    log_ratio = jnp.log(jnp.maximum(dist, max_exact).astype(F32) / max_exact)
    large = max_exact + (log_ratio / math.log(MAX_DISTANCE / max_exact)
                         * (NUM_BUCKETS - max_exact)).astype(jnp.int32)
    large = jnp.minimum(large, NUM_BUCKETS - 1)
    return jnp.where(dist < max_exact, dist, large)


def _bias_tiles_kernel(tab_ref, o_ref):
    h = pl.program_id(0)
    c_far = tab_ref[NUM_BUCKETS - 1, h]
    kk = lax.broadcasted_iota(jnp.int32, (MOBA_BLOCK, MOBA_BLOCK), 0)
    qq = lax.broadcasted_iota(jnp.int32, (MOBA_BLOCK, MOBA_BLOCK), 1)
    for which, off in ((0, MOBA_BLOCK), (1, 0)):
        rel = qq - kk + off
        bucket = _t5_bucket(rel)
        acc = jnp.zeros((MOBA_BLOCK, MOBA_BLOCK), F32)
        for b in range(NUM_BUCKETS):
            acc = jnp.where(bucket == b, tab_ref[b, h], acc)
        acc = (acc - c_far) * LOG2E
        if which == 1:
            acc = jnp.where(rel >= 0, acc, NEG_INF)
        o_ref[0, which] = acc


def _bias_tiles(table):
    return pl.pallas_call(
        _bias_tiles_kernel,
        grid=(ATTN_HEADS,),
        in_specs=[pl.BlockSpec(memory_space=pltpu.SMEM)],
        out_specs=pl.BlockSpec((1, 2, MOBA_BLOCK, MOBA_BLOCK), lambda h: (h, 0, 0, 0)),
        out_shape=jax.ShapeDtypeStruct((ATTN_HEADS, 2, MOBA_BLOCK, MOBA_BLOCK), F32),
    )(table)


def _moba_kernel(q_ref, k_ref, v_ref, bias_ref, o_ref,
                 kmf_ref, vaug_ref, qm_ref, sa_ref, sb_ref, cma_ref, cmb_ref,
                 m_ref, acc_ref, rb_ref, ot_ref, *, n_blocks):
    qi = pl.program_id(2)
    blk = MOBA_BLOCK
    nh = qm_ref.shape[0]

    @pl.when(qi == 0)
    def _():
        ones = jnp.ones((V_AUG_ROWS - HEAD_DIM, blk), BF16)
        for j in range(n_blocks):
            kb = k_ref[j * blk:(j + 1) * blk, :].astype(F32)
            kmf_ref[pl.ds(j, 1), :] = jnp.sum(kb, axis=0, keepdims=True) * (1.0 / blk)
            vt = v_ref[j].T
            for hh in range(nh):
                vaug_ref[j, hh, 0:HEAD_DIM, :] = vt[hh * HEAD_DIM:(hh + 1) * HEAD_DIM, :]
                vaug_ref[j, hh, HEAD_DIM:V_AUG_ROWS, :] = ones

    kmf = kmf_ref[...]
    km_hi = kmf.astype(BF16)
    km_lo = (kmf - km_hi.astype(F32)).astype(BF16)

    q2 = q_ref[0].T
    q_zero = jnp.zeros((HEAD_DIM, blk), q2.dtype)
    blk_idx = lax.broadcasted_iota(jnp.int32, (n_blocks, blk), 0)
    blk_idx_f = blk_idx.astype(F32)
    valid = blk_idx < qi
    hps = qm_ref.shape[1] // HEAD_DIM
    slab = lambda hh: slice((hh // hps) * hps * HEAD_DIM, (hh // hps + 1) * hps * HEAD_DIM)

    for hh in range(nh):
        vrows = slice(hh * HEAD_DIM, (hh + 1) * HEAD_DIM)
        qm = jnp.concatenate([q_zero] * (hh % hps) + [q2[vrows]] + [q_zero] * (hps - 1 - hh % hps),
                             axis=0)
        qm_ref[hh] = qm

        gate = (jnp.dot(km_hi[:, slab(hh)], qm, preferred_element_type=F32)
                + jnp.dot(km_lo[:, slab(hh)], qm, preferred_element_type=F32))
        g = jnp.where(valid, gate, NEG_INF)
        sel = blk_idx == qi
        for _ in range(MOBA_TOP_K):
            top = jnp.max(g, axis=0, keepdims=True)
            first = jnp.min(jnp.where(g == top, blk_idx_f, float(n_blocks)), axis=0, keepdims=True)
            hit = blk_idx_f == first
            sel = sel | (hit & valid)
            g = jnp.where(hit, NEG_INF, g)
        rb_ref[hh] = jnp.where(sel, 0.0, NEG_INF)
        m_ref[hh] = jnp.full((1, blk), NEG_INF, F32)
        acc_ref[hh] = jnp.zeros((V_AUG_ROWS, blk), F32)

    def group_rows(g):
        ja = qi - 2 * g
        return ja, jnp.maximum(ja - 1, 0)

    def scores(hh, g, s_buf, cm_buf, first_group):
        _, lo = group_rows(g)
        kslab = k_ref[pl.ds(pl.multiple_of(lo * blk, blk), 2 * blk), slab(hh)]
        s = jnp.dot(kslab, qm_ref[hh], preferred_element_type=F32)
        for i in range(2):
            si = s[i * blk:(i + 1) * blk]
            if first_group:
                tile = jnp.where(qi == 0, 1, 0) if i == 0 else 1
                si = si + bias_ref[hh, tile]
            s_buf[hh, i] = si
            cm_buf[hh, pl.ds(i, 1), :] = jnp.max(si, axis=0, keepdims=True)

    def attend(hh, g, s_buf, cm_buf):
        ja, lo = group_rows(g)
        masks = [rb_ref[hh, pl.ds(lo, 1), :],
                 jnp.where(ja >= 1, rb_ref[hh, pl.ds(lo + 1, 1), :], NEG_INF)]
        m_old = m_ref[hh]
        m_new = m_old
        for i in range(2):
            m_new = jnp.maximum(m_new, cm_buf[hh, pl.ds(i, 1), :] + masks[i])
        acc = jnp.exp2(m_old - m_new) * acc_ref[hh]
        for i in range(2):
            p = jnp.exp2((s_buf[hh, i] - m_new).astype(BF16))
            pv = jnp.dot(vaug_ref[lo + i, hh], p, preferred_element_type=F32)
            acc = acc + jnp.where(masks[i] == 0.0, pv, 0.0)
        acc_ref[hh] = acc
        m_ref[hh] = m_new

    n_more = qi // 2
    for hh in range(nh):
        scores(hh, 0, sa_ref, cma_ref, True)

    def stage(it, cur, nxt):
        lead = 1
        for hh in range(lead):
            scores(hh, it + 1, *nxt, False)
        for hh in range(nh):
            if hh + lead < nh:
                scores(hh + lead, it + 1, *nxt, False)
            attend(hh, it, *cur)

    buf_a, buf_b = (sa_ref, cma_ref), (sb_ref, cmb_ref)

    def step(it, carry):
        @pl.when(it % 2 == 0)
        def _():
            stage(it, buf_a, buf_b)

        @pl.when(it % 2 == 1)
        def _():
            stage(it, buf_b, buf_a)

        return carry

    lax.fori_loop(0, n_more, step, 0)

    @pl.when(n_more % 2 == 0)
    def _():
        for hh in range(nh):
            attend(hh, n_more, *buf_a)

    @pl.when(n_more % 2 == 1)
    def _():
        for hh in range(nh):
            attend(hh, n_more, *buf_b)

    for hh in range(nh):
        acc = acc_ref[hh]
        ot_ref[hh * HEAD_DIM:(hh + 1) * HEAD_DIM, :] = (
            acc[0:HEAD_DIM] * (1.0 / acc[HEAD_DIM:HEAD_DIM + 1]))

    o_ref[...] = ot_ref[...].T.astype(o_ref.dtype)


def _moba_attention(proj, bias_tiles, batch, seq):
    proj3 = proj.reshape(proj.shape[0] // MOBA_BLOCK, MOBA_BLOCK, proj.shape[1])
    n_blocks = seq // MOBA_BLOCK
    nh = MOBA_HEADS_PER_STEP
    n_hg = ATTN_HEADS // nh
    hg_w = nh * HEAD_DIM
    kernel = functools.partial(_moba_kernel, n_blocks=n_blocks)
    return pl.pallas_call(
        kernel,
        grid=(batch, n_hg, n_blocks),
        in_specs=[
            pl.BlockSpec((1, MOBA_BLOCK, hg_w), lambda b, hg, qi: (b * n_blocks + qi, 0, hg)),
            pl.BlockSpec((seq, hg_w), lambda b, hg, qi: (b, n_hg + hg)),
            pl.BlockSpec((n_blocks, MOBA_BLOCK, hg_w), lambda b, hg, qi: (b, 0, 2 * n_hg + hg)),
            pl.BlockSpec((nh, 2, MOBA_BLOCK, MOBA_BLOCK), lambda b, hg, qi: (hg, 0, 0, 0)),
        ],
        out_specs=pl.BlockSpec((MOBA_BLOCK, hg_w), lambda b, hg, qi: (b * n_blocks + qi, hg)),
        out_shape=jax.ShapeDtypeStruct((batch * seq, ATTN_HEADS * HEAD_DIM), BF16),
        scratch_shapes=[
            pltpu.VMEM((n_blocks, hg_w), F32),
            pltpu.VMEM((n_blocks, nh, V_AUG_ROWS, MOBA_BLOCK), BF16),
            pltpu.VMEM((nh, min(hg_w, MXU_DEPTH), MOBA_BLOCK), BF16),
            pltpu.VMEM((nh, 2, MOBA_BLOCK, MOBA_BLOCK), F32),
            pltpu.VMEM((nh, 2, MOBA_BLOCK, MOBA_BLOCK), F32),
            pltpu.VMEM((nh, 8, MOBA_BLOCK), F32),
            pltpu.VMEM((nh, 8, MOBA_BLOCK), F32),
            pltpu.VMEM((nh, 1, MOBA_BLOCK), F32),
            pltpu.VMEM((nh, V_AUG_ROWS, MOBA_BLOCK), F32),
            pltpu.VMEM((nh, n_blocks, MOBA_BLOCK), F32),
            pltpu.VMEM((hg_w, MOBA_BLOCK), F32),
        ],
        compiler_params=pltpu.CompilerParams(
            dimension_semantics=("parallel", "parallel", "arbitrary"),
            vmem_limit_bytes=VMEM_LIMIT),
    )(proj3, proj, proj3, bias_tiles)


def _hdot_nt(a, b):
    def split(x):
        hi = x.astype(BF16)
        return hi, (x - hi.astype(F32)).astype(BF16)

    def nt(u, v):
        return lax.dot_general(u, v, _NT_DIMS, preferred_element_type=F32)

    a_hi, a_lo = split(a)
    b_hi, b_lo = split(b)
    return nt(a_hi, b_hi) + nt(a_hi, b_lo) + nt(a_lo, b_hi)


def _repeat_rows(x, n):
    return jnp.concatenate([jnp.broadcast_to(x[i:i + 1], (n, x.shape[1]))
                            for i in range(x.shape[0])], axis=0)


def _tile_rows(x, n):
    return jnp.concatenate([x] * n, axis=0)


def _s5_tables_kernel(logdt_ref, lre_ref, lim_ref, bre_ref, bim_ref, cre_ref, cim_ref, d_ref,
                      win_ref, tg_ref, bn_ref, cn_ref, are_ref, aim_ref, wq_ref, *, q_cols, q_scale):
    wq_ref[:, :q_cols] = (win_ref[:, :q_cols] * q_scale).astype(wq_ref.dtype)
    wq_ref[:, q_cols:] = win_ref[:, q_cols:].astype(wq_ref.dtype)

    lc, ch, p = SSM_CHUNK, SSM_GROUP_CH, SSM_STATE
    w = lc * ch
    rr = lax.broadcasted_iota(jnp.int32, (w, w), 0)
    cc = lax.broadcasted_iota(jnp.int32, (w, w), 1)
    causal = rr // ch >= cc // ch
    diag = rr == cc
    e_r = lax.broadcasted_iota(jnp.int32, (ch, w), 0)
    e_c = lax.broadcasted_iota(jnp.int32, (ch, w), 1)
    lane_tile = jnp.where(e_c % ch == e_r, 1.0, 0.0).astype(F32)
    tau = lax.broadcasted_iota(jnp.int32, (2 * lc, p), 0).astype(F32)

    for k in range(tg_ref.shape[0]):
        lam_re = lre_ref[k]
        lam_im = lim_ref[k]
        dt = jnp.exp(logdt_ref[k])
        lr = lam_re * dt
        li = lam_im * dt

        mag = jnp.exp(lr * tau)
        cs = jnp.cos(li * tau)
        sn = jnp.sin(li * tau)
        pos_re, pos_im = mag * cs, mag * sn
        inv = jnp.exp(-lr * tau[:lc])
        neg_re, neg_im = inv * cs[:lc], -inv * sn[:lc]

        lam1_re, lam1_im = pos_re[1:2], pos_im[1:2]
        lamk_re, lamk_im = pos_re[lc - 1:lc], pos_im[lc - 1:lc]
        are_ref[k] = pos_re[lc:lc + 1]
        aim_ref[k] = pos_im[lc:lc + 1]

        num_re, num_im = lam1_re - 1.0, lam1_im
        den = lam_re * lam_re + lam_im * lam_im
        coef_re = (num_re * lam_re + num_im * lam_im) / den
        coef_im = (num_im * lam_re - num_re * lam_im) / den
        bt_re, bt_im = bre_ref[k], bim_ref[k]
        bb_re = coef_re * bt_re - coef_im * bt_im
        bb_im = coef_re * bt_im + coef_im * bt_re

        bbt_re, bbt_im = _tile_rows(bb_re, lc), _tile_rows(bb_im, lc)
        ngx_re, ngx_im = _repeat_rows(neg_re, ch), _repeat_rows(neg_im, ch)
        bneg_re = bbt_re * ngx_re - bbt_im * ngx_im
        bneg_im = bbt_re * ngx_im + bbt_im * ngx_re

        ct_re, ct_im = _tile_rows(cre_ref[k], lc), _tile_rows(cim_ref[k], lc)
        psx_re, psx_im = _repeat_rows(pos_re[:lc], ch), _repeat_rows(pos_im[:lc], ch)
        cpos_re = ct_re * psx_re - ct_im * psx_im
        cpos_im = ct_re * psx_im + ct_im * psx_re

        raw = _hdot_nt(cpos_re, bneg_re) - _hdot_nt(cpos_im, bneg_im)
        d_lanes = jnp.dot(jnp.broadcast_to(d_ref[k], (8, ch)), lane_tile,
                          preferred_element_type=F32, precision=lax.Precision.HIGHEST)[0:1]
        tg = jnp.where(causal, raw, 0.0) + jnp.where(diag, d_lanes, 0.0)
        tg_ref[k] = tg.astype(tg_ref.dtype)

        bn_ref[k, 0] = (bneg_re * lamk_re - bneg_im * lamk_im).T.astype(bn_ref.dtype)
        bn_ref[k, 1] = (bneg_re * lamk_im + bneg_im * lamk_re).T.astype(bn_ref.dtype)
        cp_re = cpos_re * lam1_re - cpos_im * lam1_im
        cp_im = cpos_re * lam1_im + cpos_im * lam1_re
        cn_ref[k, 0] = cp_re.astype(cn_ref.dtype)
        cn_ref[k, 1] = (-cp_im).astype(cn_ref.dtype)


def _s5_tables(log_dt, lam_re, lam_im, b_re, b_im, c_re, c_im, d_skip, w_in, q_cols, q_scale, gb=4):
    g, p, ch = SSM_GROUPS, SSM_STATE, SSM_GROUP_CH
    w = SSM_CHUNK * ch
    steps = g // gb
    d, n = w_in.shape
    row = lambda a, n: a.reshape(g, 1, n)
    spec3 = lambda s1, s2: pl.BlockSpec((gb, s1, s2), lambda i: (i, 0, 0))
    spec4 = lambda s1, s2: pl.BlockSpec((gb, 2, s1, s2), lambda i: (i, 0, 0, 0))
    slab = pl.BlockSpec((d // steps, n), lambda i: (i, 0))
    return pl.pallas_call(
        functools.partial(_s5_tables_kernel, q_cols=q_cols, q_scale=q_scale),
        grid=(steps,),
        in_specs=[spec3(1, 1), spec3(1, p), spec3(1, p), spec3(ch, p), spec3(ch, p),
                  spec3(ch, p), spec3(ch, p), spec3(1, ch), slab],
        out_specs=[spec3(w, w), spec4(p, w), spec4(w, p), spec3(1, p), spec3(1, p), slab],
        out_shape=[jax.ShapeDtypeStruct((g, w, w), BF16),
                   jax.ShapeDtypeStruct((g, 2, p, w), BF16),
                   jax.ShapeDtypeStruct((g, 2, w, p), BF16),
                   jax.ShapeDtypeStruct((g, 1, p), F32),
                   jax.ShapeDtypeStruct((g, 1, p), F32),
                   jax.ShapeDtypeStruct((d, n), BF16)],
        compiler_params=pltpu.CompilerParams(dimension_semantics=("parallel",)),
    )(row(log_dt, 1), row(lam_re, p), row(lam_im, p),
      jnp.swapaxes(b_re, 1, 2), jnp.swapaxes(b_im, 1, 2), c_re, c_im, row(d_skip, ch), w_in)


def _s5_scan_kernel(u_ref, tg_ref, bn_ref, cn_ref, are_ref, aim_ref, y_ref,
                    uf_ref, v_ref, yt_ref, sre_ref, sim_ref, xre_ref, xim_ref, *, n_batch):
    lc, ch = SSM_CHUNK, SSM_GROUP_CH
    gb = tg_ref.shape[0]
    n_rows = u_ref.shape[0] // lc
    n_chunks = n_rows // n_batch
    w = lc * ch

    uf_ref[...] = u_ref[...].astype(F32)
    for s in range(lc):
        ust = uf_ref[pl.ds(s, n_rows, stride=lc), :].T
        for k in range(gb):
            v_ref[k, s * ch:(s + 1) * ch, :] = ust[k * ch:(k + 1) * ch, :].astype(v_ref.dtype)

    for k in range(gb):
        u = v_ref[k]
        rows = pl.ds(k, n_rows, stride=gb)
        sre_ref[rows, :] = jnp.dot(bn_ref[k, 0], u, preferred_element_type=F32).T
        sim_ref[rows, :] = jnp.dot(bn_ref[k, 1], u, preferred_element_type=F32).T

    a_re = are_ref[...]
    a_im = aim_ref[...]

    def step(c, carry):
        new = []
        for b in range(n_batch):
            x_re, x_im = carry[2 * b], carry[2 * b + 1]
            rows = pl.ds(pl.multiple_of((b * n_chunks + c) * gb, gb), gb)
            xre_ref[rows, :] = x_re
            xim_ref[rows, :] = x_im
            new.append(a_re * x_re - a_im * x_im + sre_ref[rows, :])
            new.append(a_re * x_im + a_im * x_re + sim_ref[rows, :])
        return tuple(new)

    zero = jnp.zeros(a_re.shape, F32)
    lax.fori_loop(0, n_chunks, step, (zero,) * (2 * n_batch), unroll=SCAN_UNROLL)

    for k in range(gb):
        rows = pl.ds(k, n_rows, stride=gb)
        y = jnp.dot(tg_ref[k], v_ref[k], preferred_element_type=F32)
        y = y + lax.dot_general(cn_ref[k, 0], xre_ref[rows, :].astype(BF16), _NT_DIMS,
                                preferred_element_type=F32)
        y = y + lax.dot_general(cn_ref[k, 1], xim_ref[rows, :].astype(BF16), _NT_DIMS,
                                preferred_element_type=F32)
        for s in range(lc):
            yt_ref[s, k * ch:(k + 1) * ch, :] = y[s * ch:(s + 1) * ch, :]

    for s in range(lc):
        y_ref[pl.ds(s, n_rows, stride=lc), :] = yt_ref[s].T.astype(y_ref.dtype)


def _s5_scan(ku, tg, bn, cn, a_re, a_im, n_batch, u_col0, gb=8):
    t = ku.shape[0]
    g, w, _ = tg.shape
    p, ch = SSM_STATE, SSM_GROUP_CH
    n_rows = t // SSM_CHUNK
    lanes = gb * ch
    kernel = functools.partial(_s5_scan_kernel, n_batch=n_batch)
    return pl.pallas_call(
        kernel,
        grid=(g // gb,),
        in_specs=[pl.BlockSpec((t, lanes), lambda i: (0, u_col0 // lanes + i)),
                  pl.BlockSpec((gb, w, w), lambda i: (i, 0, 0)),
                  pl.BlockSpec((gb, 2, p, w), lambda i: (i, 0, 0, 0)),
                  pl.BlockSpec((gb, 2, w, p), lambda i: (i, 0, 0, 0)),
                  pl.BlockSpec((gb, p), lambda i: (i, 0)),
                  pl.BlockSpec((gb, p), lambda i: (i, 0))],
        out_specs=pl.BlockSpec((t, lanes), lambda i: (0, i)),
        out_shape=jax.ShapeDtypeStruct((t, g * ch), F32),
        scratch_shapes=[pltpu.VMEM((t, lanes), F32),
                        pltpu.VMEM((gb, w, n_rows), BF16),
                        pltpu.VMEM((SSM_CHUNK, lanes, n_rows), F32)]
                       + [pltpu.VMEM((gb * n_rows, p), F32) for _ in range(4)],
        compiler_params=pltpu.CompilerParams(dimension_semantics=("parallel",),
                                             vmem_limit_bytes=VMEM_LIMIT),
    )(ku, tg, bn, cn, a_re.reshape(g, p), a_im.reshape(g, p))


def _out_proj_kernel(x_ref, a_ref, y_ref, wglu_ref, bglu_ref, ga_ref, gs_ref, wa_ref, ws_ref, o_ref):
    z = jax.nn.gelu(y_ref[...])
    gl = jnp.dot(z.astype(BF16), wglu_ref[...], preferred_element_type=F32) + bglu_ref[...]
    s = z * jax.nn.sigmoid(gl)
    a_n = _rms_rows(a_ref[...].astype(F32), ga_ref[...]).astype(BF16)
    s_n = _rms_rows(s, gs_ref[...]).astype(BF16)
    mix = (jnp.dot(a_n, wa_ref[...], preferred_element_type=F32)
           + jnp.dot(s_n, ws_ref[...], preferred_element_type=F32))
    o_ref[...] = x_ref[...] + mix


def _out_proj(x2, attn, y, w_glu, b_glu, g_a, g_s, w_out, tm=512):
    t, d = x2.shape
    wa = attn.shape[1]
    ws = y.shape[1]
    row = lambda i: (i, 0)
    fixed = lambda i: (0, 0)
    once = pl.Buffered(1)
    return pl.pallas_call(
        _out_proj_kernel,
        grid=(t // tm,),
        in_specs=[pl.BlockSpec((tm, d), row),
                  pl.BlockSpec((tm, wa), row),
                  pl.BlockSpec((tm, ws), row),
                  pl.BlockSpec((ws, ws), fixed, pipeline_mode=once),
                  pl.BlockSpec((1, ws), fixed),
                  pl.BlockSpec((1, wa), fixed),
                  pl.BlockSpec((1, ws), fixed),
                  pl.BlockSpec((wa, d), fixed, pipeline_mode=once),
                  pl.BlockSpec((ws, d), lambda i: (wa // ws, 0), pipeline_mode=once)],
        out_specs=pl.BlockSpec((tm, d), row),
        out_shape=jax.ShapeDtypeStruct((t, d), F32),
        compiler_params=pltpu.CompilerParams(dimension_semantics=("parallel",),
                                             vmem_limit_bytes=VMEM_LIMIT),
    )(x2, attn, y, w_glu, b_glu.reshape(1, ws), g_a.reshape(1, wa), g_s.reshape(1, ws),
      w_out, w_out)


def _ffn_up_kernel(x_ref, halo_ref, g_ref, wg_ref, wv_ref, cwg_ref, cwv_ref, cbg_ref, cbv_ref,
                   wd_ref, o_ref, wdq_ref, h_ref, *, tiles_per_seq):
    tm = x_ref.shape[0]
    wdq_ref[...] = wd_ref[...].astype(wdq_ref.dtype)

    @pl.when(pl.program_id(1) == 0)
    def _():
        keep = jnp.where(pl.program_id(0) % tiles_per_seq == 0, 0.0, 1.0)
        h_ref[pl.ds(0, CONV_HALO), :] = (_rms_rows(halo_ref[...], g_ref[...]) * keep).astype(BF16)
        h_ref[pl.ds(CONV_HALO, tm), :] = _rms_rows(x_ref[...], g_ref[...]).astype(BF16)

    h = h_ref[...]

    def conv(w_ref, cw_ref, cb_ref):
        up = jnp.dot(h, w_ref[...].astype(BF16), preferred_element_type=F32)
        cw = cw_ref[...]
        out = up[CONV_HALO:] * cw[CONV_WIDTH - 1:CONV_WIDTH] + cb_ref[...]
        for j in range(CONV_WIDTH - 1):
            lag = CONV_WIDTH - 1 - j
            out = out + up[CONV_HALO - lag:CONV_HALO - lag + tm] * cw[j:j + 1]
        return out

    gate = conv(wg_ref, cwg_ref, cbg_ref)
    val = conv(wv_ref, cwv_ref, cbv_ref)
    o_ref[...] = (jax.nn.silu(gate) * val).astype(o_ref.dtype)


def _ffn_up(x1, g, w_up, conv_w, conv_b, w_down, seq, tm=1024, tn=512):
    t, d = x1.shape
    f = w_up.shape[1] // 2
    nf = f // tn
    tiles_per_seq = seq // tm
    halo_blocks = tm // CONV_HALO
    kernel = functools.partial(_ffn_up_kernel, tiles_per_seq=tiles_per_seq)
    cb = conv_b.reshape(1, 2 * f)
    wd_slab = pl.BlockSpec((w_down.shape[0] // (t // tm * nf), w_down.shape[1]),
                           lambda i, j: (i * nf + j, 0))
    return pl.pallas_call(
        kernel,
        grid=(t // tm, nf),
        in_specs=[pl.BlockSpec((tm, d), lambda i, j: (i, 0)),
                  pl.BlockSpec((CONV_HALO, d), lambda i, j: (jnp.maximum(i * halo_blocks - 1, 0), 0)),
                  pl.BlockSpec((1, d), lambda i, j: (0, 0)),
                  pl.BlockSpec((d, tn), lambda i, j: (0, j)),
                  pl.BlockSpec((d, tn), lambda i, j: (0, nf + j)),
                  pl.BlockSpec((CONV_WIDTH, tn), lambda i, j: (0, j)),
                  pl.BlockSpec((CONV_WIDTH, tn), lambda i, j: (0, nf + j)),
                  pl.BlockSpec((1, tn), lambda i, j: (0, j)),
                  pl.BlockSpec((1, tn), lambda i, j: (0, nf + j)),
                  wd_slab],
        out_specs=[pl.BlockSpec((tm, tn), lambda i, j: (i, j)), wd_slab],
        out_shape=[jax.ShapeDtypeStruct((t, f), BF16), jax.ShapeDtypeStruct(w_down.shape, BF16)],
        scratch_shapes=[pltpu.VMEM((CONV_HALO + tm, d), BF16)],
        compiler_params=pltpu.CompilerParams(dimension_semantics=("parallel", "arbitrary"),
                                             vmem_limit_bytes=VMEM_LIMIT),
    )(x1, x1, g.reshape(1, d), w_up, w_up, conv_w, conv_w, cb, cb, w_down)


def _ffn_down_kernel(a_ref, w_ref, x_ref, g_ref, o_ref, *, final_norm):
    y = x_ref[...] + jnp.dot(a_ref[...], w_ref[...], preferred_element_type=F32)
    o_ref[...] = _rms_rows(y, g_ref[...]) if final_norm else y


def _ffn_down(act, w_down, x1, g, final_norm, tm=256):
    t, f = act.shape
    d = w_down.shape[1]
    return pl.pallas_call(
        functools.partial(_ffn_down_kernel, final_norm=final_norm),
        grid=(t // tm,),
        in_specs=[pl.BlockSpec((tm, f), lambda i: (i, 0)),
                  pl.BlockSpec((f, d), lambda i: (0, 0), pipeline_mode=pl.Buffered(1)),
                  pl.BlockSpec((tm, d), lambda i: (i, 0)),
                  pl.BlockSpec((1, d), lambda i: (0, 0))],
        out_specs=pl.BlockSpec((tm, d), lambda i: (i, 0)),
        out_shape=jax.ShapeDtypeStruct((t, d), F32),
        compiler_params=pltpu.CompilerParams(dimension_semantics=("parallel",),
                                             vmem_limit_bytes=VMEM_LIMIT),
    )(act, w_down, x1, g.reshape(1, d))


def kernel(x, norm_mix, w_in, rel_bias_table, ssm_lam_re, ssm_lam_im, ssm_log_dt, ssm_b_re, ssm_b_im, ssm_c_re, ssm_c_im, ssm_d, ssm_w_glu, ssm_b_glu, norm_attn_out, norm_ssm_out, w_out, norm_ffn, w_ffn_up, ffn_conv_w, ffn_conv_b, w_ffn_down, norm_final):
    batch, seq, d_model = x.shape
    depth = w_in.shape[0]
    aw = ATTN_HEADS * HEAD_DIM
    sw = SSM_GROUPS * SSM_GROUP_CH
    t = batch * seq
    n_blocks = seq // MOBA_BLOCK
    assert seq % MOBA_BLOCK == 0 and seq % SSM_CHUNK == 0

    bias_tiles = _bias_tiles(rel_bias_table)
    x2 = x.reshape(t, d_model)
    for l in range(depth):
        tg, bn, cn, a_re, a_im, w_in_b = _s5_tables(
            ssm_log_dt[l], ssm_lam_re[l], ssm_lam_im[l], ssm_b_re[l], ssm_b_im[l], ssm_c_re[l],
            ssm_c_im[l], ssm_d[l], w_in[l], q_cols=aw, q_scale=HEAD_DIM ** -0.5 * LOG2E)

        proj, (w_out_b, w_glu_b) = _in_proj(x2, norm_mix[l], w_in_b, (w_out[l], ssm_w_glu[l]))

        attn = _moba_attention(proj, bias_tiles, batch, seq)
        y = _s5_scan(proj, tg, bn, cn, a_re, a_im, batch, u_col0=3 * aw)

        x2 = _out_proj(x2, attn, y, w_glu_b, ssm_b_glu[l], norm_attn_out[l], norm_ssm_out[l], w_out_b)

        act, w_down_b = _ffn_up(x2, norm_ffn[l], w_ffn_up[l], ffn_conv_w[l], ffn_conv_b[l],
                                w_ffn_down[l], seq)
        x2 = _ffn_down(act, w_down_b, x2, norm_final, final_norm=(l == depth - 1))
    return x2.reshape(batch, seq, d_model)
```

```python
import functools
import math

import jax
import jax.numpy as jnp
from jax import lax
from jax.experimental import pallas as pl
from jax.experimental.pallas import tpu as pltpu

F32 = jnp.float32
BF16 = jnp.bfloat16

ATTN_HEADS = 16
HEAD_DIM = 64
SSM_GROUP_CH = 16
SSM_GROUPS = 64
SSM_STATE = 64
MOBA_BLOCK = 256
MOBA_TOP_K = 3
NUM_BUCKETS = 32
MAX_DISTANCE = 128
CONV_WIDTH = 3
RMS_EPS = 1e-6

SSM_CHUNK = 16
MOBA_HEADS_PER_STEP = 8
MXU_DEPTH = 256
SCAN_UNROLL = 8
CONV_HALO = 16
VMEM_LIMIT = 52 * 1024 * 1024

NEG_INF = float("-inf")
LOG2E = math.log2(math.e)
V_AUG_ROWS = HEAD_DIM + 16


def _rms_rows(x, g):
    ms = jnp.mean(x * x, axis=-1, keepdims=True)
    return x * lax.rsqrt(ms + RMS_EPS) * g


_NT_DIMS = (((1,), (1,)), ((), ()))


def _in_proj_kernel(x_ref, g_ref, w_ref, *rest):
    n_extra = (len(rest) - 2) // 2
    extra_in, o_ref = rest[:n_extra], rest[n_extra]
    extra_out, h_ref = rest[n_extra + 1:2 * n_extra + 1], rest[-1]

    @pl.when(pl.program_id(1) == 0)
    def _():
        h_ref[...] = _rms_rows(x_ref[...], g_ref[...]).astype(BF16)

    o_ref[...] = jnp.dot(h_ref[...], w_ref[...], preferred_element_type=F32).astype(o_ref.dtype)
    for src, dst in zip(extra_in, extra_out):
        dst[...] = src[...].astype(dst.dtype)


def _in_proj(x2, g, w, cast_along, tm=1024, tn=1024):
    t, d = x2.shape
    n = w.shape[1]
    n_i, n_j = t // tm, n // tn
    steps = n_i * n_j
    slabs = [pl.BlockSpec((a.shape[0] // steps, a.shape[1]), lambda i, j: (i * n_j + j, 0))
             for a in cast_along]
    outs = pl.pallas_call(
        _in_proj_kernel,
        grid=(n_i, n_j),
        in_specs=[pl.BlockSpec((tm, d), lambda i, j: (i, 0)),
                  pl.BlockSpec((1, d), lambda i, j: (0, 0)),
                  pl.BlockSpec((d, tn), lambda i, j: (0, j))] + slabs,
        out_specs=[pl.BlockSpec((tm, tn), lambda i, j: (i, j))] + slabs,
        out_shape=[jax.ShapeDtypeStruct((t, n), BF16)]
                  + [jax.ShapeDtypeStruct(a.shape, BF16) for a in cast_along],
        scratch_shapes=[pltpu.VMEM((tm, d), BF16)],
        compiler_params=pltpu.CompilerParams(dimension_semantics=("parallel", "arbitrary"),
                                             vmem_limit_bytes=VMEM_LIMIT),
    )(x2, g.reshape(1, d), w, *cast_along)
    return outs[0], outs[1:]


def _t5_bucket(dist):
    dist = jnp.maximum(dist, 0)
    max_exact = NUM_BUCKETS // 2
    log_ratio = jnp.log(jnp.maximum(dist, max_exact).astype(F32) / max_exact)
    large = max_exact + (log_ratio / math.log(MAX_DISTANCE / max_exact)
                         * (NUM_BUCKETS - max_exact)).astype(jnp.int32)
    large = jnp.minimum(large, NUM_BUCKETS - 1)
    return jnp.where(dist < max_exact, dist, large)


def _bias_tiles_kernel(tab_ref, o_ref):
    blk = MOBA_BLOCK
    dist = lax.broadcasted_iota(jnp.int32, (1, 2 * blk), 1)
    bucket = _t5_bucket(dist)
    kk = lax.broadcasted_iota(jnp.int32, (blk, blk), 0)
    qq = lax.broadcasted_iota(jnp.int32, (blk, blk), 1)
    for i in range(o_ref.shape[0]):
        h = pl.program_id(0) * o_ref.shape[0] + i
        row = jnp.zeros((1, 2 * blk), F32)
        for b in range(NUM_BUCKETS):
            row = jnp.where(bucket == b, tab_ref[b, h], row)
        row = (row - tab_ref[NUM_BUCKETS - 1, h]) * LOG2E
        pair = pltpu.roll(jnp.broadcast_to(row, (blk, 2 * blk)), 0, 1, stride=1, stride_axis=0)
        o_ref[i, 0] = pair[:, blk:]
        o_ref[i, 1] = jnp.where(qq >= kk, pair[:, :blk], NEG_INF)


def _bias_tiles(table, heads_per_step=4):
    return pl.pallas_call(
        _bias_tiles_kernel,
        grid=(ATTN_HEADS // heads_per_step,),
        in_specs=[pl.BlockSpec(memory_space=pltpu.SMEM)],
        out_specs=pl.BlockSpec((heads_per_step, 2, MOBA_BLOCK, MOBA_BLOCK), lambda h: (h, 0, 0, 0)),
        out_shape=jax.ShapeDtypeStruct((ATTN_HEADS, 2, MOBA_BLOCK, MOBA_BLOCK), F32),
    )(table)


def _moba_kernel(q_ref, k_ref, v_ref, bias_ref, o_ref,
                 kmf_ref, vaug_ref, qm_ref, sa_ref, sb_ref, cma_ref, cmb_ref,
                 m_ref, acc_ref, rb_ref, ot_ref, *, n_blocks):
    qi = pl.program_id(2)
    blk = MOBA_BLOCK
    nh = qm_ref.shape[0]

    @pl.when(qi == 0)
    def _():
        ones = jnp.ones((V_AUG_ROWS - HEAD_DIM, blk), BF16)
        for j in range(n_blocks):
            kb = k_ref[j * blk:(j + 1) * blk, :].astype(F32)
            kmf_ref[pl.ds(j, 1), :] = jnp.sum(kb, axis=0, keepdims=True) * (1.0 / blk)
            vt = v_ref[j].T
            for hh in range(nh):
                vaug_ref[j, hh, 0:HEAD_DIM, :] = vt[hh * HEAD_DIM:(hh + 1) * HEAD_DIM, :]
                vaug_ref[j, hh, HEAD_DIM:V_AUG_ROWS, :] = ones

    kmf = kmf_ref[...]
    km_hi = kmf.astype(BF16)
    km_lo = (kmf - km_hi.astype(F32)).astype(BF16)

    q2 = q_ref[0].T
    q_zero = jnp.zeros((HEAD_DIM, blk), q2.dtype)
    blk_idx = lax.broadcasted_iota(jnp.int32, (n_blocks, blk), 0)
    blk_idx_f = blk_idx.astype(F32)
    valid = blk_idx < qi
    hps = qm_ref.shape[1] // HEAD_DIM
    slab = lambda hh: slice((hh // hps) * hps * HEAD_DIM, (hh // hps + 1) * hps * HEAD_DIM)

    for hh in range(nh):
        vrows = slice(hh * HEAD_DIM, (hh + 1) * HEAD_DIM)
        qm = jnp.concatenate([q_zero] * (hh % hps) + [q2[vrows]] + [q_zero] * (hps - 1 - hh % hps),
                             axis=0)
        qm_ref[hh] = qm

        gate = (jnp.dot(km_hi[:, slab(hh)], qm, preferred_element_type=F32)
                + jnp.dot(km_lo[:, slab(hh)], qm, preferred_element_type=F32))
        g = jnp.where(valid, gate, NEG_INF)
        sel = blk_idx == qi
        for _ in range(MOBA_TOP_K):
            top = jnp.max(g, axis=0, keepdims=True)
            first = jnp.min(jnp.where(g == top, blk_idx_f, float(n_blocks)), axis=0, keepdims=True)
            hit = blk_idx_f == first
            sel = sel | (hit & valid)
            g = jnp.where(hit, NEG_INF, g)
        rb_ref[hh] = jnp.where(sel, 0.0, NEG_INF)
        m_ref[hh] = jnp.full((1, blk), NEG_INF, F32)
        acc_ref[hh] = jnp.zeros((V_AUG_ROWS, blk), F32)

    def group_rows(g):
        ja = qi - 2 * g
        return ja, jnp.maximum(ja - 1, 0)

    def scores(hh, g, s_buf, cm_buf, first_group):
        _, lo = group_rows(g)
        kslab = k_ref[pl.ds(pl.multiple_of(lo * blk, blk), 2 * blk), slab(hh)]
        s = jnp.dot(kslab, qm_ref[hh], preferred_element_type=F32)
        for i in range(2):
            si = s[i * blk:(i + 1) * blk]
            if first_group:
                tile = jnp.where(qi == 0, 1, 0) if i == 0 else 1
                si = si + bias_ref[hh, tile]
            s_buf[hh, i] = si
            cm_buf[hh, pl.ds(i, 1), :] = jnp.max(si, axis=0, keepdims=True)

    def attend(hh, g, s_buf, cm_buf):
        ja, lo = group_rows(g)
        masks = [rb_ref[hh, pl.ds(lo, 1), :],
                 jnp.where(ja >= 1, rb_ref[hh, pl.ds(lo + 1, 1), :], NEG_INF)]
        m_old = m_ref[hh]
        m_new = m_old
        for i in range(2):
            m_new = jnp.maximum(m_new, cm_buf[hh, pl.ds(i, 1), :] + masks[i])
        acc = jnp.exp2(m_old - m_new) * acc_ref[hh]
        for i in range(2):
            p = jnp.exp2((s_buf[hh, i] - m_new).astype(BF16))
            pv = jnp.dot(vaug_ref[lo + i, hh], p, preferred_element_type=F32)
            acc = acc + jnp.where(masks[i] == 0.0, pv, 0.0)
        acc_ref[hh] = acc
        m_ref[hh] = m_new

    n_more = qi // 2
    for hh in range(nh):
        scores(hh, 0, sa_ref, cma_ref, True)

    def stage(it, cur, nxt):
        lead = 1
        for hh in range(lead):
            scores(hh, it + 1, *nxt, False)
        for hh in range(nh):
            if hh + lead < nh:
                scores(hh + lead, it + 1, *nxt, False)
            attend(hh, it, *cur)

    buf_a, buf_b = (sa_ref, cma_ref), (sb_ref, cmb_ref)

    def step(it, carry):
        @pl.when(it % 2 == 0)
        def _():
            stage(it, buf_a, buf_b)

        @pl.when(it % 2 == 1)
        def _():
            stage(it, buf_b, buf_a)

        return carry

    lax.fori_loop(0, n_more, step, 0)

    @pl.when(n_more % 2 == 0)
    def _():
        for hh in range(nh):
            attend(hh, n_more, *buf_a)

    @pl.when(n_more % 2 == 1)
    def _():
        for hh in range(nh):
            attend(hh, n_more, *buf_b)

    for hh in range(nh):
        acc = acc_ref[hh]
        ot_ref[hh * HEAD_DIM:(hh + 1) * HEAD_DIM, :] = (
            acc[0:HEAD_DIM] * (1.0 / acc[HEAD_DIM:HEAD_DIM + 1]))

    o_ref[...] = ot_ref[...].T.astype(o_ref.dtype)


def _moba_attention(proj, bias_tiles, batch, seq):
    proj3 = proj.reshape(proj.shape[0] // MOBA_BLOCK, MOBA_BLOCK, proj.shape[1])
    n_blocks = seq // MOBA_BLOCK
    nh = MOBA_HEADS_PER_STEP
    n_hg = ATTN_HEADS // nh
    hg_w = nh * HEAD_DIM
    kernel = functools.partial(_moba_kernel, n_blocks=n_blocks)
    return pl.pallas_call(
        kernel,
        grid=(batch, n_hg, n_blocks),
        in_specs=[
            pl.BlockSpec((1, MOBA_BLOCK, hg_w), lambda b, hg, qi: (b * n_blocks + qi, 0, hg)),
            pl.BlockSpec((seq, hg_w), lambda b, hg, qi: (b, n_hg + hg)),
            pl.BlockSpec((n_blocks, MOBA_BLOCK, hg_w), lambda b, hg, qi: (b, 0, 2 * n_hg + hg)),
            pl.BlockSpec((nh, 2, MOBA_BLOCK, MOBA_BLOCK), lambda b, hg, qi: (hg, 0, 0, 0)),
        ],
        out_specs=pl.BlockSpec((MOBA_BLOCK, hg_w), lambda b, hg, qi: (b * n_blocks + qi, hg)),
        out_shape=jax.ShapeDtypeStruct((batch * seq, ATTN_HEADS * HEAD_DIM), BF16),
        scratch_shapes=[
            pltpu.VMEM((n_blocks, hg_w), F32),
            pltpu.VMEM((n_blocks, nh, V_AUG_ROWS, MOBA_BLOCK), BF16),
            pltpu.VMEM((nh, min(hg_w, MXU_DEPTH), MOBA_BLOCK), BF16),
            pltpu.VMEM((nh, 2, MOBA_BLOCK, MOBA_BLOCK), F32),
            pltpu.VMEM((nh, 2, MOBA_BLOCK, MOBA_BLOCK), F32),
            pltpu.VMEM((nh, 8, MOBA_BLOCK), F32),
            pltpu.VMEM((nh, 8, MOBA_BLOCK), F32),
            pltpu.VMEM((nh, 1, MOBA_BLOCK), F32),
            pltpu.VMEM((nh, V_AUG_ROWS, MOBA_BLOCK), F32),
            pltpu.VMEM((nh, n_blocks, MOBA_BLOCK), F32),
            pltpu.VMEM((hg_w, MOBA_BLOCK), F32),
        ],
        compiler_params=pltpu.CompilerParams(
            dimension_semantics=("parallel", "parallel", "arbitrary"),
            vmem_limit_bytes=VMEM_LIMIT),
    )(proj3, proj, proj3, bias_tiles)


def _hdot_nt(a, b):
    def split(x):
        hi = x.astype(BF16)
        return hi, (x - hi.astype(F32)).astype(BF16)

    def nt(u, v):
        return lax.dot_general(u, v, _NT_DIMS, preferred_element_type=F32)

    a_hi, a_lo = split(a)
    b_hi, b_lo = split(b)
    return nt(a_hi, b_hi) + nt(a_hi, b_lo) + nt(a_lo, b_hi)


def _repeat_rows(x, n):
    return jnp.concatenate([jnp.broadcast_to(x[i:i + 1], (n, x.shape[1]))
                            for i in range(x.shape[0])], axis=0)


def _tile_rows(x, n):
    return jnp.concatenate([x] * n, axis=0)


def _s5_tables_kernel(logdt_ref, lre_ref, lim_ref, bre_ref, bim_ref, cre_ref, cim_ref, d_ref,
                      win_ref, tg_ref, bn_ref, cn_ref, are_ref, aim_ref, wq_ref, *, q_cols, q_scale):
    wq_ref[:, :q_cols] = (win_ref[:, :q_cols] * q_scale).astype(wq_ref.dtype)
    wq_ref[:, q_cols:] = win_ref[:, q_cols:].astype(wq_ref.dtype)

    lc, ch, p = SSM_CHUNK, SSM_GROUP_CH, SSM_STATE
    w = lc * ch
    rr = lax.broadcasted_iota(jnp.int32, (w, w), 0)
    cc = lax.broadcasted_iota(jnp.int32, (w, w), 1)
    causal = rr // ch >= cc // ch
    diag = rr == cc
    e_r = lax.broadcasted_iota(jnp.int32, (ch, w), 0)
    e_c = lax.broadcasted_iota(jnp.int32, (ch, w), 1)
    lane_tile = jnp.where(e_c % ch == e_r, 1.0, 0.0).astype(F32)
    tau = lax.broadcasted_iota(jnp.int32, (2 * lc, p), 0).astype(F32)

    for k in range(tg_ref.shape[0]):
        lam_re = lre_ref[k]
        lam_im = lim_ref[k]
        dt = jnp.exp(logdt_ref[k])
        lr = lam_re * dt
        li = lam_im * dt

        mag = jnp.exp(lr * tau)
        cs = jnp.cos(li * tau)
        sn = jnp.sin(li * tau)
        pos_re, pos_im = mag * cs, mag * sn
        inv = jnp.exp(-lr * tau[:lc])
        neg_re, neg_im = inv * cs[:lc], -inv * sn[:lc]

        lam1_re, lam1_im = pos_re[1:2], pos_im[1:2]
        lamk_re, lamk_im = pos_re[lc - 1:lc], pos_im[lc - 1:lc]
        are_ref[k] = pos_re[lc:lc + 1]
        aim_ref[k] = pos_im[lc:lc + 1]

        num_re, num_im = lam1_re - 1.0, lam1_im
        den = lam_re * lam_re + lam_im * lam_im
        coef_re = (num_re * lam_re + num_im * lam_im) / den
        coef_im = (num_im * lam_re - num_re * lam_im) / den
        bt_re, bt_im = bre_ref[k], bim_ref[k]
        bb_re = coef_re * bt_re - coef_im * bt_im
        bb_im = coef_re * bt_im + coef_im * bt_re

        bbt_re, bbt_im = _tile_rows(bb_re, lc), _tile_rows(bb_im, lc)
        ngx_re, ngx_im = _repeat_rows(neg_re, ch), _repeat_rows(neg_im, ch)
        bneg_re = bbt_re * ngx_re - bbt_im * ngx_im
        bneg_im = bbt_re * ngx_im + bbt_im * ngx_re

        ct_re, ct_im = _tile_rows(cre_ref[k], lc), _tile_rows(cim_ref[k], lc)
        psx_re, psx_im = _repeat_rows(pos_re[:lc], ch), _repeat_rows(pos_im[:lc], ch)
        cpos_re = ct_re * psx_re - ct_im * psx_im
        cpos_im = ct_re * psx_im + ct_im * psx_re

        raw = _hdot_nt(cpos_re, bneg_re) - _hdot_nt(cpos_im, bneg_im)
        d_lanes = jnp.dot(jnp.broadcast_to(d_ref[k], (8, ch)), lane_tile,
                          preferred_element_type=F32, precision=lax.Precision.HIGHEST)[0:1]
        tg = jnp.where(causal, raw, 0.0) + jnp.where(diag, d_lanes, 0.0)
        tg_ref[k] = tg.astype(tg_ref.dtype)

        bn_ref[k, 0] = (bneg_re * lamk_re - bneg_im * lamk_im).T.astype(bn_ref.dtype)
        bn_ref[k, 1] = (bneg_re * lamk_im + bneg_im * lamk_re).T.astype(bn_ref.dtype)
        cp_re = cpos_re * lam1_re - cpos_im * lam1_im
        cp_im = cpos_re * lam1_im + cpos_im * lam1_re
        cn_ref[k, 0] = cp_re.astype(cn_ref.dtype)
        cn_ref[k, 1] = (-cp_im).astype(cn_ref.dtype)


def _s5_tables(log_dt, lam_re, lam_im, b_re, b_im, c_re, c_im, d_skip, w_in, q_cols, q_scale, gb=8):
    g, p, ch = SSM_GROUPS, SSM_STATE, SSM_GROUP_CH
    w = SSM_CHUNK * ch
    steps = g // gb
    d, n = w_in.shape
    row = lambda a, n: a.reshape(g, 1, n)
    spec3 = lambda s1, s2: pl.BlockSpec((gb, s1, s2), lambda i: (i, 0, 0))
    spec4 = lambda s1, s2: pl.BlockSpec((gb, 2, s1, s2), lambda i: (i, 0, 0, 0))
    slab = pl.BlockSpec((d // steps, n), lambda i: (i, 0))
    return pl.pallas_call(
        functools.partial(_s5_tables_kernel, q_cols=q_cols, q_scale=q_scale),
        grid=(steps,),
        in_specs=[spec3(1, 1), spec3(1, p), spec3(1, p), spec3(ch, p), spec3(ch, p),
                  spec3(ch, p), spec3(ch, p), spec3(1, ch), slab],
        out_specs=[spec3(w, w), spec4(p, w), spec4(w, p), spec3(1, p), spec3(1, p), slab],
        out_shape=[jax.ShapeDtypeStruct((g, w, w), BF16),
                   jax.ShapeDtypeStruct((g, 2, p, w), BF16),
                   jax.ShapeDtypeStruct((g, 2, w, p), BF16),
                   jax.ShapeDtypeStruct((g, 1, p), F32),
                   jax.ShapeDtypeStruct((g, 1, p), F32),
                   jax.ShapeDtypeStruct((d, n), BF16)],
        compiler_params=pltpu.CompilerParams(dimension_semantics=("parallel",)),
    )(row(log_dt, 1), row(lam_re, p), row(lam_im, p),
      jnp.swapaxes(b_re, 1, 2), jnp.swapaxes(b_im, 1, 2), c_re, c_im, row(d_skip, ch), w_in)


def _s5_scan_kernel(u_ref, tg_ref, bn_ref, cn_ref, are_ref, aim_ref, y_ref,
                    uf_ref, v_ref, yt_ref, sre_ref, sim_ref, xre_ref, xim_ref, *, n_batch):
    lc, ch = SSM_CHUNK, SSM_GROUP_CH
    gb = tg_ref.shape[0]
    n_rows = u_ref.shape[0] // lc
    n_chunks = n_rows // n_batch
    w = lc * ch

    uf_ref[...] = u_ref[...].astype(F32)
    for s in range(lc):
        ust = uf_ref[pl.ds(s, n_rows, stride=lc), :].T
        for k in range(gb):
            v_ref[k, s * ch:(s + 1) * ch, :] = ust[k * ch:(k + 1) * ch, :].astype(v_ref.dtype)

    for k in range(gb):
        u = v_ref[k]
        rows = pl.ds(k, n_rows, stride=gb)
        sre_ref[rows, :] = jnp.dot(bn_ref[k, 0], u, preferred_element_type=F32).T
        sim_ref[rows, :] = jnp.dot(bn_ref[k, 1], u, preferred_element_type=F32).T

    a_re = are_ref[...]
    a_im = aim_ref[...]

    def step(c, carry):
        new = []
        for b in range(n_batch):
            x_re, x_im = carry[2 * b], carry[2 * b + 1]
            rows = pl.ds(pl.multiple_of((b * n_chunks + c) * gb, gb), gb)
            xre_ref[rows, :] = x_re
            xim_ref[rows, :] = x_im
            new.append(a_re * x_re - a_im * x_im + sre_ref[rows, :])
            new.append(a_re * x_im + a_im * x_re + sim_ref[rows, :])
        return tuple(new)

    zero = jnp.zeros(a_re.shape, F32)
    lax.fori_loop(0, n_chunks, step, (zero,) * (2 * n_batch), unroll=SCAN_UNROLL)

    for k in range(gb):
        rows = pl.ds(k, n_rows, stride=gb)
        y = jnp.dot(tg_ref[k], v_ref[k], preferred_element_type=F32)
        y = y + lax.dot_general(cn_ref[k, 0], xre_ref[rows, :].astype(BF16), _NT_DIMS,
                                preferred_element_type=F32)
        y = y + lax.dot_general(cn_ref[k, 1], xim_ref[rows, :].astype(BF16), _NT_DIMS,
                                preferred_element_type=F32)
        for s in range(lc):
            yt_ref[s, k * ch:(k + 1) * ch, :] = y[s * ch:(s + 1) * ch, :]

    for s in range(lc):
        y_ref[pl.ds(s, n_rows, stride=lc), :] = yt_ref[s].T.astype(y_ref.dtype)


def _s5_scan(ku, tg, bn, cn, a_re, a_im, n_batch, u_col0, gb=8):
    t = ku.shape[0]
    g, w, _ = tg.shape
    p, ch = SSM_STATE, SSM_GROUP_CH
    n_rows = t // SSM_CHUNK
    lanes = gb * ch
    kernel = functools.partial(_s5_scan_kernel, n_batch=n_batch)
    return pl.pallas_call(
        kernel,
        grid=(g // gb,),
        in_specs=[pl.BlockSpec((t, lanes), lambda i: (0, u_col0 // lanes + i)),
                  pl.BlockSpec((gb, w, w), lambda i: (i, 0, 0)),
                  pl.BlockSpec((gb, 2, p, w), lambda i: (i, 0, 0, 0)),
                  pl.BlockSpec((gb, 2, w, p), lambda i: (i, 0, 0, 0)),
                  pl.BlockSpec((gb, p), lambda i: (i, 0)),
                  pl.BlockSpec((gb, p), lambda i: (i, 0))],
        out_specs=pl.BlockSpec((t, lanes), lambda i: (0, i)),
        out_shape=jax.ShapeDtypeStruct((t, g * ch), F32),
        scratch_shapes=[pltpu.VMEM((t, lanes), F32),
                        pltpu.VMEM((gb, w, n_rows), BF16),
                        pltpu.VMEM((SSM_CHUNK, lanes, n_rows), F32)]
                       + [pltpu.VMEM((gb * n_rows, p), F32) for _ in range(4)],
        compiler_params=pltpu.CompilerParams(dimension_semantics=("parallel",),
                                             vmem_limit_bytes=VMEM_LIMIT),
    )(ku, tg, bn, cn, a_re.reshape(g, p), a_im.reshape(g, p))


def _out_proj_kernel(x_ref, a_ref, y_ref, wglu_ref, bglu_ref, ga_ref, gs_ref, wa_ref, ws_ref, o_ref):
    z = jax.nn.gelu(y_ref[...])
    gl = jnp.dot(z.astype(BF16), wglu_ref[...], preferred_element_type=F32) + bglu_ref[...]
    s = z * jax.nn.sigmoid(gl)
    a_n = _rms_rows(a_ref[...].astype(F32), ga_ref[...]).astype(BF16)
    s_n = _rms_rows(s, gs_ref[...]).astype(BF16)
    mix = (jnp.dot(a_n, wa_ref[...], preferred_element_type=F32)
           + jnp.dot(s_n, ws_ref[...], preferred_element_type=F32))
    o_ref[...] = x_ref[...] + mix


def _out_proj(x2, attn, y, w_glu, b_glu, g_a, g_s, w_out, tm=512):
    t, d = x2.shape
    wa = attn.shape[1]
    ws = y.shape[1]
    assert wa == ws
    row = lambda i: (i, 0)
    fixed = lambda i: (0, 0)
    once = pl.Buffered(1)
    return pl.pallas_call(
        _out_proj_kernel,
        grid=(t // tm,),
        in_specs=[pl.BlockSpec((tm, d), row),
                  pl.BlockSpec((tm, wa), row),
                  pl.BlockSpec((tm, ws), row),
                  pl.BlockSpec((ws, ws), fixed, pipeline_mode=once),
                  pl.BlockSpec((1, ws), fixed),
                  pl.BlockSpec((1, wa), fixed),
                  pl.BlockSpec((1, ws), fixed),
                  pl.BlockSpec((wa, d), fixed, pipeline_mode=once),
                  pl.BlockSpec((ws, d), lambda i: (1, 0), pipeline_mode=once)],
        out_specs=pl.BlockSpec((tm, d), row),
        out_shape=jax.ShapeDtypeStruct((t, d), F32),
        compiler_params=pltpu.CompilerParams(dimension_semantics=("parallel",),
                                             vmem_limit_bytes=VMEM_LIMIT),
    )(x2, attn, y, w_glu, b_glu.reshape(1, ws), g_a.reshape(1, wa), g_s.reshape(1, ws),
      w_out, w_out)


def _ffn_up_kernel(x_ref, halo_ref, g_ref, wg_ref, wv_ref, cwg_ref, cwv_ref, cbg_ref, cbv_ref,
                   wd_ref, o_ref, wdq_ref, h_ref, *, tiles_per_seq):
    tm = x_ref.shape[0]
    wdq_ref[...] = wd_ref[...].astype(wdq_ref.dtype)

    @pl.when(pl.program_id(1) == 0)
    def _():
        keep = jnp.where(pl.program_id(0) % tiles_per_seq == 0, 0.0, 1.0)
        h_ref[pl.ds(0, CONV_HALO), :] = (_rms_rows(halo_ref[...], g_ref[...]) * keep).astype(BF16)
        h_ref[pl.ds(CONV_HALO, tm), :] = _rms_rows(x_ref[...], g_ref[...]).astype(BF16)

    h = h_ref[...]

    def conv(w_ref, cw_ref, cb_ref):
        up = jnp.dot(h, w_ref[...].astype(BF16), preferred_element_type=F32)
        cw = cw_ref[...]
        out = up[CONV_HALO:] * cw[CONV_WIDTH - 1:CONV_WIDTH] + cb_ref[...]
        for j in range(CONV_WIDTH - 1):
            lag = CONV_WIDTH - 1 - j
            out = out + up[CONV_HALO - lag:CONV_HALO - lag + tm] * cw[j:j + 1]
        return out

    gate = conv(wg_ref, cwg_ref, cbg_ref)
    val = conv(wv_ref, cwv_ref, cbv_ref)
    o_ref[...] = (jax.nn.silu(gate) * val).astype(o_ref.dtype)


def _ffn_up(x1, g, w_up, conv_w, conv_b, w_down, seq, tm=1024, tn=512):
    t, d = x1.shape
    f = w_up.shape[1] // 2
    nf = f // tn
    tiles_per_seq = seq // tm
    halo_blocks = tm // CONV_HALO
    kernel = functools.partial(_ffn_up_kernel, tiles_per_seq=tiles_per_seq)
    cb = conv_b.reshape(1, 2 * f)
    wd_slab = pl.BlockSpec((w_down.shape[0] // (t // tm * nf), w_down.shape[1]),
                           lambda i, j: (i * nf + j, 0))
    return pl.pallas_call(
        kernel,
        grid=(t // tm, nf),
        in_specs=[pl.BlockSpec((tm, d), lambda i, j: (i, 0)),
                  pl.BlockSpec((CONV_HALO, d), lambda i, j: (jnp.maximum(i * halo_blocks - 1, 0), 0)),
                  pl.BlockSpec((1, d), lambda i, j: (0, 0)),
                  pl.BlockSpec((d, tn), lambda i, j: (0, j)),
                  pl.BlockSpec((d, tn), lambda i, j: (0, nf + j)),
                  pl.BlockSpec((CONV_WIDTH, tn), lambda i, j: (0, j)),
                  pl.BlockSpec((CONV_WIDTH, tn), lambda i, j: (0, nf + j)),
                  pl.BlockSpec((1, tn), lambda i, j: (0, j)),
                  pl.BlockSpec((1, tn), lambda i, j: (0, nf + j)),
                  wd_slab],
        out_specs=[pl.BlockSpec((tm, tn), lambda i, j: (i, j)), wd_slab],
        out_shape=[jax.ShapeDtypeStruct((t, f), BF16), jax.ShapeDtypeStruct(w_down.shape, BF16)],
        scratch_shapes=[pltpu.VMEM((CONV_HALO + tm, d), BF16)],
        compiler_params=pltpu.CompilerParams(dimension_semantics=("parallel", "arbitrary"),
                                             vmem_limit_bytes=VMEM_LIMIT),
    )(x1, x1, g.reshape(1, d), w_up, w_up, conv_w, conv_w, cb, cb, w_down)


def _ffn_down_kernel(a_ref, w_ref, x_ref, g_ref, o_ref, *, final_norm):
    y = x_ref[...] + jnp.dot(a_ref[...], w_ref[...], preferred_element_type=F32)
    o_ref[...] = _rms_rows(y, g_ref[...]) if final_norm else y


def _ffn_down(act, w_down, x1, g, final_norm, tm=256):
    t, f = act.shape
    d = w_down.shape[1]
    return pl.pallas_call(
        functools.partial(_ffn_down_kernel, final_norm=final_norm),
        grid=(t // tm,),
        in_specs=[pl.BlockSpec((tm, f), lambda i: (i, 0)),
                  pl.BlockSpec((f, d), lambda i: (0, 0), pipeline_mode=pl.Buffered(1)),
                  pl.BlockSpec((tm, d), lambda i: (i, 0)),
                  pl.BlockSpec((1, d), lambda i: (0, 0))],
        out_specs=pl.BlockSpec((tm, d), lambda i: (i, 0)),
        out_shape=jax.ShapeDtypeStruct((t, d), F32),
        compiler_params=pltpu.CompilerParams(dimension_semantics=("parallel",),
                                             vmem_limit_bytes=VMEM_LIMIT),
    )(act, w_down, x1, g.reshape(1, d))


def kernel(x, norm_mix, w_in, rel_bias_table, ssm_lam_re, ssm_lam_im, ssm_log_dt, ssm_b_re, ssm_b_im, ssm_c_re, ssm_c_im, ssm_d, ssm_w_glu, ssm_b_glu, norm_attn_out, norm_ssm_out, w_out, norm_ffn, w_ffn_up, ffn_conv_w, ffn_conv_b, w_ffn_down, norm_final):
    batch, seq, d_model = x.shape
    depth = w_in.shape[0]
    aw = ATTN_HEADS * HEAD_DIM
    sw = SSM_GROUPS * SSM_GROUP_CH
    t = batch * seq
    n_blocks = seq // MOBA_BLOCK
    assert seq % MOBA_BLOCK == 0 and seq % SSM_CHUNK == 0

    bias_tiles = _bias_tiles(rel_bias_table)
    x2 = x.reshape(t, d_model)
    for l in range(depth):
        tg, bn, cn, a_re, a_im, w_in_b = _s5_tables(
            ssm_log_dt[l], ssm_lam_re[l], ssm_lam_im[l], ssm_b_re[l], ssm_b_im[l], ssm_c_re[l],
            ssm_c_im[l], ssm_d[l], w_in[l], q_cols=aw, q_scale=HEAD_DIM ** -0.5 * LOG2E)

        proj, (w_out_b, w_glu_b) = _in_proj(x2, norm_mix[l], w_in_b, (w_out[l], ssm_w_glu[l]))

        attn = _moba_attention(proj, bias_tiles, batch, seq)
        y = _s5_scan(proj, tg, bn, cn, a_re, a_im, batch, u_col0=3 * aw)

        x2 = _out_proj(x2, attn, y, w_glu_b, ssm_b_glu[l], norm_attn_out[l], norm_ssm_out[l], w_out_b)

        act, w_down_b = _ffn_up(x2, norm_ffn[l], w_ffn_up[l], ffn_conv_w[l], ffn_conv_b[l],
                                w_ffn_down[l], seq)
        x2 = _ffn_down(act, w_down_b, x2, norm_final, final_norm=(l == depth - 1))
    return x2.reshape(batch, seq, d_model)
```

```python
import functools
import math

import jax
import jax.numpy as jnp
from jax import lax
from jax.experimental import pallas as pl
from jax.experimental.pallas import tpu as pltpu

F32 = jnp.float32
BF16 = jnp.bfloat16

ATTN_HEADS = 16
HEAD_DIM = 64
SSM_GROUP_CH = 16
SSM_GROUPS = 64
SSM_STATE = 64
MOBA_BLOCK = 256
MOBA_TOP_K = 3
NUM_BUCKETS = 32
MAX_DISTANCE = 128
CONV_WIDTH = 3
RMS_EPS = 1e-6

SSM_CHUNK = 16
MOBA_HEADS_PER_STEP = 8
MXU_DEPTH = 256
SCAN_UNROLL = 8
CONV_HALO = 16
VMEM_LIMIT = 52 * 1024 * 1024
VMEM_LIMIT_RESIDENT = 58 * 1024 * 1024

NEG_INF = float("-inf")
LOG2E = math.log2(math.e)
V_AUG_ROWS = HEAD_DIM + 16


def _rms_rows(x, g):
    ms = jnp.mean(x * x, axis=-1, keepdims=True)
    return x * lax.rsqrt(ms + RMS_EPS) * g


_NT_DIMS = (((1,), (1,)), ((), ()))


def _in_proj_kernel(x_ref, g_ref, w_ref, *rest):
    n_extra = (len(rest) - 2) // 2
    extra_in, o_ref = rest[:n_extra], rest[n_extra]
    extra_out, h_ref = rest[n_extra + 1:2 * n_extra + 1], rest[-1]

    @pl.when(pl.program_id(1) == 0)
    def _():
        h_ref[...] = _rms_rows(x_ref[...], g_ref[...]).astype(BF16)

    o_ref[...] = jnp.dot(h_ref[...], w_ref[...], preferred_element_type=F32).astype(o_ref.dtype)
    for src, dst in zip(extra_in, extra_out):
        dst[...] = src[...].astype(dst.dtype)


def _in_proj(x2, g, w, cast_along, tm=1024, tn=2048):
    t, d = x2.shape
    n = w.shape[1]
    n_i, n_j = t // tm, n // tn
    steps = n_i * n_j
    slabs = [pl.BlockSpec((a.shape[0] // steps, a.shape[1]), lambda i, j: (i * n_j + j, 0))
             for a in cast_along]
    outs = pl.pallas_call(
        _in_proj_kernel,
        grid=(n_i, n_j),
        in_specs=[pl.BlockSpec((tm, d), lambda i, j: (i, 0)),
                  pl.BlockSpec((1, d), lambda i, j: (0, 0)),
                  pl.BlockSpec((d, tn), lambda i, j: (0, j))] + slabs,
        out_specs=[pl.BlockSpec((tm, tn), lambda i, j: (i, j))] + slabs,
        out_shape=[jax.ShapeDtypeStruct((t, n), BF16)]
                  + [jax.ShapeDtypeStruct(a.shape, BF16) for a in cast_along],
        scratch_shapes=[pltpu.VMEM((tm, d), BF16)],
        compiler_params=pltpu.CompilerParams(dimension_semantics=("parallel", "arbitrary"),
                                             vmem_limit_bytes=VMEM_LIMIT),
    )(x2, g.reshape(1, d), w, *cast_along)
    return outs[0], outs[1:]


def _t5_bucket(dist):
    dist = jnp.maximum(dist, 0)
    max_exact = NUM_BUCKETS // 2
    log_ratio = jnp.log(jnp.maximum(dist, max_exact).astype(F32) / max_exact)
    large = max_exact + (log_ratio / math.log(MAX_DISTANCE / max_exact)
                         * (NUM_BUCKETS - max_exact)).astype(jnp.int32)
    large = jnp.minimum(large, NUM_BUCKETS - 1)
    return jnp.where(dist < max_exact, dist, large)


def _bias_tiles_kernel(tab_ref, o_ref):
    blk = MOBA_BLOCK
    dist = lax.broadcasted_iota(jnp.int32, (1, 2 * blk), 1)
    bucket = _t5_bucket(dist)
    kk = lax.broadcasted_iota(jnp.int32, (blk, blk), 0)
    qq = lax.broadcasted_iota(jnp.int32, (blk, blk), 1)
    for i in range(o_ref.shape[0]):
        h = pl.program_id(0) * o_ref.shape[0] + i
        row = jnp.zeros((1, 2 * blk), F32)
        for b in range(NUM_BUCKETS):
            row = jnp.where(bucket == b, tab_ref[b, h], row)
        row = (row - tab_ref[NUM_BUCKETS - 1, h]) * LOG2E
        pair = pltpu.roll(jnp.broadcast_to(row, (blk, 2 * blk)), 0, 1, stride=1, stride_axis=0)
        o_ref[i, 0] = pair[:, blk:]
        o_ref[i, 1] = jnp.where(qq >= kk, pair[:, :blk], NEG_INF)


def _bias_tiles(table, heads_per_step=4):
    return pl.pallas_call(
        _bias_tiles_kernel,
        grid=(ATTN_HEADS // heads_per_step,),
        in_specs=[pl.BlockSpec(memory_space=pltpu.SMEM)],
        out_specs=pl.BlockSpec((heads_per_step, 2, MOBA_BLOCK, MOBA_BLOCK), lambda h: (h, 0, 0, 0)),
        out_shape=jax.ShapeDtypeStruct((ATTN_HEADS, 2, MOBA_BLOCK, MOBA_BLOCK), F32),
    )(table)


def _moba_kernel(q_ref, k_ref, v_ref, bias_ref, o_ref,
                 kmf_ref, vaug_ref, qm_ref, sa_ref, sb_ref, cma_ref, cmb_ref,
                 m_ref, acc_ref, rb_ref, ot_ref, *, n_blocks):
    qi = pl.program_id(2)
    blk = MOBA_BLOCK
    nh = qm_ref.shape[0]

    @pl.when(qi == 0)
    def _():
        ones = jnp.ones((V_AUG_ROWS - HEAD_DIM, blk), BF16)
        for j in range(n_blocks):
            kb = k_ref[j * blk:(j + 1) * blk, :].astype(F32)
            kmf_ref[pl.ds(j, 1), :] = jnp.sum(kb, axis=0, keepdims=True) * (1.0 / blk)
            vt = v_ref[j].T
            for hh in range(nh):
                vaug_ref[j, hh, 0:HEAD_DIM, :] = vt[hh * HEAD_DIM:(hh + 1) * HEAD_DIM, :]
                vaug_ref[j, hh, HEAD_DIM:V_AUG_ROWS, :] = ones

    kmf = kmf_ref[...]
    km_hi = kmf.astype(BF16)
    km_lo = (kmf - km_hi.astype(F32)).astype(BF16)

    q2 = q_ref[0].T
    q_zero = jnp.zeros((HEAD_DIM, blk), q2.dtype)
    blk_idx = lax.broadcasted_iota(jnp.int32, (n_blocks, blk), 0)
    blk_idx_f = blk_idx.astype(F32)
    valid = blk_idx < qi
    hps = qm_ref.shape[1] // HEAD_DIM
    slab = lambda hh: slice((hh // hps) * hps * HEAD_DIM, (hh // hps + 1) * hps * HEAD_DIM)

    for hh in range(nh):
        vrows = slice(hh * HEAD_DIM, (hh + 1) * HEAD_DIM)
        qm = jnp.concatenate([q_zero] * (hh % hps) + [q2[vrows]] + [q_zero] * (hps - 1 - hh % hps),
                             axis=0)
        qm_ref[hh] = qm

        gate = (jnp.dot(km_hi[:, slab(hh)], qm, preferred_element_type=F32)
                + jnp.dot(km_lo[:, slab(hh)], qm, preferred_element_type=F32))
        g = jnp.where(valid, gate, NEG_INF)
        sel = blk_idx == qi
        for _ in range(MOBA_TOP_K):
            top = jnp.max(g, axis=0, keepdims=True)
            first = jnp.min(jnp.where(g == top, blk_idx_f, float(n_blocks)), axis=0, keepdims=True)
            hit = blk_idx_f == first
            sel = sel | (hit & valid)
            g = jnp.where(hit, NEG_INF, g)
        rb_ref[hh] = jnp.where(sel, 0.0, NEG_INF)
        m_ref[hh] = jnp.full((1, blk), NEG_INF, F32)
        acc_ref[hh] = jnp.zeros((V_AUG_ROWS, blk), F32)

    def group_rows(g):
        ja = qi - 2 * g
        return ja, jnp.maximum(ja - 1, 0)

    def scores(hh, g, s_buf, cm_buf, first_group):
        _, lo = group_rows(g)
        kslab = k_ref[pl.ds(pl.multiple_of(lo * blk, blk), 2 * blk), slab(hh)]
        s = jnp.dot(kslab, qm_ref[hh], preferred_element_type=F32)
        for i in range(2):
            si = s[i * blk:(i + 1) * blk]
            if first_group:
                tile = jnp.where(qi == 0, 1, 0) if i == 0 else 1
                si = si + bias_ref[hh, tile]
            s_buf[hh, i] = si
            cm_buf[hh, pl.ds(i, 1), :] = jnp.max(si, axis=0, keepdims=True)

    def attend(hh, g, s_buf, cm_buf):
        ja, lo = group_rows(g)
        masks = [rb_ref[hh, pl.ds(lo, 1), :],
                 jnp.where(ja >= 1, rb_ref[hh, pl.ds(lo + 1, 1), :], NEG_INF)]
        m_old = m_ref[hh]
        m_new = m_old
        for i in range(2):
            m_new = jnp.maximum(m_new, cm_buf[hh, pl.ds(i, 1), :] + masks[i])
        acc = jnp.exp2(m_old - m_new) * acc_ref[hh]
        for i in range(2):
            p = jnp.exp2((s_buf[hh, i] - m_new).astype(BF16))
            pv = jnp.dot(vaug_ref[lo + i, hh], p, preferred_element_type=F32)
            acc = acc + jnp.where(masks[i] == 0.0, pv, 0.0)
        acc_ref[hh] = acc
        m_ref[hh] = m_new

    n_more = qi // 2
    for hh in range(nh):
        scores(hh, 0, sa_ref, cma_ref, True)

    def stage(it, cur, nxt):
        lead = 1
        for hh in range(lead):
            scores(hh, it + 1, *nxt, False)
        for hh in range(nh):
            if hh + lead < nh:
                scores(hh + lead, it + 1, *nxt, False)
            attend(hh, it, *cur)

    buf_a, buf_b = (sa_ref, cma_ref), (sb_ref, cmb_ref)

    def step(it, carry):
        @pl.when(it % 2 == 0)
        def _():
            stage(it, buf_a, buf_b)

        @pl.when(it % 2 == 1)
        def _():
            stage(it, buf_b, buf_a)

        return carry

    lax.fori_loop(0, n_more, step, 0)

    @pl.when(n_more % 2 == 0)
    def _():
        for hh in range(nh):
            attend(hh, n_more, *buf_a)

    @pl.when(n_more % 2 == 1)
    def _():
        for hh in range(nh):
            attend(hh, n_more, *buf_b)

    for hh in range(nh):
        acc = acc_ref[hh]
        ot_ref[hh * HEAD_DIM:(hh + 1) * HEAD_DIM, :] = (
            acc[0:HEAD_DIM] * (1.0 / acc[HEAD_DIM:HEAD_DIM + 1]))

    o_ref[...] = ot_ref[...].T.astype(o_ref.dtype)


def _moba_attention(proj, bias_tiles, batch, seq):
    proj3 = proj.reshape(proj.shape[0] // MOBA_BLOCK, MOBA_BLOCK, proj.shape[1])
    n_blocks = seq // MOBA_BLOCK
    nh = MOBA_HEADS_PER_STEP
    n_hg = ATTN_HEADS // nh
    hg_w = nh * HEAD_DIM
    kernel = functools.partial(_moba_kernel, n_blocks=n_blocks)
    return pl.pallas_call(
        kernel,
        grid=(batch, n_hg, n_blocks),
        in_specs=[
            pl.BlockSpec((1, MOBA_BLOCK, hg_w), lambda b, hg, qi: (b * n_blocks + qi, 0, hg)),
            pl.BlockSpec((seq, hg_w), lambda b, hg, qi: (b, n_hg + hg)),
            pl.BlockSpec((n_blocks, MOBA_BLOCK, hg_w), lambda b, hg, qi: (b, 0, 2 * n_hg + hg)),
            pl.BlockSpec((nh, 2, MOBA_BLOCK, MOBA_BLOCK), lambda b, hg, qi: (hg, 0, 0, 0)),
        ],
        out_specs=pl.BlockSpec((MOBA_BLOCK, hg_w), lambda b, hg, qi: (b * n_blocks + qi, hg)),
        out_shape=jax.ShapeDtypeStruct((batch * seq, ATTN_HEADS * HEAD_DIM), BF16),
        scratch_shapes=[
            pltpu.VMEM((n_blocks, hg_w), F32),
            pltpu.VMEM((n_blocks, nh, V_AUG_ROWS, MOBA_BLOCK), BF16),
            pltpu.VMEM((nh, min(hg_w, MXU_DEPTH), MOBA_BLOCK), BF16),
            pltpu.VMEM((nh, 2, MOBA_BLOCK, MOBA_BLOCK), F32),
            pltpu.VMEM((nh, 2, MOBA_BLOCK, MOBA_BLOCK), F32),
            pltpu.VMEM((nh, 8, MOBA_BLOCK), F32),
            pltpu.VMEM((nh, 8, MOBA_BLOCK), F32),
            pltpu.VMEM((nh, 1, MOBA_BLOCK), F32),
            pltpu.VMEM((nh, V_AUG_ROWS, MOBA_BLOCK), F32),
            pltpu.VMEM((nh, n_blocks, MOBA_BLOCK), F32),
            pltpu.VMEM((hg_w, MOBA_BLOCK), F32),
        ],
        compiler_params=pltpu.CompilerParams(
            dimension_semantics=("parallel", "parallel", "arbitrary"),
            vmem_limit_bytes=VMEM_LIMIT),
    )(proj3, proj, proj3, bias_tiles)


def _hdot_nt(a, b):
    def split(x):
        hi = x.astype(BF16)
        return hi, (x - hi.astype(F32)).astype(BF16)

    def nt(u, v):
        return lax.dot_general(u, v, _NT_DIMS, preferred_element_type=F32)

    a_hi, a_lo = split(a)
    b_hi, b_lo = split(b)
    return nt(a_hi, b_hi) + nt(a_hi, b_lo) + nt(a_lo, b_hi)


def _repeat_rows(x, n):
    return jnp.concatenate([jnp.broadcast_to(x[i:i + 1], (n, x.shape[1]))
                            for i in range(x.shape[0])], axis=0)


def _tile_rows(x, n):
    return jnp.concatenate([x] * n, axis=0)


def _s5_tables_kernel(logdt_ref, lre_ref, lim_ref, bre_ref, bim_ref, cre_ref, cim_ref, d_ref,
                      win_ref, tg_ref, bn_ref, cn_ref, are_ref, aim_ref, wq_ref, *, q_cols, q_scale):
    wq_ref[:, :q_cols] = (win_ref[:, :q_cols] * q_scale).astype(wq_ref.dtype)
    wq_ref[:, q_cols:] = win_ref[:, q_cols:].astype(wq_ref.dtype)

    lc, ch, p = SSM_CHUNK, SSM_GROUP_CH, SSM_STATE
    w = lc * ch
    rr = lax.broadcasted_iota(jnp.int32, (w, w), 0)
    cc = lax.broadcasted_iota(jnp.int32, (w, w), 1)
    causal = rr // ch >= cc // ch
    diag = rr == cc
    e_r = lax.broadcasted_iota(jnp.int32, (ch, w), 0)
    e_c = lax.broadcasted_iota(jnp.int32, (ch, w), 1)
    lane_tile = jnp.where(e_c % ch == e_r, 1.0, 0.0).astype(F32)
    tau = lax.broadcasted_iota(jnp.int32, (2 * lc, p), 0).astype(F32)

    for k in range(tg_ref.shape[0]):
        lam_re = lre_ref[k]
        lam_im = lim_ref[k]
        dt = jnp.exp(logdt_ref[k])
        lr = lam_re * dt
        li = lam_im * dt

        mag = jnp.exp(lr * tau)
        cs = jnp.cos(li * tau)
        sn = jnp.sin(li * tau)
        pos_re, pos_im = mag * cs, mag * sn
        inv = jnp.exp(-lr * tau[:lc])
        neg_re, neg_im = inv * cs[:lc], -inv * sn[:lc]

        lam1_re, lam1_im = pos_re[1:2], pos_im[1:2]
        lamk_re, lamk_im = pos_re[lc - 1:lc], pos_im[lc - 1:lc]
        are_ref[k] = pos_re[lc:lc + 1]
        aim_ref[k] = pos_im[lc:lc + 1]

        num_re, num_im = lam1_re - 1.0, lam1_im
        den = lam_re * lam_re + lam_im * lam_im
        coef_re = (num_re * lam_re + num_im * lam_im) / den
        coef_im = (num_im * lam_re - num_re * lam_im) / den
        bt_re, bt_im = bre_ref[k], bim_ref[k]
        bb_re = coef_re * bt_re - coef_im * bt_im
        bb_im = coef_re * bt_im + coef_im * bt_re

        bbt_re, bbt_im = _tile_rows(bb_re, lc), _tile_rows(bb_im, lc)
        ngx_re, ngx_im = _repeat_rows(neg_re, ch), _repeat_rows(neg_im, ch)
        bneg_re = bbt_re * ngx_re - bbt_im * ngx_im
        bneg_im = bbt_re * ngx_im + bbt_im * ngx_re

        ct_re, ct_im = _tile_rows(cre_ref[k], lc), _tile_rows(cim_ref[k], lc)
        psx_re, psx_im = _repeat_rows(pos_re[:lc], ch), _repeat_rows(pos_im[:lc], ch)
        cpos_re = ct_re * psx_re - ct_im * psx_im
        cpos_im = ct_re * psx_im + ct_im * psx_re

        raw = _hdot_nt(cpos_re, bneg_re) - _hdot_nt(cpos_im, bneg_im)
        d_lanes = jnp.dot(jnp.broadcast_to(d_ref[k], (8, ch)), lane_tile,
                          preferred_element_type=F32, precision=lax.Precision.HIGHEST)[0:1]
        tg = jnp.where(causal, raw, 0.0) + jnp.where(diag, d_lanes, 0.0)
        tg_ref[k] = tg.astype(tg_ref.dtype)

        bn_ref[k, 0] = (bneg_re * lamk_re - bneg_im * lamk_im).T.astype(bn_ref.dtype)
        bn_ref[k, 1] = (bneg_re * lamk_im + bneg_im * lamk_re).T.astype(bn_ref.dtype)
        cp_re = cpos_re * lam1_re - cpos_im * lam1_im
        cp_im = cpos_re * lam1_im + cpos_im * lam1_re
        cn_ref[k, 0] = cp_re.astype(cn_ref.dtype)
        cn_ref[k, 1] = (-cp_im).astype(cn_ref.dtype)


def _s5_tables(log_dt, lam_re, lam_im, b_re, b_im, c_re, c_im, d_skip, w_in, q_cols, q_scale, gb=8):
    g, p, ch = SSM_GROUPS, SSM_STATE, SSM_GROUP_CH
    w = SSM_CHUNK * ch
    steps = g // gb
    d, n = w_in.shape
    row = lambda a, n: a.reshape(g, 1, n)
    spec3 = lambda s1, s2: pl.BlockSpec((gb, s1, s2), lambda i: (i, 0, 0))
    spec4 = lambda s1, s2: pl.BlockSpec((gb, 2, s1, s2), lambda i: (i, 0, 0, 0))
    slab = pl.BlockSpec((d // steps, n), lambda i: (i, 0))
    return pl.pallas_call(
        functools.partial(_s5_tables_kernel, q_cols=q_cols, q_scale=q_scale),
        grid=(steps,),
        in_specs=[spec3(1, 1), spec3(1, p), spec3(1, p), spec3(ch, p), spec3(ch, p),
                  spec3(ch, p), spec3(ch, p), spec3(1, ch), slab],
        out_specs=[spec3(w, w), spec4(p, w), spec4(w, p), spec3(1, p), spec3(1, p), slab],
        out_shape=[jax.ShapeDtypeStruct((g, w, w), BF16),
                   jax.ShapeDtypeStruct((g, 2, p, w), BF16),
                   jax.ShapeDtypeStruct((g, 2, w, p), BF16),
                   jax.ShapeDtypeStruct((g, 1, p), F32),
                   jax.ShapeDtypeStruct((g, 1, p), F32),
                   jax.ShapeDtypeStruct((d, n), BF16)],
        compiler_params=pltpu.CompilerParams(dimension_semantics=("parallel",)),
    )(row(log_dt, 1), row(lam_re, p), row(lam_im, p),
      jnp.swapaxes(b_re, 1, 2), jnp.swapaxes(b_im, 1, 2), c_re, c_im, row(d_skip, ch), w_in)


def _s5_scan_kernel(u_ref, tg_ref, bn_ref, cn_ref, are_ref, aim_ref, y_ref,
                    uf_ref, v_ref, yt_ref, sre_ref, sim_ref, xre_ref, xim_ref, *, n_batch):
    lc, ch = SSM_CHUNK, SSM_GROUP_CH
    gb = tg_ref.shape[0]
    n_rows = u_ref.shape[0] // lc
    n_chunks = n_rows // n_batch
    w = lc * ch

    uf_ref[...] = u_ref[...].astype(F32)
    for s in range(lc):
        ust = uf_ref[pl.ds(s, n_rows, stride=lc), :].T
        for k in range(gb):
            v_ref[k, s * ch:(s + 1) * ch, :] = ust[k * ch:(k + 1) * ch, :].astype(v_ref.dtype)

    for k in range(gb):
        u = v_ref[k]
        rows = pl.ds(k, n_rows, stride=gb)
        sre_ref[rows, :] = jnp.dot(bn_ref[k, 0], u, preferred_element_type=F32).T
        sim_ref[rows, :] = jnp.dot(bn_ref[k, 1], u, preferred_element_type=F32).T

    a_re = are_ref[...]
    a_im = aim_ref[...]

    def step(c, carry):
        new = []
        for b in range(n_batch):
            x_re, x_im = carry[2 * b], carry[2 * b + 1]
            rows = pl.ds(pl.multiple_of((b * n_chunks + c) * gb, gb), gb)
            xre_ref[rows, :] = x_re
            xim_ref[rows, :] = x_im
            new.append(a_re * x_re - a_im * x_im + sre_ref[rows, :])
            new.append(a_re * x_im + a_im * x_re + sim_ref[rows, :])
        return tuple(new)

    zero = jnp.zeros(a_re.shape, F32)
    lax.fori_loop(0, n_chunks, step, (zero,) * (2 * n_batch), unroll=SCAN_UNROLL)

    for k in range(gb):
        rows = pl.ds(k, n_rows, stride=gb)
        y = jnp.dot(tg_ref[k], v_ref[k], preferred_element_type=F32)
        y = y + lax.dot_general(cn_ref[k, 0], xre_ref[rows, :].astype(BF16), _NT_DIMS,
                                preferred_element_type=F32)
        y = y + lax.dot_general(cn_ref[k, 1], xim_ref[rows, :].astype(BF16), _NT_DIMS,
                                preferred_element_type=F32)
        for s in range(lc):
            yt_ref[s, k * ch:(k + 1) * ch, :] = y[s * ch:(s + 1) * ch, :]

    for s in range(lc):
        y_ref[pl.ds(s, n_rows, stride=lc), :] = yt_ref[s].T.astype(y_ref.dtype)


def _s5_scan(ku, tg, bn, cn, a_re, a_im, n_batch, u_col0, gb=8):
    t = ku.shape[0]
    g, w, _ = tg.shape
    p, ch = SSM_STATE, SSM_GROUP_CH
    n_rows = t // SSM_CHUNK
    lanes = gb * ch
    kernel = functools.partial(_s5_scan_kernel, n_batch=n_batch)
    return pl.pallas_call(
        kernel,
        grid=(g // gb,),
        in_specs=[pl.BlockSpec((t, lanes), lambda i: (0, u_col0 // lanes + i)),
                  pl.BlockSpec((gb, w, w), lambda i: (i, 0, 0)),
                  pl.BlockSpec((gb, 2, p, w), lambda i: (i, 0, 0, 0)),
                  pl.BlockSpec((gb, 2, w, p), lambda i: (i, 0, 0, 0)),
                  pl.BlockSpec((gb, p), lambda i: (i, 0)),
                  pl.BlockSpec((gb, p), lambda i: (i, 0))],
        out_specs=pl.BlockSpec((t, lanes), lambda i: (0, i)),
        out_shape=jax.ShapeDtypeStruct((t, g * ch), F32),
        scratch_shapes=[pltpu.VMEM((t, lanes), F32),
                        pltpu.VMEM((gb, w, n_rows), BF16),
                        pltpu.VMEM((SSM_CHUNK, lanes, n_rows), F32)]
                       + [pltpu.VMEM((gb * n_rows, p), F32) for _ in range(4)],
        compiler_params=pltpu.CompilerParams(dimension_semantics=("parallel",),
                                             vmem_limit_bytes=VMEM_LIMIT),
    )(ku, tg, bn, cn, a_re.reshape(g, p), a_im.reshape(g, p))


def _out_proj_kernel(x_ref, a_ref, y_ref, wglu_ref, bglu_ref, ga_ref, gs_ref, wa_ref, ws_ref, o_ref):
    z = jax.nn.gelu(y_ref[...])
    gl = jnp.dot(z.astype(BF16), wglu_ref[...], preferred_element_type=F32) + bglu_ref[...]
    s = z * jax.nn.sigmoid(gl)
    a_n = _rms_rows(a_ref[...].astype(F32), ga_ref[...]).astype(BF16)
    s_n = _rms_rows(s, gs_ref[...]).astype(BF16)
    mix = (jnp.dot(a_n, wa_ref[...], preferred_element_type=F32)
           + jnp.dot(s_n, ws_ref[...], preferred_element_type=F32))
    o_ref[...] = x_ref[...] + mix


def _out_proj(x2, attn, y, w_glu, b_glu, g_a, g_s, w_out, tm=512):
    t, d = x2.shape
    wa = attn.shape[1]
    ws = y.shape[1]
    assert wa == ws
    row = lambda i: (i, 0)
    fixed = lambda i: (0, 0)
    once = pl.Buffered(1)
    return pl.pallas_call(
        _out_proj_kernel,
        grid=(t // tm,),
        in_specs=[pl.BlockSpec((tm, d), row),
                  pl.BlockSpec((tm, wa), row),
                  pl.BlockSpec((tm, ws), row),
                  pl.BlockSpec((ws, ws), fixed, pipeline_mode=once),
                  pl.BlockSpec((1, ws), fixed),
                  pl.BlockSpec((1, wa), fixed),
                  pl.BlockSpec((1, ws), fixed),
                  pl.BlockSpec((wa, d), fixed, pipeline_mode=once),
                  pl.BlockSpec((ws, d), lambda i: (1, 0), pipeline_mode=once)],
        out_specs=pl.BlockSpec((tm, d), row),
        out_shape=jax.ShapeDtypeStruct((t, d), F32),
        compiler_params=pltpu.CompilerParams(dimension_semantics=("parallel",),
                                             vmem_limit_bytes=VMEM_LIMIT),
    )(x2, attn, y, w_glu, b_glu.reshape(1, ws), g_a.reshape(1, wa), g_s.reshape(1, ws),
      w_out, w_out)


def _ffn_up_kernel(x_ref, halo_ref, g_ref, wg_ref, wv_ref, cwg_ref, cwv_ref, cbg_ref, cbv_ref,
                   wd_ref, o_ref, wdq_ref, h_ref, *, tiles_per_seq):
    tm = x_ref.shape[0]
    wdq_ref[...] = wd_ref[...].astype(wdq_ref.dtype)

    @pl.when(pl.program_id(1) == 0)
    def _():
        keep = jnp.where(pl.program_id(0) % tiles_per_seq == 0, 0.0, 1.0)
        h_ref[pl.ds(0, CONV_HALO), :] = (_rms_rows(halo_ref[...], g_ref[...]) * keep).astype(BF16)
        h_ref[pl.ds(CONV_HALO, tm), :] = _rms_rows(x_ref[...], g_ref[...]).astype(BF16)

    h = h_ref[...]

    def conv(w_ref, cw_ref, cb_ref):
        up = jnp.dot(h, w_ref[...].astype(BF16), preferred_element_type=F32)
        cw = cw_ref[...]
        out = up[CONV_HALO:] * cw[CONV_WIDTH - 1:CONV_WIDTH] + cb_ref[...]
        for j in range(CONV_WIDTH - 1):
            lag = CONV_WIDTH - 1 - j
            out = out + up[CONV_HALO - lag:CONV_HALO - lag + tm] * cw[j:j + 1]
        return out

    gate = conv(wg_ref, cwg_ref, cbg_ref)
    val = conv(wv_ref, cwv_ref, cbv_ref)
    o_ref[...] = (jax.nn.silu(gate) * val).astype(o_ref.dtype)


def _ffn_up(x1, g, w_up, conv_w, conv_b, w_down, seq, tm=1024, tn=512):
    t, d = x1.shape
    f = w_up.shape[1] // 2
    nf = f // tn
    tiles_per_seq = seq // tm
    halo_blocks = tm // CONV_HALO
    kernel = functools.partial(_ffn_up_kernel, tiles_per_seq=tiles_per_seq)
    cb = conv_b.reshape(1, 2 * f)
    wd_slab = pl.BlockSpec((w_down.shape[0] // (t // tm * nf), w_down.shape[1]),
                           lambda i, j: (i * nf + j, 0))
    return pl.pallas_call(
        kernel,
        grid=(t // tm, nf),
        in_specs=[pl.BlockSpec((tm, d), lambda i, j: (i, 0)),
                  pl.BlockSpec((CONV_HALO, d), lambda i, j: (jnp.maximum(i * halo_blocks - 1, 0), 0)),
                  pl.BlockSpec((1, d), lambda i, j: (0, 0)),
                  pl.BlockSpec((d, tn), lambda i, j: (0, j)),
                  pl.BlockSpec((d, tn), lambda i, j: (0, nf + j)),
                  pl.BlockSpec((CONV_WIDTH, tn), lambda i, j: (0, j)),
                  pl.BlockSpec((CONV_WIDTH, tn), lambda i, j: (0, nf + j)),
                  pl.BlockSpec((1, tn), lambda i, j: (0, j)),
                  pl.BlockSpec((1, tn), lambda i, j: (0, nf + j)),
                  wd_slab],
        out_specs=[pl.BlockSpec((tm, tn), lambda i, j: (i, j)), wd_slab],
        out_shape=[jax.ShapeDtypeStruct((t, f), BF16), jax.ShapeDtypeStruct(w_down.shape, BF16)],
        scratch_shapes=[pltpu.VMEM((CONV_HALO + tm, d), BF16)],
        compiler_params=pltpu.CompilerParams(dimension_semantics=("parallel", "arbitrary"),
                                             vmem_limit_bytes=VMEM_LIMIT),
    )(x1, x1, g.reshape(1, d), w_up, w_up, conv_w, conv_w, cb, cb, w_down)


def _ffn_down_kernel(a_ref, w_ref, x_ref, g_ref, o_ref, *, final_norm):
    y = x_ref[...] + jnp.dot(a_ref[...], w_ref[...], preferred_element_type=F32)
    o_ref[...] = _rms_rows(y, g_ref[...]) if final_norm else y


def _ffn_down(act, w_down, x1, g, final_norm, tm=512):
    t, f = act.shape
    d = w_down.shape[1]
    return pl.pallas_call(
        functools.partial(_ffn_down_kernel, final_norm=final_norm),
        grid=(t // tm,),
        in_specs=[pl.BlockSpec((tm, f), lambda i: (i, 0)),
                  pl.BlockSpec((f, d), lambda i: (0, 0), pipeline_mode=pl.Buffered(1)),
                  pl.BlockSpec((tm, d), lambda i: (i, 0)),
                  pl.BlockSpec((1, d), lambda i: (0, 0))],
        out_specs=pl.BlockSpec((tm, d), lambda i: (i, 0)),
        out_shape=jax.ShapeDtypeStruct((t, d), F32),
        compiler_params=pltpu.CompilerParams(dimension_semantics=("parallel",),
                                             vmem_limit_bytes=VMEM_LIMIT_RESIDENT),
    )(act, w_down, x1, g.reshape(1, d))


def kernel(x, norm_mix, w_in, rel_bias_table, ssm_lam_re, ssm_lam_im, ssm_log_dt, ssm_b_re, ssm_b_im, ssm_c_re, ssm_c_im, ssm_d, ssm_w_glu, ssm_b_glu, norm_attn_out, norm_ssm_out, w_out, norm_ffn, w_ffn_up, ffn_conv_w, ffn_conv_b, w_ffn_down, norm_final):
    batch, seq, d_model = x.shape
    depth = w_in.shape[0]
    aw = ATTN_HEADS * HEAD_DIM
    sw = SSM_GROUPS * SSM_GROUP_CH
    t = batch * seq
    n_blocks = seq // MOBA_BLOCK
    assert seq % MOBA_BLOCK == 0 and seq % SSM_CHUNK == 0

    bias_tiles = _bias_tiles(rel_bias_table)
    x2 = x.reshape(t, d_model)
    for l in range(depth):
        tg, bn, cn, a_re, a_im, w_in_b = _s5_tables(
            ssm_log_dt[l], ssm_lam_re[l], ssm_lam_im[l], ssm_b_re[l], ssm_b_im[l], ssm_c_re[l],
            ssm_c_im[l], ssm_d[l], w_in[l], q_cols=aw, q_scale=HEAD_DIM ** -0.5 * LOG2E)

        proj, (w_out_b, w_glu_b) = _in_proj(x2, norm_mix[l], w_in_b, (w_out[l], ssm_w_glu[l]))

        attn = _moba_attention(proj, bias_tiles, batch, seq)
        y = _s5_scan(proj, tg, bn, cn, a_re, a_im, batch, u_col0=3 * aw)

        x2 = _out_proj(x2, attn, y, w_glu_b, ssm_b_glu[l], norm_attn_out[l], norm_ssm_out[l], w_out_b)

        act, w_down_b = _ffn_up(x2, norm_ffn[l], w_ffn_up[l], ffn_conv_w[l], ffn_conv_b[l],
                                w_ffn_down[l], seq)
        x2 = _ffn_down(act, w_down_b, x2, norm_final, final_norm=(l == depth - 1))
    return x2.reshape(batch, seq, d_model)
```

```python
import functools
import math

import jax
import jax.numpy as jnp
from jax import lax
from jax.experimental import pallas as pl
from jax.experimental.pallas import tpu as pltpu

F32 = jnp.float32
BF16 = jnp.bfloat16

ATTN_HEADS = 16
HEAD_DIM = 64
SSM_GROUP_CH = 16
SSM_GROUPS = 64
SSM_STATE = 64
MOBA_BLOCK = 256
MOBA_TOP_K = 3
NUM_BUCKETS = 32
MAX_DISTANCE = 128
CONV_WIDTH = 3
RMS_EPS = 1e-6

SSM_CHUNK = 16
MOBA_HEADS_PER_STEP = 8
MXU_DEPTH = 256
SCAN_UNROLL = 8
CONV_HALO = 16
VMEM_LIMIT = 52 * 1024 * 1024
VMEM_LIMIT_RESIDENT = 58 * 1024 * 1024

NEG_INF = float("-inf")
LOG2E = math.log2(math.e)
V_AUG_ROWS = HEAD_DIM + 16


def _rms_rows(x, g):
    ms = jnp.mean(x * x, axis=-1, keepdims=True)
    return x * lax.rsqrt(ms + RMS_EPS) * g


_NT_DIMS = (((1,), (1,)), ((), ()))


def _in_proj_kernel(x_ref, g_ref, w_ref, *rest):
    n_extra = (len(rest) - 2) // 2
    extra_in, o_ref = rest[:n_extra], rest[n_extra]
    extra_out, h_ref = rest[n_extra + 1:2 * n_extra + 1], rest[-1]

    @pl.when(pl.program_id(1) == 0)
    def _():
        h_ref[...] = _rms_rows(x_ref[...], g_ref[...]).astype(BF16)

    o_ref[...] = jnp.dot(h_ref[...], w_ref[...], preferred_element_type=F32).astype(o_ref.dtype)
    for src, dst in zip(extra_in, extra_out):
        dst[...] = src[...].astype(dst.dtype)


def _in_proj(x2, g, w, cast_along, tm=1024, tn=2048):
    t, d = x2.shape
    n = w.shape[1]
    n_i, n_j = t // tm, n // tn
    steps = n_i * n_j
    slabs = [pl.BlockSpec((a.shape[0] // steps, a.shape[1]), lambda i, j: (i * n_j + j, 0))
             for a in cast_along]
    outs = pl.pallas_call(
        _in_proj_kernel,
        grid=(n_i, n_j),
        in_specs=[pl.BlockSpec((tm, d), lambda i, j: (i, 0)),
                  pl.BlockSpec((1, d), lambda i, j: (0, 0)),
                  pl.BlockSpec((d, tn), lambda i, j: (0, j))] + slabs,
        out_specs=[pl.BlockSpec((tm, tn), lambda i, j: (i, j))] + slabs,
        out_shape=[jax.ShapeDtypeStruct((t, n), BF16)]
                  + [jax.ShapeDtypeStruct(a.shape, BF16) for a in cast_along],
        scratch_shapes=[pltpu.VMEM((tm, d), BF16)],
        compiler_params=pltpu.CompilerParams(dimension_semantics=("parallel", "arbitrary"),
                                             vmem_limit_bytes=VMEM_LIMIT),
    )(x2, g.reshape(1, d), w, *cast_along)
    return outs[0], outs[1:]


def _t5_bucket(dist):
    dist = jnp.maximum(dist, 0)
    max_exact = NUM_BUCKETS // 2
    log_ratio = jnp.log(jnp.maximum(dist, max_exact).astype(F32) / max_exact)
    large = max_exact + (log_ratio / math.log(MAX_DISTANCE / max_exact)
                         * (NUM_BUCKETS - max_exact)).astype(jnp.int32)
    large = jnp.minimum(large, NUM_BUCKETS - 1)
    return jnp.where(dist < max_exact, dist, large)


def _bias_tiles_kernel(tab_ref, o_ref):
    blk = MOBA_BLOCK
    dist = lax.broadcasted_iota(jnp.int32, (1, 2 * blk), 1)
    bucket = _t5_bucket(dist)
    kk = lax.broadcasted_iota(jnp.int32, (blk, blk), 0)
    qq = lax.broadcasted_iota(jnp.int32, (blk, blk), 1)
    for i in range(o_ref.shape[0]):
        h = pl.program_id(0) * o_ref.shape[0] + i
        row = jnp.zeros((1, 2 * blk), F32)
        for b in range(NUM_BUCKETS):
            row = jnp.where(bucket == b, tab_ref[b, h], row)
        row = (row - tab_ref[NUM_BUCKETS - 1, h]) * LOG2E
        pair = pltpu.roll(jnp.broadcast_to(row, (blk, 2 * blk)), 0, 1, stride=1, stride_axis=0)
        o_ref[i, 0] = pair[:, blk:]
        o_ref[i, 1] = jnp.where(qq >= kk, pair[:, :blk], NEG_INF)


def _bias_tiles(table, heads_per_step=4):
    return pl.pallas_call(
        _bias_tiles_kernel,
        grid=(ATTN_HEADS // heads_per_step,),
        in_specs=[pl.BlockSpec(memory_space=pltpu.SMEM)],
        out_specs=pl.BlockSpec((heads_per_step, 2, MOBA_BLOCK, MOBA_BLOCK), lambda h: (h, 0, 0, 0)),
        out_shape=jax.ShapeDtypeStruct((ATTN_HEADS, 2, MOBA_BLOCK, MOBA_BLOCK), F32),
    )(table)


def _moba_kernel(q_ref, k_ref, v_ref, bias_ref, o_ref,
                 kmf_ref, vaug_ref, qm_ref, sa_ref, sb_ref, cma_ref, cmb_ref,
                 m_ref, acc_ref, rb_ref, ot_ref, *, n_blocks):
    qi = pl.program_id(2)
    blk = MOBA_BLOCK
    nh = qm_ref.shape[0]

    @pl.when(qi == 0)
    def _():
        ones = jnp.ones((V_AUG_ROWS - HEAD_DIM, blk), BF16)
        for j in range(n_blocks):
            kb = k_ref[j * blk:(j + 1) * blk, :].astype(F32)
            kmf_ref[pl.ds(j, 1), :] = jnp.sum(kb, axis=0, keepdims=True) * (1.0 / blk)
            vt = v_ref[j].T
            for hh in range(nh):
                vaug_ref[j, hh, 0:HEAD_DIM, :] = vt[hh * HEAD_DIM:(hh + 1) * HEAD_DIM, :]
                vaug_ref[j, hh, HEAD_DIM:V_AUG_ROWS, :] = ones

    kmf = kmf_ref[...]
    km_hi = kmf.astype(BF16)
    km_lo = (kmf - km_hi.astype(F32)).astype(BF16)
    gate_rows = -(-n_blocks // 16) * 16
    if gate_rows > n_blocks:
        km_pad = jnp.zeros((gate_rows - n_blocks, kmf.shape[1]), BF16)
        km_stack = jnp.concatenate([km_hi, km_pad, km_lo, km_pad], axis=0)
    else:
        km_stack = jnp.concatenate([km_hi, km_lo], axis=0)

    q2 = q_ref[0].T
    q_zero = jnp.zeros((HEAD_DIM, blk), q2.dtype)
    blk_idx = lax.broadcasted_iota(jnp.int32, (n_blocks, blk), 0)
    blk_idx_f = blk_idx.astype(F32)
    valid = blk_idx < qi
    hps = qm_ref.shape[1] // HEAD_DIM
    slab = lambda hh: slice((hh // hps) * hps * HEAD_DIM, (hh // hps + 1) * hps * HEAD_DIM)

    for hh in range(nh):
        vrows = slice(hh * HEAD_DIM, (hh + 1) * HEAD_DIM)
        qm = jnp.concatenate([q_zero] * (hh % hps) + [q2[vrows]] + [q_zero] * (hps - 1 - hh % hps),
                             axis=0)
        qm_ref[hh] = qm
        m_ref[hh] = jnp.full((1, blk), NEG_INF, F32)
        acc_ref[hh] = jnp.zeros((V_AUG_ROWS, blk), F32)

    def select(hh, gate):
        g = jnp.where(valid, gate, NEG_INF)
        sel = blk_idx == qi
        for _ in range(MOBA_TOP_K):
            top = jnp.max(g, axis=0, keepdims=True)
            first = jnp.min(jnp.where(g == top, blk_idx_f, float(n_blocks)), axis=0, keepdims=True)
            hit = blk_idx_f == first
            sel = sel | (hit & valid)
            g = jnp.where(hit, NEG_INF, g)
        rb_ref[hh] = jnp.where(sel, 0.0, NEG_INF)

    def group_rows(g):
        ja = qi - 2 * g
        return ja, jnp.maximum(ja - 1, 0)

    def scores(hh, g, s_buf, cm_buf, first_group):
        _, lo = group_rows(g)
        kslab = k_ref[pl.ds(pl.multiple_of(lo * blk, blk), 2 * blk), slab(hh)]
        if first_group:
            kslab = jnp.concatenate([kslab, km_stack[:, slab(hh)]], axis=0)
        s = jnp.dot(kslab, qm_ref[hh], preferred_element_type=F32)
        if first_group:
            select(hh, s[2 * blk:2 * blk + n_blocks]
                   + s[2 * blk + gate_rows:2 * blk + gate_rows + n_blocks])
        for i in range(2):
            si = s[i * blk:(i + 1) * blk]
            if first_group:
                tile = jnp.where(qi == 0, 1, 0) if i == 0 else 1
                si = si + bias_ref[hh, tile]
            s_buf[hh, i] = si
            cm_buf[hh, pl.ds(i, 1), :] = jnp.max(si, axis=0, keepdims=True)

    def attend(hh, g, s_buf, cm_buf):
        ja, lo = group_rows(g)
        masks = [rb_ref[hh, pl.ds(lo, 1), :],
                 jnp.where(ja >= 1, rb_ref[hh, pl.ds(lo + 1, 1), :], NEG_INF)]
        m_old = m_ref[hh]
        m_new = m_old
        for i in range(2):
            m_new = jnp.maximum(m_new, cm_buf[hh, pl.ds(i, 1), :] + masks[i])
        acc = jnp.exp2(m_old - m_new) * acc_ref[hh]
        for i in range(2):
            p = jnp.exp2((s_buf[hh, i] - m_new).astype(BF16))
            pv = jnp.dot(vaug_ref[lo + i, hh], p, preferred_element_type=F32)
            acc = acc + jnp.where(masks[i] == 0.0, pv, 0.0)
        acc_ref[hh] = acc
        m_ref[hh] = m_new

    n_more = qi // 2
    for hh in range(nh):
        scores(hh, 0, sa_ref, cma_ref, True)

    def stage(it, cur, nxt):
        lead = 1
        for hh in range(lead):
            scores(hh, it + 1, *nxt, False)
        for hh in range(nh):
            if hh + lead < nh:
                scores(hh + lead, it + 1, *nxt, False)
            attend(hh, it, *cur)

    buf_a, buf_b = (sa_ref, cma_ref), (sb_ref, cmb_ref)

    def step(it, carry):
        @pl.when(it % 2 == 0)
        def _():
            stage(it, buf_a, buf_b)

        @pl.when(it % 2 == 1)
        def _():
            stage(it, buf_b, buf_a)

        return carry

    lax.fori_loop(0, n_more, step, 0)

    @pl.when(n_more % 2 == 0)
    def _():
        for hh in range(nh):
            attend(hh, n_more, *buf_a)

    @pl.when(n_more % 2 == 1)
    def _():
        for hh in range(nh):
            attend(hh, n_more, *buf_b)

    for hh in range(nh):
        acc = acc_ref[hh]
        ot_ref[hh * HEAD_DIM:(hh + 1) * HEAD_DIM, :] = (
            acc[0:HEAD_DIM] * (1.0 / acc[HEAD_DIM:HEAD_DIM + 1]))

    o_ref[...] = ot_ref[...].T.astype(o_ref.dtype)


def _moba_attention(proj, bias_tiles, batch, seq):
    proj3 = proj.reshape(proj.shape[0] // MOBA_BLOCK, MOBA_BLOCK, proj.shape[1])
    n_blocks = seq // MOBA_BLOCK
    nh = MOBA_HEADS_PER_STEP
    n_hg = ATTN_HEADS // nh
    hg_w = nh * HEAD_DIM
    kernel = functools.partial(_moba_kernel, n_blocks=n_blocks)
    return pl.pallas_call(
        kernel,
        grid=(batch, n_hg, n_blocks),
        in_specs=[
            pl.BlockSpec((1, MOBA_BLOCK, hg_w), lambda b, hg, qi: (b * n_blocks + qi, 0, hg)),
            pl.BlockSpec((seq, hg_w), lambda b, hg, qi: (b, n_hg + hg)),
            pl.BlockSpec((n_blocks, MOBA_BLOCK, hg_w), lambda b, hg, qi: (b, 0, 2 * n_hg + hg)),
            pl.BlockSpec((nh, 2, MOBA_BLOCK, MOBA_BLOCK), lambda b, hg, qi: (hg, 0, 0, 0)),
        ],
        out_specs=pl.BlockSpec((MOBA_BLOCK, hg_w), lambda b, hg, qi: (b * n_blocks + qi, hg)),
        out_shape=jax.ShapeDtypeStruct((batch * seq, ATTN_HEADS * HEAD_DIM), BF16),
        scratch_shapes=[
            pltpu.VMEM((n_blocks, hg_w), F32),
            pltpu.VMEM((n_blocks, nh, V_AUG_ROWS, MOBA_BLOCK), BF16),
            pltpu.VMEM((nh, min(hg_w, MXU_DEPTH), MOBA_BLOCK), BF16),
            pltpu.VMEM((nh, 2, MOBA_BLOCK, MOBA_BLOCK), F32),
            pltpu.VMEM((nh, 2, MOBA_BLOCK, MOBA_BLOCK), F32),
            pltpu.VMEM((nh, 8, MOBA_BLOCK), F32),
            pltpu.VMEM((nh, 8, MOBA_BLOCK), F32),
            pltpu.VMEM((nh, 1, MOBA_BLOCK), F32),
            pltpu.VMEM((nh, V_AUG_ROWS, MOBA_BLOCK), F32),
            pltpu.VMEM((nh, n_blocks, MOBA_BLOCK), F32),
            pltpu.VMEM((hg_w, MOBA_BLOCK), F32),
        ],
        compiler_params=pltpu.CompilerParams(
            dimension_semantics=("parallel", "parallel", "arbitrary"),
            vmem_limit_bytes=VMEM_LIMIT),
    )(proj3, proj, proj3, bias_tiles)


def _hdot_nt(a, b):
    def split(x):
        hi = x.astype(BF16)
        return hi, (x - hi.astype(F32)).astype(BF16)

    def nt(u, v):
        return lax.dot_general(u, v, _NT_DIMS, preferred_element_type=F32)

    a_hi, a_lo = split(a)
    b_hi, b_lo = split(b)
    return nt(a_hi, b_hi) + nt(a_hi, b_lo) + nt(a_lo, b_hi)


def _repeat_rows(x, n):
    return jnp.concatenate([jnp.broadcast_to(x[i:i + 1], (n, x.shape[1]))
                            for i in range(x.shape[0])], axis=0)


def _tile_rows(x, n):
    return jnp.concatenate([x] * n, axis=0)


def _s5_tables_kernel(logdt_ref, lre_ref, lim_ref, bre_ref, bim_ref, cre_ref, cim_ref, d_ref,
                      win_ref, tg_ref, bn_ref, cn_ref, are_ref, aim_ref, wq_ref, *, q_cols, q_scale):
    wq_ref[:, :q_cols] = (win_ref[:, :q_cols] * q_scale).astype(wq_ref.dtype)
    wq_ref[:, q_cols:] = win_ref[:, q_cols:].astype(wq_ref.dtype)

    lc, ch, p = SSM_CHUNK, SSM_GROUP_CH, SSM_STATE
    w = lc * ch
    rr = lax.broadcasted_iota(jnp.int32, (w, w), 0)
    cc = lax.broadcasted_iota(jnp.int32, (w, w), 1)
    causal = rr // ch >= cc // ch
    diag = rr == cc
    e_r = lax.broadcasted_iota(jnp.int32, (ch, w), 0)
    e_c = lax.broadcasted_iota(jnp.int32, (ch, w), 1)
    lane_tile = jnp.where(e_c % ch == e_r, 1.0, 0.0).astype(F32)
    tau = lax.broadcasted_iota(jnp.int32, (2 * lc, p), 0).astype(F32)
    mid = lc // 2

    for k in range(tg_ref.shape[0]):
        lam_re = lre_ref[k]
        lam_im = lim_ref[k]
        dt = jnp.exp(logdt_ref[k])
        lr = lam_re * dt
        li = lam_im * dt

        mag = jnp.exp(lr * tau)
        pos_re, pos_im = mag * jnp.cos(li * tau), mag * jnp.sin(li * tau)
        tau_c = tau[:lc] - mid
        mag_c = jnp.exp(lr * tau_c)
        inv_c = jnp.exp(-lr * tau_c)
        cs, sn = jnp.cos(li * tau_c), jnp.sin(li * tau_c)
        fwd_re, fwd_im = mag_c * cs, mag_c * sn
        neg_re, neg_im = inv_c * cs, -inv_c * sn

        lam1_re, lam1_im = pos_re[1:2], pos_im[1:2]
        lamb_re, lamb_im = pos_re[lc - 1 - mid:lc - mid], pos_im[lc - 1 - mid:lc - mid]
        lamc_re, lamc_im = pos_re[mid + 1:mid + 2], pos_im[mid + 1:mid + 2]
        are_ref[k] = pos_re[lc:lc + 1]
        aim_ref[k] = pos_im[lc:lc + 1]

        num_re, num_im = lam1_re - 1.0, lam1_im
        den = lam_re * lam_re + lam_im * lam_im
        coef_re = (num_re * lam_re + num_im * lam_im) / den
        coef_im = (num_im * lam_re - num_re * lam_im) / den
        bt_re, bt_im = bre_ref[k], bim_ref[k]
        bb_re = coef_re * bt_re - coef_im * bt_im
        bb_im = coef_re * bt_im + coef_im * bt_re

        bbt_re, bbt_im = _tile_rows(bb_re, lc), _tile_rows(bb_im, lc)
        ngx_re, ngx_im = _repeat_rows(neg_re, ch), _repeat_rows(neg_im, ch)
        bneg_re = bbt_re * ngx_re - bbt_im * ngx_im
        bneg_im = bbt_re * ngx_im + bbt_im * ngx_re

        ct_re, ct_im = _tile_rows(cre_ref[k], lc), _tile_rows(cim_ref[k], lc)
        psx_re, psx_im = _repeat_rows(fwd_re, ch), _repeat_rows(fwd_im, ch)
        cpos_re = ct_re * psx_re - ct_im * psx_im
        cpos_im = ct_re * psx_im + ct_im * psx_re

        raw = _hdot_nt(cpos_re, bneg_re) - _hdot_nt(cpos_im, bneg_im)
        d_lanes = jnp.dot(jnp.broadcast_to(d_ref[k], (8, ch)), lane_tile,
                          preferred_element_type=F32, precision=lax.Precision.HIGHEST)[0:1]
        tg = jnp.where(causal, raw, 0.0) + jnp.where(diag, d_lanes, 0.0)
        tg_ref[k] = tg.astype(tg_ref.dtype)

        bn_ref[k, 0] = (bneg_re * lamb_re - bneg_im * lamb_im).T.astype(bn_ref.dtype)
        bn_ref[k, 1] = (bneg_re * lamb_im + bneg_im * lamb_re).T.astype(bn_ref.dtype)
        cp_re = cpos_re * lamc_re - cpos_im * lamc_im
        cp_im = cpos_re * lamc_im + cpos_im * lamc_re
        cn_ref[k, 0] = cp_re.astype(cn_ref.dtype)
        cn_ref[k, 1] = (-cp_im).astype(cn_ref.dtype)


def _s5_tables(log_dt, lam_re, lam_im, b_re, b_im, c_re, c_im, d_skip, w_in, q_cols, q_scale, gb=8):
    g, p, ch = SSM_GROUPS, SSM_STATE, SSM_GROUP_CH
    w = SSM_CHUNK * ch
    steps = g // gb
    d, n = w_in.shape
    row = lambda a, n: a.reshape(g, 1, n)
    spec3 = lambda s1, s2: pl.BlockSpec((gb, s1, s2), lambda i: (i, 0, 0))
    spec4 = lambda s1, s2: pl.BlockSpec((gb, 2, s1, s2), lambda i: (i, 0, 0, 0))
    slab = pl.BlockSpec((d // steps, n), lambda i: (i, 0))
    return pl.pallas_call(
        functools.partial(_s5_tables_kernel, q_cols=q_cols, q_scale=q_scale),
        grid=(steps,),
        in_specs=[spec3(1, 1), spec3(1, p), spec3(1, p), spec3(ch, p), spec3(ch, p),
                  spec3(ch, p), spec3(ch, p), spec3(1, ch), slab],
        out_specs=[spec3(w, w), spec4(p, w), spec4(w, p), spec3(1, p), spec3(1, p), slab],
        out_shape=[jax.ShapeDtypeStruct((g, w, w), BF16),
                   jax.ShapeDtypeStruct((g, 2, p, w), BF16),
                   jax.ShapeDtypeStruct((g, 2, w, p), BF16),
                   jax.ShapeDtypeStruct((g, 1, p), F32),
                   jax.ShapeDtypeStruct((g, 1, p), F32),
                   jax.ShapeDtypeStruct((d, n), BF16)],
        compiler_params=pltpu.CompilerParams(dimension_semantics=("parallel",)),
    )(row(log_dt, 1), row(lam_re, p), row(lam_im, p),
      jnp.swapaxes(b_re, 1, 2), jnp.swapaxes(b_im, 1, 2), c_re, c_im, row(d_skip, ch), w_in)


def _s5_scan_kernel(u_ref, tg_ref, bn_ref, cn_ref, are_ref, aim_ref, y_ref,
                    uf_ref, v_ref, yt_ref, sre_ref, sim_ref, xre_ref, xim_ref, *, n_batch):
    lc, ch = SSM_CHUNK, SSM_GROUP_CH
    gb = tg_ref.shape[0]
    n_rows = u_ref.shape[0] // lc
    n_chunks = n_rows // n_batch
    w = lc * ch

    uf_ref[...] = u_ref[...].astype(F32)
    for s in range(lc):
        ust = uf_ref[pl.ds(s, n_rows, stride=lc), :].T
        for k in range(gb):
            v_ref[k, s * ch:(s + 1) * ch, :] = ust[k * ch:(k + 1) * ch, :].astype(v_ref.dtype)

    for k in range(gb):
        u = v_ref[k]
        rows = pl.ds(k, n_rows, stride=gb)
        sre_ref[rows, :] = jnp.dot(bn_ref[k, 0], u, preferred_element_type=F32).T
        sim_ref[rows, :] = jnp.dot(bn_ref[k, 1], u, preferred_element_type=F32).T

    a_re = are_ref[...]
    a_im = aim_ref[...]

    def step(c, carry):
        new = []
        for b in range(n_batch):
            x_re, x_im = carry[2 * b], carry[2 * b + 1]
            rows = pl.ds(pl.multiple_of((b * n_chunks + c) * gb, gb), gb)
            xre_ref[rows, :] = x_re
            xim_ref[rows, :] = x_im
            new.append(a_re * x_re - a_im * x_im + sre_ref[rows, :])
            new.append(a_re * x_im + a_im * x_re + sim_ref[rows, :])
        return tuple(new)

    zero = jnp.zeros(a_re.shape, F32)
    lax.fori_loop(0, n_chunks, step, (zero,) * (2 * n_batch), unroll=SCAN_UNROLL)

    for k in range(gb):
        rows = pl.ds(k, n_rows, stride=gb)
        y = jnp.dot(tg_ref[k], v_ref[k], preferred_element_type=F32)
        y = y + lax.dot_general(cn_ref[k, 0], xre_ref[rows, :].astype(BF16), _NT_DIMS,
                                preferred_element_type=F32)
        y = y + lax.dot_general(cn_ref[k, 1], xim_ref[rows, :].astype(BF16), _NT_DIMS,
                                preferred_element_type=F32)
        for s in range(lc):
            yt_ref[s, k * ch:(k + 1) * ch, :] = y[s * ch:(s + 1) * ch, :]

    for s in range(lc):
        y_ref[pl.ds(s, n_rows, stride=lc), :] = yt_ref[s].T.astype(y_ref.dtype)


def _s5_scan(ku, tg, bn, cn, a_re, a_im, n_batch, u_col0, gb=8):
    t = ku.shape[0]
    g, w, _ = tg.shape
    p, ch = SSM_STATE, SSM_GROUP_CH
    n_rows = t // SSM_CHUNK
    lanes = gb * ch
    kernel = functools.partial(_s5_scan_kernel, n_batch=n_batch)
    return pl.pallas_call(
        kernel,
        grid=(g // gb,),
        in_specs=[pl.BlockSpec((t, lanes), lambda i: (0, u_col0 // lanes + i)),
                  pl.BlockSpec((gb, w, w), lambda i: (i, 0, 0)),
                  pl.BlockSpec((gb, 2, p, w), lambda i: (i, 0, 0, 0)),
                  pl.BlockSpec((gb, 2, w, p), lambda i: (i, 0, 0, 0)),
                  pl.BlockSpec((gb, p), lambda i: (i, 0)),
                  pl.BlockSpec((gb, p), lambda i: (i, 0))],
        out_specs=pl.BlockSpec((t, lanes), lambda i: (0, i)),
        out_shape=jax.ShapeDtypeStruct((t, g * ch), F32),
        scratch_shapes=[pltpu.VMEM((t, lanes), F32),
                        pltpu.VMEM((gb, w, n_rows), BF16),
                        pltpu.VMEM((SSM_CHUNK, lanes, n_rows), F32)]
                       + [pltpu.VMEM((gb * n_rows, p), F32) for _ in range(4)],
        compiler_params=pltpu.CompilerParams(dimension_semantics=("parallel",),
                                             vmem_limit_bytes=VMEM_LIMIT),
    )(ku, tg, bn, cn, a_re.reshape(g, p), a_im.reshape(g, p))


def _out_proj_kernel(x_ref, a_ref, y_ref, wglu_ref, bglu_ref, ga_ref, gs_ref, wa_ref, ws_ref, o_ref):
    z = jax.nn.gelu(y_ref[...])
    gl = jnp.dot(z.astype(BF16), wglu_ref[...], preferred_element_type=F32) + bglu_ref[...]
    s = z * jax.nn.sigmoid(gl)
    a_n = _rms_rows(a_ref[...].astype(F32), ga_ref[...]).astype(BF16)
    s_n = _rms_rows(s, gs_ref[...]).astype(BF16)
    mix = (jnp.dot(a_n, wa_ref[...], preferred_element_type=F32)
           + jnp.dot(s_n, ws_ref[...], preferred_element_type=F32))
    o_ref[...] = x_ref[...] + mix


def _out_proj(x2, attn, y, w_glu, b_glu, g_a, g_s, w_out, tm=512):
    t, d = x2.shape
    wa = attn.shape[1]
    ws = y.shape[1]
    assert wa == ws
    row = lambda i: (i, 0)
    fixed = lambda i: (0, 0)
    once = pl.Buffered(1)
    return pl.pallas_call(
        _out_proj_kernel,
        grid=(t // tm,),
        in_specs=[pl.BlockSpec((tm, d), row),
                  pl.BlockSpec((tm, wa), row),
                  pl.BlockSpec((tm, ws), row),
                  pl.BlockSpec((ws, ws), fixed, pipeline_mode=once),
                  pl.BlockSpec((1, ws), fixed),
                  pl.BlockSpec((1, wa), fixed),
                  pl.BlockSpec((1, ws), fixed),
                  pl.BlockSpec((wa, d), fixed, pipeline_mode=once),
                  pl.BlockSpec((ws, d), lambda i: (1, 0), pipeline_mode=once)],
        out_specs=pl.BlockSpec((tm, d), row),
        out_shape=jax.ShapeDtypeStruct((t, d), F32),
        compiler_params=pltpu.CompilerParams(dimension_semantics=("parallel",),
                                             vmem_limit_bytes=VMEM_LIMIT),
    )(x2, attn, y, w_glu, b_glu.reshape(1, ws), g_a.reshape(1, wa), g_s.reshape(1, ws),
      w_out, w_out)


def _ffn_up_kernel(x_ref, halo_ref, g_ref, wg_ref, wv_ref, cwg_ref, cwv_ref, cbg_ref, cbv_ref,
                   wd_ref, o_ref, wdq_ref, h_ref, *, tiles_per_seq):
    tm = x_ref.shape[0]
    wdq_ref[...] = wd_ref[...].astype(wdq_ref.dtype)

    @pl.when(pl.program_id(1) == 0)
    def _():
        keep = jnp.where(pl.program_id(0) % tiles_per_seq == 0, 0.0, 1.0)
        h_ref[pl.ds(0, CONV_HALO), :] = (_rms_rows(halo_ref[...], g_ref[...]) * keep).astype(BF16)
        h_ref[pl.ds(CONV_HALO, tm), :] = _rms_rows(x_ref[...], g_ref[...]).astype(BF16)

    h = h_ref[...]

    def conv(w_ref, cw_ref, cb_ref):
        up = jnp.dot(h, w_ref[...].astype(BF16), preferred_element_type=F32)
        cw = cw_ref[...]
        out = up[CONV_HALO:] * cw[CONV_WIDTH - 1:CONV_WIDTH] + cb_ref[...]
        for j in range(CONV_WIDTH - 1):
            lag = CONV_WIDTH - 1 - j
            out = out + up[CONV_HALO - lag:CONV_HALO - lag + tm] * cw[j:j + 1]
        return out

    gate = conv(wg_ref, cwg_ref, cbg_ref)
    val = conv(wv_ref, cwv_ref, cbv_ref)
    o_ref[...] = (jax.nn.silu(gate) * val).astype(o_ref.dtype)


def _ffn_up(x1, g, w_up, conv_w, conv_b, w_down, seq, tm=1024, tn=512):
    t, d = x1.shape
    f = w_up.shape[1] // 2
    nf = f // tn
    tiles_per_seq = seq // tm
    halo_blocks = tm // CONV_HALO
    kernel = functools.partial(_ffn_up_kernel, tiles_per_seq=tiles_per_seq)
    cb = conv_b.reshape(1, 2 * f)
    wd_slab = pl.BlockSpec((w_down.shape[0] // (t // tm * nf), w_down.shape[1]),
                           lambda i, j: (i * nf + j, 0))
    return pl.pallas_call(
        kernel,
        grid=(t // tm, nf),
        in_specs=[pl.BlockSpec((tm, d), lambda i, j: (i, 0)),
                  pl.BlockSpec((CONV_HALO, d), lambda i, j: (jnp.maximum(i * halo_blocks - 1, 0), 0)),
                  pl.BlockSpec((1, d), lambda i, j: (0, 0)),
                  pl.BlockSpec((d, tn), lambda i, j: (0, j)),
                  pl.BlockSpec((d, tn), lambda i, j: (0, nf + j)),
                  pl.BlockSpec((CONV_WIDTH, tn), lambda i, j: (0, j)),
                  pl.BlockSpec((CONV_WIDTH, tn), lambda i, j: (0, nf + j)),
                  pl.BlockSpec((1, tn), lambda i, j: (0, j)),
                  pl.BlockSpec((1, tn), lambda i, j: (0, nf + j)),
                  wd_slab],
        out_specs=[pl.BlockSpec((tm, tn), lambda i, j: (i, j)), wd_slab],
        out_shape=[jax.ShapeDtypeStruct((t, f), BF16), jax.ShapeDtypeStruct(w_down.shape, BF16)],
        scratch_shapes=[pltpu.VMEM((CONV_HALO + tm, d), BF16)],
        compiler_params=pltpu.CompilerParams(dimension_semantics=("parallel", "arbitrary"),
                                             vmem_limit_bytes=VMEM_LIMIT),
    )(x1, x1, g.reshape(1, d), w_up, w_up, conv_w, conv_w, cb, cb, w_down)


def _ffn_down_kernel(a_ref, w_ref, x_ref, g_ref, o_ref, *, final_norm):
    y = x_ref[...] + jnp.dot(a_ref[...], w_ref[...], preferred_element_type=F32)
    o_ref[...] = _rms_rows(y, g_ref[...]) if final_norm else y


def _ffn_down(act, w_down, x1, g, final_norm, tm=512):
    t, f = act.shape
    d = w_down.shape[1]
    return pl.pallas_call(
        functools.partial(_ffn_down_kernel, final_norm=final_norm),
        grid=(t // tm,),
        in_specs=[pl.BlockSpec((tm, f), lambda i: (i, 0)),
                  pl.BlockSpec((f, d), lambda i: (0, 0), pipeline_mode=pl.Buffered(1)),
                  pl.BlockSpec((tm, d), lambda i: (i, 0)),
                  pl.BlockSpec((1, d), lambda i: (0, 0))],
        out_specs=pl.BlockSpec((tm, d), lambda i: (i, 0)),
        out_shape=jax.ShapeDtypeStruct((t, d), F32),
        compiler_params=pltpu.CompilerParams(dimension_semantics=("parallel",),
                                             vmem_limit_bytes=VMEM_LIMIT_RESIDENT),
    )(act, w_down, x1, g.reshape(1, d))


def kernel(x, norm_mix, w_in, rel_bias_table, ssm_lam_re, ssm_lam_im, ssm_log_dt, ssm_b_re, ssm_b_im, ssm_c_re, ssm_c_im, ssm_d, ssm_w_glu, ssm_b_glu, norm_attn_out, norm_ssm_out, w_out, norm_ffn, w_ffn_up, ffn_conv_w, ffn_conv_b, w_ffn_down, norm_final):
    batch, seq, d_model = x.shape
    depth = w_in.shape[0]
    aw = ATTN_HEADS * HEAD_DIM
    t = batch * seq
    assert seq % MOBA_BLOCK == 0 and seq % SSM_CHUNK == 0
    assert MOBA_BLOCK >= MAX_DISTANCE

    bias_tiles = _bias_tiles(rel_bias_table)
    x2 = x.reshape(t, d_model)
    for l in range(depth):
        tg, bn, cn, a_re, a_im, w_in_b = _s5_tables(
            ssm_log_dt[l], ssm_lam_re[l], ssm_lam_im[l], ssm_b_re[l], ssm_b_im[l], ssm_c_re[l],
            ssm_c_im[l], ssm_d[l], w_in[l], q_cols=aw, q_scale=HEAD_DIM ** -0.5 * LOG2E)

        proj, (w_out_b, w_glu_b) = _in_proj(x2, norm_mix[l], w_in_b, (w_out[l], ssm_w_glu[l]))

        attn = _moba_attention(proj, bias_tiles, batch, seq)
        y = _s5_scan(proj, tg, bn, cn, a_re, a_im, batch, u_col0=3 * aw)

        x2 = _out_proj(x2, attn, y, w_glu_b, ssm_b_glu[l], norm_attn_out[l], norm_ssm_out[l], w_out_b)

        act, w_down_b = _ffn_up(x2, norm_ffn[l], w_ffn_up[l], ffn_conv_w[l], ffn_conv_b[l],
                                w_ffn_down[l], seq)
        x2 = _ffn_down(act, w_down_b, x2, norm_final, final_norm=(l == depth - 1))
    return x2.reshape(batch, seq, d_model)
```

```python
import functools
import math

import jax
import jax.numpy as jnp
from jax import lax
from jax.experimental import pallas as pl
from jax.experimental.pallas import tpu as pltpu

F32 = jnp.float32
BF16 = jnp.bfloat16

ATTN_HEADS = 16
HEAD_DIM = 64
SSM_GROUP_CH = 16
SSM_GROUPS = 64
SSM_STATE = 64
MOBA_BLOCK = 256
MOBA_TOP_K = 3
NUM_BUCKETS = 32
MAX_DISTANCE = 128
CONV_WIDTH = 3
RMS_EPS = 1e-6

SSM_CHUNK = 16
MOBA_HEADS_PER_STEP = 8
MXU_DEPTH = 256
SCAN_UNROLL = 8
CONV_HALO = 16
VMEM_LIMIT = 52 * 1024 * 1024
VMEM_LIMIT_RESIDENT = 58 * 1024 * 1024

NEG_INF = float("-inf")
LOG2E = math.log2(math.e)
V_AUG_ROWS = HEAD_DIM + 16


def _rms_rows(x, g):
    ms = jnp.mean(x * x, axis=-1, keepdims=True)
    return x * lax.rsqrt(ms + RMS_EPS) * g


_NT_DIMS = (((1,), (1,)), ((), ()))


def _in_proj_kernel(x_ref, g_ref, w_ref, *rest):
    n_extra = (len(rest) - 2) // 2
    extra_in, o_ref = rest[:n_extra], rest[n_extra]
    extra_out, h_ref = rest[n_extra + 1:2 * n_extra + 1], rest[-1]

    @pl.when(pl.program_id(1) == 0)
    def _():
        h_ref[...] = _rms_rows(x_ref[...], g_ref[...]).astype(BF16)

    o_ref[...] = jnp.dot(h_ref[...], w_ref[...], preferred_element_type=F32).astype(o_ref.dtype)
    for src, dst in zip(extra_in, extra_out):
        dst[...] = src[...].astype(dst.dtype)


def _in_proj(x2, g, w, cast_along, tm=1024, tn=2048):
    t, d = x2.shape
    n = w.shape[1]
    n_i, n_j = t // tm, n // tn
    steps = n_i * n_j
    slabs = [pl.BlockSpec((a.shape[0] // steps, a.shape[1]), lambda i, j: (i * n_j + j, 0))
             for a in cast_along]
    outs = pl.pallas_call(
        _in_proj_kernel,
        grid=(n_i, n_j),
        in_specs=[pl.BlockSpec((tm, d), lambda i, j: (i, 0)),
                  pl.BlockSpec((1, d), lambda i, j: (0, 0)),
                  pl.BlockSpec((d, tn), lambda i, j: (0, j))] + slabs,
        out_specs=[pl.BlockSpec((tm, tn), lambda i, j: (i, j))] + slabs,
        out_shape=[jax.ShapeDtypeStruct((t, n), BF16)]
                  + [jax.ShapeDtypeStruct(a.shape, BF16) for a in cast_along],
        scratch_shapes=[pltpu.VMEM((tm, d), BF16)],
        compiler_params=pltpu.CompilerParams(dimension_semantics=("parallel", "arbitrary"),
                                             vmem_limit_bytes=VMEM_LIMIT),
    )(x2, g.reshape(1, d), w, *cast_along)
    return outs[0], outs[1:]


def _t5_bucket(dist):
    dist = jnp.maximum(dist, 0)
    max_exact = NUM_BUCKETS // 2
    log_ratio = jnp.log(jnp.maximum(dist, max_exact).astype(F32) / max_exact)
    large = max_exact + (log_ratio / math.log(MAX_DISTANCE / max_exact)
                         * (NUM_BUCKETS - max_exact)).astype(jnp.int32)
    large = jnp.minimum(large, NUM_BUCKETS - 1)
    return jnp.where(dist < max_exact, dist, large)


def _bias_tiles_kernel(tab_ref, o_ref):
    blk = MOBA_BLOCK
    dist = lax.broadcasted_iota(jnp.int32, (1, 2 * blk), 1)
    bucket = _t5_bucket(dist)
    kk = lax.broadcasted_iota(jnp.int32, (blk, blk), 0)
    qq = lax.broadcasted_iota(jnp.int32, (blk, blk), 1)
    for i in range(o_ref.shape[0]):
        h = pl.program_id(0) * o_ref.shape[0] + i
        row = jnp.zeros((1, 2 * blk), F32)
        for b in range(NUM_BUCKETS):
            row = jnp.where(bucket == b, tab_ref[b, h], row)
        row = (row - tab_ref[NUM_BUCKETS - 1, h]) * LOG2E
        pair = pltpu.roll(jnp.broadcast_to(row, (blk, 2 * blk)), 0, 1, stride=1, stride_axis=0)
        o_ref[i, 0] = pair[:, blk:]
        o_ref[i, 1] = jnp.where(qq >= kk, pair[:, :blk], NEG_INF)


def _bias_tiles(table, heads_per_step=4):
    return pl.pallas_call(
        _bias_tiles_kernel,
        grid=(ATTN_HEADS // heads_per_step,),
        in_specs=[pl.BlockSpec(memory_space=pltpu.SMEM)],
        out_specs=pl.BlockSpec((heads_per_step, 2, MOBA_BLOCK, MOBA_BLOCK), lambda h: (h, 0, 0, 0)),
        out_shape=jax.ShapeDtypeStruct((ATTN_HEADS, 2, MOBA_BLOCK, MOBA_BLOCK), F32),
    )(table)


def _moba_kernel(q_ref, qn_ref, k_ref, v_ref, bias_ref, o_ref,
                 kmf_ref, vaug_ref, qm_ref, sa_ref, sb_ref, cma_ref, cmb_ref,
                 m_ref, acc_ref, rb_ref, rbn_ref, ot_ref, par_ref, *, n_blocks):
    qi = pl.program_id(2)
    blk = MOBA_BLOCK
    nh = qm_ref.shape[0]

    @pl.when(qi == 0)
    def _():
        ones = jnp.ones((V_AUG_ROWS - HEAD_DIM, blk), BF16)
        for j in range(n_blocks):
            kb = k_ref[j * blk:(j + 1) * blk, :].astype(F32)
            kmf_ref[pl.ds(j, 1), :] = jnp.sum(kb, axis=0, keepdims=True) * (1.0 / blk)
            vt = v_ref[j].T
            for hh in range(nh):
                vaug_ref[j, hh, 0:HEAD_DIM, :] = vt[hh * HEAD_DIM:(hh + 1) * HEAD_DIM, :]
                vaug_ref[j, hh, HEAD_DIM:V_AUG_ROWS, :] = ones

    kmf = kmf_ref[...]
    km_hi = kmf.astype(BF16)
    km_lo = (kmf - km_hi.astype(F32)).astype(BF16)
    gate_rows = -(-n_blocks // 16) * 16
    if gate_rows > n_blocks:
        km_pad = jnp.zeros((gate_rows - n_blocks, kmf.shape[1]), BF16)
        km_stack = jnp.concatenate([km_hi, km_pad, km_lo, km_pad], axis=0)
    else:
        km_stack = jnp.concatenate([km_hi, km_lo], axis=0)

    q_zero = jnp.zeros((HEAD_DIM, blk), q_ref.dtype)
    blk_idx = lax.broadcasted_iota(jnp.int32, (n_blocks, blk), 0)
    blk_idx_f = blk_idx.astype(F32)
    hps = qm_ref.shape[1] // HEAD_DIM
    slab = lambda hh: slice((hh // hps) * hps * HEAD_DIM, (hh // hps + 1) * hps * HEAD_DIM)

    def select(qt, hh, gate):
        valid = blk_idx < qt
        g = jnp.where(valid, gate, NEG_INF)
        sel = blk_idx == qt
        for _ in range(MOBA_TOP_K):
            top = jnp.max(g, axis=0, keepdims=True)
            first = jnp.min(jnp.where(g == top, blk_idx_f, float(n_blocks)), axis=0, keepdims=True)
            hit = blk_idx_f == first
            sel = sel | (hit & valid)
            g = jnp.where(hit, NEG_INF, g)
        rbn_ref[hh] = jnp.where(sel, 0.0, NEG_INF)

    def group_rows(qt, g):
        ja = qt - 2 * g
        return ja, jnp.maximum(ja - 1, 0)

    def scores(qt, hh, g, s_buf, cm_buf, first_group):
        _, lo = group_rows(qt, g)
        kslab = k_ref[pl.ds(pl.multiple_of(lo * blk, blk), 2 * blk), slab(hh)]
        if first_group:
            kslab = jnp.concatenate([kslab, km_stack[:, slab(hh)]], axis=0)
        s = jnp.dot(kslab, qm_ref[hh], preferred_element_type=F32)
        if first_group:
            select(qt, hh, s[2 * blk:2 * blk + n_blocks]
                   + s[2 * blk + gate_rows:2 * blk + gate_rows + n_blocks])
        for i in range(2):
            si = s[i * blk:(i + 1) * blk]
            if first_group:
                tile = jnp.where(qt == 0, 1, 0) if i == 0 else 1
                si = si + bias_ref[hh, tile]
            s_buf[hh, i] = si
            cm_buf[hh, pl.ds(i, 1), :] = jnp.max(si, axis=0, keepdims=True)

    def prepare(qt, q2, hh, s_buf, cm_buf):
        vrows = slice(hh * HEAD_DIM, (hh + 1) * HEAD_DIM)
        qm_ref[hh] = jnp.concatenate(
            [q_zero] * (hh % hps) + [q2[vrows]] + [q_zero] * (hps - 1 - hh % hps), axis=0)
        scores(qt, hh, 0, s_buf, cm_buf, True)

    def attend(hh, g, s_buf, cm_buf):
        ja, lo = group_rows(qi, g)
        masks = [rb_ref[hh, pl.ds(lo, 1), :],
                 jnp.where(ja >= 1, rb_ref[hh, pl.ds(lo + 1, 1), :], NEG_INF)]
        m_old = m_ref[hh]
        m_new = m_old
        for i in range(2):
            m_new = jnp.maximum(m_new, cm_buf[hh, pl.ds(i, 1), :] + masks[i])
        acc = jnp.exp2(m_old - m_new) * acc_ref[hh]
        for i in range(2):
            p = jnp.exp2((s_buf[hh, i] - m_new).astype(BF16))
            pv = jnp.dot(vaug_ref[lo + i, hh], p, preferred_element_type=F32)
            acc = acc + jnp.where(masks[i] == 0.0, pv, 0.0)
        acc_ref[hh] = acc
        m_ref[hh] = m_new

    n_more = qi // 2
    buf_a, buf_b = (sa_ref, cma_ref), (sb_ref, cmb_ref)

    @pl.when(qi == 0)
    def _():
        q2 = q_ref[0].T
        for hh in range(nh):
            prepare(qi, q2, hh, *buf_a)
        par_ref[0] = 0

    rb_ref[...] = rbn_ref[...]
    for hh in range(nh):
        m_ref[hh] = jnp.full((1, blk), NEG_INF, F32)
        acc_ref[hh] = jnp.zeros((V_AUG_ROWS, blk), F32)
    par = par_ref[0]

    def stage(it, cur, nxt):
        scores(qi, 0, it + 1, *nxt, False)
        for hh in range(nh):
            if hh + 1 < nh:
                scores(qi, hh + 1, it + 1, *nxt, False)
            attend(hh, it, *cur)

    def step(it, carry):
        @pl.when((it + par) % 2 == 0)
        def _():
            stage(it, buf_a, buf_b)

        @pl.when((it + par) % 2 == 1)
        def _():
            stage(it, buf_b, buf_a)

        return carry

    lax.fori_loop(0, n_more, step, 0)

    last = (n_more + par) % 2
    qn = jnp.minimum(qi + 1, n_blocks - 1)

    def last_stage(cur, nxt):
        q2 = qn_ref[0].T
        prepare(qn, q2, 0, *nxt)
        for hh in range(nh):
            if hh + 1 < nh:
                prepare(qn, q2, hh + 1, *nxt)
            attend(hh, n_more, *cur)

    @pl.when(last == 0)
    def _():
        last_stage(buf_a, buf_b)

    @pl.when(last == 1)
    def _():
        last_stage(buf_b, buf_a)

    par_ref[0] = 1 - last

    for hh in range(nh):
        acc = acc_ref[hh]
        ot_ref[hh * HEAD_DIM:(hh + 1) * HEAD_DIM, :] = (
            acc[0:HEAD_DIM] * (1.0 / acc[HEAD_DIM:HEAD_DIM + 1])).astype(ot_ref.dtype)

    o_ref[...] = ot_ref[...].T


def _moba_attention(proj, bias_tiles, batch, seq):
    proj3 = proj.reshape(proj.shape[0] // MOBA_BLOCK, MOBA_BLOCK, proj.shape[1])
    n_blocks = seq // MOBA_BLOCK
    nh = MOBA_HEADS_PER_STEP
    n_hg = ATTN_HEADS // nh
    hg_w = nh * HEAD_DIM
    kernel = functools.partial(_moba_kernel, n_blocks=n_blocks)
    return pl.pallas_call(
        kernel,
        grid=(batch, n_hg, n_blocks),
        in_specs=[
            pl.BlockSpec((1, MOBA_BLOCK, hg_w), lambda b, hg, qi: (b * n_blocks + qi, 0, hg)),
            pl.BlockSpec((1, MOBA_BLOCK, hg_w),
                         lambda b, hg, qi: (b * n_blocks + jnp.minimum(qi + 1, n_blocks - 1), 0, hg)),
            pl.BlockSpec((seq, hg_w), lambda b, hg, qi: (b, n_hg + hg)),
            pl.BlockSpec((n_blocks, MOBA_BLOCK, hg_w), lambda b, hg, qi: (b, 0, 2 * n_hg + hg)),
            pl.BlockSpec((nh, 2, MOBA_BLOCK, MOBA_BLOCK), lambda b, hg, qi: (hg, 0, 0, 0)),
        ],
        out_specs=pl.BlockSpec((MOBA_BLOCK, hg_w), lambda b, hg, qi: (b * n_blocks + qi, hg)),
        out_shape=jax.ShapeDtypeStruct((batch * seq, ATTN_HEADS * HEAD_DIM), BF16),
        scratch_shapes=[
            pltpu.VMEM((n_blocks, hg_w), F32),
            pltpu.VMEM((n_blocks, nh, V_AUG_ROWS, MOBA_BLOCK), BF16),
            pltpu.VMEM((nh, min(hg_w, MXU_DEPTH), MOBA_BLOCK), BF16),
            pltpu.VMEM((nh, 2, MOBA_BLOCK, MOBA_BLOCK), F32),
            pltpu.VMEM((nh, 2, MOBA_BLOCK, MOBA_BLOCK), F32),
            pltpu.VMEM((nh, 8, MOBA_BLOCK), F32),
            pltpu.VMEM((nh, 8, MOBA_BLOCK), F32),
            pltpu.VMEM((nh, 1, MOBA_BLOCK), F32),
            pltpu.VMEM((nh, V_AUG_ROWS, MOBA_BLOCK), F32),
            pltpu.VMEM((nh, n_blocks, MOBA_BLOCK), F32),
            pltpu.VMEM((nh, n_blocks, MOBA_BLOCK), F32),
            pltpu.VMEM((hg_w, MOBA_BLOCK), BF16),
            pltpu.SMEM((1,), jnp.int32),
        ],
        compiler_params=pltpu.CompilerParams(
            dimension_semantics=("parallel", "parallel", "arbitrary"),
            vmem_limit_bytes=VMEM_LIMIT),
    )(proj3, proj3, proj, proj3, bias_tiles)


def _hdot_nt(a, b):
    def split(x):
        hi = x.astype(BF16)
        return hi, (x - hi.astype(F32)).astype(BF16)

    def nt(u, v):
        return lax.dot_general(u, v, _NT_DIMS, preferred_element_type=F32)

    a_hi, a_lo = split(a)
    b_hi, b_lo = split(b)
    return nt(a_hi, b_hi) + nt(a_hi, b_lo) + nt(a_lo, b_hi)


def _repeat_rows(x, n):
    return jnp.concatenate([jnp.broadcast_to(x[i:i + 1], (n, x.shape[1]))
                            for i in range(x.shape[0])], axis=0)


def _tile_rows(x, n):
    return jnp.concatenate([x] * n, axis=0)


def _s5_tables_kernel(logdt_ref, lre_ref, lim_ref, bre_ref, bim_ref, cre_ref, cim_ref, d_ref,
                      win_ref, tg_ref, bn_ref, cn_ref, are_ref, aim_ref, wq_ref, *, q_cols, q_scale):
    wq_ref[:, :q_cols] = (win_ref[:, :q_cols] * q_scale).astype(wq_ref.dtype)
    wq_ref[:, q_cols:] = win_ref[:, q_cols:].astype(wq_ref.dtype)

    lc, ch, p = SSM_CHUNK, SSM_GROUP_CH, SSM_STATE
    w = lc * ch
    rr = lax.broadcasted_iota(jnp.int32, (w, w), 0)
    cc = lax.broadcasted_iota(jnp.int32, (w, w), 1)
    causal = rr // ch >= cc // ch
    diag = rr == cc
    e_r = lax.broadcasted_iota(jnp.int32, (ch, w), 0)
    e_c = lax.broadcasted_iota(jnp.int32, (ch, w), 1)
    lane_tile = jnp.where(e_c % ch == e_r, 1.0, 0.0).astype(F32)
    tau = lax.broadcasted_iota(jnp.int32, (2 * lc, p), 0).astype(F32)
    mid = lc // 2

    for k in range(tg_ref.shape[0]):
        lam_re = lre_ref[k]
        lam_im = lim_ref[k]
        dt = jnp.exp(logdt_ref[k])
        lr = lam_re * dt
        li = lam_im * dt

        mag = jnp.exp(lr * tau)
        pos_re, pos_im = mag * jnp.cos(li * tau), mag * jnp.sin(li * tau)
        tau_c = tau[:lc] - mid
        mag_c = jnp.exp(lr * tau_c)
        inv_c = jnp.exp(-lr * tau_c)
        cs, sn = jnp.cos(li * tau_c), jnp.sin(li * tau_c)
        fwd_re, fwd_im = mag_c * cs, mag_c * sn
        neg_re, neg_im = inv_c * cs, -inv_c * sn

        lam1_re, lam1_im = pos_re[1:2], pos_im[1:2]
        lamb_re, lamb_im = pos_re[lc - 1 - mid:lc - mid], pos_im[lc - 1 - mid:lc - mid]
        lamc_re, lamc_im = pos_re[mid + 1:mid + 2], pos_im[mid + 1:mid + 2]
        are_ref[k] = pos_re[lc:lc + 1]
        aim_ref[k] = pos_im[lc:lc + 1]

        num_re, num_im = lam1_re - 1.0, lam1_im
        den = lam_re * lam_re + lam_im * lam_im
        coef_re = (num_re * lam_re + num_im * lam_im) / den
        coef_im = (num_im * lam_re - num_re * lam_im) / den
        bt_re, bt_im = bre_ref[k], bim_ref[k]
        bb_re = coef_re * bt_re - coef_im * bt_im
        bb_im = coef_re * bt_im + coef_im * bt_re

        bbt_re, bbt_im = _tile_rows(bb_re, lc), _tile_rows(bb_im, lc)
        ngx_re, ngx_im = _repeat_rows(neg_re, ch), _repeat_rows(neg_im, ch)
        bneg_re = bbt_re * ngx_re - bbt_im * ngx_im
        bneg_im = bbt_re * ngx_im + bbt_im * ngx_re

        ct_re, ct_im = _tile_rows(cre_ref[k], lc), _tile_rows(cim_ref[k], lc)
        psx_re, psx_im = _repeat_rows(fwd_re, ch), _repeat_rows(fwd_im, ch)
        cpos_re = ct_re * psx_re - ct_im * psx_im
        cpos_im = ct_re * psx_im + ct_im * psx_re

        raw = _hdot_nt(cpos_re, bneg_re) - _hdot_nt(cpos_im, bneg_im)
        d_lanes = jnp.dot(jnp.broadcast_to(d_ref[k], (8, ch)), lane_tile,
                          preferred_element_type=F32, precision=lax.Precision.HIGHEST)[0:1]
        tg = jnp.where(causal, raw, 0.0) + jnp.where(diag, d_lanes, 0.0)
        tg_ref[k] = tg.astype(tg_ref.dtype)

        bn_ref[k, 0] = (bneg_re * lamb_re - bneg_im * lamb_im).T.astype(bn_ref.dtype)
        bn_ref[k, 1] = (bneg_re * lamb_im + bneg_im * lamb_re).T.astype(bn_ref.dtype)
        cp_re = cpos_re * lamc_re - cpos_im * lamc_im
        cp_im = cpos_re * lamc_im + cpos_im * lamc_re
        cn_ref[k, 0] = cp_re.astype(cn_ref.dtype)
        cn_ref[k, 1] = (-cp_im).astype(cn_ref.dtype)


def _s5_tables(log_dt, lam_re, lam_im, b_re, b_im, c_re, c_im, d_skip, w_in, q_cols, q_scale, gb=8):
    g, p, ch = SSM_GROUPS, SSM_STATE, SSM_GROUP_CH
    w = SSM_CHUNK * ch
    steps = g // gb
    d, n = w_in.shape
    row = lambda a, n: a.reshape(g, 1, n)
    spec3 = lambda s1, s2: pl.BlockSpec((gb, s1, s2), lambda i: (i, 0, 0))
    spec4 = lambda s1, s2: pl.BlockSpec((gb, 2, s1, s2), lambda i: (i, 0, 0, 0))
    slab = pl.BlockSpec((d // steps, n), lambda i: (i, 0))
    return pl.pallas_call(
        functools.partial(_s5_tables_kernel, q_cols=q_cols, q_scale=q_scale),
        grid=(steps,),
        in_specs=[spec3(1, 1), spec3(1, p), spec3(1, p), spec3(ch, p), spec3(ch, p),
                  spec3(ch, p), spec3(ch, p), spec3(1, ch), slab],
        out_specs=[spec3(w, w), spec4(p, w), spec4(w, p), spec3(1, p), spec3(1, p), slab],
        out_shape=[jax.ShapeDtypeStruct((g, w, w), BF16),
                   jax.ShapeDtypeStruct((g, 2, p, w), BF16),
                   jax.ShapeDtypeStruct((g, 2, w, p), BF16),
                   jax.ShapeDtypeStruct((g, 1, p), F32),
                   jax.ShapeDtypeStruct((g, 1, p), F32),
                   jax.ShapeDtypeStruct((d, n), BF16)],
        compiler_params=pltpu.CompilerParams(dimension_semantics=("parallel",)),
    )(row(log_dt, 1), row(lam_re, p), row(lam_im, p),
      jnp.swapaxes(b_re, 1, 2), jnp.swapaxes(b_im, 1, 2), c_re, c_im, row(d_skip, ch), w_in)


def _s5_scan_kernel(u_ref, tg_ref, bn_ref, cn_ref, are_ref, aim_ref, y_ref,
                    uf_ref, v_ref, yt_ref, sre_ref, sim_ref, xre_ref, xim_ref, *, n_batch):
    lc, ch = SSM_CHUNK, SSM_GROUP_CH
    gb = tg_ref.shape[0]
    n_rows = u_ref.shape[0] // lc
    n_chunks = n_rows // n_batch
    w = lc * ch

    uf_ref[...] = u_ref[...].astype(F32)
    for s in range(lc):
        ust = uf_ref[pl.ds(s, n_rows, stride=lc), :].T
        for k in range(gb):
            v_ref[k, s * ch:(s + 1) * ch, :] = ust[k * ch:(k + 1) * ch, :].astype(v_ref.dtype)

    for k in range(gb):
        u = v_ref[k]
        rows = pl.ds(k, n_rows, stride=gb)
        sre_ref[rows, :] = jnp.dot(bn_ref[k, 0], u, preferred_element_type=F32).T
        sim_ref[rows, :] = jnp.dot(bn_ref[k, 1], u, preferred_element_type=F32).T

    a_re = are_ref[...]
    a_im = aim_ref[...]

    def step(c, carry):
        new = []
        for b in range(n_batch):
            x_re, x_im = carry[2 * b], carry[2 * b + 1]
            rows = pl.ds(pl.multiple_of((b * n_chunks + c) * gb, gb), gb)
            xre_ref[rows, :] = x_re
            xim_ref[rows, :] = x_im
            new.append(a_re * x_re - a_im * x_im + sre_ref[rows, :])
            new.append(a_re * x_im + a_im * x_re + sim_ref[rows, :])
        return tuple(new)

    zero = jnp.zeros(a_re.shape, F32)
    lax.fori_loop(0, n_chunks, step, (zero,) * (2 * n_batch), unroll=SCAN_UNROLL)

    for k in range(gb):
        rows = pl.ds(k, n_rows, stride=gb)
        y = jnp.dot(tg_ref[k], v_ref[k], preferred_element_type=F32)
        y = y + lax.dot_general(cn_ref[k, 0], xre_ref[rows, :].astype(BF16), _NT_DIMS,
                                preferred_element_type=F32)
        y = y + lax.dot_general(cn_ref[k, 1], xim_ref[rows, :].astype(BF16), _NT_DIMS,
                                preferred_element_type=F32)
        for s in range(lc):
            yt_ref[s, k * ch:(k + 1) * ch, :] = y[s * ch:(s + 1) * ch, :]

    for s in range(lc):
        y_ref[pl.ds(s, n_rows, stride=lc), :] = yt_ref[s].T.astype(y_ref.dtype)


def _s5_scan(ku, tg, bn, cn, a_re, a_im, n_batch, u_col0, gb=8):
    t = ku.shape[0]
    g, w, _ = tg.shape
    p, ch = SSM_STATE, SSM_GROUP_CH
    n_rows = t // SSM_CHUNK
    lanes = gb * ch
    kernel = functools.partial(_s5_scan_kernel, n_batch=n_batch)
    return pl.pallas_call(
        kernel,
        grid=(g // gb,),
        in_specs=[pl.BlockSpec((t, lanes), lambda i: (0, u_col0 // lanes + i)),
                  pl.BlockSpec((gb, w, w), lambda i: (i, 0, 0)),
                  pl.BlockSpec((gb, 2, p, w), lambda i: (i, 0, 0, 0)),
                  pl.BlockSpec((gb, 2, w, p), lambda i: (i, 0, 0, 0)),
                  pl.BlockSpec((gb, p), lambda i: (i, 0)),
                  pl.BlockSpec((gb, p), lambda i: (i, 0))],
        out_specs=pl.BlockSpec((t, lanes), lambda i: (0, i)),
        out_shape=jax.ShapeDtypeStruct((t, g * ch), F32),
        scratch_shapes=[pltpu.VMEM((t, lanes), F32),
                        pltpu.VMEM((gb, w, n_rows), BF16),
                        pltpu.VMEM((SSM_CHUNK, lanes, n_rows), F32)]
                       + [pltpu.VMEM((gb * n_rows, p), F32) for _ in range(4)],
        compiler_params=pltpu.CompilerParams(dimension_semantics=("parallel",),
                                             vmem_limit_bytes=VMEM_LIMIT),
    )(ku, tg, bn, cn, a_re.reshape(g, p), a_im.reshape(g, p))


def _out_proj_kernel(x_ref, a_ref, y_ref, wglu_ref, bglu_ref, ga_ref, gs_ref, wa_ref, ws_ref, o_ref):
    z = jax.nn.gelu(y_ref[...])
    gl = jnp.dot(z.astype(BF16), wglu_ref[...], preferred_element_type=F32) + bglu_ref[...]
    s = z * jax.nn.sigmoid(gl)
    a_n = _rms_rows(a_ref[...].astype(F32), ga_ref[...]).astype(BF16)
    s_n = _rms_rows(s, gs_ref[...]).astype(BF16)
    mix = (jnp.dot(a_n, wa_ref[...], preferred_element_type=F32)
           + jnp.dot(s_n, ws_ref[...], preferred_element_type=F32))
    o_ref[...] = x_ref[...] + mix


def _out_proj(x2, attn, y, w_glu, b_glu, g_a, g_s, w_out, tm=512):
    t, d = x2.shape
    wa = attn.shape[1]
    ws = y.shape[1]
    assert wa == ws
    row = lambda i: (i, 0)
    fixed = lambda i: (0, 0)
    once = pl.Buffered(1)
    return pl.pallas_call(
        _out_proj_kernel,
        grid=(t // tm,),
        in_specs=[pl.BlockSpec((tm, d), row),
                  pl.BlockSpec((tm, wa), row),
                  pl.BlockSpec((tm, ws), row),
                  pl.BlockSpec((ws, ws), fixed, pipeline_mode=once),
                  pl.BlockSpec((1, ws), fixed),
                  pl.BlockSpec((1, wa), fixed),
                  pl.BlockSpec((1, ws), fixed),
                  pl.BlockSpec((wa, d), fixed, pipeline_mode=once),
                  pl.BlockSpec((ws, d), lambda i: (1, 0), pipeline_mode=once)],
        out_specs=pl.BlockSpec((tm, d), row),
        out_shape=jax.ShapeDtypeStruct((t, d), F32),
        compiler_params=pltpu.CompilerParams(dimension_semantics=("parallel",),
                                             vmem_limit_bytes=VMEM_LIMIT),
    )(x2, attn, y, w_glu, b_glu.reshape(1, ws), g_a.reshape(1, wa), g_s.reshape(1, ws),
      w_out, w_out)


def _ffn_up_kernel(x_ref, halo_ref, g_ref, wg_ref, wv_ref, cwg_ref, cwv_ref, cbg_ref, cbv_ref,
                   wd_ref, o_ref, wdq_ref, h_ref, *, tiles_per_seq):
    tm = x_ref.shape[0]
    wdq_ref[...] = wd_ref[...].astype(wdq_ref.dtype)

    @pl.when(pl.program_id(1) == 0)
    def _():
        keep = jnp.where(pl.program_id(0) % tiles_per_seq == 0, 0.0, 1.0)
        h_ref[pl.ds(0, CONV_HALO), :] = (_rms_rows(halo_ref[...], g_ref[...]) * keep).astype(BF16)
        h_ref[pl.ds(CONV_HALO, tm), :] = _rms_rows(x_ref[...], g_ref[...]).astype(BF16)

    h = h_ref[...]

    def conv(w_ref, cw_ref, cb_ref):
        up = jnp.dot(h, w_ref[...].astype(BF16), preferred_element_type=F32)
        cw = cw_ref[...]
        out = up[CONV_HALO:] * cw[CONV_WIDTH - 1:CONV_WIDTH] + cb_ref[...]
        for j in range(CONV_WIDTH - 1):
            lag = CONV_WIDTH - 1 - j
            out = out + up[CONV_HALO - lag:CONV_HALO - lag + tm] * cw[j:j + 1]
        return out

    gate = conv(wg_ref, cwg_ref, cbg_ref)
    val = conv(wv_ref, cwv_ref, cbv_ref)
    o_ref[...] = (jax.nn.silu(gate) * val).astype(o_ref.dtype)


def _ffn_up(x1, g, w_up, conv_w, conv_b, w_down, seq, tm=1024, tn=512):
    t, d = x1.shape
    f = w_up.shape[1] // 2
    nf = f // tn
    tiles_per_seq = seq // tm
    halo_blocks = tm // CONV_HALO
    kernel = functools.partial(_ffn_up_kernel, tiles_per_seq=tiles_per_seq)
    cb = conv_b.reshape(1, 2 * f)
    wd_slab = pl.BlockSpec((w_down.shape[0] // (t // tm * nf), w_down.shape[1]),
                           lambda i, j: (i * nf + j, 0))
    return pl.pallas_call(
        kernel,
        grid=(t // tm, nf),
        in_specs=[pl.BlockSpec((tm, d), lambda i, j: (i, 0)),
                  pl.BlockSpec((CONV_HALO, d), lambda i, j: (jnp.maximum(i * halo_blocks - 1, 0), 0)),
                  pl.BlockSpec((1, d), lambda i, j: (0, 0)),
                  pl.BlockSpec((d, tn), lambda i, j: (0, j)),
                  pl.BlockSpec((d, tn), lambda i, j: (0, nf + j)),
                  pl.BlockSpec((CONV_WIDTH, tn), lambda i, j: (0, j)),
                  pl.BlockSpec((CONV_WIDTH, tn), lambda i, j: (0, nf + j)),
                  pl.BlockSpec((1, tn), lambda i, j: (0, j)),
                  pl.BlockSpec((1, tn), lambda i, j: (0, nf + j)),
                  wd_slab],
        out_specs=[pl.BlockSpec((tm, tn), lambda i, j: (i, j)), wd_slab],
        out_shape=[jax.ShapeDtypeStruct((t, f), BF16), jax.ShapeDtypeStruct(w_down.shape, BF16)],
        scratch_shapes=[pltpu.VMEM((CONV_HALO + tm, d), BF16)],
        compiler_params=pltpu.CompilerParams(dimension_semantics=("parallel", "arbitrary"),
                                             vmem_limit_bytes=VMEM_LIMIT),
    )(x1, x1, g.reshape(1, d), w_up, w_up, conv_w, conv_w, cb, cb, w_down)


def _ffn_down_kernel(a_ref, w_ref, x_ref, g_ref, o_ref, *, final_norm):
    y = x_ref[...] + jnp.dot(a_ref[...], w_ref[...], preferred_element_type=F32)
    o_ref[...] = _rms_rows(y, g_ref[...]) if final_norm else y


def _ffn_down(act, w_down, x1, g, final_norm, tm=512):
    t, f = act.shape
    d = w_down.shape[1]
    return pl.pallas_call(
        functools.partial(_ffn_down_kernel, final_norm=final_norm),
        grid=(t // tm,),
        in_specs=[pl.BlockSpec((tm, f), lambda i: (i, 0)),
                  pl.BlockSpec((f, d), lambda i: (0, 0), pipeline_mode=pl.Buffered(1)),
                  pl.BlockSpec((tm, d), lambda i: (i, 0)),
                  pl.BlockSpec((1, d), lambda i: (0, 0))],
        out_specs=pl.BlockSpec((tm, d), lambda i: (i, 0)),
        out_shape=jax.ShapeDtypeStruct((t, d), F32),
        compiler_params=pltpu.CompilerParams(dimension_semantics=("parallel",),
                                             vmem_limit_bytes=VMEM_LIMIT_RESIDENT),
    )(act, w_down, x1, g.reshape(1, d))


def kernel(x, norm_mix, w_in, rel_bias_table, ssm_lam_re, ssm_lam_im, ssm_log_dt, ssm_b_re, ssm_b_im, ssm_c_re, ssm_c_im, ssm_d, ssm_w_glu, ssm_b_glu, norm_attn_out, norm_ssm_out, w_out, norm_ffn, w_ffn_up, ffn_conv_w, ffn_conv_b, w_ffn_down, norm_final):
    batch, seq, d_model = x.shape
    depth = w_in.shape[0]
    aw = ATTN_HEADS * HEAD_DIM
    t = batch * seq
    assert seq % MOBA_BLOCK == 0 and seq % SSM_CHUNK == 0
    assert MOBA_BLOCK >= MAX_DISTANCE

    bias_tiles = _bias_tiles(rel_bias_table)
    x2 = x.reshape(t, d_model)
    for l in range(depth):
        tg, bn, cn, a_re, a_im, w_in_b = _s5_tables(
            ssm_log_dt[l], ssm_lam_re[l], ssm_lam_im[l], ssm_b_re[l], ssm_b_im[l], ssm_c_re[l],
            ssm_c_im[l], ssm_d[l], w_in[l], q_cols=aw, q_scale=HEAD_DIM ** -0.5 * LOG2E)

        proj, (w_out_b, w_glu_b) = _in_proj(x2, norm_mix[l], w_in_b, (w_out[l], ssm_w_glu[l]))

        attn = _moba_attention(proj, bias_tiles, batch, seq)
        y = _s5_scan(proj, tg, bn, cn, a_re, a_im, batch, u_col0=3 * aw)

        x2 = _out_proj(x2, attn, y, w_glu_b, ssm_b_glu[l], norm_attn_out[l], norm_ssm_out[l], w_out_b)

        act, w_down_b = _ffn_up(x2, norm_ffn[l], w_ffn_up[l], ffn_conv_w[l], ffn_conv_b[l],
                                w_ffn_down[l], seq)
        x2 = _ffn_down(act, w_down_b, x2, norm_final, final_norm=(l == depth - 1))
    return x2.reshape(batch, seq, d_model)
```

```python
import functools
import math

import jax
import jax.numpy as jnp
from jax import lax
from jax.experimental import pallas as pl
from jax.experimental.pallas import tpu as pltpu

F32 = jnp.float32
BF16 = jnp.bfloat16

ATTN_HEADS = 16
HEAD_DIM = 64
SSM_GROUP_CH = 16
SSM_GROUPS = 64
SSM_STATE = 64
MOBA_BLOCK = 256
MOBA_TOP_K = 3
NUM_BUCKETS = 32
MAX_DISTANCE = 128
CONV_WIDTH = 3
RMS_EPS = 1e-6

SSM_CHUNK = 16
MOBA_HEADS_PER_STEP = 8
MXU_DEPTH = 256
SCAN_UNROLL = 8
CONV_HALO = 16
VMEM_LIMIT = 52 * 1024 * 1024
VMEM_LIMIT_RESIDENT = 58 * 1024 * 1024

NEG_INF = float("-inf")
LOG2E = math.log2(math.e)
V_AUG_ROWS = HEAD_DIM + 16


def _rms_rows(x, g):
    ms = jnp.mean(x * x, axis=-1, keepdims=True)
    return x * lax.rsqrt(ms + RMS_EPS) * g


_NT_DIMS = (((1,), (1,)), ((), ()))


def _in_proj_kernel(x_ref, g_ref, w_ref, *rest):
    n_extra = (len(rest) - 2) // 2
    extra_in, o_ref = rest[:n_extra], rest[n_extra]
    extra_out, h_ref = rest[n_extra + 1:2 * n_extra + 1], rest[-1]

    @pl.when(pl.program_id(1) == 0)
    def _():
        h_ref[...] = _rms_rows(x_ref[...], g_ref[...]).astype(BF16)

    o_ref[...] = jnp.dot(h_ref[...], w_ref[...], preferred_element_type=F32).astype(o_ref.dtype)
    for src, dst in zip(extra_in, extra_out):
        dst[...] = src[...].astype(dst.dtype)


def _in_proj(x2, g, w, cast_along, tm=1024, tn=2048):
    t, d = x2.shape
    n = w.shape[1]
    n_i, n_j = t // tm, n // tn
    steps = n_i * n_j
    slabs = [pl.BlockSpec((a.shape[0] // steps, a.shape[1]), lambda i, j: (i * n_j + j, 0))
             for a in cast_along]
    outs = pl.pallas_call(
        _in_proj_kernel,
        grid=(n_i, n_j),
        in_specs=[pl.BlockSpec((tm, d), lambda i, j: (i, 0)),
                  pl.BlockSpec((1, d), lambda i, j: (0, 0)),
                  pl.BlockSpec((d, tn), lambda i, j: (0, j))] + slabs,
        out_specs=[pl.BlockSpec((tm, tn), lambda i, j: (i, j))] + slabs,
        out_shape=[jax.ShapeDtypeStruct((t, n), BF16)]
                  + [jax.ShapeDtypeStruct(a.shape, BF16) for a in cast_along],
        scratch_shapes=[pltpu.VMEM((tm, d), BF16)],
        compiler_params=pltpu.CompilerParams(dimension_semantics=("parallel", "arbitrary"),
                                             vmem_limit_bytes=VMEM_LIMIT),
    )(x2, g.reshape(1, d), w, *cast_along)
    return outs[0], outs[1:]


def _t5_bucket(dist):
    dist = jnp.maximum(dist, 0)
    max_exact = NUM_BUCKETS // 2
    log_ratio = jnp.log(jnp.maximum(dist, max_exact).astype(F32) / max_exact)
    large = max_exact + (log_ratio / math.log(MAX_DISTANCE / max_exact)
                         * (NUM_BUCKETS - max_exact)).astype(jnp.int32)
    large = jnp.minimum(large, NUM_BUCKETS - 1)
    return jnp.where(dist < max_exact, dist, large)


def _bias_tiles_kernel(tab_ref, o_ref):
    blk = MOBA_BLOCK
    dist = lax.broadcasted_iota(jnp.int32, (1, 2 * blk), 1)
    bucket = _t5_bucket(dist)
    kk = lax.broadcasted_iota(jnp.int32, (blk, blk), 0)
    qq = lax.broadcasted_iota(jnp.int32, (blk, blk), 1)
    for i in range(o_ref.shape[0]):
        h = pl.program_id(0) * o_ref.shape[0] + i
        row = jnp.zeros((1, 2 * blk), F32)
        for b in range(NUM_BUCKETS):
            row = jnp.where(bucket == b, tab_ref[b, h], row)
        row = (row - tab_ref[NUM_BUCKETS - 1, h]) * LOG2E
        pair = pltpu.roll(jnp.broadcast_to(row, (blk, 2 * blk)), 0, 1, stride=1, stride_axis=0)
        o_ref[i, 0] = pair[:, blk:]
        o_ref[i, 1] = jnp.where(qq >= kk, pair[:, :blk], NEG_INF)


def _bias_tiles(table, heads_per_step=4):
    return pl.pallas_call(
        _bias_tiles_kernel,
        grid=(ATTN_HEADS // heads_per_step,),
        in_specs=[pl.BlockSpec(memory_space=pltpu.SMEM)],
        out_specs=pl.BlockSpec((heads_per_step, 2, MOBA_BLOCK, MOBA_BLOCK), lambda h: (h, 0, 0, 0)),
        out_shape=jax.ShapeDtypeStruct((ATTN_HEADS, 2, MOBA_BLOCK, MOBA_BLOCK), F32),
    )(table)


def _moba_kernel(q_ref, qn_ref, k_ref, v_ref, bias_ref, o_ref,
                 kmf_ref, vaug_ref, qm_ref, sa_ref, sb_ref, cma_ref, cmb_ref,
                 m_ref, acc_ref, rb_ref, rbn_ref, ot_ref, par_ref, *, n_blocks):
    qi = pl.program_id(2)
    blk = MOBA_BLOCK
    nh = qm_ref.shape[0]

    @pl.when(qi == 0)
    def _():
        ones = jnp.ones((V_AUG_ROWS - HEAD_DIM, blk), BF16)
        for j in range(n_blocks):
            kb = k_ref[j * blk:(j + 1) * blk, :].astype(F32)
            kmf_ref[pl.ds(j, 1), :] = jnp.sum(kb, axis=0, keepdims=True) * (1.0 / blk)
            vt = v_ref[j].T
            for hh in range(nh):
                vaug_ref[j, hh, 0:HEAD_DIM, :] = vt[hh * HEAD_DIM:(hh + 1) * HEAD_DIM, :]
                vaug_ref[j, hh, HEAD_DIM:V_AUG_ROWS, :] = ones

    kmf = kmf_ref[...]
    km_hi = kmf.astype(BF16)
    km_lo = (kmf - km_hi.astype(F32)).astype(BF16)
    gate_rows = -(-n_blocks // 16) * 16
    if gate_rows > n_blocks:
        km_pad = jnp.zeros((gate_rows - n_blocks, kmf.shape[1]), BF16)
        km_stack = jnp.concatenate([km_hi, km_pad, km_lo, km_pad], axis=0)
    else:
        km_stack = jnp.concatenate([km_hi, km_lo], axis=0)

    q_zero = jnp.zeros((HEAD_DIM, blk), q_ref.dtype)
    blk_idx = lax.broadcasted_iota(jnp.int32, (n_blocks, blk), 0)
    blk_idx_f = blk_idx.astype(F32)
    hps = qm_ref.shape[1] // HEAD_DIM
    slab = lambda hh: slice((hh // hps) * hps * HEAD_DIM, (hh // hps + 1) * hps * HEAD_DIM)

    def select(qt, hh, gate):
        valid = blk_idx < qt
        g = jnp.where(valid, gate, NEG_INF)
        sel = blk_idx == qt
        for _ in range(MOBA_TOP_K):
            top = jnp.max(g, axis=0, keepdims=True)
            first = jnp.min(jnp.where(g == top, blk_idx_f, float(n_blocks)), axis=0, keepdims=True)
            hit = blk_idx_f == first
            sel = sel | (hit & valid)
            g = jnp.where(hit, NEG_INF, g)
        rbn_ref[hh] = jnp.where(sel, 0.0, NEG_INF)

    def group_rows(qt, g):
        ja = qt - 2 * g
        return ja, jnp.maximum(ja - 1, 0)

    def scores(qt, hh, g, s_buf, cm_buf, first_group):
        _, lo = group_rows(qt, g)
        kslab = k_ref[pl.ds(pl.multiple_of(lo * blk, blk), 2 * blk), slab(hh)]
        if first_group:
            kslab = jnp.concatenate([kslab, km_stack[:, slab(hh)]], axis=0)
        s = jnp.dot(kslab, qm_ref[hh], preferred_element_type=F32)
        if first_group:
            select(qt, hh, s[2 * blk:2 * blk + n_blocks]
                   + s[2 * blk + gate_rows:2 * blk + gate_rows + n_blocks])
        for i in range(2):
            si = s[i * blk:(i + 1) * blk]
            if first_group:
                tile = jnp.where(qt == 0, 1, 0) if i == 0 else 1
                si = si + bias_ref[hh, tile]
            s_buf[hh, i] = si
            cm_buf[hh, pl.ds(i, 1), :] = jnp.max(si, axis=0, keepdims=True)

    def prepare(qt, q2, hh, s_buf, cm_buf):
        vrows = slice(hh * HEAD_DIM, (hh + 1) * HEAD_DIM)
        qm_ref[hh] = jnp.concatenate(
            [q_zero] * (hh % hps) + [q2[vrows]] + [q_zero] * (hps - 1 - hh % hps), axis=0)
        scores(qt, hh, 0, s_buf, cm_buf, True)

    def attend(hh, g, s_buf, cm_buf, final=False):
        ja, lo = group_rows(qi, g)
        masks = [rb_ref[hh, pl.ds(lo, 1), :],
                 jnp.where(ja >= 1, rb_ref[hh, pl.ds(lo + 1, 1), :], NEG_INF)]
        m_old = m_ref[hh]
        m_new = m_old
        for i in range(2):
            m_new = jnp.maximum(m_new, cm_buf[hh, pl.ds(i, 1), :] + masks[i])
        acc = jnp.exp2(m_old - m_new) * acc_ref[hh]
        for i in range(2):
            p = jnp.exp2((s_buf[hh, i] - m_new).astype(BF16))
            pv = jnp.dot(vaug_ref[lo + i, hh], p, preferred_element_type=F32)
            acc = acc + jnp.where(masks[i] == 0.0, pv, 0.0)
        if final:
            ot_ref[hh * HEAD_DIM:(hh + 1) * HEAD_DIM, :] = (
                acc[0:HEAD_DIM] * (1.0 / acc[HEAD_DIM:HEAD_DIM + 1])).astype(ot_ref.dtype)
        else:
            acc_ref[hh] = acc
            m_ref[hh] = m_new

    n_more = qi // 2
    buf_a, buf_b = (sa_ref, cma_ref), (sb_ref, cmb_ref)

    @pl.when(qi == 0)
    def _():
        q2 = q_ref[0].T
        for hh in range(nh):
            prepare(qi, q2, hh, *buf_a)
        par_ref[0] = 0

    rb_ref[...] = rbn_ref[...]
    for hh in range(nh):
        m_ref[hh] = jnp.full((1, blk), NEG_INF, F32)
        acc_ref[hh] = jnp.zeros((V_AUG_ROWS, blk), F32)
    par = par_ref[0]

    def stage(it, cur, nxt):
        scores(qi, 0, it + 1, *nxt, False)
        for hh in range(nh):
            if hh + 1 < nh:
                scores(qi, hh + 1, it + 1, *nxt, False)
            attend(hh, it, *cur)

    def step(it, carry):
        @pl.when((it + par) % 2 == 0)
        def _():
            stage(it, buf_a, buf_b)

        @pl.when((it + par) % 2 == 1)
        def _():
            stage(it, buf_b, buf_a)

        return carry

    lax.fori_loop(0, n_more, step, 0)

    last = (n_more + par) % 2
    qn = jnp.minimum(qi + 1, n_blocks - 1)

    def last_stage(cur, nxt):
        q2 = qn_ref[0].T
        prepare(qn, q2, 0, *nxt)
        for hh in range(nh):
            if hh + 1 < nh:
                prepare(qn, q2, hh + 1, *nxt)
            attend(hh, n_more, *cur, final=True)
            if hh % 2 == 1:
                pair = slice((hh - 1) * HEAD_DIM, (hh + 1) * HEAD_DIM)
                o_ref[:, pair] = ot_ref[pair, :].T

    @pl.when(last == 0)
    def _():
        last_stage(buf_a, buf_b)

    @pl.when(last == 1)
    def _():
        last_stage(buf_b, buf_a)

    par_ref[0] = 1 - last


def _moba_attention(proj, bias_tiles, batch, seq):
    proj3 = proj.reshape(proj.shape[0] // MOBA_BLOCK, MOBA_BLOCK, proj.shape[1])
    n_blocks = seq // MOBA_BLOCK
    nh = MOBA_HEADS_PER_STEP
    assert nh % 2 == 0
    n_hg = ATTN_HEADS // nh
    hg_w = nh * HEAD_DIM
    kernel = functools.partial(_moba_kernel, n_blocks=n_blocks)
    return pl.pallas_call(
        kernel,
        grid=(batch, n_hg, n_blocks),
        in_specs=[
            pl.BlockSpec((1, MOBA_BLOCK, hg_w), lambda b, hg, qi: (b * n_blocks + qi, 0, hg)),
            pl.BlockSpec((1, MOBA_BLOCK, hg_w),
                         lambda b, hg, qi: (b * n_blocks + jnp.minimum(qi + 1, n_blocks - 1), 0, hg)),
            pl.BlockSpec((seq, hg_w), lambda b, hg, qi: (b, n_hg + hg)),
            pl.BlockSpec((n_blocks, MOBA_BLOCK, hg_w), lambda b, hg, qi: (b, 0, 2 * n_hg + hg)),
            pl.BlockSpec((nh, 2, MOBA_BLOCK, MOBA_BLOCK), lambda b, hg, qi: (hg, 0, 0, 0)),
        ],
        out_specs=pl.BlockSpec((MOBA_BLOCK, hg_w), lambda b, hg, qi: (b * n_blocks + qi, hg)),
        out_shape=jax.ShapeDtypeStruct((batch * seq, ATTN_HEADS * HEAD_DIM), BF16),
        scratch_shapes=[
            pltpu.VMEM((n_blocks, hg_w), F32),
            pltpu.VMEM((n_blocks, nh, V_AUG_ROWS, MOBA_BLOCK), BF16),
            pltpu.VMEM((nh, min(hg_w, MXU_DEPTH), MOBA_BLOCK), BF16),
            pltpu.VMEM((nh, 2, MOBA_BLOCK, MOBA_BLOCK), F32),
            pltpu.VMEM((nh, 2, MOBA_BLOCK, MOBA_BLOCK), F32),
            pltpu.VMEM((nh, 8, MOBA_BLOCK), F32),
            pltpu.VMEM((nh, 8, MOBA_BLOCK), F32),
            pltpu.VMEM((nh, 1, MOBA_BLOCK), F32),
            pltpu.VMEM((nh, V_AUG_ROWS, MOBA_BLOCK), F32),
            pltpu.VMEM((nh, n_blocks, MOBA_BLOCK), F32),
            pltpu.VMEM((nh, n_blocks, MOBA_BLOCK), F32),
            pltpu.VMEM((hg_w, MOBA_BLOCK), BF16),
            pltpu.SMEM((1,), jnp.int32),
        ],
        compiler_params=pltpu.CompilerParams(
            dimension_semantics=("parallel", "parallel", "arbitrary"),
            vmem_limit_bytes=VMEM_LIMIT),
    )(proj3, proj3, proj, proj3, bias_tiles)


def _hdot_nt(a, b):
    def split(x):
        hi = x.astype(BF16)
        return hi, (x - hi.astype(F32)).astype(BF16)

    def nt(u, v):
        return lax.dot_general(u, v, _NT_DIMS, preferred_element_type=F32)

    a_hi, a_lo = split(a)
    b_hi, b_lo = split(b)
    return nt(a_hi, b_hi) + nt(a_hi, b_lo) + nt(a_lo, b_hi)


def _repeat_rows(x, n):
    return jnp.concatenate([jnp.broadcast_to(x[i:i + 1], (n, x.shape[1]))
                            for i in range(x.shape[0])], axis=0)


def _tile_rows(x, n):
    return jnp.concatenate([x] * n, axis=0)


def _s5_tables_kernel(logdt_ref, lre_ref, lim_ref, bre_ref, bim_ref, cre_ref, cim_ref, d_ref,
                      win_ref, tg_ref, bn_ref, cn_ref, are_ref, aim_ref, wq_ref, *, q_cols, q_scale):
    wq_ref[:, :q_cols] = (win_ref[:, :q_cols] * q_scale).astype(wq_ref.dtype)
    wq_ref[:, q_cols:] = win_ref[:, q_cols:].astype(wq_ref.dtype)

    lc, ch, p = SSM_CHUNK, SSM_GROUP_CH, SSM_STATE
    w = lc * ch
    rr = lax.broadcasted_iota(jnp.int32, (w, w), 0)
    cc = lax.broadcasted_iota(jnp.int32, (w, w), 1)
    causal = rr // ch >= cc // ch
    diag = rr == cc
    e_r = lax.broadcasted_iota(jnp.int32, (ch, w), 0)
    e_c = lax.broadcasted_iota(jnp.int32, (ch, w), 1)
    lane_tile = jnp.where(e_c % ch == e_r, 1.0, 0.0).astype(F32)
    tau = lax.broadcasted_iota(jnp.int32, (2 * lc, p), 0).astype(F32)
    mid = lc // 2

    for k in range(tg_ref.shape[0]):
        lam_re = lre_ref[k]
        lam_im = lim_ref[k]
        dt = jnp.exp(logdt_ref[k])
        lr = lam_re * dt
        li = lam_im * dt

        mag = jnp.exp(lr * tau)
        pos_re, pos_im = mag * jnp.cos(li * tau), mag * jnp.sin(li * tau)
        tau_c = tau[:lc] - mid
        mag_c = jnp.exp(lr * tau_c)
        inv_c = jnp.exp(-lr * tau_c)
        cs, sn = jnp.cos(li * tau_c), jnp.sin(li * tau_c)
        fwd_re, fwd_im = mag_c * cs, mag_c * sn
        neg_re, neg_im = inv_c * cs, -inv_c * sn

        lam1_re, lam1_im = pos_re[1:2], pos_im[1:2]
        lamb_re, lamb_im = pos_re[lc - 1 - mid:lc - mid], pos_im[lc - 1 - mid:lc - mid]
        lamc_re, lamc_im = pos_re[mid + 1:mid + 2], pos_im[mid + 1:mid + 2]
        are_ref[k] = pos_re[lc:lc + 1]
        aim_ref[k] = pos_im[lc:lc + 1]

        num_re, num_im = lam1_re - 1.0, lam1_im
        den = lam_re * lam_re + lam_im * lam_im
        coef_re = (num_re * lam_re + num_im * lam_im) / den
        coef_im = (num_im * lam_re - num_re * lam_im) / den
        bt_re, bt_im = bre_ref[k], bim_ref[k]
        bb_re = coef_re * bt_re - coef_im * bt_im
        bb_im = coef_re * bt_im + coef_im * bt_re

        bbt_re, bbt_im = _tile_rows(bb_re, lc), _tile_rows(bb_im, lc)
        ngx_re, ngx_im = _repeat_rows(neg_re, ch), _repeat_rows(neg_im, ch)
        bneg_re = bbt_re * ngx_re - bbt_im * ngx_im
        bneg_im = bbt_re * ngx_im + bbt_im * ngx_re

        ct_re, ct_im = _tile_rows(cre_ref[k], lc), _tile_rows(cim_ref[k], lc)
        psx_re, psx_im = _repeat_rows(fwd_re, ch), _repeat_rows(fwd_im, ch)
        cpos_re = ct_re * psx_re - ct_im * psx_im
        cpos_im = ct_re * psx_im + ct_im * psx_re

        raw = _hdot_nt(cpos_re, bneg_re) - _hdot_nt(cpos_im, bneg_im)
        d_lanes = jnp.dot(jnp.broadcast_to(d_ref[k], (8, ch)), lane_tile,
                          preferred_element_type=F32, precision=lax.Precision.HIGHEST)[0:1]
        tg = jnp.where(causal, raw, 0.0) + jnp.where(diag, d_lanes, 0.0)
        tg_ref[k] = tg.astype(tg_ref.dtype)

        bn_ref[k, 0] = (bneg_re * lamb_re - bneg_im * lamb_im).T.astype(bn_ref.dtype)
        bn_ref[k, 1] = (bneg_re * lamb_im + bneg_im * lamb_re).T.astype(bn_ref.dtype)
        cp_re = cpos_re * lamc_re - cpos_im * lamc_im
        cp_im = cpos_re * lamc_im + cpos_im * lamc_re
        cn_ref[k, 0] = cp_re.astype(cn_ref.dtype)
        cn_ref[k, 1] = (-cp_im).astype(cn_ref.dtype)


def _s5_tables(log_dt, lam_re, lam_im, b_re, b_im, c_re, c_im, d_skip, w_in, q_cols, q_scale, gb=8):
    g, p, ch = SSM_GROUPS, SSM_STATE, SSM_GROUP_CH
    w = SSM_CHUNK * ch
    steps = g // gb
    d, n = w_in.shape
    row = lambda a, n: a.reshape(g, 1, n)
    spec3 = lambda s1, s2: pl.BlockSpec((gb, s1, s2), lambda i: (i, 0, 0))
    spec4 = lambda s1, s2: pl.BlockSpec((gb, 2, s1, s2), lambda i: (i, 0, 0, 0))
    slab = pl.BlockSpec((d // steps, n), lambda i: (i, 0))
    return pl.pallas_call(
        functools.partial(_s5_tables_kernel, q_cols=q_cols, q_scale=q_scale),
        grid=(steps,),
        in_specs=[spec3(1, 1), spec3(1, p), spec3(1, p), spec3(ch, p), spec3(ch, p),
                  spec3(ch, p), spec3(ch, p), spec3(1, ch), slab],
        out_specs=[spec3(w, w), spec4(p, w), spec4(w, p), spec3(1, p), spec3(1, p), slab],
        out_shape=[jax.ShapeDtypeStruct((g, w, w), BF16),
                   jax.ShapeDtypeStruct((g, 2, p, w), BF16),
                   jax.ShapeDtypeStruct((g, 2, w, p), BF16),
                   jax.ShapeDtypeStruct((g, 1, p), F32),
                   jax.ShapeDtypeStruct((g, 1, p), F32),
                   jax.ShapeDtypeStruct((d, n), BF16)],
        compiler_params=pltpu.CompilerParams(dimension_semantics=("parallel",)),
    )(row(log_dt, 1), row(lam_re, p), row(lam_im, p),
      jnp.swapaxes(b_re, 1, 2), jnp.swapaxes(b_im, 1, 2), c_re, c_im, row(d_skip, ch), w_in)


def _s5_scan_kernel(u_ref, tg_ref, bn_ref, cn_ref, are_ref, aim_ref, y_ref,
                    uf_ref, v_ref, yt_ref, sre_ref, sim_ref, xre_ref, xim_ref, *, n_batch):
    lc, ch = SSM_CHUNK, SSM_GROUP_CH
    gb = tg_ref.shape[0]
    n_rows = u_ref.shape[0] // lc
    n_chunks = n_rows // n_batch
    w = lc * ch

    uf_ref[...] = u_ref[...].astype(F32)
    for s in range(lc):
        ust = uf_ref[pl.ds(s, n_rows, stride=lc), :].T
        for k in range(gb):
            v_ref[k, s * ch:(s + 1) * ch, :] = ust[k * ch:(k + 1) * ch, :].astype(v_ref.dtype)

    for k in range(gb):
        u = v_ref[k]
        rows = pl.ds(k, n_rows, stride=gb)
        sre_ref[rows, :] = jnp.dot(bn_ref[k, 0], u, preferred_element_type=F32).T
        sim_ref[rows, :] = jnp.dot(bn_ref[k, 1], u, preferred_element_type=F32).T

    a_re = are_ref[...]
    a_im = aim_ref[...]

    def step(c, carry):
        new = []
        for b in range(n_batch):
            x_re, x_im = carry[2 * b], carry[2 * b + 1]
            rows = pl.ds(pl.multiple_of((b * n_chunks + c) * gb, gb), gb)
            xre_ref[rows, :] = x_re
            xim_ref[rows, :] = x_im
            new.append(a_re * x_re - a_im * x_im + sre_ref[rows, :])
            new.append(a_re * x_im + a_im * x_re + sim_ref[rows, :])
        return tuple(new)

    zero = jnp.zeros(a_re.shape, F32)
    lax.fori_loop(0, n_chunks, step, (zero,) * (2 * n_batch), unroll=SCAN_UNROLL)

    for k in range(gb):
        rows = pl.ds(k, n_rows, stride=gb)
        y = jnp.dot(tg_ref[k], v_ref[k], preferred_element_type=F32)
        y = y + lax.dot_general(cn_ref[k, 0], xre_ref[rows, :].astype(BF16), _NT_DIMS,
                                preferred_element_type=F32)
        y = y + lax.dot_general(cn_ref[k, 1], xim_ref[rows, :].astype(BF16), _NT_DIMS,
                                preferred_element_type=F32)
        for s in range(lc):
            yt_ref[s, k * ch:(k + 1) * ch, :] = y[s * ch:(s + 1) * ch, :]

    for s in range(lc):
        y_ref[pl.ds(s, n_rows, stride=lc), :] = yt_ref[s].T.astype(y_ref.dtype)


def _s5_scan(ku, tg, bn, cn, a_re, a_im, n_batch, u_col0, gb=8):
    t = ku.shape[0]
    g, w, _ = tg.shape
    p, ch = SSM_STATE, SSM_GROUP_CH
    n_rows = t // SSM_CHUNK
    lanes = gb * ch
    kernel = functools.partial(_s5_scan_kernel, n_batch=n_batch)
    return pl.pallas_call(
        kernel,
        grid=(g // gb,),
        in_specs=[pl.BlockSpec((t, lanes), lambda i: (0, u_col0 // lanes + i)),
                  pl.BlockSpec((gb, w, w), lambda i: (i, 0, 0)),
                  pl.BlockSpec((gb, 2, p, w), lambda i: (i, 0, 0, 0)),
                  pl.BlockSpec((gb, 2, w, p), lambda i: (i, 0, 0, 0)),
                  pl.BlockSpec((gb, p), lambda i: (i, 0)),
                  pl.BlockSpec((gb, p), lambda i: (i, 0))],
        out_specs=pl.BlockSpec((t, lanes), lambda i: (0, i)),
        out_shape=jax.ShapeDtypeStruct((t, g * ch), F32),
        scratch_shapes=[pltpu.VMEM((t, lanes), F32),
                        pltpu.VMEM((gb, w, n_rows), BF16),
                        pltpu.VMEM((SSM_CHUNK, lanes, n_rows), F32)]
                       + [pltpu.VMEM((gb * n_rows, p), F32) for _ in range(4)],
        compiler_params=pltpu.CompilerParams(dimension_semantics=("parallel",),
                                             vmem_limit_bytes=VMEM_LIMIT),
    )(ku, tg, bn, cn, a_re.reshape(g, p), a_im.reshape(g, p))


def _out_proj_kernel(x_ref, a_ref, y_ref, wglu_ref, bglu_ref, ga_ref, gs_ref, wa_ref, ws_ref, o_ref):
    z = jax.nn.gelu(y_ref[...])
    gl = jnp.dot(z.astype(BF16), wglu_ref[...], preferred_element_type=F32) + bglu_ref[...]
    s = z * jax.nn.sigmoid(gl)
    a_n = _rms_rows(a_ref[...].astype(F32), ga_ref[...]).astype(BF16)
    s_n = _rms_rows(s, gs_ref[...]).astype(BF16)
    mix = (jnp.dot(a_n, wa_ref[...], preferred_element_type=F32)
           + jnp.dot(s_n, ws_ref[...], preferred_element_type=F32))
    o_ref[...] = x_ref[...] + mix


def _out_proj(x2, attn, y, w_glu, b_glu, g_a, g_s, w_out, tm=512):
    t, d = x2.shape
    wa = attn.shape[1]
    ws = y.shape[1]
    assert wa == ws
    row = lambda i: (i, 0)
    fixed = lambda i: (0, 0)
    once = pl.Buffered(1)
    return pl.pallas_call(
        _out_proj_kernel,
        grid=(t // tm,),
        in_specs=[pl.BlockSpec((tm, d), row),
                  pl.BlockSpec((tm, wa), row),
                  pl.BlockSpec((tm, ws), row),
                  pl.BlockSpec((ws, ws), fixed, pipeline_mode=once),
                  pl.BlockSpec((1, ws), fixed),
                  pl.BlockSpec((1, wa), fixed),
                  pl.BlockSpec((1, ws), fixed),
                  pl.BlockSpec((wa, d), fixed, pipeline_mode=once),
                  pl.BlockSpec((ws, d), lambda i: (1, 0), pipeline_mode=once)],
        out_specs=pl.BlockSpec((tm, d), row),
        out_shape=jax.ShapeDtypeStruct((t, d), F32),
        compiler_params=pltpu.CompilerParams(dimension_semantics=("parallel",),
                                             vmem_limit_bytes=VMEM_LIMIT),
    )(x2, attn, y, w_glu, b_glu.reshape(1, ws), g_a.reshape(1, wa), g_s.reshape(1, ws),
      w_out, w_out)


def _ffn_up_kernel(x_ref, halo_ref, g_ref, wg_ref, wv_ref, cwg_ref, cwv_ref, cbg_ref, cbv_ref,
                   wd_ref, o_ref, wdq_ref, h_ref, *, tiles_per_seq):
    tm = x_ref.shape[0]
    wdq_ref[...] = wd_ref[...].astype(wdq_ref.dtype)

    @pl.when(pl.program_id(1) == 0)
    def _():
        keep = jnp.where(pl.program_id(0) % tiles_per_seq == 0, 0.0, 1.0)
        h_ref[pl.ds(0, CONV_HALO), :] = (_rms_rows(halo_ref[...], g_ref[...]) * keep).astype(BF16)
        h_ref[pl.ds(CONV_HALO, tm), :] = _rms_rows(x_ref[...], g_ref[...]).astype(BF16)

    h = h_ref[...]

    def conv(w_ref, cw_ref, cb_ref):
        up = jnp.dot(h, w_ref[...].astype(BF16), preferred_element_type=F32)
        cw = cw_ref[...]
        out = up[CONV_HALO:] * cw[CONV_WIDTH - 1:CONV_WIDTH] + cb_ref[...]
        for j in range(CONV_WIDTH - 1):
            lag = CONV_WIDTH - 1 - j
            out = out + up[CONV_HALO - lag:CONV_HALO - lag + tm] * cw[j:j + 1]
        return out

    gate = conv(wg_ref, cwg_ref, cbg_ref)
    val = conv(wv_ref, cwv_ref, cbv_ref)
    o_ref[...] = (jax.nn.silu(gate) * val).astype(o_ref.dtype)


def _ffn_up(x1, g, w_up, conv_w, conv_b, w_down, seq, tm=1024, tn=512):
    t, d = x1.shape
    f = w_up.shape[1] // 2
    nf = f // tn
    tiles_per_seq = seq // tm
    halo_blocks = tm // CONV_HALO
    kernel = functools.partial(_ffn_up_kernel, tiles_per_seq=tiles_per_seq)
    cb = conv_b.reshape(1, 2 * f)
    wd_slab = pl.BlockSpec((w_down.shape[0] // (t // tm * nf), w_down.shape[1]),
                           lambda i, j: (i * nf + j, 0))
    return pl.pallas_call(
        kernel,
        grid=(t // tm, nf),
        in_specs=[pl.BlockSpec((tm, d), lambda i, j: (i, 0)),
                  pl.BlockSpec((CONV_HALO, d), lambda i, j: (jnp.maximum(i * halo_blocks - 1, 0), 0)),
                  pl.BlockSpec((1, d), lambda i, j: (0, 0)),
                  pl.BlockSpec((d, tn), lambda i, j: (0, j)),
                  pl.BlockSpec((d, tn), lambda i, j: (0, nf + j)),
                  pl.BlockSpec((CONV_WIDTH, tn), lambda i, j: (0, j)),
                  pl.BlockSpec((CONV_WIDTH, tn), lambda i, j: (0, nf + j)),
                  pl.BlockSpec((1, tn), lambda i, j: (0, j)),
                  pl.BlockSpec((1, tn), lambda i, j: (0, nf + j)),
                  wd_slab],
        out_specs=[pl.BlockSpec((tm, tn), lambda i, j: (i, j)), wd_slab],
        out_shape=[jax.ShapeDtypeStruct((t, f), BF16), jax.ShapeDtypeStruct(w_down.shape, BF16)],
        scratch_shapes=[pltpu.VMEM((CONV_HALO + tm, d), BF16)],
        compiler_params=pltpu.CompilerParams(dimension_semantics=("parallel", "arbitrary"),
                                             vmem_limit_bytes=VMEM_LIMIT),
    )(x1, x1, g.reshape(1, d), w_up, w_up, conv_w, conv_w, cb, cb, w_down)


def _ffn_down_kernel(a_ref, w_ref, x_ref, g_ref, o_ref, *, final_norm):
    y = x_ref[...] + jnp.dot(a_ref[...], w_ref[...], preferred_element_type=F32)
    o_ref[...] = _rms_rows(y, g_ref[...]) if final_norm else y


def _ffn_down(act, w_down, x1, g, final_norm, tm=512):
    t, f = act.shape
    d = w_down.shape[1]
    return pl.pallas_call(
        functools.partial(_ffn_down_kernel, final_norm=final_norm),
        grid=(t // tm,),
        in_specs=[pl.BlockSpec((tm, f), lambda i: (i, 0)),
                  pl.BlockSpec((f, d), lambda i: (0, 0), pipeline_mode=pl.Buffered(1)),
                  pl.BlockSpec((tm, d), lambda i: (i, 0)),
                  pl.BlockSpec((1, d), lambda i: (0, 0))],
        out_specs=pl.BlockSpec((tm, d), lambda i: (i, 0)),
        out_shape=jax.ShapeDtypeStruct((t, d), F32),
        compiler_params=pltpu.CompilerParams(dimension_semantics=("parallel",),
                                             vmem_limit_bytes=VMEM_LIMIT_RESIDENT),
    )(act, w_down, x1, g.reshape(1, d))


def kernel(x, norm_mix, w_in, rel_bias_table, ssm_lam_re, ssm_lam_im, ssm_log_dt, ssm_b_re, ssm_b_im, ssm_c_re, ssm_c_im, ssm_d, ssm_w_glu, ssm_b_glu, norm_attn_out, norm_ssm_out, w_out, norm_ffn, w_ffn_up, ffn_conv_w, ffn_conv_b, w_ffn_down, norm_final):
    batch, seq, d_model = x.shape
    depth = w_in.shape[0]
    aw = ATTN_HEADS * HEAD_DIM
    t = batch * seq
    assert seq % MOBA_BLOCK == 0 and seq % SSM_CHUNK == 0
    assert MOBA_BLOCK >= MAX_DISTANCE

    bias_tiles = _bias_tiles(rel_bias_table)
    x2 = x.reshape(t, d_model)
    for l in range(depth):
        tg, bn, cn, a_re, a_im, w_in_b = _s5_tables(
            ssm_log_dt[l], ssm_lam_re[l], ssm_lam_im[l], ssm_b_re[l], ssm_b_im[l], ssm_c_re[l],
            ssm_c_im[l], ssm_d[l], w_in[l], q_cols=aw, q_scale=HEAD_DIM ** -0.5 * LOG2E)

        proj, (w_out_b, w_glu_b) = _in_proj(x2, norm_mix[l], w_in_b, (w_out[l], ssm_w_glu[l]))

        attn = _moba_attention(proj, bias_tiles, batch, seq)
        y = _s5_scan(proj, tg, bn, cn, a_re, a_im, batch, u_col0=3 * aw)

        x2 = _out_proj(x2, attn, y, w_glu_b, ssm_b_glu[l], norm_attn_out[l], norm_ssm_out[l], w_out_b)

        act, w_down_b = _ffn_up(x2, norm_ffn[l], w_ffn_up[l], ffn_conv_w[l], ffn_conv_b[l],
                                w_ffn_down[l], seq)
        x2 = _ffn_down(act, w_down_b, x2, norm_final, final_norm=(l == depth - 1))
    return x2.reshape(batch, seq, d_model)
```

```python
import functools
import math

import jax
import jax.numpy as jnp
from jax import lax
from jax.experimental import pallas as pl
from jax.experimental.pallas import tpu as pltpu

F32 = jnp.float32
BF16 = jnp.bfloat16

ATTN_HEADS = 16
HEAD_DIM = 64
SSM_GROUP_CH = 16
SSM_GROUPS = 64
SSM_STATE = 64
MOBA_BLOCK = 256
MOBA_TOP_K = 3
NUM_BUCKETS = 32
MAX_DISTANCE = 128
CONV_WIDTH = 3
RMS_EPS = 1e-6

SSM_CHUNK = 16
MOBA_HEADS_PER_STEP = 8
MXU_DEPTH = 256
SCAN_UNROLL = 8
CONV_HALO = 16
VMEM_LIMIT = 52 * 1024 * 1024
VMEM_LIMIT_RESIDENT = 58 * 1024 * 1024

NEG_INF = float("-inf")
LOG2E = math.log2(math.e)
V_AUG_ROWS = HEAD_DIM + 16


def _rms_rows(x, g):
    ms = jnp.mean(x * x, axis=-1, keepdims=True)
    return x * lax.rsqrt(ms + RMS_EPS) * g


_NT_DIMS = (((1,), (1,)), ((), ()))


def _in_proj_kernel(x_ref, g_ref, w_ref, *rest):
    n_extra = (len(rest) - 2) // 2
    extra_in, o_ref = rest[:n_extra], rest[n_extra]
    extra_out, h_ref = rest[n_extra + 1:2 * n_extra + 1], rest[-1]

    @pl.when(pl.program_id(1) == 0)
    def _():
        h_ref[...] = _rms_rows(x_ref[...], g_ref[...]).astype(BF16)

    o_ref[...] = jnp.dot(h_ref[...], w_ref[...], preferred_element_type=F32).astype(o_ref.dtype)
    for src, dst in zip(extra_in, extra_out):
        dst[...] = src[...].astype(dst.dtype)


def _in_proj(x2, g, w, cast_along, tm=1024, tn=2048):
    t, d = x2.shape
    n = w.shape[1]
    n_i, n_j = t // tm, n // tn
    steps = n_i * n_j
    slabs = [pl.BlockSpec((a.shape[0] // steps, a.shape[1]), lambda i, j: (i * n_j + j, 0))
             for a in cast_along]
    outs = pl.pallas_call(
        _in_proj_kernel,
        grid=(n_i, n_j),
        in_specs=[pl.BlockSpec((tm, d), lambda i, j: (i, 0)),
                  pl.BlockSpec((1, d), lambda i, j: (0, 0)),
                  pl.BlockSpec((d, tn), lambda i, j: (0, j))] + slabs,
        out_specs=[pl.BlockSpec((tm, tn), lambda i, j: (i, j))] + slabs,
        out_shape=[jax.ShapeDtypeStruct((t, n), BF16)]
                  + [jax.ShapeDtypeStruct(a.shape, BF16) for a in cast_along],
        scratch_shapes=[pltpu.VMEM((tm, d), BF16)],
        compiler_params=pltpu.CompilerParams(dimension_semantics=("parallel", "arbitrary"),
                                             vmem_limit_bytes=VMEM_LIMIT),
    )(x2, g.reshape(1, d), w, *cast_along)
    return outs[0], outs[1:]


def _t5_bucket(dist):
    dist = jnp.maximum(dist, 0)
    max_exact = NUM_BUCKETS // 2
    log_ratio = jnp.log(jnp.maximum(dist, max_exact).astype(F32) / max_exact)
    large = max_exact + (log_ratio / math.log(MAX_DISTANCE / max_exact)
                         * (NUM_BUCKETS - max_exact)).astype(jnp.int32)
    large = jnp.minimum(large, NUM_BUCKETS - 1)
    return jnp.where(dist < max_exact, dist, large)


def _bias_tiles_kernel(tab_ref, o_ref):
    blk = MOBA_BLOCK
    dist = lax.broadcasted_iota(jnp.int32, (1, 2 * blk), 1)
    bucket = _t5_bucket(dist)
    kk = lax.broadcasted_iota(jnp.int32, (blk, blk), 0)
    qq = lax.broadcasted_iota(jnp.int32, (blk, blk), 1)
    for i in range(o_ref.shape[0]):
        h = pl.program_id(0) * o_ref.shape[0] + i
        row = jnp.zeros((1, 2 * blk), F32)
        for b in range(NUM_BUCKETS):
            row = jnp.where(bucket == b, tab_ref[b, h], row)
        row = (row - tab_ref[NUM_BUCKETS - 1, h]) * LOG2E
        pair = pltpu.roll(jnp.broadcast_to(row, (blk, 2 * blk)), 0, 1, stride=1, stride_axis=0)
        o_ref[i, 0] = pair[:, blk:]
        o_ref[i, 1] = jnp.where(qq >= kk, pair[:, :blk], NEG_INF)


def _bias_tiles(table, heads_per_step=4):
    return pl.pallas_call(
        _bias_tiles_kernel,
        grid=(ATTN_HEADS // heads_per_step,),
        in_specs=[pl.BlockSpec(memory_space=pltpu.SMEM)],
        out_specs=pl.BlockSpec((heads_per_step, 2, MOBA_BLOCK, MOBA_BLOCK), lambda h: (h, 0, 0, 0)),
        out_shape=jax.ShapeDtypeStruct((ATTN_HEADS, 2, MOBA_BLOCK, MOBA_BLOCK), F32),
    )(table)


def _moba_kernel(q_ref, qn_ref, k_ref, v_ref, bias_ref, o_ref,
                 kmf_ref, vaug_ref, qm_ref, sa_ref, sb_ref, cma_ref, cmb_ref,
                 m_ref, acc_ref, rb_ref, rbn_ref, ot_ref, par_ref, *, n_blocks):
    qi = pl.program_id(2)
    blk = MOBA_BLOCK
    nh = qm_ref.shape[0]

    def add_block(j):
        kb = k_ref[pl.ds(pl.multiple_of(j * blk, blk), blk), :].astype(F32)
        kmf_ref[pl.ds(j, 1), :] = jnp.sum(kb, axis=0, keepdims=True) * (1.0 / blk)
        vt = v_ref[j].T
        ones = jnp.ones((V_AUG_ROWS - HEAD_DIM, blk), BF16)
        for hh in range(nh):
            vaug_ref[j, hh, 0:HEAD_DIM, :] = vt[hh * HEAD_DIM:(hh + 1) * HEAD_DIM, :]
            vaug_ref[j, hh, HEAD_DIM:V_AUG_ROWS, :] = ones

    @pl.when(qi == 0)
    def _():
        kmf_ref[...] = jnp.zeros(kmf_ref.shape, F32)
        add_block(qi)

    kmf = kmf_ref[...]
    km_hi = kmf.astype(BF16)
    km_lo = (kmf - km_hi.astype(F32)).astype(BF16)
    gate_rows = -(-n_blocks // 16) * 16
    if gate_rows > n_blocks:
        km_pad = jnp.zeros((gate_rows - n_blocks, kmf.shape[1]), BF16)
        km_stack = jnp.concatenate([km_hi, km_pad, km_lo, km_pad], axis=0)
    else:
        km_stack = jnp.concatenate([km_hi, km_lo], axis=0)

    q_zero = jnp.zeros((HEAD_DIM, blk), q_ref.dtype)
    blk_idx = lax.broadcasted_iota(jnp.int32, (n_blocks, blk), 0)
    blk_idx_f = blk_idx.astype(F32)
    hps = qm_ref.shape[1] // HEAD_DIM
    slab = lambda hh: slice((hh // hps) * hps * HEAD_DIM, (hh // hps + 1) * hps * HEAD_DIM)

    def select(qt, hh, gate):
        valid = blk_idx < qt
        g = jnp.where(valid, gate, NEG_INF)
        sel = blk_idx == qt
        for _ in range(MOBA_TOP_K):
            top = jnp.max(g, axis=0, keepdims=True)
            first = jnp.min(jnp.where(g == top, blk_idx_f, float(n_blocks)), axis=0, keepdims=True)
            hit = blk_idx_f == first
            sel = sel | (hit & valid)
            g = jnp.where(hit, NEG_INF, g)
        rbn_ref[hh] = jnp.where(sel, 0.0, NEG_INF)

    def group_rows(qt, g):
        ja = qt - 2 * g
        return ja, jnp.maximum(ja - 1, 0)

    def scores(qt, hh, g, s_buf, cm_buf, first_group):
        _, lo = group_rows(qt, g)
        kslab = k_ref[pl.ds(pl.multiple_of(lo * blk, blk), 2 * blk), slab(hh)]
        if first_group:
            kslab = jnp.concatenate([kslab, km_stack[:, slab(hh)]], axis=0)
        s = jnp.dot(kslab, qm_ref[hh], preferred_element_type=F32)
        if first_group:
            select(qt, hh, s[2 * blk:2 * blk + n_blocks]
                   + s[2 * blk + gate_rows:2 * blk + gate_rows + n_blocks])
        for i in range(2):
            si = s[i * blk:(i + 1) * blk]
            if first_group:
                tile = jnp.where(qt == 0, 1, 0) if i == 0 else 1
                si = si + bias_ref[hh, tile]
            s_buf[hh, i] = si
            cm_buf[hh, pl.ds(i, 1), :] = jnp.max(si, axis=0, keepdims=True)

    def prepare(qt, q2, hh, s_buf, cm_buf):
        vrows = slice(hh * HEAD_DIM, (hh + 1) * HEAD_DIM)
        qm_ref[hh] = jnp.concatenate(
            [q_zero] * (hh % hps) + [q2[vrows]] + [q_zero] * (hps - 1 - hh % hps), axis=0)
        scores(qt, hh, 0, s_buf, cm_buf, True)

    def attend(hh, g, s_buf, cm_buf, final=False):
        ja, lo = group_rows(qi, g)
        masks = [rb_ref[hh, pl.ds(lo, 1), :],
                 jnp.where(ja >= 1, rb_ref[hh, pl.ds(lo + 1, 1), :], NEG_INF)]
        m_old = m_ref[hh]
        m_new = m_old
        for i in range(2):
            m_new = jnp.maximum(m_new, cm_buf[hh, pl.ds(i, 1), :] + masks[i])
        acc = jnp.exp2(m_old - m_new) * acc_ref[hh]
        for i in range(2):
            p = jnp.exp2((s_buf[hh, i] - m_new).astype(BF16))
            pv = jnp.dot(vaug_ref[lo + i, hh], p, preferred_element_type=F32)
            acc = acc + jnp.where(masks[i] == 0.0, pv, 0.0)
        if final:
            ot_ref[hh * HEAD_DIM:(hh + 1) * HEAD_DIM, :] = (
                acc[0:HEAD_DIM] * (1.0 / acc[HEAD_DIM:HEAD_DIM + 1])).astype(ot_ref.dtype)
        else:
            acc_ref[hh] = acc
            m_ref[hh] = m_new

    n_more = qi // 2
    buf_a, buf_b = (sa_ref, cma_ref), (sb_ref, cmb_ref)

    @pl.when(qi == 0)
    def _():
        q2 = q_ref[0].T
        for hh in range(nh):
            prepare(qi, q2, hh, *buf_a)
        par_ref[0] = 0

    rb_ref[...] = rbn_ref[...]
    for hh in range(nh):
        m_ref[hh] = jnp.full((1, blk), NEG_INF, F32)
        acc_ref[hh] = jnp.zeros((V_AUG_ROWS, blk), F32)
    par = par_ref[0]

    def stage(it, cur, nxt):
        scores(qi, 0, it + 1, *nxt, False)
        for hh in range(nh):
            if hh + 1 < nh:
                scores(qi, hh + 1, it + 1, *nxt, False)
            attend(hh, it, *cur)

    def step(it, carry):
        @pl.when((it + par) % 2 == 0)
        def _():
            stage(it, buf_a, buf_b)

        @pl.when((it + par) % 2 == 1)
        def _():
            stage(it, buf_b, buf_a)

        return carry

    lax.fori_loop(0, n_more, step, 0)

    last = (n_more + par) % 2
    qn = jnp.minimum(qi + 1, n_blocks - 1)

    def last_stage(cur, nxt):
        q2 = qn_ref[0].T
        prepare(qn, q2, 0, *nxt)
        for hh in range(nh):
            if hh + 1 < nh:
                prepare(qn, q2, hh + 1, *nxt)
            attend(hh, n_more, *cur, final=True)
            if hh % 2 == 1:
                pair = slice((hh - 1) * HEAD_DIM, (hh + 1) * HEAD_DIM)
                o_ref[:, pair] = ot_ref[pair, :].T
        add_block(qn)

    @pl.when(last == 0)
    def _():
        last_stage(buf_a, buf_b)

    @pl.when(last == 1)
    def _():
        last_stage(buf_b, buf_a)

    par_ref[0] = 1 - last


def _moba_attention(proj, bias_tiles, batch, seq):
    proj3 = proj.reshape(proj.shape[0] // MOBA_BLOCK, MOBA_BLOCK, proj.shape[1])
    n_blocks = seq // MOBA_BLOCK
    nh = MOBA_HEADS_PER_STEP
    assert nh % 2 == 0
    n_hg = ATTN_HEADS // nh
    hg_w = nh * HEAD_DIM
    kernel = functools.partial(_moba_kernel, n_blocks=n_blocks)
    return pl.pallas_call(
        kernel,
        grid=(batch, n_hg, n_blocks),
        in_specs=[
            pl.BlockSpec((1, MOBA_BLOCK, hg_w), lambda b, hg, qi: (b * n_blocks + qi, 0, hg)),
            pl.BlockSpec((1, MOBA_BLOCK, hg_w),
                         lambda b, hg, qi: (b * n_blocks + jnp.minimum(qi + 1, n_blocks - 1), 0, hg)),
            pl.BlockSpec((seq, hg_w), lambda b, hg, qi: (b, n_hg + hg)),
            pl.BlockSpec((n_blocks, MOBA_BLOCK, hg_w), lambda b, hg, qi: (b, 0, 2 * n_hg + hg)),
            pl.BlockSpec((nh, 2, MOBA_BLOCK, MOBA_BLOCK), lambda b, hg, qi: (hg, 0, 0, 0)),
        ],
        out_specs=pl.BlockSpec((MOBA_BLOCK, hg_w), lambda b, hg, qi: (b * n_blocks + qi, hg)),
        out_shape=jax.ShapeDtypeStruct((batch * seq, ATTN_HEADS * HEAD_DIM), BF16),
        scratch_shapes=[
            pltpu.VMEM((n_blocks, hg_w), F32),
            pltpu.VMEM((n_blocks, nh, V_AUG_ROWS, MOBA_BLOCK), BF16),
            pltpu.VMEM((nh, min(hg_w, MXU_DEPTH), MOBA_BLOCK), BF16),
            pltpu.VMEM((nh, 2, MOBA_BLOCK, MOBA_BLOCK), F32),
            pltpu.VMEM((nh, 2, MOBA_BLOCK, MOBA_BLOCK), F32),
            pltpu.VMEM((nh, 8, MOBA_BLOCK), F32),
            pltpu.VMEM((nh, 8, MOBA_BLOCK), F32),
            pltpu.VMEM((nh, 1, MOBA_BLOCK), F32),
            pltpu.VMEM((nh, V_AUG_ROWS, MOBA_BLOCK), F32),
            pltpu.VMEM((nh, n_blocks, MOBA_BLOCK), F32),
            pltpu.VMEM((nh, n_blocks, MOBA_BLOCK), F32),
            pltpu.VMEM((hg_w, MOBA_BLOCK), BF16),
            pltpu.SMEM((1,), jnp.int32),
        ],
        compiler_params=pltpu.CompilerParams(
            dimension_semantics=("parallel", "parallel", "arbitrary"),
            vmem_limit_bytes=VMEM_LIMIT),
    )(proj3, proj3, proj, proj3, bias_tiles)


def _hdot_nt(a, b):
    def split(x):
        hi = x.astype(BF16)
        return hi, (x - hi.astype(F32)).astype(BF16)

    def nt(u, v):
        return lax.dot_general(u, v, _NT_DIMS, preferred_element_type=F32)

    a_hi, a_lo = split(a)
    b_hi, b_lo = split(b)
    return nt(a_hi, b_hi) + nt(a_hi, b_lo) + nt(a_lo, b_hi)


def _repeat_rows(x, n):
    return jnp.concatenate([jnp.broadcast_to(x[i:i + 1], (n, x.shape[1]))
                            for i in range(x.shape[0])], axis=0)


def _tile_rows(x, n):
    return jnp.concatenate([x] * n, axis=0)


def _s5_tables_kernel(logdt_ref, lre_ref, lim_ref, bre_ref, bim_ref, cre_ref, cim_ref, d_ref,
                      win_ref, tg_ref, bn_ref, cn_ref, are_ref, aim_ref, wq_ref, *, q_cols, q_scale):
    wq_ref[:, :q_cols] = (win_ref[:, :q_cols] * q_scale).astype(wq_ref.dtype)
    wq_ref[:, q_cols:] = win_ref[:, q_cols:].astype(wq_ref.dtype)

    lc, ch, p = SSM_CHUNK, SSM_GROUP_CH, SSM_STATE
    w = lc * ch
    rr = lax.broadcasted_iota(jnp.int32, (w, w), 0)
    cc = lax.broadcasted_iota(jnp.int32, (w, w), 1)
    causal = rr // ch >= cc // ch
    diag = rr == cc
    e_r = lax.broadcasted_iota(jnp.int32, (ch, w), 0)
    e_c = lax.broadcasted_iota(jnp.int32, (ch, w), 1)
    lane_tile = jnp.where(e_c % ch == e_r, 1.0, 0.0).astype(F32)
    tau = lax.broadcasted_iota(jnp.int32, (2 * lc, p), 0).astype(F32)
    mid = lc // 2

    for k in range(tg_ref.shape[0]):
        lam_re = lre_ref[k]
        lam_im = lim_ref[k]
        dt = jnp.exp(logdt_ref[k])
        lr = lam_re * dt
        li = lam_im * dt

        mag = jnp.exp(lr * tau)
        pos_re, pos_im = mag * jnp.cos(li * tau), mag * jnp.sin(li * tau)
        tau_c = tau[:lc] - mid
        mag_c = jnp.exp(lr * tau_c)
        inv_c = jnp.exp(-lr * tau_c)
        cs, sn = jnp.cos(li * tau_c), jnp.sin(li * tau_c)
        fwd_re, fwd_im = mag_c * cs, mag_c * sn
        neg_re, neg_im = inv_c * cs, -inv_c * sn

        lam1_re, lam1_im = pos_re[1:2], pos_im[1:2]
        lamb_re, lamb_im = pos_re[lc - 1 - mid:lc - mid], pos_im[lc - 1 - mid:lc - mid]
        lamc_re, lamc_im = pos_re[mid + 1:mid + 2], pos_im[mid + 1:mid + 2]
        are_ref[k] = pos_re[lc:lc + 1]
        aim_ref[k] = pos_im[lc:lc + 1]

        num_re, num_im = lam1_re - 1.0, lam1_im
        den = lam_re * lam_re + lam_im * lam_im
        coef_re = (num_re * lam_re + num_im * lam_im) / den
        coef_im = (num_im * lam_re - num_re * lam_im) / den
        bt_re, bt_im = bre_ref[k], bim_ref[k]
        bb_re = coef_re * bt_re - coef_im * bt_im
        bb_im = coef_re * bt_im + coef_im * bt_re

        bbt_re, bbt_im = _tile_rows(bb_re, lc), _tile_rows(bb_im, lc)
        ngx_re, ngx_im = _repeat_rows(neg_re, ch), _repeat_rows(neg_im, ch)
        bneg_re = bbt_re * ngx_re - bbt_im * ngx_im
        bneg_im = bbt_re * ngx_im + bbt_im * ngx_re

        ct_re, ct_im = _tile_rows(cre_ref[k], lc), _tile_rows(cim_ref[k], lc)
        psx_re, psx_im = _repeat_rows(fwd_re, ch), _repeat_rows(fwd_im, ch)
        cpos_re = ct_re * psx_re - ct_im * psx_im
        cpos_im = ct_re * psx_im + ct_im * psx_re

        raw = _hdot_nt(cpos_re, bneg_re) - _hdot_nt(cpos_im, bneg_im)
        d_lanes = jnp.dot(jnp.broadcast_to(d_ref[k], (8, ch)), lane_tile,
                          preferred_element_type=F32, precision=lax.Precision.HIGHEST)[0:1]
        tg = jnp.where(causal, raw, 0.0) + jnp.where(diag, d_lanes, 0.0)
        tg_ref[k] = tg.astype(tg_ref.dtype)

        bn_ref[k, 0] = (bneg_re * lamb_re - bneg_im * lamb_im).T.astype(bn_ref.dtype)
        bn_ref[k, 1] = (bneg_re * lamb_im + bneg_im * lamb_re).T.astype(bn_ref.dtype)
        cp_re = cpos_re * lamc_re - cpos_im * lamc_im
        cp_im = cpos_re * lamc_im + cpos_im * lamc_re
        cn_ref[k, 0] = cp_re.astype(cn_ref.dtype)
        cn_ref[k, 1] = (-cp_im).astype(cn_ref.dtype)


def _s5_tables(log_dt, lam_re, lam_im, b_re, b_im, c_re, c_im, d_skip, w_in, q_cols, q_scale, gb=8):
    g, p, ch = SSM_GROUPS, SSM_STATE, SSM_GROUP_CH
    w = SSM_CHUNK * ch
    steps = g // gb
    d, n = w_in.shape
    row = lambda a, n: a.reshape(g, 1, n)
    spec3 = lambda s1, s2: pl.BlockSpec((gb, s1, s2), lambda i: (i, 0, 0))
    spec4 = lambda s1, s2: pl.BlockSpec((gb, 2, s1, s2), lambda i: (i, 0, 0, 0))
    slab = pl.BlockSpec((d // steps, n), lambda i: (i, 0))
    return pl.pallas_call(
        functools.partial(_s5_tables_kernel, q_cols=q_cols, q_scale=q_scale),
        grid=(steps,),
        in_specs=[spec3(1, 1), spec3(1, p), spec3(1, p), spec3(ch, p), spec3(ch, p),
                  spec3(ch, p), spec3(ch, p), spec3(1, ch), slab],
        out_specs=[spec3(w, w), spec4(p, w), spec4(w, p), spec3(1, p), spec3(1, p), slab],
        out_shape=[jax.ShapeDtypeStruct((g, w, w), BF16),
                   jax.ShapeDtypeStruct((g, 2, p, w), BF16),
                   jax.ShapeDtypeStruct((g, 2, w, p), BF16),
                   jax.ShapeDtypeStruct((g, 1, p), F32),
                   jax.ShapeDtypeStruct((g, 1, p), F32),
                   jax.ShapeDtypeStruct((d, n), BF16)],
        compiler_params=pltpu.CompilerParams(dimension_semantics=("parallel",)),
    )(row(log_dt, 1), row(lam_re, p), row(lam_im, p),
      jnp.swapaxes(b_re, 1, 2), jnp.swapaxes(b_im, 1, 2), c_re, c_im, row(d_skip, ch), w_in)


def _s5_scan_kernel(u_ref, tg_ref, bn_ref, cn_ref, are_ref, aim_ref, y_ref,
                    uf_ref, v_ref, yt_ref, sre_ref, sim_ref, xre_ref, xim_ref, *, n_batch):
    lc, ch = SSM_CHUNK, SSM_GROUP_CH
    gb = tg_ref.shape[0]
    n_rows = u_ref.shape[0] // lc
    n_chunks = n_rows // n_batch
    w = lc * ch

    uf_ref[...] = u_ref[...].astype(F32)
    for s in range(lc):
        ust = uf_ref[pl.ds(s, n_rows, stride=lc), :].T
        for k in range(gb):
            v_ref[k, s * ch:(s + 1) * ch, :] = ust[k * ch:(k + 1) * ch, :].astype(v_ref.dtype)

    for k in range(gb):
        u = v_ref[k]
        rows = pl.ds(k, n_rows, stride=gb)
        sre_ref[rows, :] = jnp.dot(bn_ref[k, 0], u, preferred_element_type=F32).T
        sim_ref[rows, :] = jnp.dot(bn_ref[k, 1], u, preferred_element_type=F32).T

    a_re = are_ref[...]
    a_im = aim_ref[...]

    def step(c, carry):
        new = []
        for b in range(n_batch):
            x_re, x_im = carry[2 * b], carry[2 * b + 1]
            rows = pl.ds(pl.multiple_of((b * n_chunks + c) * gb, gb), gb)
            xre_ref[rows, :] = x_re
            xim_ref[rows, :] = x_im
            new.append(a_re * x_re - a_im * x_im + sre_ref[rows, :])
            new.append(a_re * x_im + a_im * x_re + sim_ref[rows, :])
        return tuple(new)

    zero = jnp.zeros(a_re.shape, F32)
    lax.fori_loop(0, n_chunks, step, (zero,) * (2 * n_batch), unroll=SCAN_UNROLL)

    for k in range(gb):
        rows = pl.ds(k, n_rows, stride=gb)
        y = jnp.dot(tg_ref[k], v_ref[k], preferred_element_type=F32)
        y = y + lax.dot_general(cn_ref[k, 0], xre_ref[rows, :].astype(BF16), _NT_DIMS,
                                preferred_element_type=F32)
        y = y + lax.dot_general(cn_ref[k, 1], xim_ref[rows, :].astype(BF16), _NT_DIMS,
                                preferred_element_type=F32)
        for s in range(lc):
            yt_ref[s, k * ch:(k + 1) * ch, :] = y[s * ch:(s + 1) * ch, :]

    for s in range(lc):
        y_ref[pl.ds(s, n_rows, stride=lc), :] = yt_ref[s].T.astype(y_ref.dtype)


def _s5_scan(ku, tg, bn, cn, a_re, a_im, n_batch, u_col0, gb=8):
    t = ku.shape[0]
    g, w, _ = tg.shape
    p, ch = SSM_STATE, SSM_GROUP_CH
    n_rows = t // SSM_CHUNK
    lanes = gb * ch
    kernel = functools.partial(_s5_scan_kernel, n_batch=n_batch)
    return pl.pallas_call(
        kernel,
        grid=(g // gb,),
        in_specs=[pl.BlockSpec((t, lanes), lambda i: (0, u_col0 // lanes + i)),
                  pl.BlockSpec((gb, w, w), lambda i: (i, 0, 0)),
                  pl.BlockSpec((gb, 2, p, w), lambda i: (i, 0, 0, 0)),
                  pl.BlockSpec((gb, 2, w, p), lambda i: (i, 0, 0, 0)),
                  pl.BlockSpec((gb, p), lambda i: (i, 0)),
                  pl.BlockSpec((gb, p), lambda i: (i, 0))],
        out_specs=pl.BlockSpec((t, lanes), lambda i: (0, i)),
        out_shape=jax.ShapeDtypeStruct((t, g * ch), F32),
        scratch_shapes=[pltpu.VMEM((t, lanes), F32),
                        pltpu.VMEM((gb, w, n_rows), BF16),
                        pltpu.VMEM((SSM_CHUNK, lanes, n_rows), F32)]
                       + [pltpu.VMEM((gb * n_rows, p), F32) for _ in range(4)],
        compiler_params=pltpu.CompilerParams(dimension_semantics=("parallel",),
                                             vmem_limit_bytes=VMEM_LIMIT),
    )(ku, tg, bn, cn, a_re.reshape(g, p), a_im.reshape(g, p))


def _out_proj_kernel(x_ref, a_ref, y_ref, wglu_ref, bglu_ref, ga_ref, gs_ref, wa_ref, ws_ref, o_ref):
    z = jax.nn.gelu(y_ref[...])
    gl = jnp.dot(z.astype(BF16), wglu_ref[...], preferred_element_type=F32) + bglu_ref[...]
    s = z * jax.nn.sigmoid(gl)
    a_n = _rms_rows(a_ref[...].astype(F32), ga_ref[...]).astype(BF16)
    s_n = _rms_rows(s, gs_ref[...]).astype(BF16)
    mix = (jnp.dot(a_n, wa_ref[...], preferred_element_type=F32)
           + jnp.dot(s_n, ws_ref[...], preferred_element_type=F32))
    o_ref[...] = x_ref[...] + mix


def _out_proj(x2, attn, y, w_glu, b_glu, g_a, g_s, w_out, tm=512):
    t, d = x2.shape
    wa = attn.shape[1]
    ws = y.shape[1]
    assert wa == ws
    row = lambda i: (i, 0)
    fixed = lambda i: (0, 0)
    once = pl.Buffered(1)
    return pl.pallas_call(
        _out_proj_kernel,
        grid=(t // tm,),
        in_specs=[pl.BlockSpec((tm, d), row),
                  pl.BlockSpec((tm, wa), row),
                  pl.BlockSpec((tm, ws), row),
                  pl.BlockSpec((ws, ws), fixed, pipeline_mode=once),
                  pl.BlockSpec((1, ws), fixed),
                  pl.BlockSpec((1, wa), fixed),
                  pl.BlockSpec((1, ws), fixed),
                  pl.BlockSpec((wa, d), fixed, pipeline_mode=once),
                  pl.BlockSpec((ws, d), lambda i: (1, 0), pipeline_mode=once)],
        out_specs=pl.BlockSpec((tm, d), row),
        out_shape=jax.ShapeDtypeStruct((t, d), F32),
        compiler_params=pltpu.CompilerParams(dimension_semantics=("parallel",),
                                             vmem_limit_bytes=VMEM_LIMIT),
    )(x2, attn, y, w_glu, b_glu.reshape(1, ws), g_a.reshape(1, wa), g_s.reshape(1, ws),
      w_out, w_out)


def _ffn_up_kernel(x_ref, halo_ref, g_ref, wg_ref, wv_ref, cwg_ref, cwv_ref, cbg_ref, cbv_ref,
                   wd_ref, o_ref, wdq_ref, h_ref, *, tiles_per_seq):
    tm = x_ref.shape[0]
    wdq_ref[...] = wd_ref[...].astype(wdq_ref.dtype)

    @pl.when(pl.program_id(1) == 0)
    def _():
        keep = jnp.where(pl.program_id(0) % tiles_per_seq == 0, 0.0, 1.0)
        h_ref[pl.ds(0, CONV_HALO), :] = (_rms_rows(halo_ref[...], g_ref[...]) * keep).astype(BF16)
        h_ref[pl.ds(CONV_HALO, tm), :] = _rms_rows(x_ref[...], g_ref[...]).astype(BF16)

    h = h_ref[...]

    def conv(w_ref, cw_ref, cb_ref):
        up = jnp.dot(h, w_ref[...].astype(BF16), preferred_element_type=F32)
        cw = cw_ref[...]
        out = up[CONV_HALO:] * cw[CONV_WIDTH - 1:CONV_WIDTH] + cb_ref[...]
        for j in range(CONV_WIDTH - 1):
            lag = CONV_WIDTH - 1 - j
            out = out + up[CONV_HALO - lag:CONV_HALO - lag + tm] * cw[j:j + 1]
        return out

    gate = conv(wg_ref, cwg_ref, cbg_ref)
    val = conv(wv_ref, cwv_ref, cbv_ref)
    o_ref[...] = (jax.nn.silu(gate) * val).astype(o_ref.dtype)


def _ffn_up(x1, g, w_up, conv_w, conv_b, w_down, seq, tm=1024, tn=512):
    t, d = x1.shape
    f = w_up.shape[1] // 2
    nf = f // tn
    tiles_per_seq = seq // tm
    halo_blocks = tm // CONV_HALO
    kernel = functools.partial(_ffn_up_kernel, tiles_per_seq=tiles_per_seq)
    cb = conv_b.reshape(1, 2 * f)
    wd_slab = pl.BlockSpec((w_down.shape[0] // (t // tm * nf), w_down.shape[1]),
                           lambda i, j: (i * nf + j, 0))
    return pl.pallas_call(
        kernel,
        grid=(t // tm, nf),
        in_specs=[pl.BlockSpec((tm, d), lambda i, j: (i, 0)),
                  pl.BlockSpec((CONV_HALO, d), lambda i, j: (jnp.maximum(i * halo_blocks - 1, 0), 0)),
                  pl.BlockSpec((1, d), lambda i, j: (0, 0)),
                  pl.BlockSpec((d, tn), lambda i, j: (0, j)),
                  pl.BlockSpec((d, tn), lambda i, j: (0, nf + j)),
                  pl.BlockSpec((CONV_WIDTH, tn), lambda i, j: (0, j)),
                  pl.BlockSpec((CONV_WIDTH, tn), lambda i, j: (0, nf + j)),
                  pl.BlockSpec((1, tn), lambda i, j: (0, j)),
                  pl.BlockSpec((1, tn), lambda i, j: (0, nf + j)),
                  wd_slab],
        out_specs=[pl.BlockSpec((tm, tn), lambda i, j: (i, j)), wd_slab],
        out_shape=[jax.ShapeDtypeStruct((t, f), BF16), jax.ShapeDtypeStruct(w_down.shape, BF16)],
        scratch_shapes=[pltpu.VMEM((CONV_HALO + tm, d), BF16)],
        compiler_params=pltpu.CompilerParams(dimension_semantics=("parallel", "arbitrary"),
                                             vmem_limit_bytes=VMEM_LIMIT),
    )(x1, x1, g.reshape(1, d), w_up, w_up, conv_w, conv_w, cb, cb, w_down)


def _ffn_down_kernel(a_ref, w_ref, x_ref, g_ref, o_ref, *, final_norm):
    y = x_ref[...] + jnp.dot(a_ref[...], w_ref[...], preferred_element_type=F32)
    o_ref[...] = _rms_rows(y, g_ref[...]) if final_norm else y


def _ffn_down(act, w_down, x1, g, final_norm, tm=512):
    t, f = act.shape
    d = w_down.shape[1]
    return pl.pallas_call(
        functools.partial(_ffn_down_kernel, final_norm=final_norm),
        grid=(t // tm,),
        in_specs=[pl.BlockSpec((tm, f), lambda i: (i, 0)),
                  pl.BlockSpec((f, d), lambda i: (0, 0), pipeline_mode=pl.Buffered(1)),
                  pl.BlockSpec((tm, d), lambda i: (i, 0)),
                  pl.BlockSpec((1, d), lambda i: (0, 0))],
        out_specs=pl.BlockSpec((tm, d), lambda i: (i, 0)),
        out_shape=jax.ShapeDtypeStruct((t, d), F32),
        compiler_params=pltpu.CompilerParams(dimension_semantics=("parallel",),
                                             vmem_limit_bytes=VMEM_LIMIT_RESIDENT),
    )(act, w_down, x1, g.reshape(1, d))


def kernel(x, norm_mix, w_in, rel_bias_table, ssm_lam_re, ssm_lam_im, ssm_log_dt, ssm_b_re, ssm_b_im, ssm_c_re, ssm_c_im, ssm_d, ssm_w_glu, ssm_b_glu, norm_attn_out, norm_ssm_out, w_out, norm_ffn, w_ffn_up, ffn_conv_w, ffn_conv_b, w_ffn_down, norm_final):
    batch, seq, d_model = x.shape
    depth = w_in.shape[0]
    aw = ATTN_HEADS * HEAD_DIM
    t = batch * seq
    assert seq % MOBA_BLOCK == 0 and seq % SSM_CHUNK == 0
    assert MOBA_BLOCK >= MAX_DISTANCE

    bias_tiles = _bias_tiles(rel_bias_table)
    x2 = x.reshape(t, d_model)
    for l in range(depth):
        tg, bn, cn, a_re, a_im, w_in_b = _s5_tables(
            ssm_log_dt[l], ssm_lam_re[l], ssm_lam_im[l], ssm_b_re[l], ssm_b_im[l], ssm_c_re[l],
            ssm_c_im[l], ssm_d[l], w_in[l], q_cols=aw, q_scale=HEAD_DIM ** -0.5 * LOG2E)

        proj, (w_out_b, w_glu_b) = _in_proj(x2, norm_mix[l], w_in_b, (w_out[l], ssm_w_glu[l]))

        attn = _moba_attention(proj, bias_tiles, batch, seq)
        y = _s5_scan(proj, tg, bn, cn, a_re, a_im, batch, u_col0=3 * aw)

        x2 = _out_proj(x2, attn, y, w_glu_b, ssm_b_glu[l], norm_attn_out[l], norm_ssm_out[l], w_out_b)

        act, w_down_b = _ffn_up(x2, norm_ffn[l], w_ffn_up[l], ffn_conv_w[l], ffn_conv_b[l],
                                w_ffn_down[l], seq)
        x2 = _ffn_down(act, w_down_b, x2, norm_final, final_norm=(l == depth - 1))
    return x2.reshape(batch, seq, d_model)
```

```python
import functools
import math

import jax
import jax.numpy as jnp
from jax import lax
from jax.experimental import pallas as pl
from jax.experimental.pallas import tpu as pltpu

F32 = jnp.float32
BF16 = jnp.bfloat16

ATTN_HEADS = 16
HEAD_DIM = 64
SSM_GROUP_CH = 16
SSM_GROUPS = 64
SSM_STATE = 64
MOBA_BLOCK = 256
MOBA_TOP_K = 3
NUM_BUCKETS = 32
MAX_DISTANCE = 128
CONV_WIDTH = 3
RMS_EPS = 1e-6

SSM_CHUNK = 16
MOBA_HEADS_PER_STEP = 8
MXU_DEPTH = 256
SCAN_UNROLL = 8
CONV_HALO = 16
VMEM_LIMIT = 52 * 1024 * 1024
VMEM_LIMIT_RESIDENT = 58 * 1024 * 1024

NEG_INF = float("-inf")
LOG2E = math.log2(math.e)
V_AUG_ROWS = HEAD_DIM + 16


def _rms_rows(x, g):
    ms = jnp.mean(x * x, axis=-1, keepdims=True)
    return x * lax.rsqrt(ms + RMS_EPS) * g


_NT_DIMS = (((1,), (1,)), ((), ()))


def _in_proj_kernel(x_ref, g_ref, w_ref, *rest):
    n_extra = (len(rest) - 2) // 2
    extra_in, o_ref = rest[:n_extra], rest[n_extra]
    extra_out, h_ref = rest[n_extra + 1:2 * n_extra + 1], rest[-1]

    @pl.when(pl.program_id(1) == 0)
    def _():
        h_ref[...] = _rms_rows(x_ref[...], g_ref[...]).astype(BF16)

    o_ref[...] = jnp.dot(h_ref[...], w_ref[...], preferred_element_type=F32).astype(o_ref.dtype)
    for src, dst in zip(extra_in, extra_out):
        dst[...] = src[...].astype(dst.dtype)


def _in_proj(x2, g, w, cast_along, tm=1024, tn=2048):
    t, d = x2.shape
    n = w.shape[1]
    n_i, n_j = t // tm, n // tn
    steps = n_i * n_j
    slabs = [pl.BlockSpec((a.shape[0] // steps, a.shape[1]), lambda i, j: (i * n_j + j, 0))
             for a in cast_along]
    outs = pl.pallas_call(
        _in_proj_kernel,
        grid=(n_i, n_j),
        in_specs=[pl.BlockSpec((tm, d), lambda i, j: (i, 0)),
                  pl.BlockSpec((1, d), lambda i, j: (0, 0)),
                  pl.BlockSpec((d, tn), lambda i, j: (0, j))] + slabs,
        out_specs=[pl.BlockSpec((tm, tn), lambda i, j: (i, j))] + slabs,
        out_shape=[jax.ShapeDtypeStruct((t, n), BF16)]
                  + [jax.ShapeDtypeStruct(a.shape, BF16) for a in cast_along],
        scratch_shapes=[pltpu.VMEM((tm, d), BF16)],
        compiler_params=pltpu.CompilerParams(dimension_semantics=("parallel", "arbitrary"),
                                             vmem_limit_bytes=VMEM_LIMIT),
    )(x2, g.reshape(1, d), w, *cast_along)
    return outs[0], outs[1:]


def _t5_bucket(dist):
    dist = jnp.maximum(dist, 0)
    max_exact = NUM_BUCKETS // 2
    log_ratio = jnp.log(jnp.maximum(dist, max_exact).astype(F32) / max_exact)
    large = max_exact + (log_ratio / math.log(MAX_DISTANCE / max_exact)
                         * (NUM_BUCKETS - max_exact)).astype(jnp.int32)
    large = jnp.minimum(large, NUM_BUCKETS - 1)
    return jnp.where(dist < max_exact, dist, large)


def _bias_tiles_kernel(tab_ref, o_ref):
    blk = MOBA_BLOCK
    dist = lax.broadcasted_iota(jnp.int32, (1, 2 * blk), 1)
    bucket = _t5_bucket(dist)
    kk = lax.broadcasted_iota(jnp.int32, (blk, blk), 0)
    qq = lax.broadcasted_iota(jnp.int32, (blk, blk), 1)
    for i in range(o_ref.shape[0]):
        h = pl.program_id(0) * o_ref.shape[0] + i
        row = jnp.zeros((1, 2 * blk), F32)
        for b in range(NUM_BUCKETS):
            row = jnp.where(bucket == b, tab_ref[b, h], row)
        row = (row - tab_ref[NUM_BUCKETS - 1, h]) * LOG2E
        pair = pltpu.roll(jnp.broadcast_to(row, (blk, 2 * blk)), 0, 1, stride=1, stride_axis=0)
        o_ref[i, 0] = pair[:, blk:]
        o_ref[i, 1] = jnp.where(qq >= kk, pair[:, :blk], NEG_INF)


def _bias_tiles(table, heads_per_step=4):
    return pl.pallas_call(
        _bias_tiles_kernel,
        grid=(ATTN_HEADS // heads_per_step,),
        in_specs=[pl.BlockSpec(memory_space=pltpu.SMEM)],
        out_specs=pl.BlockSpec((heads_per_step, 2, MOBA_BLOCK, MOBA_BLOCK), lambda h: (h, 0, 0, 0)),
        out_shape=jax.ShapeDtypeStruct((ATTN_HEADS, 2, MOBA_BLOCK, MOBA_BLOCK), F32),
    )(table)


def _moba_kernel(q_ref, qn_ref, k_ref, v_ref, bias_ref, o_ref,
                 kmf_ref, vaug_ref, qm_ref, sa_ref, sb_ref, cma_ref, cmb_ref,
                 m_ref, acc_ref, rb_ref, rbn_ref, ot_ref, par_ref, *, n_blocks):
    qi = pl.program_id(2)
    blk = MOBA_BLOCK
    nh = qm_ref.shape[0]

    @pl.when(qi == 0)
    def _():
        ones = jnp.ones((V_AUG_ROWS - HEAD_DIM, blk), BF16)
        for j in range(n_blocks):
            kb = k_ref[j * blk:(j + 1) * blk, :].astype(F32)
            kmf_ref[pl.ds(j, 1), :] = jnp.sum(kb, axis=0, keepdims=True) * (1.0 / blk)
            vt = v_ref[j].T
            for hh in range(nh):
                vaug_ref[j, hh, 0:HEAD_DIM, :] = vt[hh * HEAD_DIM:(hh + 1) * HEAD_DIM, :]
                vaug_ref[j, hh, HEAD_DIM:V_AUG_ROWS, :] = ones

    kmf = kmf_ref[...]
    km_hi = kmf.astype(BF16)
    km_lo = (kmf - km_hi.astype(F32)).astype(BF16)
    gate_rows = -(-n_blocks // 16) * 16
    if gate_rows > n_blocks:
        km_pad = jnp.zeros((gate_rows - n_blocks, kmf.shape[1]), BF16)
        km_stack = jnp.concatenate([km_hi, km_pad, km_lo, km_pad], axis=0)
    else:
        km_stack = jnp.concatenate([km_hi, km_lo], axis=0)

    q_zero = jnp.zeros((HEAD_DIM, blk), q_ref.dtype)
    blk_idx = lax.broadcasted_iota(jnp.int32, (n_blocks, blk), 0)
    blk_idx_f = blk_idx.astype(F32)
    hps = qm_ref.shape[1] // HEAD_DIM
    slab = lambda hh: slice((hh // hps) * hps * HEAD_DIM, (hh // hps + 1) * hps * HEAD_DIM)

    def select(qt, hh, gate):
        valid = blk_idx < qt
        g = jnp.where(valid, gate, NEG_INF)
        sel = blk_idx == qt
        for _ in range(MOBA_TOP_K):
            top = jnp.max(g, axis=0, keepdims=True)
            first = jnp.min(jnp.where(g == top, blk_idx_f, float(n_blocks)), axis=0, keepdims=True)
            hit = blk_idx_f == first
            sel = sel | (hit & valid)
            g = jnp.where(hit, NEG_INF, g)
        rbn_ref[hh] = jnp.where(sel, 0.0, NEG_INF)

    def group_rows(qt, g):
        ja = qt - 2 * g
        return ja, jnp.maximum(ja - 1, 0)

    def scores(qt, hh, g, s_buf, cm_buf, first_group):
        _, lo = group_rows(qt, g)
        kslab = k_ref[pl.ds(pl.multiple_of(lo * blk, blk), 2 * blk), slab(hh)]
        if first_group:
            kslab = jnp.concatenate([kslab, km_stack[:, slab(hh)]], axis=0)
        s = jnp.dot(kslab, qm_ref[hh], preferred_element_type=F32)
        if first_group:
            select(qt, hh, s[2 * blk:2 * blk + n_blocks]
                   + s[2 * blk + gate_rows:2 * blk + gate_rows + n_blocks])
        for i in range(2):
            si = s[i * blk:(i + 1) * blk]
            if first_group:
                tile = jnp.where(qt == 0, 1, 0) if i == 0 else 1
                si = si + bias_ref[hh, tile]
            s_buf[hh, i] = si
            cm_buf[hh, pl.ds(i, 1), :] = jnp.max(si, axis=0, keepdims=True)

    def prepare(qt, q2, hh, s_buf, cm_buf):
        vrows = slice(hh * HEAD_DIM, (hh + 1) * HEAD_DIM)
        qm_ref[hh] = jnp.concatenate(
            [q_zero] * (hh % hps) + [q2[vrows]] + [q_zero] * (hps - 1 - hh % hps), axis=0)
        scores(qt, hh, 0, s_buf, cm_buf, True)

    def attend(hh, g, s_buf, cm_buf, final=False):
        ja, lo = group_rows(qi, g)
        masks = [rb_ref[hh, pl.ds(lo, 1), :],
                 jnp.where(ja >= 1, rb_ref[hh, pl.ds(lo + 1, 1), :], NEG_INF)]
        m_old = m_ref[hh]
        m_new = m_old
        for i in range(2):
            m_new = jnp.maximum(m_new, cm_buf[hh, pl.ds(i, 1), :] + masks[i])
        acc = jnp.exp2(m_old - m_new) * acc_ref[hh]
        for i in range(2):
            p = jnp.exp2((s_buf[hh, i] - m_new).astype(BF16))
            pv = jnp.dot(vaug_ref[lo + i, hh], p, preferred_element_type=F32)
            acc = acc + jnp.where(masks[i] == 0.0, pv, 0.0)
        if final:
            ot_ref[hh * HEAD_DIM:(hh + 1) * HEAD_DIM, :] = (
                acc[0:HEAD_DIM] * (1.0 / acc[HEAD_DIM:HEAD_DIM + 1])).astype(ot_ref.dtype)
        else:
            acc_ref[hh] = acc
            m_ref[hh] = m_new

    n_more = qi // 2
    buf_a, buf_b = (sa_ref, cma_ref), (sb_ref, cmb_ref)

    @pl.when(qi == 0)
    def _():
        q2 = q_ref[0].T
        for hh in range(nh):
            prepare(qi, q2, hh, *buf_a)
        par_ref[0] = 0

    rb_ref[...] = rbn_ref[...]
    for hh in range(nh):
        m_ref[hh] = jnp.full((1, blk), NEG_INF, F32)
        acc_ref[hh] = jnp.zeros((V_AUG_ROWS, blk), F32)
    par = par_ref[0]

    def stage(it, cur, nxt):
        scores(qi, 0, it + 1, *nxt, False)
        for hh in range(nh):
            if hh + 1 < nh:
                scores(qi, hh + 1, it + 1, *nxt, False)
            attend(hh, it, *cur)

    def step(it, carry):
        @pl.when((it + par) % 2 == 0)
        def _():
            stage(it, buf_a, buf_b)

        @pl.when((it + par) % 2 == 1)
        def _():
            stage(it, buf_b, buf_a)

        return carry

    lax.fori_loop(0, n_more, step, 0)

    last = (n_more + par) % 2
    qn = jnp.minimum(qi + 1, n_blocks - 1)

    def last_stage(cur, nxt):
        q2 = qn_ref[0].T
        prepare(qn, q2, 0, *nxt)
        for hh in range(nh):
            if hh + 1 < nh:
                prepare(qn, q2, hh + 1, *nxt)
            attend(hh, n_more, *cur, final=True)
            if hh % 2 == 1:
                pair = slice((hh - 1) * HEAD_DIM, (hh + 1) * HEAD_DIM)
                o_ref[:, pair] = ot_ref[pair, :].T

    @pl.when(last == 0)
    def _():
        last_stage(buf_a, buf_b)

    @pl.when(last == 1)
    def _():
        last_stage(buf_b, buf_a)

    par_ref[0] = 1 - last


def _moba_attention(proj, bias_tiles, batch, seq):
    proj3 = proj.reshape(proj.shape[0] // MOBA_BLOCK, MOBA_BLOCK, proj.shape[1])
    n_blocks = seq // MOBA_BLOCK
    nh = MOBA_HEADS_PER_STEP
    assert nh % 2 == 0
    n_hg = ATTN_HEADS // nh
    hg_w = nh * HEAD_DIM
    kernel = functools.partial(_moba_kernel, n_blocks=n_blocks)
    return pl.pallas_call(
        kernel,
        grid=(n_hg, batch, n_blocks),
        in_specs=[
            pl.BlockSpec((1, MOBA_BLOCK, hg_w), lambda hg, b, qi: (b * n_blocks + qi, 0, hg)),
            pl.BlockSpec((1, MOBA_BLOCK, hg_w),
                         lambda hg, b, qi: (b * n_blocks + jnp.minimum(qi + 1, n_blocks - 1), 0, hg)),
            pl.BlockSpec((seq, hg_w), lambda hg, b, qi: (b, n_hg + hg)),
            pl.BlockSpec((n_blocks, MOBA_BLOCK, hg_w), lambda hg, b, qi: (b, 0, 2 * n_hg + hg)),
            pl.BlockSpec((nh, 2, MOBA_BLOCK, MOBA_BLOCK), lambda hg, b, qi: (hg, 0, 0, 0)),
        ],
        out_specs=pl.BlockSpec((MOBA_BLOCK, hg_w), lambda hg, b, qi: (b * n_blocks + qi, hg)),
        out_shape=jax.ShapeDtypeStruct((batch * seq, ATTN_HEADS * HEAD_DIM), BF16),
        scratch_shapes=[
            pltpu.VMEM((n_blocks, hg_w), F32),
            pltpu.VMEM((n_blocks, nh, V_AUG_ROWS, MOBA_BLOCK), BF16),
            pltpu.VMEM((nh, min(hg_w, MXU_DEPTH), MOBA_BLOCK), BF16),
            pltpu.VMEM((nh, 2, MOBA_BLOCK, MOBA_BLOCK), F32),
            pltpu.VMEM((nh, 2, MOBA_BLOCK, MOBA_BLOCK), F32),
            pltpu.VMEM((nh, 8, MOBA_BLOCK), F32),
            pltpu.VMEM((nh, 8, MOBA_BLOCK), F32),
            pltpu.VMEM((nh, 1, MOBA_BLOCK), F32),
            pltpu.VMEM((nh, V_AUG_ROWS, MOBA_BLOCK), F32),
            pltpu.VMEM((nh, n_blocks, MOBA_BLOCK), F32),
            pltpu.VMEM((nh, n_blocks, MOBA_BLOCK), F32),
            pltpu.VMEM((hg_w, MOBA_BLOCK), BF16),
            pltpu.SMEM((1,), jnp.int32),
        ],
        compiler_params=pltpu.CompilerParams(
            dimension_semantics=("parallel", "parallel", "arbitrary"),
            vmem_limit_bytes=VMEM_LIMIT),
    )(proj3, proj3, proj, proj3, bias_tiles)


def _hdot_nt(a, b):
    def split(x):
        hi = x.astype(BF16)
        return hi, (x - hi.astype(F32)).astype(BF16)

    def nt(u, v):
        return lax.dot_general(u, v, _NT_DIMS, preferred_element_type=F32)

    a_hi, a_lo = split(a)
    b_hi, b_lo = split(b)
    return nt(a_hi, b_hi) + nt(a_hi, b_lo) + nt(a_lo, b_hi)


def _repeat_rows(x, n):
    return jnp.concatenate([jnp.broadcast_to(x[i:i + 1], (n, x.shape[1]))
                            for i in range(x.shape[0])], axis=0)


def _tile_rows(x, n):
    return jnp.concatenate([x] * n, axis=0)


def _s5_tables_kernel(logdt_ref, lre_ref, lim_ref, bre_ref, bim_ref, cre_ref, cim_ref, d_ref,
                      win_ref, tg_ref, bn_ref, cn_ref, are_ref, aim_ref, wq_ref, *, q_cols, q_scale):
    wq_ref[:, :q_cols] = (win_ref[:, :q_cols] * q_scale).astype(wq_ref.dtype)
    wq_ref[:, q_cols:] = win_ref[:, q_cols:].astype(wq_ref.dtype)

    lc, ch, p = SSM_CHUNK, SSM_GROUP_CH, SSM_STATE
    w = lc * ch
    rr = lax.broadcasted_iota(jnp.int32, (w, w), 0)
    cc = lax.broadcasted_iota(jnp.int32, (w, w), 1)
    causal = rr // ch >= cc // ch
    diag = rr == cc
    e_r = lax.broadcasted_iota(jnp.int32, (ch, w), 0)
    e_c = lax.broadcasted_iota(jnp.int32, (ch, w), 1)
    lane_tile = jnp.where(e_c % ch == e_r, 1.0, 0.0).astype(F32)
    tau = lax.broadcasted_iota(jnp.int32, (2 * lc, p), 0).astype(F32)
    mid = lc // 2

    for k in range(tg_ref.shape[0]):
        lam_re = lre_ref[k]
        lam_im = lim_ref[k]
        dt = jnp.exp(logdt_ref[k])
        lr = lam_re * dt
        li = lam_im * dt

        mag = jnp.exp(lr * tau)
        pos_re, pos_im = mag * jnp.cos(li * tau), mag * jnp.sin(li * tau)
        tau_c = tau[:lc] - mid
        mag_c = jnp.exp(lr * tau_c)
        inv_c = jnp.exp(-lr * tau_c)
        cs, sn = jnp.cos(li * tau_c), jnp.sin(li * tau_c)
        fwd_re, fwd_im = mag_c * cs, mag_c * sn
        neg_re, neg_im = inv_c * cs, -inv_c * sn

        lam1_re, lam1_im = pos_re[1:2], pos_im[1:2]
        lamb_re, lamb_im = pos_re[lc - 1 - mid:lc - mid], pos_im[lc - 1 - mid:lc - mid]
        lamc_re, lamc_im = pos_re[mid + 1:mid + 2], pos_im[mid + 1:mid + 2]
        are_ref[k] = pos_re[lc:lc + 1]
        aim_ref[k] = pos_im[lc:lc + 1]

        num_re, num_im = lam1_re - 1.0, lam1_im
        den = lam_re * lam_re + lam_im * lam_im
        coef_re = (num_re * lam_re + num_im * lam_im) / den
        coef_im = (num_im * lam_re - num_re * lam_im) / den
        bt_re, bt_im = bre_ref[k], bim_ref[k]
        bb_re = coef_re * bt_re - coef_im * bt_im
        bb_im = coef_re * bt_im + coef_im * bt_re

        bbt_re, bbt_im = _tile_rows(bb_re, lc), _tile_rows(bb_im, lc)
        ngx_re, ngx_im = _repeat_rows(neg_re, ch), _repeat_rows(neg_im, ch)
        bneg_re = bbt_re * ngx_re - bbt_im * ngx_im
        bneg_im = bbt_re * ngx_im + bbt_im * ngx_re

        ct_re, ct_im = _tile_rows(cre_ref[k], lc), _tile_rows(cim_ref[k], lc)
        psx_re, psx_im = _repeat_rows(fwd_re, ch), _repeat_rows(fwd_im, ch)
        cpos_re = ct_re * psx_re - ct_im * psx_im
        cpos_im = ct_re * psx_im + ct_im * psx_re

        raw = _hdot_nt(cpos_re, bneg_re) - _hdot_nt(cpos_im, bneg_im)
        d_lanes = jnp.dot(jnp.broadcast_to(d_ref[k], (8, ch)), lane_tile,
                          preferred_element_type=F32, precision=lax.Precision.HIGHEST)[0:1]
        tg = jnp.where(causal, raw, 0.0) + jnp.where(diag, d_lanes, 0.0)
        tg_ref[k] = tg.astype(tg_ref.dtype)

        bn_ref[k, 0] = (bneg_re * lamb_re - bneg_im * lamb_im).T.astype(bn_ref.dtype)
        bn_ref[k, 1] = (bneg_re * lamb_im + bneg_im * lamb_re).T.astype(bn_ref.dtype)
        cp_re = cpos_re * lamc_re - cpos_im * lamc_im
        cp_im = cpos_re * lamc_im + cpos_im * lamc_re
        cn_ref[k, 0] = cp_re.astype(cn_ref.dtype)
        cn_ref[k, 1] = (-cp_im).astype(cn_ref.dtype)


def _s5_tables(log_dt, lam_re, lam_im, b_re, b_im, c_re, c_im, d_skip, w_in, q_cols, q_scale, gb=8):
    g, p, ch = SSM_GROUPS, SSM_STATE, SSM_GROUP_CH
    w = SSM_CHUNK * ch
    steps = g // gb
    d, n = w_in.shape
    row = lambda a, n: a.reshape(g, 1, n)
    spec3 = lambda s1, s2: pl.BlockSpec((gb, s1, s2), lambda i: (i, 0, 0))
    spec4 = lambda s1, s2: pl.BlockSpec((gb, 2, s1, s2), lambda i: (i, 0, 0, 0))
    slab = pl.BlockSpec((d // steps, n), lambda i: (i, 0))
    return pl.pallas_call(
        functools.partial(_s5_tables_kernel, q_cols=q_cols, q_scale=q_scale),
        grid=(steps,),
        in_specs=[spec3(1, 1), spec3(1, p), spec3(1, p), spec3(ch, p), spec3(ch, p),
                  spec3(ch, p), spec3(ch, p), spec3(1, ch), slab],
        out_specs=[spec3(w, w), spec4(p, w), spec4(w, p), spec3(1, p), spec3(1, p), slab],
        out_shape=[jax.ShapeDtypeStruct((g, w, w), BF16),
                   jax.ShapeDtypeStruct((g, 2, p, w), BF16),
                   jax.ShapeDtypeStruct((g, 2, w, p), BF16),
                   jax.ShapeDtypeStruct((g, 1, p), F32),
                   jax.ShapeDtypeStruct((g, 1, p), F32),
                   jax.ShapeDtypeStruct((d, n), BF16)],
        compiler_params=pltpu.CompilerParams(dimension_semantics=("parallel",)),
    )(row(log_dt, 1), row(lam_re, p), row(lam_im, p),
      jnp.swapaxes(b_re, 1, 2), jnp.swapaxes(b_im, 1, 2), c_re, c_im, row(d_skip, ch), w_in)


def _s5_scan_kernel(u_ref, tg_ref, bn_ref, cn_ref, are_ref, aim_ref, y_ref,
                    uf_ref, v_ref, yt_ref, sre_ref, sim_ref, xre_ref, xim_ref, *, n_batch):
    lc, ch = SSM_CHUNK, SSM_GROUP_CH
    gb = tg_ref.shape[0]
    n_rows = u_ref.shape[0] // lc
    n_chunks = n_rows // n_batch
    w = lc * ch

    uf_ref[...] = u_ref[...].astype(F32)
    for s in range(lc):
        ust = uf_ref[pl.ds(s, n_rows, stride=lc), :].T
        for k in range(gb):
            v_ref[k, s * ch:(s + 1) * ch, :] = ust[k * ch:(k + 1) * ch, :].astype(v_ref.dtype)

    for k in range(gb):
        u = v_ref[k]
        rows = pl.ds(k, n_rows, stride=gb)
        sre_ref[rows, :] = jnp.dot(bn_ref[k, 0], u, preferred_element_type=F32).T
        sim_ref[rows, :] = jnp.dot(bn_ref[k, 1], u, preferred_element_type=F32).T

    a_re = are_ref[...]
    a_im = aim_ref[...]

    def step(c, carry):
        new = []
        for b in range(n_batch):
            x_re, x_im = carry[2 * b], carry[2 * b + 1]
            rows = pl.ds(pl.multiple_of((b * n_chunks + c) * gb, gb), gb)
            xre_ref[rows, :] = x_re
            xim_ref[rows, :] = x_im
            new.append(a_re * x_re - a_im * x_im + sre_ref[rows, :])
            new.append(a_re * x_im + a_im * x_re + sim_ref[rows, :])
        return tuple(new)

    zero = jnp.zeros(a_re.shape, F32)
    lax.fori_loop(0, n_chunks, step, (zero,) * (2 * n_batch), unroll=SCAN_UNROLL)

    for k in range(gb):
        rows = pl.ds(k, n_rows, stride=gb)
        y = jnp.dot(tg_ref[k], v_ref[k], preferred_element_type=F32)
        y = y + lax.dot_general(cn_ref[k, 0], xre_ref[rows, :].astype(BF16), _NT_DIMS,
                                preferred_element_type=F32)
        y = y + lax.dot_general(cn_ref[k, 1], xim_ref[rows, :].astype(BF16), _NT_DIMS,
                                preferred_element_type=F32)
        for s in range(lc):
            yt_ref[s, k * ch:(k + 1) * ch, :] = y[s * ch:(s + 1) * ch, :]

    for s in range(lc):
        y_ref[pl.ds(s, n_rows, stride=lc), :] = yt_ref[s].T.astype(y_ref.dtype)


def _s5_scan(ku, tg, bn, cn, a_re, a_im, n_batch, u_col0, gb=8):
    t = ku.shape[0]
    g, w, _ = tg.shape
    p, ch = SSM_STATE, SSM_GROUP_CH
    n_rows = t // SSM_CHUNK
    lanes = gb * ch
    kernel = functools.partial(_s5_scan_kernel, n_batch=n_batch)
    return pl.pallas_call(
        kernel,
        grid=(g // gb,),
        in_specs=[pl.BlockSpec((t, lanes), lambda i: (0, u_col0 // lanes + i)),
                  pl.BlockSpec((gb, w, w), lambda i: (i, 0, 0)),
                  pl.BlockSpec((gb, 2, p, w), lambda i: (i, 0, 0, 0)),
                  pl.BlockSpec((gb, 2, w, p), lambda i: (i, 0, 0, 0)),
                  pl.BlockSpec((gb, p), lambda i: (i, 0)),
                  pl.BlockSpec((gb, p), lambda i: (i, 0))],
        out_specs=pl.BlockSpec((t, lanes), lambda i: (0, i)),
        out_shape=jax.ShapeDtypeStruct((t, g * ch), F32),
        scratch_shapes=[pltpu.VMEM((t, lanes), F32),
                        pltpu.VMEM((gb, w, n_rows), BF16),
                        pltpu.VMEM((SSM_CHUNK, lanes, n_rows), F32)]
                       + [pltpu.VMEM((gb * n_rows, p), F32) for _ in range(4)],
        compiler_params=pltpu.CompilerParams(dimension_semantics=("parallel",),
                                             vmem_limit_bytes=VMEM_LIMIT),
    )(ku, tg, bn, cn, a_re.reshape(g, p), a_im.reshape(g, p))


def _out_proj_kernel(x_ref, a_ref, y_ref, wglu_ref, bglu_ref, ga_ref, gs_ref, wa_ref, ws_ref, o_ref):
    z = jax.nn.gelu(y_ref[...])
    gl = jnp.dot(z.astype(BF16), wglu_ref[...], preferred_element_type=F32) + bglu_ref[...]
    s = z * jax.nn.sigmoid(gl)
    a_n = _rms_rows(a_ref[...].astype(F32), ga_ref[...]).astype(BF16)
    s_n = _rms_rows(s, gs_ref[...]).astype(BF16)
    mix = (jnp.dot(a_n, wa_ref[...], preferred_element_type=F32)
           + jnp.dot(s_n, ws_ref[...], preferred_element_type=F32))
    o_ref[...] = x_ref[...] + mix


def _out_proj(x2, attn, y, w_glu, b_glu, g_a, g_s, w_out, tm=512):
    t, d = x2.shape
    wa = attn.shape[1]
    ws = y.shape[1]
    assert wa == ws
    row = lambda i: (i, 0)
    fixed = lambda i: (0, 0)
    once = pl.Buffered(1)
    return pl.pallas_call(
        _out_proj_kernel,
        grid=(t // tm,),
        in_specs=[pl.BlockSpec((tm, d), row),
                  pl.BlockSpec((tm, wa), row),
                  pl.BlockSpec((tm, ws), row),
                  pl.BlockSpec((ws, ws), fixed, pipeline_mode=once),
                  pl.BlockSpec((1, ws), fixed),
                  pl.BlockSpec((1, wa), fixed),
                  pl.BlockSpec((1, ws), fixed),
                  pl.BlockSpec((wa, d), fixed, pipeline_mode=once),
                  pl.BlockSpec((ws, d), lambda i: (1, 0), pipeline_mode=once)],
        out_specs=pl.BlockSpec((tm, d), row),
        out_shape=jax.ShapeDtypeStruct((t, d), F32),
        compiler_params=pltpu.CompilerParams(dimension_semantics=("parallel",),
                                             vmem_limit_bytes=VMEM_LIMIT),
    )(x2, attn, y, w_glu, b_glu.reshape(1, ws), g_a.reshape(1, wa), g_s.reshape(1, ws),
      w_out, w_out)


def _ffn_up_kernel(x_ref, halo_ref, g_ref, wg_ref, wv_ref, cwg_ref, cwv_ref, cbg_ref, cbv_ref,
                   wd_ref, o_ref, wdq_ref, h_ref, *, tiles_per_seq):
    tm = x_ref.shape[0]
    wdq_ref[...] = wd_ref[...].astype(wdq_ref.dtype)

    @pl.when(pl.program_id(1) == 0)
    def _():
        keep = jnp.where(pl.program_id(0) % tiles_per_seq == 0, 0.0, 1.0)
        h_ref[pl.ds(0, CONV_HALO), :] = (_rms_rows(halo_ref[...], g_ref[...]) * keep).astype(BF16)
        h_ref[pl.ds(CONV_HALO, tm), :] = _rms_rows(x_ref[...], g_ref[...]).astype(BF16)

    h = h_ref[...]

    def conv(w_ref, cw_ref, cb_ref):
        up = jnp.dot(h, w_ref[...].astype(BF16), preferred_element_type=F32)
        cw = cw_ref[...]
        out = up[CONV_HALO:] * cw[CONV_WIDTH - 1:CONV_WIDTH] + cb_ref[...]
        for j in range(CONV_WIDTH - 1):
            lag = CONV_WIDTH - 1 - j
            out = out + up[CONV_HALO - lag:CONV_HALO - lag + tm] * cw[j:j + 1]
        return out

    gate = conv(wg_ref, cwg_ref, cbg_ref)
    val = conv(wv_ref, cwv_ref, cbv_ref)
    o_ref[...] = (jax.nn.silu(gate) * val).astype(o_ref.dtype)


def _ffn_up(x1, g, w_up, conv_w, conv_b, w_down, seq, tm=1024, tn=512):
    t, d = x1.shape
    f = w_up.shape[1] // 2
    nf = f // tn
    tiles_per_seq = seq // tm
    halo_blocks = tm // CONV_HALO
    kernel = functools.partial(_ffn_up_kernel, tiles_per_seq=tiles_per_seq)
    cb = conv_b.reshape(1, 2 * f)
    wd_slab = pl.BlockSpec((w_down.shape[0] // (t // tm * nf), w_down.shape[1]),
                           lambda i, j: (i * nf + j, 0))
    return pl.pallas_call(
        kernel,
        grid=(t // tm, nf),
        in_specs=[pl.BlockSpec((tm, d), lambda i, j: (i, 0)),
                  pl.BlockSpec((CONV_HALO, d), lambda i, j: (jnp.maximum(i * halo_blocks - 1, 0), 0)),
                  pl.BlockSpec((1, d), lambda i, j: (0, 0)),
                  pl.BlockSpec((d, tn), lambda i, j: (0, j)),
                  pl.BlockSpec((d, tn), lambda i, j: (0, nf + j)),
                  pl.BlockSpec((CONV_WIDTH, tn), lambda i, j: (0, j)),
                  pl.BlockSpec((CONV_WIDTH, tn), lambda i, j: (0, nf + j)),
                  pl.BlockSpec((1, tn), lambda i, j: (0, j)),
                  pl.BlockSpec((1, tn), lambda i, j: (0, nf + j)),
                  wd_slab],
        out_specs=[pl.BlockSpec((tm, tn), lambda i, j: (i, j)), wd_slab],
        out_shape=[jax.ShapeDtypeStruct((t, f), BF16), jax.ShapeDtypeStruct(w_down.shape, BF16)],
        scratch_shapes=[pltpu.VMEM((CONV_HALO + tm, d), BF16)],
        compiler_params=pltpu.CompilerParams(dimension_semantics=("parallel", "arbitrary"),
                                             vmem_limit_bytes=VMEM_LIMIT),
    )(x1, x1, g.reshape(1, d), w_up, w_up, conv_w, conv_w, cb, cb, w_down)


def _ffn_down_kernel(a_ref, w_ref, x_ref, g_ref, o_ref, *, final_norm):
    y = x_ref[...] + jnp.dot(a_ref[...], w_ref[...], preferred_element_type=F32)
    o_ref[...] = _rms_rows(y, g_ref[...]) if final_norm else y


def _ffn_down(act, w_down, x1, g, final_norm, tm=512):
    t, f = act.shape
    d = w_down.shape[1]
    return pl.pallas_call(
        functools.partial(_ffn_down_kernel, final_norm=final_norm),
        grid=(t // tm,),
        in_specs=[pl.BlockSpec((tm, f), lambda i: (i, 0)),
                  pl.BlockSpec((f, d), lambda i: (0, 0), pipeline_mode=pl.Buffered(1)),
                  pl.BlockSpec((tm, d), lambda i: (i, 0)),
                  pl.BlockSpec((1, d), lambda i: (0, 0))],
        out_specs=pl.BlockSpec((tm, d), lambda i: (i, 0)),
        out_shape=jax.ShapeDtypeStruct((t, d), F32),
        compiler_params=pltpu.CompilerParams(dimension_semantics=("parallel",),
                                             vmem_limit_bytes=VMEM_LIMIT_RESIDENT),
    )(act, w_down, x1, g.reshape(1, d))


def kernel(x, norm_mix, w_in, rel_bias_table, ssm_lam_re, ssm_lam_im, ssm_log_dt, ssm_b_re, ssm_b_im, ssm_c_re, ssm_c_im, ssm_d, ssm_w_glu, ssm_b_glu, norm_attn_out, norm_ssm_out, w_out, norm_ffn, w_ffn_up, ffn_conv_w, ffn_conv_b, w_ffn_down, norm_final):
    batch, seq, d_model = x.shape
    depth = w_in.shape[0]
    aw = ATTN_HEADS * HEAD_DIM
    t = batch * seq
    assert seq % MOBA_BLOCK == 0 and seq % SSM_CHUNK == 0
    assert MOBA_BLOCK >= MAX_DISTANCE

    bias_tiles = _bias_tiles(rel_bias_table)
    x2 = x.reshape(t, d_model)
    for l in range(depth):
        tg, bn, cn, a_re, a_im, w_in_b = _s5_tables(
            ssm_log_dt[l], ssm_lam_re[l], ssm_lam_im[l], ssm_b_re[l], ssm_b_im[l], ssm_c_re[l],
            ssm_c_im[l], ssm_d[l], w_in[l], q_cols=aw, q_scale=HEAD_DIM ** -0.5 * LOG2E)

        proj, (w_out_b, w_glu_b) = _in_proj(x2, norm_mix[l], w_in_b, (w_out[l], ssm_w_glu[l]))

        attn = _moba_attention(proj, bias_tiles, batch, seq)
        y = _s5_scan(proj, tg, bn, cn, a_re, a_im, batch, u_col0=3 * aw)

        x2 = _out_proj(x2, attn, y, w_glu_b, ssm_b_glu[l], norm_attn_out[l], norm_ssm_out[l], w_out_b)

        act, w_down_b = _ffn_up(x2, norm_ffn[l], w_ffn_up[l], ffn_conv_w[l], ffn_conv_b[l],
                                w_ffn_down[l], seq)
        x2 = _ffn_down(act, w_down_b, x2, norm_final, final_norm=(l == depth - 1))
    return x2.reshape(batch, seq, d_model)
```
